```python
import jax, jax.numpy as jnp
from jax import lax
import numpy as np

D_MODEL = 1024
BATCH = 8
SEQ = 2048
DEPTH = 1
DEC_BATCH = 4
DEC_SEQ = 8192
PAST_LEN = 128

D_MIX = D_MODEL
N_MLSTM_HEADS = 4
MLSTM_HEAD_DIM = 128
D_MLSTM = N_MLSTM_HEADS * MLSTM_HEAD_DIM
D_CONV = D_MIX - D_MLSTM
CONV_WIDTH = 31
CHUNK = 64
N_DIR = 2
N_GATE_COLS = N_DIR * 2 * N_MLSTM_HEADS
OFF_Q = 0
OFF_K = OFF_Q + D_MLSTM
OFF_V = OFF_K + D_MLSTM
OFF_O = OFF_V + D_MLSTM
OFF_G = OFF_O + D_MLSTM
OFF_A = OFF_G + N_GATE_COLS
OFF_B = OFF_A + D_CONV
D_IN_PROJ = OFF_B + D_CONV
N_GROUPS = 4
EXPERTS_PER_GROUP = 4
N_EXPERTS = N_GROUPS * EXPERTS_PER_GROUP
TOP_K_IN_GROUP = 2
D_EXPERT = 512
EPS = 1e-6

kernel_name = 'hybrid_mlstm_conformer_hmoe_encoder'


def _rmsnorm(x, g):
    xf = x.astype(jnp.float32)
    y = xf * lax.rsqrt(jnp.mean(xf * xf, axis=-1, keepdims=True) + EPS)
    return (y * g.astype(jnp.float32)).astype(x.dtype)


def _layernorm(x, g, b):
    xf = x.astype(jnp.float32)
    mu = jnp.mean(xf, axis=-1, keepdims=True)
    xc = xf - mu
    y = xc * lax.rsqrt(jnp.mean(xc * xc, axis=-1, keepdims=True) + EPS)
    return (y * g.astype(jnp.float32) + b.astype(jnp.float32)).astype(x.dtype)


def _mlstm_chunkwise(q, k, v, ig, lf):
    n_seq, n_h, length, d = q.shape
    nc = length // CHUNK

    def to_chunks(a):
        a = a.reshape((n_seq, n_h, nc, CHUNK) + a.shape[3:])
        return jnp.moveaxis(a, 2, 0)

    lower = jnp.tril(jnp.ones((CHUNK, CHUNK), dtype=bool))

    def step(carry, xs):
        c_mat, n_vec, m = carry
        qc, kc, vc, ic, fc = xs
        bc = jnp.cumsum(fc, axis=-1)
        d_mat = jnp.where(lower, bc[..., :, None] - bc[..., None, :] + ic[..., None, :], -jnp.inf)
        inter = bc + m[..., None]
        m_j = jnp.maximum(inter, jnp.max(d_mat, axis=-1))
        s = jnp.einsum('nhjd,nhsd->nhjs', qc, kc) * jnp.exp(d_mat - m_j[..., None])
        w_inter = jnp.exp(inter - m_j)
        num = (jnp.einsum('nhjs,nhse->nhje', s, vc)
               + w_inter[..., None] * jnp.einsum('nhed,nhjd->nhje', c_mat, qc))
        den = jnp.sum(s, axis=-1) + w_inter * jnp.einsum('nhd,nhjd->nhj', n_vec, qc)
        h = num / jnp.maximum(jnp.abs(den), jnp.exp(-m_j))[..., None]
        b_last = bc[..., -1]
        g = b_last[..., None] - bc + ic
        m_new = jnp.maximum(b_last + m, jnp.max(g, axis=-1))
        w_s = jnp.exp(g - m_new[..., None])
        decay = jnp.exp(b_last + m - m_new)
        c_new = decay[..., None, None] * c_mat + jnp.einsum('nhse,nhsd->nhed', vc * w_s[..., None], kc)
        n_new = decay[..., None] * n_vec + jnp.einsum('nhs,nhsd->nhd', w_s, kc)
        return (c_new, n_new, m_new), h

    init = (jnp.zeros((n_seq, n_h, d, d), jnp.float32),
            jnp.zeros((n_seq, n_h, d), jnp.float32),
            jnp.zeros((n_seq, n_h), jnp.float32))
    _, hs = lax.scan(step, init, (to_chunks(q), to_chunks(k), to_chunks(v), to_chunks(ig), to_chunks(lf)))
    return jnp.moveaxis(hs, 0, 2).reshape(n_seq, n_h, length, d)


def _mlstm_group(p, b_gate):
    bsz, length, _ = p.shape
    f32 = jnp.float32

    def heads(a):
        return a.reshape(bsz, length, N_MLSTM_HEADS, MLSTM_HEAD_DIM).transpose(0, 2, 1, 3).astype(f32)

    q = heads(p[..., OFF_Q:OFF_K])
    k = heads(p[..., OFF_K:OFF_V]) * (MLSTM_HEAD_DIM ** -0.5)
    v = heads(p[..., OFF_V:OFF_O])
    gates = (p[..., OFF_G:OFF_A].astype(f32) + b_gate.astype(f32)).reshape(bsz, length, N_DIR, 2, N_MLSTM_HEADS)
    gates = gates.transpose(2, 3, 0, 4, 1)
    ig = gates[:, 0]
    lf = jax.nn.log_sigmoid(gates[:, 1])

    def both(a, b_dir):
        return jnp.stack([a, jnp.flip(b_dir, axis=2)]).reshape((N_DIR * bsz,) + a.shape[1:])

    h = _mlstm_chunkwise(both(q, q), both(k, k), both(v, v), both(ig[0], ig[1]), both(lf[0], lf[1]))
    h = h.reshape(N_DIR, bsz, N_MLSTM_HEADS, length, MLSTM_HEAD_DIM)
    h = h[0] + jnp.flip(h[1], axis=2)
    h = h.transpose(0, 2, 1, 3).reshape(bsz, length, D_MLSTM)
    o = jax.nn.sigmoid(p[..., OFF_O:OFF_G].astype(f32))
    return (o * h).astype(p.dtype)


def _conv_group(p, conv_w, conv_b, ln_g, ln_b):
    u = p[..., OFF_A:OFF_B] * jax.nn.sigmoid(p[..., OFF_B:D_IN_PROJ])
    u = lax.conv_general_dilated(u, conv_w, window_strides=(1,),
                                 padding=[(CONV_WIDTH // 2, CONV_WIDTH // 2)],
                                 dimension_numbers=('NWC', 'WIO', 'NWC'),
                                 feature_group_count=D_CONV) + conv_b
    return jax.nn.silu(_layernorm(u, ln_g, ln_b))


def _hier_moe(h, w_rg, b_rg, w_re, b_re, w_gate, w_up, w_down):
    bsz, length, d = h.shape
    f32 = jnp.float32
    t = h.reshape(bsz * length, d)
    p_group = jax.nn.softmax((t @ w_rg).astype(f32) + b_rg.astype(f32), axis=-1)
    p_top, g_idx = lax.top_k(p_group, 1)
    expert_logits = ((t @ w_re).astype(f32) + b_re.astype(f32)).reshape(-1, N_GROUPS, EXPERTS_PER_GROUP)
    in_group = jnp.take_along_axis(expert_logits, g_idx[:, :, None], axis=1)[:, 0]
    w_top, j_idx = lax.top_k(jax.nn.softmax(in_group, axis=-1), TOP_K_IN_GROUP)
    w_top = w_top / jnp.sum(w_top, axis=-1, keepdims=True)
    gate_w = p_top * w_top
    expert_idx = g_idx * EXPERTS_PER_GROUP + j_idx
    gates = jnp.einsum('tk,tke->te', gate_w, jax.nn.one_hot(expert_idx, N_EXPERTS, dtype=f32)).astype(h.dtype)
    out = jnp.zeros_like(t)
    for e in range(N_EXPERTS):
        hid = jax.nn.silu(t @ w_gate[e]) * (t @ w_up[e])
        out = out + gates[:, e:e + 1] * (hid @ w_down[e])
    return out.reshape(bsz, length, d)


def _encoder_trunk(x, g_mix, w_in, b_gate, conv_w, conv_b, ln_g, ln_b, w_out, g_ffn,
                   w_router_group, b_router_group, w_router_expert, b_router_expert,
                   w_gate, w_up, w_down, g_final):
    for layer in range(DEPTH):
        p = _rmsnorm(x, g_mix[layer]) @ w_in[layer]
        y_m = _mlstm_group(p, b_gate[layer])
        y_c = _conv_group(p, conv_w[layer], conv_b[layer], ln_g[layer], ln_b[layer])
        x = x + jnp.concatenate([y_m, y_c], axis=-1) @ w_out[layer]
        x = x + _hier_moe(_rmsnorm(x, g_ffn[layer]), w_router_group[layer], b_router_group[layer],
                          w_router_expert[layer], b_router_expert[layer],
                          w_gate[layer], w_up[layer], w_down[layer])
    return _rmsnorm(x, g_final)


def setup_inputs(seed: int = 0) -> dict:
    key = jax.random.key(seed)
    ks = jax.random.split(key, 24)
    nrm = jax.random.normal
    f32 = jnp.float32
    gate_bias = (0.1 * nrm(ks[3], (DEPTH, N_DIR, 2, N_MLSTM_HEADS), f32)
                 + jnp.stack([jnp.zeros((N_MLSTM_HEADS,), f32),
                              jnp.linspace(3.0, 6.0, N_MLSTM_HEADS, dtype=f32)])[None, None])
    return {
        'x_prompt': nrm(ks[0], (BATCH, SEQ, D_MODEL), f32),
        'x_sample': nrm(ks[1], (DEC_BATCH, DEC_SEQ, D_MODEL), f32),
        'g_mix': 1.0 + 0.02 * nrm(ks[2], (DEPTH, D_MODEL), f32),
        'w_in': nrm(ks[4], (DEPTH, D_MODEL, D_IN_PROJ), f32) * D_MODEL ** -0.5,
        'b_gate': gate_bias.reshape(DEPTH, N_GATE_COLS),
        'conv_w': nrm(ks[5], (DEPTH, CONV_WIDTH, 1, D_CONV), f32) * CONV_WIDTH ** -0.5,
        'conv_b': 0.01 * nrm(ks[6], (DEPTH, D_CONV), f32),
        'ln_g': 1.0 + 0.02 * nrm(ks[7], (DEPTH, D_CONV), f32),
        'ln_b': 0.01 * nrm(ks[8], (DEPTH, D_CONV), f32),
        'w_out': nrm(ks[9], (DEPTH, D_MIX, D_MODEL), f32) * D_MIX ** -0.5,
        'g_ffn': 1.0 + 0.02 * nrm(ks[10], (DEPTH, D_MODEL), f32),
        'w_router_group': nrm(ks[11], (DEPTH, D_MODEL, N_GROUPS), f32) * D_MODEL ** -0.5,
        'b_router_group': 0.01 * nrm(ks[12], (DEPTH, N_GROUPS), f32),
        'w_router_expert': nrm(ks[13], (DEPTH, D_MODEL, N_EXPERTS), f32) * D_MODEL ** -0.5,
        'b_router_expert': 0.01 * nrm(ks[14], (DEPTH, N_EXPERTS), f32),
        'w_gate': nrm(ks[15], (DEPTH, N_EXPERTS, D_MODEL, D_EXPERT), f32) * D_MODEL ** -0.5,
        'w_up': nrm(ks[16], (DEPTH, N_EXPERTS, D_MODEL, D_EXPERT), f32) * D_MODEL ** -0.5,
        'w_down': nrm(ks[17], (DEPTH, N_EXPERTS, D_EXPERT, D_MODEL), f32) * D_EXPERT ** -0.5,
        'g_final': 1.0 + 0.02 * nrm(ks[18], (D_MODEL,), f32),
    }


def reference(x_prompt, x_sample, g_mix, w_in, b_gate, conv_w, conv_b, ln_g, ln_b, w_out, g_ffn,
              w_router_group, b_router_group, w_router_expert, b_router_expert,
              w_gate, w_up, w_down, g_final):
    y_prompt = _encoder_trunk(x_prompt, g_mix, w_in, b_gate, conv_w, conv_b, ln_g, ln_b, w_out, g_ffn,
                              w_router_group, b_router_group, w_router_expert, b_router_expert,
                              w_gate, w_up, w_down, g_final)
    y_sample = _encoder_trunk(x_sample, g_mix, w_in, b_gate, conv_w, conv_b, ln_g, ln_b, w_out, g_ffn,
                              w_router_group, b_router_group, w_router_expert, b_router_expert,
                              w_gate, w_up, w_down, g_final)
    return (y_prompt, y_sample)
```

```python
import functools

import jax
import jax.numpy as jnp
from jax import lax
from jax.experimental import pallas as pl
from jax.experimental.pallas import tpu as pltpu

F32 = jnp.float32
BF16 = jnp.bfloat16

D_MODEL = 1024
N_HEADS = 4
HEAD_DIM = 128
D_MLSTM = N_HEADS * HEAD_DIM
D_CONV = D_MODEL - D_MLSTM
CONV_WIDTH = 31
CONV_HALO = 16
N_DIR = 2
N_GROUPS = 4
EXPERTS_PER_GROUP = 4
N_EXPERTS = N_GROUPS * EXPERTS_PER_GROUP
D_EXPERT = 512
EPS = 1e-6
K_SCALE = HEAD_DIM ** -0.5

GATE_ROWS = 16
ROUTER_LANES = 128
EXPERT_LANE0 = N_GROUPS
META_LANES = 8

TM_IN = 512
CHUNK = 128
TM_MIX = 512
TM_EXP = 256
TM_OUT = 256
VMEM_LIMIT = 48 * 1024 * 1024


def _cparams(n_axes=1):
    return pltpu.CompilerParams(dimension_semantics=("arbitrary",) * n_axes,
                                vmem_limit_bytes=VMEM_LIMIT)


def _nt_dot(a, b):
    return lax.dot_general(a, b, (((1,), (1,)), ((), ())), preferred_element_type=F32)


def _rms(x, g):
    return x * lax.rsqrt(jnp.mean(x * x, axis=-1, keepdims=True) + EPS) * g


def _inproj_kernel(x_ref, g_ref, wq_ref, wv_ref, wo_ref, wa_ref, wb_ref, wkt_ref, wgt_ref, bg_ref,
                   q_ref, v_ref, o_ref, u_ref, kt_ref, gr_ref):
    xn = _rms(x_ref[...], g_ref[...]).astype(BF16)
    q_ref[...] = jnp.dot(xn, wq_ref[...], preferred_element_type=F32).astype(BF16)
    v_ref[...] = jnp.dot(xn, wv_ref[...], preferred_element_type=F32).astype(BF16)
    o_ref[...] = jnp.dot(xn, wo_ref[...], preferred_element_type=F32).astype(BF16)
    a = jnp.dot(xn, wa_ref[...], preferred_element_type=F32)
    b = jnp.dot(xn, wb_ref[...], preferred_element_type=F32)
    u_ref[...] = (a * jax.nn.sigmoid(b)).astype(BF16)
    kt_ref[...] = (_nt_dot(wkt_ref[...], xn) * K_SCALE).astype(BF16)
    gr_ref[...] = _nt_dot(wgt_ref[...], xn) + bg_ref[...]


def _inproj(x, g_mix, w_in, b_gate):
    t = x.shape[0]
    assert t % TM_IN == 0
    off_k, off_v, off_o, off_g = D_MLSTM, 2 * D_MLSTM, 3 * D_MLSTM, 4 * D_MLSTM
    off_a = off_g + 2 * N_DIR * N_HEADS
    off_b = off_a + D_CONV
    wq = w_in[:, 0:off_k].astype(BF16)
    wkt = w_in[:, off_k:off_v].T.astype(BF16)
    wv = w_in[:, off_v:off_o].astype(BF16)
    wo = w_in[:, off_o:off_g].astype(BF16)
    wa = w_in[:, off_a:off_b].astype(BF16)
    wb = w_in[:, off_b:off_b + D_CONV].astype(BF16)
    wg = w_in[:, off_g:off_a].T.reshape(N_DIR, 2, N_HEADS, D_MODEL)
    wgt = jnp.zeros((N_DIR, 2, GATE_ROWS // 2, D_MODEL), F32).at[:, :, :N_HEADS].set(wg)
    wgt = wgt.reshape(N_DIR * GATE_ROWS, D_MODEL).astype(BF16)
    bg = jnp.zeros((N_DIR, 2, GATE_ROWS // 2), F32).at[:, :, :N_HEADS].set(
        b_gate.reshape(N_DIR, 2, N_HEADS)).reshape(N_DIR * GATE_ROWS, 1)

    tok = lambda i: (i, 0)
    fixed = lambda i: (0, 0)
    wspec = pl.BlockSpec((D_MODEL, D_MLSTM), fixed)
    return pl.pallas_call(
        _inproj_kernel,
        grid=(t // TM_IN,),
        in_specs=[pl.BlockSpec((TM_IN, D_MODEL), tok), pl.BlockSpec((1, D_MODEL), fixed),
                  wspec, wspec, wspec, wspec, wspec,
                  pl.BlockSpec((D_MLSTM, D_MODEL), fixed),
                  pl.BlockSpec((N_DIR * GATE_ROWS, D_MODEL), fixed),
                  pl.BlockSpec((N_DIR * GATE_ROWS, 1), fixed)],
        out_specs=[pl.BlockSpec((TM_IN, D_MLSTM), tok)] * 4 + [
            pl.BlockSpec((D_MLSTM, TM_IN), lambda i: (0, i)),
            pl.BlockSpec((N_DIR * GATE_ROWS, TM_IN), lambda i: (0, i))],
        out_shape=[jax.ShapeDtypeStruct((t, D_MLSTM), BF16)] * 4 + [
            jax.ShapeDtypeStruct((D_MLSTM, t), BF16),
            jax.ShapeDtypeStruct((N_DIR * GATE_ROWS, t), F32)],
        compiler_params=_cparams(),
        name="inproj",
    )(x, g_mix.reshape(1, D_MODEL), wq, wv, wo, wa, wb, wkt, wgt, bg)


def _log_sigmoid(x):
    return jnp.minimum(x, 0.0) - jnp.log1p(jnp.exp(-jnp.abs(x)))


def _mlstm_kernel(reset_ref, q_ref, kt_ref, v_ref, g_ref, h_ref, cst_ref, m_ref, *, rev):
    c = q_ref.shape[0]
    step = pl.program_id(0)

    @pl.when(reset_ref[step] == 1)
    def _():
        cst_ref[...] = jnp.zeros_like(cst_ref)
        m_ref[...] = jnp.zeros_like(m_ref)

    ig = g_ref[0:8, :]
    lf = _log_sigmoid(g_ref[8:16, :])
    lane = lax.broadcasted_iota(jnp.int32, (8, c), 1)

    def scan(x, op, ident):
        k = 1
        while k < c:
            if rev:
                shifted, valid = pltpu.roll(x, c - k, axis=1), lane < c - k
            else:
                shifted, valid = pltpu.roll(x, k, axis=1), lane >= k
            x = op(x, jnp.where(valid, shifted, ident))
            k *= 2
        return x

    bc = scan(lf, jnp.add, 0.0)
    a = ig - bc
    m_old = m_ref[...]
    mx = jnp.maximum(m_old, scan(a, jnp.maximum, -jnp.inf))
    mx_last = jnp.max(mx, axis=1, keepdims=True)
    b_last = jnp.sum(lf, axis=1, keepdims=True)
    w_row = jnp.exp(a - mx_last)
    decay = jnp.exp(m_old - mx_last)
    m_ref[...] = jnp.broadcast_to(b_last + mx_last, (8, c))

    mx_col = mx.T
    e1_col = jnp.exp(m_old - mx).T
    floor_col = jnp.exp(-(mx + bc)).T

    row_i = lax.broadcasted_iota(jnp.int32, (c, c), 0)
    col_i = lax.broadcasted_iota(jnp.int32, (c, c), 1)
    mask = (col_i >= row_i) if rev else (col_i <= row_i)
    ones = jnp.ones((c, HEAD_DIM), BF16)

    for h in range(N_HEADS):
        hs = slice(h * HEAD_DIM, (h + 1) * HEAD_DIM)
        qh = q_ref[:, hs]
        kth = kt_ref[hs, :]
        vext = jnp.concatenate([v_ref[:, hs], ones], axis=1)
        s = jnp.dot(qh, kth, preferred_element_type=F32)
        e = jnp.exp(jnp.where(mask, a[h:h + 1, :] - mx_col[:, h:h + 1], -jnp.inf))
        r1 = jnp.dot((s * e).astype(BF16), vext, preferred_element_type=F32)
        cst = cst_ref[h]
        r2 = jnp.dot(qh, cst.astype(BF16), preferred_element_type=F32)
        e1 = e1_col[:, h:h + 1]
        num = r1[:, :HEAD_DIM] + e1 * r2[:, :HEAD_DIM]
        den = r1[:, HEAD_DIM:] + e1 * r2[:, HEAD_DIM:]
        h_ref[:, hs] = num / jnp.maximum(jnp.abs(den), floor_col[:, h:h + 1])
        kw = (kth.astype(F32) * w_row[h:h + 1, :]).astype(BF16)
        cst_ref[h] = decay[h:h + 1, 0:1] * cst + jnp.dot(kw, vext, preferred_element_type=F32)


def _mlstm(q, kt, v, grow, seq_lens, rev):
    t = q.shape[0]
    n_chunks = t // CHUNK
    starts = []
    pos = 0
    for ln in seq_lens:
        assert ln % CHUNK == 0
        starts.append(pos // CHUNK if not rev else (pos + ln) // CHUNK - 1)
        pos += ln
    reset_chunks = jnp.zeros((n_chunks,), jnp.int32).at[jnp.array(starts)].set(1)
    if rev:
        reset = reset_chunks[::-1]
        cidx = lambda s: n_chunks - 1 - s
    else:
        reset = reset_chunks
        cidx = lambda s: s
    d = 1 if rev else 0
    grid_spec = pltpu.PrefetchScalarGridSpec(
        num_scalar_prefetch=1,
        grid=(n_chunks,),
        in_specs=[pl.BlockSpec((CHUNK, D_MLSTM), lambda s, r: (cidx(s), 0)),
                  pl.BlockSpec((D_MLSTM, CHUNK), lambda s, r: (0, cidx(s))),
                  pl.BlockSpec((CHUNK, D_MLSTM), lambda s, r: (cidx(s), 0)),
                  pl.BlockSpec((GATE_ROWS, CHUNK), lambda s, r: (d, cidx(s)))],
        out_specs=pl.BlockSpec((CHUNK, D_MLSTM), lambda s, r: (cidx(s), 0)),
        scratch_shapes=[pltpu.VMEM((N_HEADS, HEAD_DIM, 2 * HEAD_DIM), F32),
                        pltpu.VMEM((8, CHUNK), F32)],
    )
    return pl.pallas_call(
        functools.partial(_mlstm_kernel, rev=rev),
        grid_spec=grid_spec,
        out_shape=jax.ShapeDtypeStruct((t, D_MLSTM), F32),
        compiler_params=_cparams(),
        name="mlstm_bwd" if rev else "mlstm_fwd",
    )(reset, q, kt, v, grow)


CONV_ROWS = 32


def _mixout_kernel(first_ref, last_ref,
                   x_ref, hf_ref, hb_ref, o_ref, u_ref, up_ref, un_ref, cw_ref, cb_ref, lng_ref, lnb_ref,
                   wout_ref, gffn_ref, wr_ref, br_ref, tri_ref,
                   x1_ref, hn_ref, meta_ref, cnt_ref,
                   win_ref, sh_ref, y_ref, run_ref):
    i = pl.program_id(0)
    tm = x_ref.shape[0]

    @pl.when(i == 0)
    def _():
        run_ref[...] = jnp.zeros_like(run_ref)

    win_ref[0:CONV_HALO, :] = jnp.where(first_ref[i] == 1, 0.0, up_ref[...].astype(F32))
    win_ref[CONV_HALO:CONV_HALO + tm, :] = u_ref[...].astype(F32)
    win_ref[CONV_HALO + tm:, :] = jnp.where(last_ref[i] == 1, 0.0, un_ref[...].astype(F32))
    sh_rows = sh_ref.shape[1]
    sh_ref[0] = win_ref[0:sh_rows, :]
    for s in range(1, 8):
        sh_ref[s] = win_ref[s:s + sh_rows, :]

    def conv_rows(rc, carry):
        r0 = pl.multiple_of(rc * CONV_ROWS, CONV_ROWS)
        acc = jnp.broadcast_to(cb_ref[...], (CONV_ROWS, D_CONV))
        for j in range(CONV_WIDTH):
            off = CONV_HALO - CONV_WIDTH // 2 + j
            acc = acc + sh_ref[off % 8, pl.ds(r0 + off // 8 * 8, CONV_ROWS), :] * cw_ref[j:j + 1, :]
        mu = jnp.mean(acc, axis=-1, keepdims=True)
        xc = acc - mu
        y = xc * lax.rsqrt(jnp.mean(xc * xc, axis=-1, keepdims=True) + EPS) * lng_ref[...] + lnb_ref[...]
        y_ref[pl.ds(r0, CONV_ROWS), D_MLSTM:] = (y * jax.nn.sigmoid(y)).astype(BF16)
        return carry

    lax.fori_loop(0, tm // CONV_ROWS, conv_rows, 0)

    y_ref[:, :D_MLSTM] = (jax.nn.sigmoid(o_ref[...].astype(F32)) * (hf_ref[...] + hb_ref[...])).astype(BF16)

    x1 = x_ref[...] + jnp.dot(y_ref[...], wout_ref[...], preferred_element_type=F32)
    x1_ref[...] = x1
    hn = _rms(x1, gffn_ref[...])
    hn_ref[...] = hn

    logits = jnp.dot(hn.astype(BF16), wr_ref[...], preferred_element_type=F32) + br_ref[...]
    lane = lax.broadcasted_iota(jnp.int32, (tm, ROUTER_LANES), 1)
    neg = -jnp.inf
    gl = jnp.where(lane < N_GROUPS, logits, neg)
    gmax = jnp.max(gl, axis=1, keepdims=True)
    p_top = 1.0 / jnp.sum(jnp.exp(gl - gmax), axis=1, keepdims=True)
    g_idx = jnp.min(jnp.where(gl == gmax, lane, ROUTER_LANES), axis=1, keepdims=True)
    lo = EXPERT_LANE0 + EXPERTS_PER_GROUP * g_idx
    in_grp = (lane >= lo) & (lane < lo + EXPERTS_PER_GROUP)
    el = jnp.where(in_grp, logits, neg)
    ee = jnp.exp(el - jnp.max(el, axis=1, keepdims=True))
    pe = jnp.where(in_grp, ee / jnp.sum(ee, axis=1, keepdims=True), -1.0)
    v1 = jnp.max(pe, axis=1, keepdims=True)
    i1 = jnp.min(jnp.where(pe == v1, lane, ROUTER_LANES), axis=1, keepdims=True)
    pe2 = jnp.where(lane == i1, -1.0, pe)
    v2 = jnp.max(pe2, axis=1, keepdims=True)
    i2 = jnp.min(jnp.where(pe2 == v2, lane, ROUTER_LANES), axis=1, keepdims=True)
    wsum = v1 + v2
    gate1 = p_top * (v1 / wsum)
    gate2 = p_top * (v2 / wsum)

    oh1 = (lane == i1).astype(F32)
    oh2 = (lane == i2).astype(F32)
    c1 = jnp.dot(tri_ref[...], oh1.astype(BF16), preferred_element_type=F32)
    c2 = jnp.dot(tri_ref[...], oh2.astype(BF16), preferred_element_type=F32)
    run = run_ref[...]
    tot1 = jnp.sum(oh1, axis=0, keepdims=True)
    tot2 = jnp.sum(oh2, axis=0, keepdims=True)
    rank1 = jnp.sum(oh1 * (run + c1), axis=1, keepdims=True)
    rank2 = jnp.sum(oh2 * (run + tot1 + c2), axis=1, keepdims=True)
    run = run + tot1 + tot2
    run_ref[...] = run
    cnt_ref[...] = run

    ml = lax.broadcasted_iota(jnp.int32, (tm, META_LANES), 1)
    meta = jnp.where(ml == 0, (i1 - EXPERT_LANE0).astype(F32), 0.0)
    meta = jnp.where(ml == 1, (i2 - EXPERT_LANE0).astype(F32), meta)
    meta = jnp.where(ml == 2, gate1, meta)
    meta = jnp.where(ml == 3, gate2, meta)
    meta = jnp.where(ml == 4, rank1, meta)
    meta = jnp.where(ml == 5, rank2, meta)
    meta_ref[...] = meta


def _mixout(x, hf, hb, o, u, seq_lens, conv_w, conv_b, ln_g, ln_b, w_out, g_ffn, w_rg, b_rg, w_re, b_re):
    t = x.shape[0]
    n_tiles = t // TM_MIX
    hpt = TM_MIX // CONV_HALO
    n_halo = t // CONV_HALO
    firsts, lasts, pos = [], [], 0
    for ln in seq_lens:
        assert ln % TM_MIX == 0
        firsts.append(pos // TM_MIX)
        lasts.append((pos + ln) // TM_MIX - 1)
        pos += ln
    first = jnp.zeros((n_tiles,), jnp.int32).at[jnp.array(firsts)].set(1)
    last = jnp.zeros((n_tiles,), jnp.int32).at[jnp.array(lasts)].set(1)

    cw = jnp.zeros((32, D_CONV), F32).at[:CONV_WIDTH].set(conv_w.reshape(CONV_WIDTH, D_CONV))
    wr = jnp.zeros((D_MODEL, ROUTER_LANES), F32)
    wr = wr.at[:, :N_GROUPS].set(w_rg).at[:, EXPERT_LANE0:EXPERT_LANE0 + N_EXPERTS].set(w_re).astype(BF16)
    br = jnp.zeros((1, ROUTER_LANES), F32)
    br = br.at[0, :N_GROUPS].set(b_rg).at[0, EXPERT_LANE0:EXPERT_LANE0 + N_EXPERTS].set(b_re)
    tri = (lax.broadcasted_iota(jnp.int32, (TM_MIX, TM_MIX), 0)
           > lax.broadcasted_iota(jnp.int32, (TM_MIX, TM_MIX), 1)).astype(BF16)

    tok = lambda i, f, l: (i, 0)
    fixed = lambda i, f, l: (0, 0)
    row = lambda n: pl.BlockSpec((1, n), fixed)
    grid_spec = pltpu.PrefetchScalarGridSpec(
        num_scalar_prefetch=2,
        grid=(n_tiles,),
        in_specs=[pl.BlockSpec((TM_MIX, D_MODEL), tok),
                  pl.BlockSpec((TM_MIX, D_MLSTM), tok), pl.BlockSpec((TM_MIX, D_MLSTM), tok),
                  pl.BlockSpec((TM_MIX, D_MLSTM), tok), pl.BlockSpec((TM_MIX, D_CONV), tok),
                  pl.BlockSpec((CONV_HALO, D_CONV), lambda i, f, l: (jnp.maximum(i * hpt - 1, 0), 0)),
                  pl.BlockSpec((CONV_HALO, D_CONV), lambda i, f, l: (jnp.minimum((i + 1) * hpt, n_halo - 1), 0)),
                  pl.BlockSpec((32, D_CONV), fixed), row(D_CONV), row(D_CONV), row(D_CONV),
                  pl.BlockSpec((D_MODEL, D_MODEL), fixed), row(D_MODEL),
                  pl.BlockSpec((D_MODEL, ROUTER_LANES), fixed), row(ROUTER_LANES),
                  pl.BlockSpec((TM_MIX, TM_MIX), fixed)],
        out_specs=[pl.BlockSpec((TM_MIX, D_MODEL), tok), pl.BlockSpec((TM_MIX, D_MODEL), tok),
                   pl.BlockSpec((TM_MIX, META_LANES), tok), row(ROUTER_LANES)],
        scratch_shapes=[pltpu.VMEM((TM_MIX + 2 * CONV_HALO, D_CONV), F32),
                        pltpu.VMEM((8, TM_MIX + 2 * CONV_HALO - 8, D_CONV), F32),
                        pltpu.VMEM((TM_MIX, D_MODEL), BF16),
                        pltpu.VMEM((1, ROUTER_LANES), F32)],
    )
    return pl.pallas_call(
        _mixout_kernel,
        grid_spec=grid_spec,
        out_shape=[jax.ShapeDtypeStruct((t, D_MODEL), F32), jax.ShapeDtypeStruct((t, D_MODEL), F32),
                   jax.ShapeDtypeStruct((t, META_LANES), F32), jax.ShapeDtypeStruct((1, ROUTER_LANES), F32)],
        compiler_params=_cparams(),
        name="mixout",
    )(first, last, x, hf, hb, o, u, u, u, cw, conv_b.reshape(1, D_CONV), ln_g.reshape(1, D_CONV),
      ln_b.reshape(1, D_CONV), w_out.astype(BF16), g_ffn.reshape(1, D_MODEL), wr, br, tri)


def _row_gather(src_hbm, idx_ref, n, dst_ref, sem):
    def body(r, carry):
        pltpu.make_async_copy(src_hbm.at[pl.ds(idx_ref[0, 0, r], 1)], dst_ref.at[pl.ds(r, 1)], sem).start()
        return carry
    lax.fori_loop(0, n, body, 0)


def _row_gather_wait(src_hbm, n, dst_ref, sem):
    pltpu.make_async_copy(src_hbm.at[pl.ds(0, n)], dst_ref, sem).wait()


def _experts_kernel(te_ref, rows_ref, tok_ref, hn_hbm, wg_ref, wu_ref, wd_ref, ys_ref, xbuf, sem):
    i = pl.program_id(0)

    @pl.when(rows_ref[i] > 0)
    def _():
        _row_gather(hn_hbm, tok_ref, TM_EXP, xbuf, sem)
        _row_gather_wait(hn_hbm, TM_EXP, xbuf, sem)
        x = xbuf[...].astype(BF16)
        hg = jnp.dot(x, wg_ref[...], preferred_element_type=F32)
        hu = jnp.dot(x, wu_ref[...], preferred_element_type=F32)
        hid = (hg * jax.nn.sigmoid(hg) * hu).astype(BF16)
        ys_ref[...] = jnp.dot(hid, wd_ref[...], preferred_element_type=F32)

    @pl.when(rows_ref[i] == 0)
    def _():
        ys_ref[...] = jnp.zeros_like(ys_ref)


def _experts(hn, tile_expert, tile_rows, token_of_slot, w_gate, w_up, w_down):
    n_tiles = tile_expert.shape[0]
    tok3 = token_of_slot.reshape(n_tiles, 1, TM_EXP)
    wsel = lambda i, te, rows: (te[i], 0, 0)
    grid_spec = pltpu.PrefetchScalarGridSpec(
        num_scalar_prefetch=2,
        grid=(n_tiles,),
        in_specs=[pl.BlockSpec((1, 1, TM_EXP), lambda i, te, rows: (i, 0, 0), memory_space=pltpu.SMEM),
                  pl.BlockSpec(memory_space=pl.ANY),
                  pl.BlockSpec((None, D_MODEL, D_EXPERT), wsel),
                  pl.BlockSpec((None, D_MODEL, D_EXPERT), wsel),
                  pl.BlockSpec((None, D_EXPERT, D_MODEL), wsel)],
        out_specs=pl.BlockSpec((TM_EXP, D_MODEL), lambda i, te, rows: (i, 0)),
        scratch_shapes=[pltpu.VMEM((TM_EXP, D_MODEL), F32), pltpu.SemaphoreType.DMA],
    )
    return pl.pallas_call(
        _experts_kernel,
        grid_spec=grid_spec,
        out_shape=jax.ShapeDtypeStruct((n_tiles * TM_EXP, D_MODEL), F32),
        compiler_params=_cparams(),
        name="experts",
    )(tile_expert, tile_rows, tok3, hn, w_gate.astype(BF16), w_up.astype(BF16), w_down.astype(BF16))


def _combine_kernel(pos1_ref, pos2_ref, x1_ref, meta_ref, ys_hbm, gfin_ref, out_ref, y1buf, y2buf, sem):
    _row_gather(ys_hbm, pos1_ref, TM_OUT, y1buf, sem.at[0])
    _row_gather(ys_hbm, pos2_ref, TM_OUT, y2buf, sem.at[1])
    _row_gather_wait(ys_hbm, TM_OUT, y1buf, sem.at[0])
    _row_gather_wait(ys_hbm, TM_OUT, y2buf, sem.at[1])
    meta = meta_ref[...]
    x2 = x1_ref[...] + meta[:, 2:3] * y1buf[...] + meta[:, 3:4] * y2buf[...]
    out_ref[...] = _rms(x2, gfin_ref[...])


def _combine(x1, meta, pos, ys, g_final):
    t = x1.shape[0]
    n_tiles = t // TM_OUT
    pos1 = pos[:, 0].reshape(n_tiles, 1, TM_OUT)
    pos2 = pos[:, 1].reshape(n_tiles, 1, TM_OUT)
    smem = pl.BlockSpec((1, 1, TM_OUT), lambda i: (i, 0, 0), memory_space=pltpu.SMEM)
    return pl.pallas_call(
        _combine_kernel,
        grid=(n_tiles,),
        in_specs=[smem, smem,
                  pl.BlockSpec((TM_OUT, D_MODEL), lambda i: (i, 0)),
                  pl.BlockSpec((TM_OUT, META_LANES), lambda i: (i, 0)),
                  pl.BlockSpec(memory_space=pl.ANY),
                  pl.BlockSpec((1, D_MODEL), lambda i: (0, 0))],
        out_specs=pl.BlockSpec((TM_OUT, D_MODEL), lambda i: (i, 0)),
        out_shape=jax.ShapeDtypeStruct((t, D_MODEL), F32),
        scratch_shapes=[pltpu.VMEM((TM_OUT, D_MODEL), F32), pltpu.VMEM((TM_OUT, D_MODEL), F32),
                        pltpu.SemaphoreType.DMA((2,))],
        compiler_params=_cparams(),
        name="combine",
    )(pos1, pos2, x1, meta, ys, g_final.reshape(1, D_MODEL))


def _slot_layout(meta, counts_row, n_tiles):
    t = meta.shape[0]
    eid = meta[:, 0:2].astype(jnp.int32)
    rank = meta[:, 4:6].astype(jnp.int32)
    counts = counts_row[0, EXPERT_LANE0:EXPERT_LANE0 + N_EXPERTS].astype(jnp.int32)
    padded = (counts + TM_EXP - 1) // TM_EXP * TM_EXP
    ends = jnp.cumsum(padded)
    offs = ends - padded
    pos = offs[eid] + rank
    tile_start = jnp.arange(n_tiles, dtype=jnp.int32) * TM_EXP
    tile_expert = jnp.minimum(jnp.sum(tile_start[:, None] >= ends[None, :], axis=1), N_EXPERTS - 1)
    tile_expert = tile_expert.astype(jnp.int32)
    tile_rows = jnp.clip(counts[tile_expert] - (tile_start - offs[tile_expert]), 0, TM_EXP)
    tile_rows = jnp.where(tile_start < ends[-1], tile_rows, 0).astype(jnp.int32)
    token_ids = jnp.repeat(jnp.arange(t, dtype=jnp.int32), 2)
    token_of_slot = jnp.zeros((n_tiles * TM_EXP,), jnp.int32).at[pos.reshape(-1)].set(token_ids)
    return pos, tile_expert, tile_rows, token_of_slot


def kernel(x_prompt, x_sample, g_mix, w_in, b_gate, conv_w, conv_b, ln_g, ln_b, w_out, g_ffn,
           w_router_group, b_router_group, w_router_expert, b_router_expert, w_gate, w_up, w_down, g_final):
    assert g_mix.shape[0] == 1, "one layer"
    bp, lp, _ = x_prompt.shape
    bs, ls, _ = x_sample.shape
    seq_lens = [lp] * bp + [ls] * bs
    tp = bp * lp
    x = jnp.concatenate([x_prompt.reshape(tp, D_MODEL), x_sample.reshape(bs * ls, D_MODEL)], axis=0)
    t = x.shape[0]

    q, v, o, u, kt, grow = _inproj(x, g_mix[0], w_in[0], b_gate[0])
    hf = _mlstm(q, kt, v, grow, seq_lens, rev=False)
    hb = _mlstm(q, kt, v, grow, seq_lens, rev=True)
    x1, hn, meta, counts = _mixout(x, hf, hb, o, u, seq_lens, conv_w[0], conv_b[0], ln_g[0], ln_b[0],
                                   w_out[0], g_ffn[0], w_router_group[0], b_router_group[0],
                                   w_router_expert[0], b_router_expert[0])
    n_tiles = (2 * t + N_EXPERTS * (TM_EXP - 1)) // TM_EXP + 1
    pos, tile_expert, tile_rows, token_of_slot = _slot_layout(meta, counts, n_tiles)
    ys = _experts(hn, tile_expert, tile_rows, token_of_slot, w_gate[0], w_up[0], w_down[0])
    out = _combine(x1, meta, pos, ys, g_final)
    return out[:tp].reshape(bp, lp, D_MODEL), out[tp:].reshape(bs, ls, D_MODEL)
```

```python
import functools

import jax
import jax.numpy as jnp
from jax import lax
from jax.experimental import pallas as pl
from jax.experimental.pallas import tpu as pltpu
from jax.experimental.pallas import tpu_sc as plsc

F32 = jnp.float32
BF16 = jnp.bfloat16

D_MODEL = 1024
N_HEADS = 4
HEAD_DIM = 128
D_MLSTM = N_HEADS * HEAD_DIM
D_CONV = D_MODEL - D_MLSTM
CONV_WIDTH = 31
CONV_HALO = 16
N_DIR = 2
N_GROUPS = 4
EXPERTS_PER_GROUP = 4
N_EXPERTS = N_GROUPS * EXPERTS_PER_GROUP
D_EXPERT = 512
EPS = 1e-6
K_SCALE = HEAD_DIM ** -0.5

GATE_ROWS = 16
ROUTER_LANES = 128
EXPERT_LANE0 = N_GROUPS
META_LANES = 8

TM_IN = 512
CHUNK = 128
TM_MIX = 512
TM_EXP = 256
TM_OUT = 256
VMEM_LIMIT = 48 * 1024 * 1024

SC_CORES = 2
SC_SUBCORES = 16
SC_WORKERS = SC_CORES * SC_SUBCORES
SC_ROWS = 64


def _cparams(n_axes=1):
    return pltpu.CompilerParams(dimension_semantics=("arbitrary",) * n_axes,
                                vmem_limit_bytes=VMEM_LIMIT)


def _nt_dot(a, b):
    return lax.dot_general(a, b, (((1,), (1,)), ((), ())), preferred_element_type=F32)


def _rms(x, g):
    return x * lax.rsqrt(jnp.mean(x * x, axis=-1, keepdims=True) + EPS) * g


def _inproj_kernel(x_ref, g_ref, wq_ref, wv_ref, wo_ref, wa_ref, wb_ref, wkt_ref, wgt_ref, bg_ref,
                   q_ref, v_ref, o_ref, u_ref, kt_ref, gr_ref):
    xn = _rms(x_ref[...], g_ref[...]).astype(BF16)
    q_ref[...] = jnp.dot(xn, wq_ref[...], preferred_element_type=F32).astype(BF16)
    v_ref[...] = jnp.dot(xn, wv_ref[...], preferred_element_type=F32).astype(BF16)
    o_ref[...] = jnp.dot(xn, wo_ref[...], preferred_element_type=F32).astype(BF16)
    a = jnp.dot(xn, wa_ref[...], preferred_element_type=F32)
    b = jnp.dot(xn, wb_ref[...], preferred_element_type=F32)
    u_ref[...] = (a * jax.nn.sigmoid(b)).astype(BF16)
    kt_ref[...] = (_nt_dot(wkt_ref[...], xn) * K_SCALE).astype(BF16)
    gr_ref[...] = _nt_dot(wgt_ref[...], xn) + bg_ref[...]


def _inproj(x, g_mix, w_in, b_gate):
    t = x.shape[0]
    assert t % TM_IN == 0
    off_k, off_v, off_o, off_g = D_MLSTM, 2 * D_MLSTM, 3 * D_MLSTM, 4 * D_MLSTM
    off_a = off_g + 2 * N_DIR * N_HEADS
    off_b = off_a + D_CONV
    wq = w_in[:, 0:off_k].astype(BF16)
    wkt = w_in[:, off_k:off_v].T.astype(BF16)
    wv = w_in[:, off_v:off_o].astype(BF16)
    wo = w_in[:, off_o:off_g].astype(BF16)
    wa = w_in[:, off_a:off_b].astype(BF16)
    wb = w_in[:, off_b:off_b + D_CONV].astype(BF16)
    wg = w_in[:, off_g:off_a].T.reshape(N_DIR, 2, N_HEADS, D_MODEL)
    wgt = jnp.zeros((N_DIR, 2, GATE_ROWS // 2, D_MODEL), F32).at[:, :, :N_HEADS].set(wg)
    wgt = wgt.reshape(N_DIR * GATE_ROWS, D_MODEL).astype(BF16)
    bg = jnp.zeros((N_DIR, 2, GATE_ROWS // 2), F32).at[:, :, :N_HEADS].set(
        b_gate.reshape(N_DIR, 2, N_HEADS)).reshape(N_DIR * GATE_ROWS, 1)

    tok = lambda i: (i, 0)
    fixed = lambda i: (0, 0)
    wspec = pl.BlockSpec((D_MODEL, D_MLSTM), fixed)
    return pl.pallas_call(
        _inproj_kernel,
        grid=(t // TM_IN,),
        in_specs=[pl.BlockSpec((TM_IN, D_MODEL), tok), pl.BlockSpec((1, D_MODEL), fixed),
                  wspec, wspec, wspec, wspec, wspec,
                  pl.BlockSpec((D_MLSTM, D_MODEL), fixed),
                  pl.BlockSpec((N_DIR * GATE_ROWS, D_MODEL), fixed),
                  pl.BlockSpec((N_DIR * GATE_ROWS, 1), fixed)],
        out_specs=[pl.BlockSpec((TM_IN, D_MLSTM), tok)] * 4 + [
            pl.BlockSpec((D_MLSTM, TM_IN), lambda i: (0, i)),
            pl.BlockSpec((N_DIR * GATE_ROWS, TM_IN), lambda i: (0, i))],
        out_shape=[jax.ShapeDtypeStruct((t, D_MLSTM), BF16)] * 4 + [
            jax.ShapeDtypeStruct((D_MLSTM, t), BF16),
            jax.ShapeDtypeStruct((N_DIR * GATE_ROWS, t), F32)],
        compiler_params=_cparams(),
        name="inproj",
    )(x, g_mix.reshape(1, D_MODEL), wq, wv, wo, wa, wb, wkt, wgt, bg)


def _log_sigmoid(x):
    return jnp.minimum(x, 0.0) - jnp.log1p(jnp.exp(-jnp.abs(x)))


def _mlstm_kernel(reset_ref, q_ref, kt_ref, v_ref, g_ref, h_ref, cst_ref, m_ref, *, rev):
    c = q_ref.shape[0]
    step = pl.program_id(0)

    @pl.when(reset_ref[step] == 1)
    def _():
        cst_ref[...] = jnp.zeros_like(cst_ref)
        m_ref[...] = jnp.zeros_like(m_ref)

    ig = g_ref[0:8, :]
    lf = _log_sigmoid(g_ref[8:16, :])
    lane = lax.broadcasted_iota(jnp.int32, (8, c), 1)

    def scan(x, op, ident):
        k = 1
        while k < c:
            if rev:
                shifted, valid = pltpu.roll(x, c - k, axis=1), lane < c - k
            else:
                shifted, valid = pltpu.roll(x, k, axis=1), lane >= k
            x = op(x, jnp.where(valid, shifted, ident))
            k *= 2
        return x

    bc = scan(lf, jnp.add, 0.0)
    a = ig - bc
    m_old = m_ref[...]
    mx = jnp.maximum(m_old, scan(a, jnp.maximum, -jnp.inf))
    mx_last = jnp.max(mx, axis=1, keepdims=True)
    b_last = jnp.sum(lf, axis=1, keepdims=True)
    w_row = jnp.exp(a - mx_last)
    decay = jnp.exp(m_old - mx_last)
    m_ref[...] = jnp.broadcast_to(b_last + mx_last, (8, c))

    mx_col = mx.T
    e1_col = jnp.exp(m_old - mx).T
    floor_col = jnp.exp(-(mx + bc)).T

    row_i = lax.broadcasted_iota(jnp.int32, (c, c), 0)
    col_i = lax.broadcasted_iota(jnp.int32, (c, c), 1)
    mask = (col_i >= row_i) if rev else (col_i <= row_i)
    ones = jnp.ones((c, HEAD_DIM), BF16)

    for h in range(N_HEADS):
        hs = slice(h * HEAD_DIM, (h + 1) * HEAD_DIM)
        qh = q_ref[:, hs]
        kth = kt_ref[hs, :]
        vext = jnp.concatenate([v_ref[:, hs], ones], axis=1)
        s = jnp.dot(qh, kth, preferred_element_type=F32)
        e = jnp.exp(jnp.where(mask, a[h:h + 1, :] - mx_col[:, h:h + 1], -jnp.inf))
        r1 = jnp.dot((s * e).astype(BF16), vext, preferred_element_type=F32)
        cst = cst_ref[h]
        r2 = jnp.dot(qh, cst.astype(BF16), preferred_element_type=F32)
        e1 = e1_col[:, h:h + 1]
        num = r1[:, :HEAD_DIM] + e1 * r2[:, :HEAD_DIM]
        den = r1[:, HEAD_DIM:] + e1 * r2[:, HEAD_DIM:]
        h_ref[:, hs] = num / jnp.maximum(jnp.abs(den), floor_col[:, h:h + 1])
        kw = (kth.astype(F32) * w_row[h:h + 1, :]).astype(BF16)
        cst_ref[h] = decay[h:h + 1, 0:1] * cst + jnp.dot(kw, vext, preferred_element_type=F32)


def _mlstm(q, kt, v, grow, seq_lens, rev):
    t = q.shape[0]
    n_chunks = t // CHUNK
    starts = []
    pos = 0
    for ln in seq_lens:
        assert ln % CHUNK == 0
        starts.append(pos // CHUNK if not rev else (pos + ln) // CHUNK - 1)
        pos += ln
    reset_chunks = jnp.zeros((n_chunks,), jnp.int32).at[jnp.array(starts)].set(1)
    if rev:
        reset = reset_chunks[::-1]
        cidx = lambda s: n_chunks - 1 - s
    else:
        reset = reset_chunks
        cidx = lambda s: s
    d = 1 if rev else 0
    grid_spec = pltpu.PrefetchScalarGridSpec(
        num_scalar_prefetch=1,
        grid=(n_chunks,),
        in_specs=[pl.BlockSpec((CHUNK, D_MLSTM), lambda s, r: (cidx(s), 0)),
                  pl.BlockSpec((D_MLSTM, CHUNK), lambda s, r: (0, cidx(s))),
                  pl.BlockSpec((CHUNK, D_MLSTM), lambda s, r: (cidx(s), 0)),
                  pl.BlockSpec((GATE_ROWS, CHUNK), lambda s, r: (d, cidx(s)))],
        out_specs=pl.BlockSpec((CHUNK, D_MLSTM), lambda s, r: (cidx(s), 0)),
        scratch_shapes=[pltpu.VMEM((N_HEADS, HEAD_DIM, 2 * HEAD_DIM), F32),
                        pltpu.VMEM((8, CHUNK), F32)],
    )
    return pl.pallas_call(
        functools.partial(_mlstm_kernel, rev=rev),
        grid_spec=grid_spec,
        out_shape=jax.ShapeDtypeStruct((t, D_MLSTM), F32),
        compiler_params=_cparams(),
        name="mlstm_bwd" if rev else "mlstm_fwd",
    )(reset, q, kt, v, grow)


CONV_ROWS = 32


def _mixout_kernel(first_ref, last_ref,
                   x_ref, hf_ref, hb_ref, o_ref, u_ref, up_ref, un_ref, cw_ref, cb_ref, lng_ref, lnb_ref,
                   wout_ref, gffn_ref, wr_ref, br_ref, tri_ref,
                   x1_ref, hn_ref, meta_ref, cnt_ref,
                   win_ref, sh_ref, y_ref, run_ref):
    i = pl.program_id(0)
    tm = x_ref.shape[0]

    @pl.when(i == 0)
    def _():
        run_ref[...] = jnp.zeros_like(run_ref)

    win_ref[0:CONV_HALO, :] = jnp.where(first_ref[i] == 1, 0.0, up_ref[...].astype(F32))
    win_ref[CONV_HALO:CONV_HALO + tm, :] = u_ref[...].astype(F32)
    win_ref[CONV_HALO + tm:, :] = jnp.where(last_ref[i] == 1, 0.0, un_ref[...].astype(F32))
    sh_rows = sh_ref.shape[1]
    sh_ref[0] = win_ref[0:sh_rows, :]
    for s in range(1, 8):
        sh_ref[s] = win_ref[s:s + sh_rows, :]

    def conv_rows(rc, carry):
        r0 = pl.multiple_of(rc * CONV_ROWS, CONV_ROWS)
        acc = jnp.broadcast_to(cb_ref[...], (CONV_ROWS, D_CONV))
        for j in range(CONV_WIDTH):
            off = CONV_HALO - CONV_WIDTH // 2 + j
            acc = acc + sh_ref[off % 8, pl.ds(r0 + off // 8 * 8, CONV_ROWS), :] * cw_ref[j:j + 1, :]
        mu = jnp.mean(acc, axis=-1, keepdims=True)
        xc = acc - mu
        y = xc * lax.rsqrt(jnp.mean(xc * xc, axis=-1, keepdims=True) + EPS) * lng_ref[...] + lnb_ref[...]
        y_ref[pl.ds(r0, CONV_ROWS), D_MLSTM:] = (y * jax.nn.sigmoid(y)).astype(BF16)
        return carry

    lax.fori_loop(0, tm // CONV_ROWS, conv_rows, 0)

    y_ref[:, :D_MLSTM] = (jax.nn.sigmoid(o_ref[...].astype(F32)) * (hf_ref[...] + hb_ref[...])).astype(BF16)

    x1 = x_ref[...] + jnp.dot(y_ref[...], wout_ref[...], preferred_element_type=F32)
    x1_ref[...] = x1
    hn = _rms(x1, gffn_ref[...])
    hn_ref[...] = hn

    logits = jnp.dot(hn.astype(BF16), wr_ref[...], preferred_element_type=F32) + br_ref[...]
    lane = lax.broadcasted_iota(jnp.int32, (tm, ROUTER_LANES), 1)
    neg = -jnp.inf
    gl = jnp.where(lane < N_GROUPS, logits, neg)
    gmax = jnp.max(gl, axis=1, keepdims=True)
    p_top = 1.0 / jnp.sum(jnp.exp(gl - gmax), axis=1, keepdims=True)
    g_idx = jnp.min(jnp.where(gl == gmax, lane, ROUTER_LANES), axis=1, keepdims=True)
    lo = EXPERT_LANE0 + EXPERTS_PER_GROUP * g_idx
    in_grp = (lane >= lo) & (lane < lo + EXPERTS_PER_GROUP)
    el = jnp.where(in_grp, logits, neg)
    ee = jnp.exp(el - jnp.max(el, axis=1, keepdims=True))
    pe = jnp.where(in_grp, ee / jnp.sum(ee, axis=1, keepdims=True), -1.0)
    v1 = jnp.max(pe, axis=1, keepdims=True)
    i1 = jnp.min(jnp.where(pe == v1, lane, ROUTER_LANES), axis=1, keepdims=True)
    pe2 = jnp.where(lane == i1, -1.0, pe)
    v2 = jnp.max(pe2, axis=1, keepdims=True)
    i2 = jnp.min(jnp.where(pe2 == v2, lane, ROUTER_LANES), axis=1, keepdims=True)
    wsum = v1 + v2
    gate1 = p_top * (v1 / wsum)
    gate2 = p_top * (v2 / wsum)

    oh1 = (lane == i1).astype(F32)
    oh2 = (lane == i2).astype(F32)
    c1 = jnp.dot(tri_ref[...], oh1.astype(BF16), preferred_element_type=F32)
    c2 = jnp.dot(tri_ref[...], oh2.astype(BF16), preferred_element_type=F32)
    run = run_ref[...]
    tot1 = jnp.sum(oh1, axis=0, keepdims=True)
    tot2 = jnp.sum(oh2, axis=0, keepdims=True)
    rank1 = jnp.sum(oh1 * (run + c1), axis=1, keepdims=True)
    rank2 = jnp.sum(oh2 * (run + tot1 + c2), axis=1, keepdims=True)
    run = run + tot1 + tot2
    run_ref[...] = run
    cnt_ref[...] = run

    ml = lax.broadcasted_iota(jnp.int32, (tm, META_LANES), 1)
    meta = jnp.where(ml == 0, (i1 - EXPERT_LANE0).astype(F32), 0.0)
    meta = jnp.where(ml == 1, (i2 - EXPERT_LANE0).astype(F32), meta)
    meta = jnp.where(ml == 2, gate1, meta)
    meta = jnp.where(ml == 3, gate2, meta)
    meta = jnp.where(ml == 4, rank1, meta)
    meta = jnp.where(ml == 5, rank2, meta)
    meta_ref[...] = meta


def _mixout(x, hf, hb, o, u, seq_lens, conv_w, conv_b, ln_g, ln_b, w_out, g_ffn, w_rg, b_rg, w_re, b_re):
    t = x.shape[0]
    n_tiles = t // TM_MIX
    hpt = TM_MIX // CONV_HALO
    n_halo = t // CONV_HALO
    firsts, lasts, pos = [], [], 0
    for ln in seq_lens:
        assert ln % TM_MIX == 0
        firsts.append(pos // TM_MIX)
        lasts.append((pos + ln) // TM_MIX - 1)
        pos += ln
    first = jnp.zeros((n_tiles,), jnp.int32).at[jnp.array(firsts)].set(1)
    last = jnp.zeros((n_tiles,), jnp.int32).at[jnp.array(lasts)].set(1)

    cw = jnp.zeros((32, D_CONV), F32).at[:CONV_WIDTH].set(conv_w.reshape(CONV_WIDTH, D_CONV))
    wr = jnp.zeros((D_MODEL, ROUTER_LANES), F32)
    wr = wr.at[:, :N_GROUPS].set(w_rg).at[:, EXPERT_LANE0:EXPERT_LANE0 + N_EXPERTS].set(w_re).astype(BF16)
    br = jnp.zeros((1, ROUTER_LANES), F32)
    br = br.at[0, :N_GROUPS].set(b_rg).at[0, EXPERT_LANE0:EXPERT_LANE0 + N_EXPERTS].set(b_re)
    tri = (lax.broadcasted_iota(jnp.int32, (TM_MIX, TM_MIX), 0)
           > lax.broadcasted_iota(jnp.int32, (TM_MIX, TM_MIX), 1)).astype(BF16)

    tok = lambda i, f, l: (i, 0)
    fixed = lambda i, f, l: (0, 0)
    row = lambda n: pl.BlockSpec((1, n), fixed)
    grid_spec = pltpu.PrefetchScalarGridSpec(
        num_scalar_prefetch=2,
        grid=(n_tiles,),
        in_specs=[pl.BlockSpec((TM_MIX, D_MODEL), tok),
                  pl.BlockSpec((TM_MIX, D_MLSTM), tok), pl.BlockSpec((TM_MIX, D_MLSTM), tok),
                  pl.BlockSpec((TM_MIX, D_MLSTM), tok), pl.BlockSpec((TM_MIX, D_CONV), tok),
                  pl.BlockSpec((CONV_HALO, D_CONV), lambda i, f, l: (jnp.maximum(i * hpt - 1, 0), 0)),
                  pl.BlockSpec((CONV_HALO, D_CONV), lambda i, f, l: (jnp.minimum((i + 1) * hpt, n_halo - 1), 0)),
                  pl.BlockSpec((32, D_CONV), fixed), row(D_CONV), row(D_CONV), row(D_CONV),
                  pl.BlockSpec((D_MODEL, D_MODEL), fixed), row(D_MODEL),
                  pl.BlockSpec((D_MODEL, ROUTER_LANES), fixed), row(ROUTER_LANES),
                  pl.BlockSpec((TM_MIX, TM_MIX), fixed)],
        out_specs=[pl.BlockSpec((TM_MIX, D_MODEL), tok), pl.BlockSpec((TM_MIX, D_MODEL), tok),
                   pl.BlockSpec((TM_MIX, META_LANES), tok), row(ROUTER_LANES)],
        scratch_shapes=[pltpu.VMEM((TM_MIX + 2 * CONV_HALO, D_CONV), F32),
                        pltpu.VMEM((8, TM_MIX + 2 * CONV_HALO - 8, D_CONV), F32),
                        pltpu.VMEM((TM_MIX, D_MODEL), BF16),
                        pltpu.VMEM((1, ROUTER_LANES), F32)],
    )
    return pl.pallas_call(
        _mixout_kernel,
        grid_spec=grid_spec,
        out_shape=[jax.ShapeDtypeStruct((t, D_MODEL), F32), jax.ShapeDtypeStruct((t, D_MODEL), F32),
                   jax.ShapeDtypeStruct((t, META_LANES), F32), jax.ShapeDtypeStruct((1, ROUTER_LANES), F32)],
        compiler_params=_cparams(),
        name="mixout",
    )(first, last, x, hf, hb, o, u, u, u, cw, conv_b.reshape(1, D_CONV), ln_g.reshape(1, D_CONV),
      ln_b.reshape(1, D_CONV), w_out.astype(BF16), g_ffn.reshape(1, D_MODEL), wr, br, tri)


def _sc_mesh():
    return plsc.VectorSubcoreMesh(core_axis_name="c", subcore_axis_name="s")


def _sc_worker_base(per_worker):
    wid = lax.axis_index("s") * SC_CORES + lax.axis_index("c")
    return wid * per_worker


def _dispatch(hn, pos1, pos2, n_slots):
    t, d = hn.shape
    per_worker = t // SC_WORKERS
    assert per_worker * SC_WORKERS == t and per_worker % SC_ROWS == 0

    def body(h_hbm, p1_hbm, p2_hbm, out_hbm, i1_v, i2_v, rows_v, sem1, sem2):
        base0 = _sc_worker_base(per_worker)

        @pl.loop(0, per_worker // SC_ROWS)
        def _(c):
            base = pl.multiple_of(base0 + c * SC_ROWS, SC_ROWS)
            pltpu.sync_copy(p1_hbm.at[pl.ds(base, SC_ROWS)], i1_v)
            pltpu.sync_copy(p2_hbm.at[pl.ds(base, SC_ROWS)], i2_v)
            pltpu.sync_copy(h_hbm.at[pl.ds(base, SC_ROWS)], rows_v)
            c1 = pltpu.async_copy(rows_v, out_hbm.at[i1_v], sem1)
            c2 = pltpu.async_copy(rows_v, out_hbm.at[i2_v], sem2)
            c1.wait()
            c2.wait()

    return pl.kernel(
        body,
        out_type=jax.ShapeDtypeStruct((n_slots, d), hn.dtype),
        mesh=_sc_mesh(),
        scratch_types=[pltpu.VMEM((SC_ROWS,), jnp.int32), pltpu.VMEM((SC_ROWS,), jnp.int32),
                       pltpu.VMEM((SC_ROWS, d), hn.dtype), pltpu.SemaphoreType.DMA, pltpu.SemaphoreType.DMA],
        name="dispatch",
    )(hn, pos1, pos2)


def _collect(ys, pos1, pos2):
    t = pos1.shape[0]
    d = ys.shape[1]
    per_worker = t // SC_WORKERS
    assert per_worker * SC_WORKERS == t and per_worker % SC_ROWS == 0

    def body(ys_hbm, p1_hbm, p2_hbm, y1_hbm, y2_hbm, i_v, rows_v, sem):
        base0 = _sc_worker_base(per_worker)

        @pl.loop(0, per_worker // SC_ROWS)
        def _(c):
            base = pl.multiple_of(base0 + c * SC_ROWS, SC_ROWS)
            for p_hbm, y_hbm in ((p1_hbm, y1_hbm), (p2_hbm, y2_hbm)):
                pltpu.sync_copy(p_hbm.at[pl.ds(base, SC_ROWS)], i_v)
                pltpu.async_copy(ys_hbm.at[i_v], rows_v, sem).wait()
                pltpu.sync_copy(rows_v, y_hbm.at[pl.ds(base, SC_ROWS)])

    out = jax.ShapeDtypeStruct((t, d), ys.dtype)
    return pl.kernel(
        body,
        out_type=(out, out),
        mesh=_sc_mesh(),
        scratch_types=[pltpu.VMEM((SC_ROWS,), jnp.int32), pltpu.VMEM((SC_ROWS, d), ys.dtype),
                       pltpu.SemaphoreType.DMA],
        name="collect",
    )(ys, pos1, pos2)


def _experts_kernel(te_ref, rows_ref, hs_ref, wg_ref, wu_ref, wd_ref, ys_ref):
    i = pl.program_id(0)

    @pl.when(rows_ref[i] > 0)
    def _():
        x = hs_ref[...].astype(BF16)
        hg = jnp.dot(x, wg_ref[...], preferred_element_type=F32)
        hu = jnp.dot(x, wu_ref[...], preferred_element_type=F32)
        hid = (hg * jax.nn.sigmoid(hg) * hu).astype(BF16)
        ys_ref[...] = jnp.dot(hid, wd_ref[...], preferred_element_type=F32)

    @pl.when(rows_ref[i] == 0)
    def _():
        ys_ref[...] = jnp.zeros_like(ys_ref)


def _experts(hs, tile_expert, tile_rows, w_gate, w_up, w_down):
    n_tiles = tile_expert.shape[0]
    wsel = lambda i, te, rows: (te[i], 0, 0)
    slot = lambda i, te, rows: (i, 0)
    grid_spec = pltpu.PrefetchScalarGridSpec(
        num_scalar_prefetch=2,
        grid=(n_tiles,),
        in_specs=[pl.BlockSpec((TM_EXP, D_MODEL), slot),
                  pl.BlockSpec((None, D_MODEL, D_EXPERT), wsel),
                  pl.BlockSpec((None, D_MODEL, D_EXPERT), wsel),
                  pl.BlockSpec((None, D_EXPERT, D_MODEL), wsel)],
        out_specs=pl.BlockSpec((TM_EXP, D_MODEL), slot),
    )
    return pl.pallas_call(
        _experts_kernel,
        grid_spec=grid_spec,
        out_shape=jax.ShapeDtypeStruct((n_tiles * TM_EXP, D_MODEL), F32),
        compiler_params=_cparams(),
        name="experts",
    )(tile_expert, tile_rows, hs, w_gate.astype(BF16), w_up.astype(BF16), w_down.astype(BF16))


def _combine_kernel(x1_ref, meta_ref, y1_ref, y2_ref, gfin_ref, out_ref):
    meta = meta_ref[...]
    x2 = x1_ref[...] + meta[:, 2:3] * y1_ref[...] + meta[:, 3:4] * y2_ref[...]
    out_ref[...] = _rms(x2, gfin_ref[...])


def _combine(x1, meta, y1, y2, g_final):
    t = x1.shape[0]
    tok = pl.BlockSpec((TM_OUT, D_MODEL), lambda i: (i, 0))
    return pl.pallas_call(
        _combine_kernel,
        grid=(t // TM_OUT,),
        in_specs=[tok, pl.BlockSpec((TM_OUT, META_LANES), lambda i: (i, 0)), tok, tok,
                  pl.BlockSpec((1, D_MODEL), lambda i: (0, 0))],
        out_specs=tok,
        out_shape=jax.ShapeDtypeStruct((t, D_MODEL), F32),
        compiler_params=_cparams(),
        name="combine",
    )(x1, meta, y1, y2, g_final.reshape(1, D_MODEL))


def _slot_layout(meta, counts_row, n_tiles):
    eid = meta[:, 0:2].astype(jnp.int32)
    rank = meta[:, 4:6].astype(jnp.int32)
    counts = counts_row[0, EXPERT_LANE0:EXPERT_LANE0 + N_EXPERTS].astype(jnp.int32)
    padded = (counts + TM_EXP - 1) // TM_EXP * TM_EXP
    ends = jnp.cumsum(padded)
    offs = ends - padded
    pos = offs[eid] + rank
    tile_start = jnp.arange(n_tiles, dtype=jnp.int32) * TM_EXP
    tile_expert = jnp.minimum(jnp.sum(tile_start[:, None] >= ends[None, :], axis=1), N_EXPERTS - 1)
    tile_expert = tile_expert.astype(jnp.int32)
    tile_rows = jnp.clip(counts[tile_expert] - (tile_start - offs[tile_expert]), 0, TM_EXP)
    tile_rows = jnp.where(tile_start < ends[-1], tile_rows, 0).astype(jnp.int32)
    return pos[:, 0], pos[:, 1], tile_expert, tile_rows


def kernel(x_prompt, x_sample, g_mix, w_in, b_gate, conv_w, conv_b, ln_g, ln_b, w_out, g_ffn,
           w_router_group, b_router_group, w_router_expert, b_router_expert, w_gate, w_up, w_down, g_final):
    assert g_mix.shape[0] == 1, "one layer"
    bp, lp, _ = x_prompt.shape
    bs, ls, _ = x_sample.shape
    seq_lens = [lp] * bp + [ls] * bs
    tp = bp * lp
    x = jnp.concatenate([x_prompt.reshape(tp, D_MODEL), x_sample.reshape(bs * ls, D_MODEL)], axis=0)
    t = x.shape[0]

    q, v, o, u, kt, grow = _inproj(x, g_mix[0], w_in[0], b_gate[0])
    hf = _mlstm(q, kt, v, grow, seq_lens, rev=False)
    hb = _mlstm(q, kt, v, grow, seq_lens, rev=True)
    x1, hn, meta, counts = _mixout(x, hf, hb, o, u, seq_lens, conv_w[0], conv_b[0], ln_g[0], ln_b[0],
                                   w_out[0], g_ffn[0], w_router_group[0], b_router_group[0],
                                   w_router_expert[0], b_router_expert[0])
    n_tiles = (2 * t + N_EXPERTS * (TM_EXP - 1)) // TM_EXP + 1
    pos1, pos2, tile_expert, tile_rows = _slot_layout(meta, counts, n_tiles)
    hs = _dispatch(hn, pos1, pos2, n_tiles * TM_EXP)
    ys = _experts(hs, tile_expert, tile_rows, w_gate[0], w_up[0], w_down[0])
    y1, y2 = _collect(ys, pos1, pos2)
    out = _combine(x1, meta, y1, y2, g_final)
    return out[:tp].reshape(bp, lp, D_MODEL), out[tp:].reshape(bs, ls, D_MODEL)
```

```python
import functools

import jax
import jax.numpy as jnp
from jax import lax
from jax.experimental import pallas as pl
from jax.experimental.pallas import tpu as pltpu
from jax.experimental.pallas import tpu_sc as plsc

F32 = jnp.float32
BF16 = jnp.bfloat16

D_MODEL = 1024
N_HEADS = 4
HEAD_DIM = 128
D_MLSTM = N_HEADS * HEAD_DIM
D_CONV = D_MODEL - D_MLSTM
CONV_WIDTH = 31
CONV_HALO = 16
N_DIR = 2
N_GROUPS = 4
EXPERTS_PER_GROUP = 4
N_EXPERTS = N_GROUPS * EXPERTS_PER_GROUP
D_EXPERT = 512
EPS = 1e-6
K_SCALE = HEAD_DIM ** -0.5

GATE_ROWS = 16
QROWS = 24
ROUTER_LANES = 128
EXPERT_LANE0 = N_GROUPS
META_LANES = 8

TM_IN = 512
CHUNK = 128
GP_CHUNKS = 16
TM_MIX = 512
CONV_ROWS = 128
TM_EXP = 256
TM_OUT = 256
VMEM_LIMIT = 48 * 1024 * 1024

SC_CORES = 2
SC_SUBCORES = 16
SC_WORKERS = SC_CORES * SC_SUBCORES
SC_ROWS = 64


def _cparams(n_axes=1):
    return pltpu.CompilerParams(dimension_semantics=("arbitrary",) * n_axes,
                                vmem_limit_bytes=VMEM_LIMIT)


def _nt_dot(a, b):
    return lax.dot_general(a, b, (((1,), (1,)), ((), ())), preferred_element_type=F32)


def _rms(x, g):
    return x * lax.rsqrt(jnp.mean(x * x, axis=-1, keepdims=True) + EPS) * g


def _two_batch_specs(block, n_first, n_second):
    first = pl.BlockSpec(block, lambda i, *_: (jnp.minimum(i, n_first - 1), 0))
    second = pl.BlockSpec(block, lambda i, *_: (jnp.maximum(i - n_first, 0), 0))
    return first, second


def _inproj_kernel(xp_ref, xs_ref, g_ref, wq_ref, wv_ref, wo_ref, wa_ref, wb_ref, wkt_ref, wgt_ref, bg_ref,
                   q_ref, v_ref, o_ref, u_ref, kt_ref, gr_ref, *, n_first):
    x = jnp.where(pl.program_id(0) < n_first, xp_ref[...], xs_ref[...])
    xn = _rms(x, g_ref[...]).astype(BF16)
    q_ref[...] = jnp.dot(xn, wq_ref[...], preferred_element_type=F32).astype(BF16)
    v_ref[...] = jnp.dot(xn, wv_ref[...], preferred_element_type=F32).astype(BF16)
    o_ref[...] = jnp.dot(xn, wo_ref[...], preferred_element_type=F32).astype(BF16)
    a = jnp.dot(xn, wa_ref[...], preferred_element_type=F32)
    b = jnp.dot(xn, wb_ref[...], preferred_element_type=F32)
    u_ref[...] = (a * jax.nn.sigmoid(b)).astype(BF16)
    kt_ref[...] = (_nt_dot(wkt_ref[...], xn) * K_SCALE).astype(BF16)
    gr = _nt_dot(wgt_ref[...], xn) + bg_ref[...]
    for c in range(gr_ref.shape[0]):
        gr_ref[c] = gr[:, c * CHUNK:(c + 1) * CHUNK]


def _inproj(xp, xs, g_mix, w_in, b_gate):
    tp, ts = xp.shape[0], xs.shape[0]
    t = tp + ts
    assert tp % TM_IN == 0 and ts % TM_IN == 0
    off_k, off_v, off_o, off_g = D_MLSTM, 2 * D_MLSTM, 3 * D_MLSTM, 4 * D_MLSTM
    off_a = off_g + 2 * N_DIR * N_HEADS
    off_b = off_a + D_CONV
    wq = w_in[:, 0:off_k].astype(BF16)
    wkt = w_in[:, off_k:off_v].T.astype(BF16)
    wv = w_in[:, off_v:off_o].astype(BF16)
    wo = w_in[:, off_o:off_g].astype(BF16)
    wa = w_in[:, off_a:off_b].astype(BF16)
    wb = w_in[:, off_b:off_b + D_CONV].astype(BF16)
    wg = w_in[:, off_g:off_a].T.reshape(N_DIR, 2, N_HEADS, D_MODEL)
    wgt = jnp.zeros((N_DIR, 2, GATE_ROWS // 2, D_MODEL), F32).at[:, :, :N_HEADS].set(wg)
    wgt = wgt.reshape(N_DIR * GATE_ROWS, D_MODEL).astype(BF16)
    bg = jnp.zeros((N_DIR, 2, GATE_ROWS // 2), F32).at[:, :, :N_HEADS].set(
        b_gate.reshape(N_DIR, 2, N_HEADS)).reshape(N_DIR * GATE_ROWS, 1)

    tok = lambda i: (i, 0)
    fixed = lambda i: (0, 0)
    wspec = pl.BlockSpec((D_MODEL, D_MLSTM), fixed)
    xp_spec, xs_spec = _two_batch_specs((TM_IN, D_MODEL), tp // TM_IN, ts // TM_IN)
    cpt = TM_IN // CHUNK
    return pl.pallas_call(
        functools.partial(_inproj_kernel, n_first=tp // TM_IN),
        grid=(t // TM_IN,),
        in_specs=[xp_spec, xs_spec, pl.BlockSpec((1, D_MODEL), fixed),
                  wspec, wspec, wspec, wspec, wspec,
                  pl.BlockSpec((D_MLSTM, D_MODEL), fixed),
                  pl.BlockSpec((N_DIR * GATE_ROWS, D_MODEL), fixed),
                  pl.BlockSpec((N_DIR * GATE_ROWS, 1), fixed)],
        out_specs=[pl.BlockSpec((TM_IN, D_MLSTM), tok)] * 4 + [
            pl.BlockSpec((D_MLSTM, TM_IN), lambda i: (0, i)),
            pl.BlockSpec((cpt, N_DIR * GATE_ROWS, CHUNK), lambda i: (i, 0, 0))],
        out_shape=[jax.ShapeDtypeStruct((t, D_MLSTM), BF16)] * 4 + [
            jax.ShapeDtypeStruct((D_MLSTM, t), BF16),
            jax.ShapeDtypeStruct((t // CHUNK, N_DIR * GATE_ROWS, CHUNK), F32)],
        compiler_params=_cparams(),
        name="inproj",
    )(xp, xs, g_mix.reshape(1, D_MODEL), wq, wv, wo, wa, wb, wkt, wgt, bg)


def _log_sigmoid(x):
    return jnp.minimum(x, 0.0) - jnp.log1p(jnp.exp(-jnp.abs(x)))


def _gateprep_kernel(reset_ref, g_ref, rowq_ref, colq_ref, m_ref, *, rev):
    n, _, c = g_ref.shape
    step = pl.program_id(0)
    blk = pl.num_programs(0) - 1 - step if rev else step

    @pl.when(step == 0)
    def _():
        m_ref[...] = jnp.zeros_like(m_ref)

    ig = g_ref[:, 0:8, :]
    lf = _log_sigmoid(g_ref[:, 8:16, :])
    lane = lax.broadcasted_iota(jnp.int32, (n, 8, c), 2)

    def scan(x, op, ident):
        k = 1
        while k < c:
            if rev:
                shifted, valid = pltpu.roll(x, c - k, axis=2), lane < c - k
            else:
                shifted, valid = pltpu.roll(x, k, axis=2), lane >= k
            x = op(x, jnp.where(valid, shifted, ident))
            k *= 2
        return x

    bc = scan(lf, jnp.add, 0.0)
    a = ig - bc
    cm = scan(a, jnp.maximum, -jnp.inf)
    b_tot = jnp.sum(lf, axis=2, keepdims=True)
    a_max = jnp.max(a, axis=2, keepdims=True)

    m = m_ref[...]
    m_in = [None] * n
    for j in (range(n - 1, -1, -1) if rev else range(n)):
        m = jnp.where(reset_ref[blk * n + j] == 1, 0.0, m)
        m_in[j] = m
        m = b_tot[j] + jnp.maximum(m, a_max[j])
    m_ref[...] = m
    m_old = jnp.stack(m_in)

    mx = jnp.maximum(m_old, cm)
    mx_last = jnp.maximum(m_old, a_max)
    rowq_ref[:, 0:8, :] = a
    rowq_ref[:, 8:16, :] = jnp.exp(a - mx_last)
    rowq_ref[:, 16:24, :] = jnp.exp(m_old - mx_last)
    e1 = jnp.exp(m_old - mx)
    fl = jnp.exp(-(mx + bc))
    for j in range(n):
        colq_ref[j] = jnp.concatenate([mx[j], e1[j], fl[j]], axis=0).T


def _gateprep(gr, reset, rev):
    n_chunks = gr.shape[0]
    assert n_chunks % GP_CHUNKS == 0
    nb = n_chunks // GP_CHUNKS
    d = 1 if rev else 0
    bidx = (lambda s: nb - 1 - s) if rev else (lambda s: s)
    grid_spec = pltpu.PrefetchScalarGridSpec(
        num_scalar_prefetch=1,
        grid=(nb,),
        in_specs=[pl.BlockSpec((GP_CHUNKS, GATE_ROWS, CHUNK), lambda s, r: (bidx(s), d, 0))],
        out_specs=[pl.BlockSpec((GP_CHUNKS, QROWS, CHUNK), lambda s, r: (bidx(s), 0, 0)),
                   pl.BlockSpec((GP_CHUNKS, CHUNK, QROWS), lambda s, r: (bidx(s), 0, 0))],
        scratch_shapes=[pltpu.VMEM((8, CHUNK), F32)],
    )
    return pl.pallas_call(
        functools.partial(_gateprep_kernel, rev=rev),
        grid_spec=grid_spec,
        out_shape=[jax.ShapeDtypeStruct((n_chunks, QROWS, CHUNK), F32),
                   jax.ShapeDtypeStruct((n_chunks, CHUNK, QROWS), F32)],
        compiler_params=_cparams(),
        name="gateprep_bwd" if rev else "gateprep_fwd",
    )(reset, gr)


def _mlstm_kernel(rf_ref, rb_ref,
                  qf_ref, ktf_ref, vf_ref, rowf_ref, colf_ref,
                  qb_ref, ktb_ref, vb_ref, rowb_ref, colb_ref,
                  hf_ref, hb_ref, cst_ref):
    c = qf_ref.shape[0]
    step = pl.program_id(0)
    last = pl.num_programs(0) - 1
    row_i = lax.broadcasted_iota(jnp.int32, (c, c), 0)
    col_i = lax.broadcasted_iota(jnp.int32, (c, c), 1)
    ones = jnp.ones((c, HEAD_DIM), BF16)

    dirs = ((rf_ref[step], col_i <= row_i, qf_ref, ktf_ref, vf_ref, rowf_ref, colf_ref, hf_ref),
            (rb_ref[last - step], col_i >= row_i, qb_ref, ktb_ref, vb_ref, rowb_ref, colb_ref, hb_ref))
    for d, (reset, mask, q_ref, kt_ref, v_ref, row_ref, col_ref, h_ref) in enumerate(dirs):
        @pl.when(reset == 1)
        def _():
            cst_ref[d] = jnp.zeros(cst_ref.shape[1:], F32)

        rowq = row_ref[...]
        colq = col_ref[...]
        for h in range(N_HEADS):
            hs = slice(h * HEAD_DIM, (h + 1) * HEAD_DIM)
            qh = q_ref[:, hs]
            kth = kt_ref[hs, :]
            vext = jnp.concatenate([v_ref[:, hs], ones], axis=1)
            s = jnp.dot(qh, kth, preferred_element_type=F32)
            e = jnp.exp(jnp.where(mask, rowq[h:h + 1, :] - colq[:, h:h + 1], -jnp.inf))
            r1 = jnp.dot((s * e).astype(BF16), vext, preferred_element_type=F32)
            cst = cst_ref[d, h]
            r2 = jnp.dot(qh, cst.astype(BF16), preferred_element_type=F32)
            e1 = colq[:, 8 + h:9 + h]
            num = r1[:, :HEAD_DIM] + e1 * r2[:, :HEAD_DIM]
            den = r1[:, HEAD_DIM:] + e1 * r2[:, HEAD_DIM:]
            h_ref[:, hs] = num / jnp.maximum(jnp.abs(den), colq[:, 16 + h:17 + h])
            kw = (kth.astype(F32) * rowq[8 + h:9 + h, :]).astype(BF16)
            cst_ref[d, h] = rowq[16 + h:17 + h, 0:1] * cst + jnp.dot(kw, vext, preferred_element_type=F32)


def _mlstm(q, kt, v, gr, seq_lens):
    t = q.shape[0]
    n = t // CHUNK
    starts, ends, pos = [], [], 0
    for ln in seq_lens:
        assert ln % CHUNK == 0
        starts.append(pos // CHUNK)
        ends.append((pos + ln) // CHUNK - 1)
        pos += ln
    reset_f = jnp.zeros((n,), jnp.int32).at[jnp.array(starts)].set(1)
    reset_b = jnp.zeros((n,), jnp.int32).at[jnp.array(ends)].set(1)
    rowf, colf = _gateprep(gr, reset_f, rev=False)
    rowb, colb = _gateprep(gr, reset_b, rev=True)

    def specs(cidx):
        return [pl.BlockSpec((CHUNK, D_MLSTM), lambda s, rf, rb: (cidx(s), 0)),
                pl.BlockSpec((D_MLSTM, CHUNK), lambda s, rf, rb: (0, cidx(s))),
                pl.BlockSpec((CHUNK, D_MLSTM), lambda s, rf, rb: (cidx(s), 0)),
                pl.BlockSpec((None, QROWS, CHUNK), lambda s, rf, rb: (cidx(s), 0, 0)),
                pl.BlockSpec((None, CHUNK, QROWS), lambda s, rf, rb: (cidx(s), 0, 0))]

    fwd = lambda s: s
    bwd = lambda s: n - 1 - s
    grid_spec = pltpu.PrefetchScalarGridSpec(
        num_scalar_prefetch=2,
        grid=(n,),
        in_specs=specs(fwd) + specs(bwd),
        out_specs=[pl.BlockSpec((CHUNK, D_MLSTM), lambda s, rf, rb: (fwd(s), 0)),
                   pl.BlockSpec((CHUNK, D_MLSTM), lambda s, rf, rb: (bwd(s), 0))],
        scratch_shapes=[pltpu.VMEM((N_DIR, N_HEADS, HEAD_DIM, 2 * HEAD_DIM), F32)],
    )
    return pl.pallas_call(
        _mlstm_kernel,
        grid_spec=grid_spec,
        out_shape=[jax.ShapeDtypeStruct((t, D_MLSTM), F32)] * 2,
        compiler_params=_cparams(),
        name="mlstm",
    )(reset_f, reset_b, q, kt, v, rowf, colf, q, kt, v, rowb, colb)


def _mixout_kernel(first_ref, last_ref,
                   xp_ref, xs_ref, hf_ref, hb_ref, o_ref, u_ref, up_ref, un_ref, cw_ref, cb_ref, lng_ref, lnb_ref,
                   wout_ref, gffn_ref, wr_ref, br_ref, tri_ref,
                   x1_ref, hn_ref, meta_ref, cnt_ref,
                   win_ref, cv_ref, y_ref, run_ref, *, n_first):
    i = pl.program_id(0)
    tm = hf_ref.shape[0]

    @pl.when(i == 0)
    def _():
        run_ref[...] = jnp.zeros_like(run_ref)

    win_ref[0:CONV_HALO, :] = jnp.where(first_ref[i] == 1, 0.0, up_ref[...].astype(F32))
    win_ref[CONV_HALO:CONV_HALO + tm, :] = u_ref[...].astype(F32)
    win_ref[CONV_HALO + tm:, :] = jnp.where(last_ref[i] == 1, 0.0, un_ref[...].astype(F32))

    off0 = CONV_HALO - CONV_WIDTH // 2

    def conv_rows(rc, carry):
        r0 = pl.multiple_of(rc * CONV_ROWS, CONV_ROWS)
        for lt in range(D_CONV // 128):
            ls = slice(lt * 128, (lt + 1) * 128)
            acc = jnp.broadcast_to(cb_ref[:, ls], (CONV_ROWS, 128))
            for s in range(8):
                part = None
                for j in range(CONV_WIDTH):
                    if (off0 + j) % 8 != s:
                        continue
                    base = (off0 + j) // 8 * 8
                    term = win_ref[pl.ds(r0 + base, CONV_ROWS + 8), ls] * cw_ref[j:j + 1, ls]
                    part = term if part is None else part + term
                acc = acc + part[s:s + CONV_ROWS, :]
            cv_ref[pl.ds(r0, CONV_ROWS), ls] = acc
        return carry

    lax.fori_loop(0, tm // CONV_ROWS, conv_rows, 0)
    cv = cv_ref[...]
    xc = cv - jnp.mean(cv, axis=-1, keepdims=True)
    yc = xc * lax.rsqrt(jnp.mean(xc * xc, axis=-1, keepdims=True) + EPS) * lng_ref[...] + lnb_ref[...]
    y_ref[:, D_MLSTM:] = (yc * jax.nn.sigmoid(yc)).astype(BF16)

    y_ref[:, :D_MLSTM] = (jax.nn.sigmoid(o_ref[...].astype(F32)) * (hf_ref[...] + hb_ref[...])).astype(BF16)

    x = jnp.where(i < n_first, xp_ref[...], xs_ref[...])
    x1 = x + jnp.dot(y_ref[...], wout_ref[...], preferred_element_type=F32)
    x1_ref[...] = x1
    hn = _rms(x1, gffn_ref[...])
    hn_ref[...] = hn

    logits = jnp.dot(hn.astype(BF16), wr_ref[...], preferred_element_type=F32) + br_ref[...]
    lane = lax.broadcasted_iota(jnp.int32, (tm, ROUTER_LANES), 1).astype(F32)
    neg = -jnp.inf
    no_lane = float(ROUTER_LANES)
    gl = jnp.where(lane < N_GROUPS, logits, neg)
    gmax = jnp.max(gl, axis=1, keepdims=True)
    p_top = 1.0 / jnp.sum(jnp.exp(gl - gmax), axis=1, keepdims=True)
    g_idx = jnp.min(jnp.where(gl == gmax, lane, no_lane), axis=1, keepdims=True)
    lo = EXPERT_LANE0 + EXPERTS_PER_GROUP * g_idx
    in_grp = (lane >= lo) & (lane < lo + EXPERTS_PER_GROUP)
    el = jnp.where(in_grp, logits, neg)
    ee = jnp.exp(el - jnp.max(el, axis=1, keepdims=True))
    pe = jnp.where(in_grp, ee / jnp.sum(ee, axis=1, keepdims=True), -1.0)
    v1 = jnp.max(pe, axis=1, keepdims=True)
    i1 = jnp.min(jnp.where(pe == v1, lane, no_lane), axis=1, keepdims=True)
    pe2 = jnp.where(lane == i1, -1.0, pe)
    v2 = jnp.max(pe2, axis=1, keepdims=True)
    i2 = jnp.min(jnp.where(pe2 == v2, lane, no_lane), axis=1, keepdims=True)
    wsum = v1 + v2
    gate1 = p_top * (v1 / wsum)
    gate2 = p_top * (v2 / wsum)

    oh1 = (lane == i1).astype(F32)
    oh2 = (lane == i2).astype(F32)
    c1 = jnp.dot(tri_ref[...], oh1.astype(BF16), preferred_element_type=F32)
    c2 = jnp.dot(tri_ref[...], oh2.astype(BF16), preferred_element_type=F32)
    run = run_ref[...]
    tot1 = jnp.sum(oh1, axis=0, keepdims=True)
    tot2 = jnp.sum(oh2, axis=0, keepdims=True)
    rank1 = jnp.sum(oh1 * (run + c1), axis=1, keepdims=True)
    rank2 = jnp.sum(oh2 * (run + tot1 + c2), axis=1, keepdims=True)
    run = run + tot1 + tot2
    run_ref[...] = run
    cnt_ref[...] = run

    ml = lax.broadcasted_iota(jnp.int32, (tm, META_LANES), 1)
    meta = jnp.where(ml == 0, i1 - EXPERT_LANE0, 0.0)
    meta = jnp.where(ml == 1, i2 - EXPERT_LANE0, meta)
    meta = jnp.where(ml == 2, gate1, meta)
    meta = jnp.where(ml == 3, gate2, meta)
    meta = jnp.where(ml == 4, rank1, meta)
    meta = jnp.where(ml == 5, rank2, meta)
    meta_ref[...] = meta


def _mixout(xp, xs, hf, hb, o, u, seq_lens, conv_w, conv_b, ln_g, ln_b, w_out, g_ffn, w_rg, b_rg, w_re, b_re):
    tp, ts = xp.shape[0], xs.shape[0]
    t = tp + ts
    n_tiles = t // TM_MIX
    hpt = TM_MIX // CONV_HALO
    n_halo = t // CONV_HALO
    firsts, lasts, pos = [], [], 0
    for ln in seq_lens:
        assert ln % TM_MIX == 0
        firsts.append(pos // TM_MIX)
        lasts.append((pos + ln) // TM_MIX - 1)
        pos += ln
    first = jnp.zeros((n_tiles,), jnp.int32).at[jnp.array(firsts)].set(1)
    last = jnp.zeros((n_tiles,), jnp.int32).at[jnp.array(lasts)].set(1)

    cw = jnp.zeros((32, D_CONV), F32).at[:CONV_WIDTH].set(conv_w.reshape(CONV_WIDTH, D_CONV))
    wr = jnp.zeros((D_MODEL, ROUTER_LANES), F32)
    wr = wr.at[:, :N_GROUPS].set(w_rg).at[:, EXPERT_LANE0:EXPERT_LANE0 + N_EXPERTS].set(w_re).astype(BF16)
    br = jnp.zeros((1, ROUTER_LANES), F32)
    br = br.at[0, :N_GROUPS].set(b_rg).at[0, EXPERT_LANE0:EXPERT_LANE0 + N_EXPERTS].set(b_re)
    tri = (lax.broadcasted_iota(jnp.int32, (TM_MIX, TM_MIX), 0)
           > lax.broadcasted_iota(jnp.int32, (TM_MIX, TM_MIX), 1)).astype(BF16)

    tok = lambda i, f, l: (i, 0)
    fixed = lambda i, f, l: (0, 0)
    row = lambda n: pl.BlockSpec((1, n), fixed)
    xp_spec, xs_spec = _two_batch_specs((TM_MIX, D_MODEL), tp // TM_MIX, ts // TM_MIX)
    grid_spec = pltpu.PrefetchScalarGridSpec(
        num_scalar_prefetch=2,
        grid=(n_tiles,),
        in_specs=[xp_spec, xs_spec,
                  pl.BlockSpec((TM_MIX, D_MLSTM), tok), pl.BlockSpec((TM_MIX, D_MLSTM), tok),
                  pl.BlockSpec((TM_MIX, D_MLSTM), tok), pl.BlockSpec((TM_MIX, D_CONV), tok),
                  pl.BlockSpec((CONV_HALO, D_CONV), lambda i, f, l: (jnp.maximum(i * hpt - 1, 0), 0)),
                  pl.BlockSpec((CONV_HALO, D_CONV), lambda i, f, l: (jnp.minimum((i + 1) * hpt, n_halo - 1), 0)),
                  pl.BlockSpec((32, D_CONV), fixed), row(D_CONV), row(D_CONV), row(D_CONV),
                  pl.BlockSpec((D_MODEL, D_MODEL), fixed), row(D_MODEL),
                  pl.BlockSpec((D_MODEL, ROUTER_LANES), fixed), row(ROUTER_LANES),
                  pl.BlockSpec((TM_MIX, TM_MIX), fixed)],
        out_specs=[pl.BlockSpec((TM_MIX, D_MODEL), tok), pl.BlockSpec((TM_MIX, D_MODEL), tok),
                   pl.BlockSpec((TM_MIX, META_LANES), tok), row(ROUTER_LANES)],
        scratch_shapes=[pltpu.VMEM((TM_MIX + 2 * CONV_HALO, D_CONV), F32),
                        pltpu.VMEM((TM_MIX, D_CONV), F32),
                        pltpu.VMEM((TM_MIX, D_MODEL), BF16),
                        pltpu.VMEM((1, ROUTER_LANES), F32)],
    )
    return pl.pallas_call(
        functools.partial(_mixout_kernel, n_first=tp // TM_MIX),
        grid_spec=grid_spec,
        out_shape=[jax.ShapeDtypeStruct((t, D_MODEL), F32), jax.ShapeDtypeStruct((t, D_MODEL), F32),
                   jax.ShapeDtypeStruct((t, META_LANES), F32), jax.ShapeDtypeStruct((1, ROUTER_LANES), F32)],
        compiler_params=_cparams(),
        name="mixout",
    )(first, last, xp, xs, hf, hb, o, u, u, u, cw, conv_b.reshape(1, D_CONV), ln_g.reshape(1, D_CONV),
      ln_b.reshape(1, D_CONV), w_out.astype(BF16), g_ffn.reshape(1, D_MODEL), wr, br, tri)


def _sc_mesh():
    return plsc.VectorSubcoreMesh(core_axis_name="c", subcore_axis_name="s")


def _sc_worker_base(per_worker):
    wid = lax.axis_index("s") * SC_CORES + lax.axis_index("c")
    return wid * per_worker


def _dispatch(hn, pos1, pos2, n_slots):
    t, d = hn.shape
    per_worker = t // SC_WORKERS
    assert per_worker * SC_WORKERS == t and per_worker % SC_ROWS == 0

    def body(h_hbm, p1_hbm, p2_hbm, out_hbm, i1_v, i2_v, rows_v, sem1, sem2):
        base0 = _sc_worker_base(per_worker)

        @pl.loop(0, per_worker // SC_ROWS)
        def _(c):
            base = pl.multiple_of(base0 + c * SC_ROWS, SC_ROWS)
            pltpu.sync_copy(p1_hbm.at[pl.ds(base, SC_ROWS)], i1_v)
            pltpu.sync_copy(p2_hbm.at[pl.ds(base, SC_ROWS)], i2_v)
            pltpu.sync_copy(h_hbm.at[pl.ds(base, SC_ROWS)], rows_v)
            c1 = pltpu.async_copy(rows_v, out_hbm.at[i1_v], sem1)
            c2 = pltpu.async_copy(rows_v, out_hbm.at[i2_v], sem2)
            c1.wait()
            c2.wait()

    return pl.kernel(
        body,
        out_type=jax.ShapeDtypeStruct((n_slots, d), hn.dtype),
        mesh=_sc_mesh(),
        scratch_types=[pltpu.VMEM((SC_ROWS,), jnp.int32), pltpu.VMEM((SC_ROWS,), jnp.int32),
                       pltpu.VMEM((SC_ROWS, d), hn.dtype), pltpu.SemaphoreType.DMA, pltpu.SemaphoreType.DMA],
        name="dispatch",
    )(hn, pos1, pos2)


def _collect(ys, pos1, pos2):
    t = pos1.shape[0]
    d = ys.shape[1]
    per_worker = t // SC_WORKERS
    assert per_worker * SC_WORKERS == t and per_worker % SC_ROWS == 0

    def body(ys_hbm, p1_hbm, p2_hbm, y1_hbm, y2_hbm, i_v, rows_v, sem):
        base0 = _sc_worker_base(per_worker)

        @pl.loop(0, per_worker // SC_ROWS)
        def _(c):
            base = pl.multiple_of(base0 + c * SC_ROWS, SC_ROWS)
            for p_hbm, y_hbm in ((p1_hbm, y1_hbm), (p2_hbm, y2_hbm)):
                pltpu.sync_copy(p_hbm.at[pl.ds(base, SC_ROWS)], i_v)
                pltpu.async_copy(ys_hbm.at[i_v], rows_v, sem).wait()
                pltpu.sync_copy(rows_v, y_hbm.at[pl.ds(base, SC_ROWS)])

    out = jax.ShapeDtypeStruct((t, d), ys.dtype)
    return pl.kernel(
        body,
        out_type=(out, out),
        mesh=_sc_mesh(),
        scratch_types=[pltpu.VMEM((SC_ROWS,), jnp.int32), pltpu.VMEM((SC_ROWS, d), ys.dtype),
                       pltpu.SemaphoreType.DMA],
        name="collect",
    )(ys, pos1, pos2)


def _experts_kernel(te_ref, rows_ref, hs_ref, wg_ref, wu_ref, wd_ref, ys_ref):
    i = pl.program_id(0)

    @pl.when(rows_ref[i] > 0)
    def _():
        x = hs_ref[...].astype(BF16)
        hg = jnp.dot(x, wg_ref[...], preferred_element_type=F32)
        hu = jnp.dot(x, wu_ref[...], preferred_element_type=F32)
        hid = (hg * jax.nn.sigmoid(hg) * hu).astype(BF16)
        ys_ref[...] = jnp.dot(hid, wd_ref[...], preferred_element_type=F32)

    @pl.when(rows_ref[i] == 0)
    def _():
        ys_ref[...] = jnp.zeros_like(ys_ref)


def _experts(hs, tile_expert, tile_rows, w_gate, w_up, w_down):
    n_tiles = tile_expert.shape[0]
    wsel = lambda i, te, rows: (te[i], 0, 0)
    slot = lambda i, te, rows: (i, 0)
    grid_spec = pltpu.PrefetchScalarGridSpec(
        num_scalar_prefetch=2,
        grid=(n_tiles,),
        in_specs=[pl.BlockSpec((TM_EXP, D_MODEL), slot),
                  pl.BlockSpec((None, D_MODEL, D_EXPERT), wsel),
                  pl.BlockSpec((None, D_MODEL, D_EXPERT), wsel),
                  pl.BlockSpec((None, D_EXPERT, D_MODEL), wsel)],
        out_specs=pl.BlockSpec((TM_EXP, D_MODEL), slot),
    )
    return pl.pallas_call(
        _experts_kernel,
        grid_spec=grid_spec,
        out_shape=jax.ShapeDtypeStruct((n_tiles * TM_EXP, D_MODEL), F32),
        compiler_params=_cparams(),
        name="experts",
    )(tile_expert, tile_rows, hs, w_gate.astype(BF16), w_up.astype(BF16), w_down.astype(BF16))


def _combine_kernel(x1_ref, meta_ref, y1_ref, y2_ref, gfin_ref, outp_ref, outs_ref, *, n_first):
    i = pl.program_id(0)
    meta = meta_ref[...]
    x2 = x1_ref[...] + meta[:, 2:3] * y1_ref[...] + meta[:, 3:4] * y2_ref[...]
    out = _rms(x2, gfin_ref[...])

    @pl.when(i < n_first)
    def _():
        outp_ref[...] = out

    @pl.when(i >= n_first)
    def _():
        outs_ref[...] = out


def _combine(x1, meta, y1, y2, g_final, tp):
    t = x1.shape[0]
    ts = t - tp
    assert tp % TM_OUT == 0 and ts % TM_OUT == 0
    tok = pl.BlockSpec((TM_OUT, D_MODEL), lambda i: (i, 0))
    outp_spec, outs_spec = _two_batch_specs((TM_OUT, D_MODEL), tp // TM_OUT, ts // TM_OUT)
    return pl.pallas_call(
        functools.partial(_combine_kernel, n_first=tp // TM_OUT),
        grid=(t // TM_OUT,),
        in_specs=[tok, pl.BlockSpec((TM_OUT, META_LANES), lambda i: (i, 0)), tok, tok,
                  pl.BlockSpec((1, D_MODEL), lambda i: (0, 0))],
        out_specs=[outp_spec, outs_spec],
        out_shape=[jax.ShapeDtypeStruct((tp, D_MODEL), F32), jax.ShapeDtypeStruct((ts, D_MODEL), F32)],
        compiler_params=_cparams(),
        name="combine",
    )(x1, meta, y1, y2, g_final.reshape(1, D_MODEL))


def _slot_layout(meta, counts_row, n_tiles):
    eid = meta[:, 0:2].astype(jnp.int32)
    rank = meta[:, 4:6].astype(jnp.int32)
    counts = counts_row[0, EXPERT_LANE0:EXPERT_LANE0 + N_EXPERTS].astype(jnp.int32)
    padded = (counts + TM_EXP - 1) // TM_EXP * TM_EXP
    ends = jnp.cumsum(padded)
    offs = ends - padded
    pos = offs[eid] + rank
    tile_start = jnp.arange(n_tiles, dtype=jnp.int32) * TM_EXP
    tile_expert = jnp.minimum(jnp.sum(tile_start[:, None] >= ends[None, :], axis=1), N_EXPERTS - 1)
    tile_expert = tile_expert.astype(jnp.int32)
    tile_rows = jnp.clip(counts[tile_expert] - (tile_start - offs[tile_expert]), 0, TM_EXP)
    tile_rows = jnp.where(tile_start < ends[-1], tile_rows, 0).astype(jnp.int32)
    return pos[:, 0], pos[:, 1], tile_expert, tile_rows


def kernel(x_prompt, x_sample, g_mix, w_in, b_gate, conv_w, conv_b, ln_g, ln_b, w_out, g_ffn,
           w_router_group, b_router_group, w_router_expert, b_router_expert, w_gate, w_up, w_down, g_final):
    assert g_mix.shape[0] == 1, "one layer"
    bp, lp, _ = x_prompt.shape
    bs, ls, _ = x_sample.shape
    seq_lens = [lp] * bp + [ls] * bs
    tp, ts = bp * lp, bs * ls
    t = tp + ts
    xp = x_prompt.reshape(tp, D_MODEL)
    xs = x_sample.reshape(ts, D_MODEL)

    q, v, o, u, kt, gr = _inproj(xp, xs, g_mix[0], w_in[0], b_gate[0])
    hf, hb = _mlstm(q, kt, v, gr, seq_lens)
    x1, hn, meta, counts = _mixout(xp, xs, hf, hb, o, u, seq_lens, conv_w[0], conv_b[0], ln_g[0], ln_b[0],
                                   w_out[0], g_ffn[0], w_router_group[0], b_router_group[0],
                                   w_router_expert[0], b_router_expert[0])
    n_tiles = (2 * t + N_EXPERTS * (TM_EXP - 1)) // TM_EXP + 1
    pos1, pos2, tile_expert, tile_rows = _slot_layout(meta, counts, n_tiles)
    hs = _dispatch(hn, pos1, pos2, n_tiles * TM_EXP)
    ys = _experts(hs, tile_expert, tile_rows, w_gate[0], w_up[0], w_down[0])
    y1, y2 = _collect(ys, pos1, pos2)
    outp, outs = _combine(x1, meta, y1, y2, g_final, tp)
    return outp.reshape(bp, lp, D_MODEL), outs.reshape(bs, ls, D_MODEL)
```

```python
import functools

import jax
import jax.numpy as jnp
from jax import lax
from jax.experimental import pallas as pl
from jax.experimental.pallas import tpu as pltpu
from jax.experimental.pallas import tpu_sc as plsc

F32 = jnp.float32
BF16 = jnp.bfloat16

D_MODEL = 1024
N_HEADS = 4
HEAD_DIM = 128
D_MLSTM = N_HEADS * HEAD_DIM
D_CONV = D_MODEL - D_MLSTM
CONV_WIDTH = 31
CONV_HALO = 16
N_DIR = 2
N_GROUPS = 4
EXPERTS_PER_GROUP = 4
N_EXPERTS = N_GROUPS * EXPERTS_PER_GROUP
D_EXPERT = 512
EPS = 1e-6
K_SCALE = HEAD_DIM ** -0.5

GATE_ROWS = 16
QROWS = 24
ROUTER_LANES = 128
EXPERT_LANE0 = N_GROUPS
META_LANES = 8

TM_IN = 512
CHUNK = 256
GP_CHUNKS = 8
TM_MIX = 512
CONV_ROWS = 128
TM_EXP = 512
TM_OUT = 256
VMEM_LIMIT = 48 * 1024 * 1024

SC_CORES = 2
SC_SUBCORES = 16
SC_WORKERS = SC_CORES * SC_SUBCORES
SC_ROWS = 64


def _cparams(n_axes=1):
    return pltpu.CompilerParams(dimension_semantics=("arbitrary",) * n_axes,
                                vmem_limit_bytes=VMEM_LIMIT)


def _nt_dot(a, b):
    return lax.dot_general(a, b, (((1,), (1,)), ((), ())), preferred_element_type=F32)


def _rms(x, g):
    return x * lax.rsqrt(jnp.mean(x * x, axis=-1, keepdims=True) + EPS) * g


def _two_batch_specs(block, n_first, n_second):
    first = pl.BlockSpec(block, lambda i, *_: (jnp.minimum(i, n_first - 1), 0))
    second = pl.BlockSpec(block, lambda i, *_: (jnp.maximum(i - n_first, 0), 0))
    return first, second


def _inproj_kernel(xp_ref, xs_ref, g_ref, wq_ref, wv_ref, wo_ref, wa_ref, wb_ref, wkt_ref, wgt_ref, bg_ref,
                   q_ref, v_ref, o_ref, u_ref, kt_ref, gr_ref, *, n_first):
    x = jnp.where(pl.program_id(0) < n_first, xp_ref[...], xs_ref[...])
    xn = _rms(x, g_ref[...]).astype(BF16)
    q_ref[...] = jnp.dot(xn, wq_ref[...], preferred_element_type=F32).astype(BF16)
    v_ref[...] = jnp.dot(xn, wv_ref[...], preferred_element_type=F32).astype(BF16)
    o_ref[...] = jnp.dot(xn, wo_ref[...], preferred_element_type=F32).astype(BF16)
    a = jnp.dot(xn, wa_ref[...], preferred_element_type=F32)
    b = jnp.dot(xn, wb_ref[...], preferred_element_type=F32)
    u_ref[...] = (a * jax.nn.sigmoid(b)).astype(BF16)
    kt_ref[...] = (_nt_dot(wkt_ref[...], xn) * K_SCALE).astype(BF16)
    gr = _nt_dot(wgt_ref[...], xn) + bg_ref[...]
    for c in range(gr_ref.shape[0]):
        gr_ref[c] = gr[:, c * CHUNK:(c + 1) * CHUNK]


def _inproj(xp, xs, g_mix, w_in, b_gate):
    tp, ts = xp.shape[0], xs.shape[0]
    t = tp + ts
    assert tp % TM_IN == 0 and ts % TM_IN == 0
    off_k, off_v, off_o, off_g = D_MLSTM, 2 * D_MLSTM, 3 * D_MLSTM, 4 * D_MLSTM
    off_a = off_g + 2 * N_DIR * N_HEADS
    off_b = off_a + D_CONV
    wq = w_in[:, 0:off_k].astype(BF16)
    wkt = w_in[:, off_k:off_v].T.astype(BF16)
    wv = w_in[:, off_v:off_o].astype(BF16)
    wo = w_in[:, off_o:off_g].astype(BF16)
    wa = w_in[:, off_a:off_b].astype(BF16)
    wb = w_in[:, off_b:off_b + D_CONV].astype(BF16)
    wg = w_in[:, off_g:off_a].T.reshape(N_DIR, 2, N_HEADS, D_MODEL)
    wgt = jnp.zeros((N_DIR, 2, GATE_ROWS // 2, D_MODEL), F32).at[:, :, :N_HEADS].set(wg)
    wgt = wgt.reshape(N_DIR * GATE_ROWS, D_MODEL).astype(BF16)
    bg = jnp.zeros((N_DIR, 2, GATE_ROWS // 2), F32).at[:, :, :N_HEADS].set(
        b_gate.reshape(N_DIR, 2, N_HEADS)).reshape(N_DIR * GATE_ROWS, 1)

    tok = lambda i: (i, 0)
    fixed = lambda i: (0, 0)
    wspec = pl.BlockSpec((D_MODEL, D_MLSTM), fixed)
    xp_spec, xs_spec = _two_batch_specs((TM_IN, D_MODEL), tp // TM_IN, ts // TM_IN)
    cpt = TM_IN // CHUNK
    return pl.pallas_call(
        functools.partial(_inproj_kernel, n_first=tp // TM_IN),
        grid=(t // TM_IN,),
        in_specs=[xp_spec, xs_spec, pl.BlockSpec((1, D_MODEL), fixed),
                  wspec, wspec, wspec, wspec, wspec,
                  pl.BlockSpec((D_MLSTM, D_MODEL), fixed),
                  pl.BlockSpec((N_DIR * GATE_ROWS, D_MODEL), fixed),
                  pl.BlockSpec((N_DIR * GATE_ROWS, 1), fixed)],
        out_specs=[pl.BlockSpec((TM_IN, D_MLSTM), tok)] * 4 + [
            pl.BlockSpec((D_MLSTM, TM_IN), lambda i: (0, i)),
            pl.BlockSpec((cpt, N_DIR * GATE_ROWS, CHUNK), lambda i: (i, 0, 0))],
        out_shape=[jax.ShapeDtypeStruct((t, D_MLSTM), BF16)] * 4 + [
            jax.ShapeDtypeStruct((D_MLSTM, t), BF16),
            jax.ShapeDtypeStruct((t // CHUNK, N_DIR * GATE_ROWS, CHUNK), F32)],
        compiler_params=_cparams(),
        name="inproj",
    )(xp, xs, g_mix.reshape(1, D_MODEL), wq, wv, wo, wa, wb, wkt, wgt, bg)


def _log_sigmoid(x):
    return jnp.minimum(x, 0.0) - jnp.log1p(jnp.exp(-jnp.abs(x)))


def _gateprep_kernel(reset_ref, g_ref, rowq_ref, colq_ref, m_ref, *, rev):
    n, _, c = g_ref.shape
    step = pl.program_id(0)
    blk = pl.num_programs(0) - 1 - step if rev else step

    @pl.when(step == 0)
    def _():
        m_ref[...] = jnp.zeros_like(m_ref)

    ig = g_ref[:, 0:8, :]
    lf = _log_sigmoid(g_ref[:, 8:16, :])
    lane = lax.broadcasted_iota(jnp.int32, (n, 8, c), 2)

    def scan(x, op, ident):
        k = 1
        while k < c:
            if rev:
                shifted, valid = pltpu.roll(x, c - k, axis=2), lane < c - k
            else:
                shifted, valid = pltpu.roll(x, k, axis=2), lane >= k
            x = op(x, jnp.where(valid, shifted, ident))
            k *= 2
        return x

    bc = scan(lf, jnp.add, 0.0)
    a = ig - bc
    cm = scan(a, jnp.maximum, -jnp.inf)
    b_tot = jnp.sum(lf, axis=2, keepdims=True)
    a_max = jnp.max(a, axis=2, keepdims=True)

    m = m_ref[...]
    m_in = [None] * n
    for j in (range(n - 1, -1, -1) if rev else range(n)):
        m = jnp.where(reset_ref[blk * n + j] == 1, 0.0, m)
        m_in[j] = m
        m = b_tot[j] + jnp.maximum(m, a_max[j])
    m_ref[...] = m
    m_old = jnp.stack(m_in)

    mx = jnp.maximum(m_old, cm)
    mx_last = jnp.maximum(m_old, a_max)
    rowq_ref[:, 0:8, :] = a
    rowq_ref[:, 8:16, :] = jnp.exp(a - mx_last)
    rowq_ref[:, 16:24, :] = jnp.exp(m_old - mx_last)
    e1 = jnp.exp(m_old - mx)
    fl = jnp.exp(-(mx + bc))
    for j in range(n):
        colq_ref[j] = jnp.concatenate([mx[j], e1[j], fl[j]], axis=0).T


def _gateprep(gr, reset, rev):
    n_chunks = gr.shape[0]
    assert n_chunks % GP_CHUNKS == 0
    nb = n_chunks // GP_CHUNKS
    d = 1 if rev else 0
    bidx = (lambda s: nb - 1 - s) if rev else (lambda s: s)
    grid_spec = pltpu.PrefetchScalarGridSpec(
        num_scalar_prefetch=1,
        grid=(nb,),
        in_specs=[pl.BlockSpec((GP_CHUNKS, GATE_ROWS, CHUNK), lambda s, r: (bidx(s), d, 0))],
        out_specs=[pl.BlockSpec((GP_CHUNKS, QROWS, CHUNK), lambda s, r: (bidx(s), 0, 0)),
                   pl.BlockSpec((GP_CHUNKS, CHUNK, QROWS), lambda s, r: (bidx(s), 0, 0))],
        scratch_shapes=[pltpu.VMEM((8, CHUNK), F32)],
    )
    return pl.pallas_call(
        functools.partial(_gateprep_kernel, rev=rev),
        grid_spec=grid_spec,
        out_shape=[jax.ShapeDtypeStruct((n_chunks, QROWS, CHUNK), F32),
                   jax.ShapeDtypeStruct((n_chunks, CHUNK, QROWS), F32)],
        compiler_params=_cparams(),
        name="gateprep_bwd" if rev else "gateprep_fwd",
    )(reset, gr)


def _mlstm_kernel(rf_ref, rb_ref,
                  qf_ref, ktf_ref, vf_ref, rowf_ref, colf_ref,
                  qb_ref, ktb_ref, vb_ref, rowb_ref, colb_ref,
                  hf_ref, hb_ref, cst_ref):
    c = qf_ref.shape[0]
    step = pl.program_id(0)
    last = pl.num_programs(0) - 1
    row_i = lax.broadcasted_iota(jnp.int32, (c, c), 0)
    col_i = lax.broadcasted_iota(jnp.int32, (c, c), 1)
    ones = jnp.ones((c, HEAD_DIM), BF16)

    dirs = ((rf_ref[step], col_i <= row_i, qf_ref, ktf_ref, vf_ref, rowf_ref, colf_ref, hf_ref),
            (rb_ref[last - step], col_i >= row_i, qb_ref, ktb_ref, vb_ref, rowb_ref, colb_ref, hb_ref))
    for d, (reset, mask, q_ref, kt_ref, v_ref, row_ref, col_ref, h_ref) in enumerate(dirs):
        @pl.when(reset == 1)
        def _():
            cst_ref[d] = jnp.zeros(cst_ref.shape[1:], F32)

        rowq = row_ref[...]
        colq = col_ref[...]
        for h in range(N_HEADS):
            hs = slice(h * HEAD_DIM, (h + 1) * HEAD_DIM)
            qh = q_ref[:, hs]
            kth = kt_ref[hs, :]
            vext = jnp.concatenate([v_ref[:, hs], ones], axis=1)
            s = jnp.dot(qh, kth, preferred_element_type=F32)
            e = jnp.exp(jnp.where(mask, rowq[h:h + 1, :] - colq[:, h:h + 1], -jnp.inf))
            r1 = jnp.dot((s * e).astype(BF16), vext, preferred_element_type=F32)
            cst = cst_ref[d, h]
            r2 = jnp.dot(qh, cst.astype(BF16), preferred_element_type=F32)
            e1 = colq[:, 8 + h:9 + h]
            num = r1[:, :HEAD_DIM] + e1 * r2[:, :HEAD_DIM]
            den = r1[:, HEAD_DIM:] + e1 * r2[:, HEAD_DIM:]
            h_ref[:, hs] = num / jnp.maximum(jnp.abs(den), colq[:, 16 + h:17 + h])
            kw = (kth.astype(F32) * rowq[8 + h:9 + h, :]).astype(BF16)
            cst_ref[d, h] = rowq[16 + h:17 + h, 0:1] * cst + jnp.dot(kw, vext, preferred_element_type=F32)


def _mlstm(q, kt, v, gr, seq_lens):
    t = q.shape[0]
    n = t // CHUNK
    starts, ends, pos = [], [], 0
    for ln in seq_lens:
        assert ln % CHUNK == 0
        starts.append(pos // CHUNK)
        ends.append((pos + ln) // CHUNK - 1)
        pos += ln
    reset_f = jnp.zeros((n,), jnp.int32).at[jnp.array(starts)].set(1)
    reset_b = jnp.zeros((n,), jnp.int32).at[jnp.array(ends)].set(1)
    rowf, colf = _gateprep(gr, reset_f, rev=False)
    rowb, colb = _gateprep(gr, reset_b, rev=True)

    def specs(cidx):
        return [pl.BlockSpec((CHUNK, D_MLSTM), lambda s, rf, rb: (cidx(s), 0)),
                pl.BlockSpec((D_MLSTM, CHUNK), lambda s, rf, rb: (0, cidx(s))),
                pl.BlockSpec((CHUNK, D_MLSTM), lambda s, rf, rb: (cidx(s), 0)),
                pl.BlockSpec((None, QROWS, CHUNK), lambda s, rf, rb: (cidx(s), 0, 0)),
                pl.BlockSpec((None, CHUNK, QROWS), lambda s, rf, rb: (cidx(s), 0, 0))]

    fwd = lambda s: s
    bwd = lambda s: n - 1 - s
    grid_spec = pltpu.PrefetchScalarGridSpec(
        num_scalar_prefetch=2,
        grid=(n,),
        in_specs=specs(fwd) + specs(bwd),
        out_specs=[pl.BlockSpec((CHUNK, D_MLSTM), lambda s, rf, rb: (fwd(s), 0)),
                   pl.BlockSpec((CHUNK, D_MLSTM), lambda s, rf, rb: (bwd(s), 0))],
        scratch_shapes=[pltpu.VMEM((N_DIR, N_HEADS, HEAD_DIM, 2 * HEAD_DIM), F32)],
    )
    return pl.pallas_call(
        _mlstm_kernel,
        grid_spec=grid_spec,
        out_shape=[jax.ShapeDtypeStruct((t, D_MLSTM), F32)] * 2,
        compiler_params=_cparams(),
        name="mlstm",
    )(reset_f, reset_b, q, kt, v, rowf, colf, q, kt, v, rowb, colb)


def _mixout_kernel(first_ref, last_ref,
                   xp_ref, xs_ref, hf_ref, hb_ref, o_ref, u_ref, up_ref, un_ref, cw_ref, cb_ref, lng_ref, lnb_ref,
                   wout_ref, gffn_ref, wr_ref, br_ref, tri_ref,
                   x1_ref, hn_ref, meta_ref, cnt_ref,
                   win_ref, y_ref, run_ref, *, n_first):
    i = pl.program_id(0)
    tm = hf_ref.shape[0]

    @pl.when(i == 0)
    def _():
        run_ref[...] = jnp.zeros_like(run_ref)

    ym = (jax.nn.sigmoid(o_ref[...].astype(F32)) * (hf_ref[...] + hb_ref[...])).astype(BF16)
    x = jnp.where(i < n_first, xp_ref[...], xs_ref[...])
    x1m = x + jnp.dot(ym, wout_ref[:D_MLSTM, :], preferred_element_type=F32)

    win_ref[0:CONV_HALO, :] = jnp.where(first_ref[i] == 1, 0.0, up_ref[...].astype(F32))
    win_ref[CONV_HALO:CONV_HALO + tm, :] = u_ref[...].astype(F32)
    win_ref[CONV_HALO + tm:, :] = jnp.where(last_ref[i] == 1, 0.0, un_ref[...].astype(F32))

    off0 = CONV_HALO - CONV_WIDTH // 2
    for r0 in range(0, tm, CONV_ROWS):
        tiles = []
        for lt in range(D_CONV // 128):
            ls = slice(lt * 128, (lt + 1) * 128)
            acc = jnp.broadcast_to(cb_ref[:, ls], (CONV_ROWS, 128))
            for s in range(8):
                part = None
                for j in range(CONV_WIDTH):
                    if (off0 + j) % 8 != s:
                        continue
                    base = (off0 + j) // 8 * 8
                    term = win_ref[r0 + base:r0 + base + CONV_ROWS + 8, ls] * cw_ref[j:j + 1, ls]
                    part = term if part is None else part + term
                acc = acc + part[s:s + CONV_ROWS, :]
            tiles.append(acc)
        cv = jnp.concatenate(tiles, axis=1)
        xc = cv - jnp.mean(cv, axis=-1, keepdims=True)
        yc = xc * lax.rsqrt(jnp.mean(xc * xc, axis=-1, keepdims=True) + EPS) * lng_ref[...] + lnb_ref[...]
        y_ref[r0:r0 + CONV_ROWS, :] = (yc * jax.nn.sigmoid(yc)).astype(BF16)

    x1 = x1m + jnp.dot(y_ref[...], wout_ref[D_MLSTM:, :], preferred_element_type=F32)
    x1_ref[...] = x1
    hn = _rms(x1, gffn_ref[...])
    hn_ref[...] = hn

    logits = jnp.dot(hn.astype(BF16), wr_ref[...], preferred_element_type=F32) + br_ref[...]
    lane = lax.broadcasted_iota(jnp.int32, (tm, ROUTER_LANES), 1).astype(F32)
    neg = -jnp.inf
    no_lane = float(ROUTER_LANES)
    gl = jnp.where(lane < N_GROUPS, logits, neg)
    gmax = jnp.max(gl, axis=1, keepdims=True)
    p_top = 1.0 / jnp.sum(jnp.exp(gl - gmax), axis=1, keepdims=True)
    g_idx = jnp.min(jnp.where(gl == gmax, lane, no_lane), axis=1, keepdims=True)
    lo = EXPERT_LANE0 + EXPERTS_PER_GROUP * g_idx
    in_grp = (lane >= lo) & (lane < lo + EXPERTS_PER_GROUP)
    el = jnp.where(in_grp, logits, neg)
    ee = jnp.exp(el - jnp.max(el, axis=1, keepdims=True))
    pe = jnp.where(in_grp, ee / jnp.sum(ee, axis=1, keepdims=True), -1.0)
    v1 = jnp.max(pe, axis=1, keepdims=True)
    i1 = jnp.min(jnp.where(pe == v1, lane, no_lane), axis=1, keepdims=True)
    pe2 = jnp.where(lane == i1, -1.0, pe)
    v2 = jnp.max(pe2, axis=1, keepdims=True)
    i2 = jnp.min(jnp.where(pe2 == v2, lane, no_lane), axis=1, keepdims=True)
    wsum = v1 + v2
    gate1 = p_top * (v1 / wsum)
    gate2 = p_top * (v2 / wsum)

    oh1 = (lane == i1).astype(F32)
    oh2 = (lane == i2).astype(F32)
    c1 = jnp.dot(tri_ref[...], oh1.astype(BF16), preferred_element_type=F32)
    c2 = jnp.dot(tri_ref[...], oh2.astype(BF16), preferred_element_type=F32)
    run = run_ref[...]
    tot1 = jnp.sum(oh1, axis=0, keepdims=True)
    tot2 = jnp.sum(oh2, axis=0, keepdims=True)
    rank1 = jnp.sum(oh1 * (run + c1), axis=1, keepdims=True)
    rank2 = jnp.sum(oh2 * (run + tot1 + c2), axis=1, keepdims=True)
    run = run + tot1 + tot2
    run_ref[...] = run
    cnt_ref[...] = run

    ml = lax.broadcasted_iota(jnp.int32, (tm, META_LANES), 1)
    meta = jnp.where(ml == 0, i1 - EXPERT_LANE0, 0.0)
    meta = jnp.where(ml == 1, i2 - EXPERT_LANE0, meta)
    meta = jnp.where(ml == 2, gate1, meta)
    meta = jnp.where(ml == 3, gate2, meta)
    meta = jnp.where(ml == 4, rank1, meta)
    meta = jnp.where(ml == 5, rank2, meta)
    meta_ref[...] = meta


def _mixout(xp, xs, hf, hb, o, u, seq_lens, conv_w, conv_b, ln_g, ln_b, w_out, g_ffn, w_rg, b_rg, w_re, b_re):
    tp, ts = xp.shape[0], xs.shape[0]
    t = tp + ts
    n_tiles = t // TM_MIX
    hpt = TM_MIX // CONV_HALO
    n_halo = t // CONV_HALO
    firsts, lasts, pos = [], [], 0
    for ln in seq_lens:
        assert ln % TM_MIX == 0
        firsts.append(pos // TM_MIX)
        lasts.append((pos + ln) // TM_MIX - 1)
        pos += ln
    first = jnp.zeros((n_tiles,), jnp.int32).at[jnp.array(firsts)].set(1)
    last = jnp.zeros((n_tiles,), jnp.int32).at[jnp.array(lasts)].set(1)

    cw = jnp.zeros((32, D_CONV), F32).at[:CONV_WIDTH].set(conv_w.reshape(CONV_WIDTH, D_CONV))
    wr = jnp.zeros((D_MODEL, ROUTER_LANES), F32)
    wr = wr.at[:, :N_GROUPS].set(w_rg).at[:, EXPERT_LANE0:EXPERT_LANE0 + N_EXPERTS].set(w_re).astype(BF16)
    br = jnp.zeros((1, ROUTER_LANES), F32)
    br = br.at[0, :N_GROUPS].set(b_rg).at[0, EXPERT_LANE0:EXPERT_LANE0 + N_EXPERTS].set(b_re)
    tri = (lax.broadcasted_iota(jnp.int32, (TM_MIX, TM_MIX), 0)
           > lax.broadcasted_iota(jnp.int32, (TM_MIX, TM_MIX), 1)).astype(BF16)

    tok = lambda i, f, l: (i, 0)
    fixed = lambda i, f, l: (0, 0)
    row = lambda n: pl.BlockSpec((1, n), fixed)
    xp_spec, xs_spec = _two_batch_specs((TM_MIX, D_MODEL), tp // TM_MIX, ts // TM_MIX)
    grid_spec = pltpu.PrefetchScalarGridSpec(
        num_scalar_prefetch=2,
        grid=(n_tiles,),
        in_specs=[xp_spec, xs_spec,
                  pl.BlockSpec((TM_MIX, D_MLSTM), tok), pl.BlockSpec((TM_MIX, D_MLSTM), tok),
                  pl.BlockSpec((TM_MIX, D_MLSTM), tok), pl.BlockSpec((TM_MIX, D_CONV), tok),
                  pl.BlockSpec((CONV_HALO, D_CONV), lambda i, f, l: (jnp.maximum(i * hpt - 1, 0), 0)),
                  pl.BlockSpec((CONV_HALO, D_CONV), lambda i, f, l: (jnp.minimum((i + 1) * hpt, n_halo - 1), 0)),
                  pl.BlockSpec((32, D_CONV), fixed), row(D_CONV), row(D_CONV), row(D_CONV),
                  pl.BlockSpec((D_MODEL, D_MODEL), fixed), row(D_MODEL),
                  pl.BlockSpec((D_MODEL, ROUTER_LANES), fixed), row(ROUTER_LANES),
                  pl.BlockSpec((TM_MIX, TM_MIX), fixed)],
        out_specs=[pl.BlockSpec((TM_MIX, D_MODEL), tok), pl.BlockSpec((TM_MIX, D_MODEL), tok),
                   pl.BlockSpec((TM_MIX, META_LANES), tok), row(ROUTER_LANES)],
        scratch_shapes=[pltpu.VMEM((TM_MIX + 2 * CONV_HALO, D_CONV), F32),
                        pltpu.VMEM((TM_MIX, D_CONV), BF16),
                        pltpu.VMEM((1, ROUTER_LANES), F32)],
    )
    return pl.pallas_call(
        functools.partial(_mixout_kernel, n_first=tp // TM_MIX),
        grid_spec=grid_spec,
        out_shape=[jax.ShapeDtypeStruct((t, D_MODEL), F32), jax.ShapeDtypeStruct((t, D_MODEL), F32),
                   jax.ShapeDtypeStruct((t, META_LANES), F32), jax.ShapeDtypeStruct((1, ROUTER_LANES), F32)],
        compiler_params=_cparams(),
        name="mixout",
    )(first, last, xp, xs, hf, hb, o, u, u, u, cw, conv_b.reshape(1, D_CONV), ln_g.reshape(1, D_CONV),
      ln_b.reshape(1, D_CONV), w_out.astype(BF16), g_ffn.reshape(1, D_MODEL), wr, br, tri)


def _sc_mesh():
    return plsc.VectorSubcoreMesh(core_axis_name="c", subcore_axis_name="s")


def _sc_worker_base(per_worker):
    wid = lax.axis_index("s") * SC_CORES + lax.axis_index("c")
    return wid * per_worker


def _dispatch(hn, pos1, pos2, n_slots):
    t, d = hn.shape
    per_worker = t // SC_WORKERS
    assert per_worker * SC_WORKERS == t and per_worker % SC_ROWS == 0

    def body(h_hbm, p1_hbm, p2_hbm, out_hbm, i1_v, i2_v, rows_v, sem1, sem2):
        base0 = _sc_worker_base(per_worker)

        @pl.loop(0, per_worker // SC_ROWS)
        def _(c):
            base = pl.multiple_of(base0 + c * SC_ROWS, SC_ROWS)
            pltpu.sync_copy(p1_hbm.at[pl.ds(base, SC_ROWS)], i1_v)
            pltpu.sync_copy(p2_hbm.at[pl.ds(base, SC_ROWS)], i2_v)
            pltpu.sync_copy(h_hbm.at[pl.ds(base, SC_ROWS)], rows_v)
            c1 = pltpu.async_copy(rows_v, out_hbm.at[i1_v], sem1)
            c2 = pltpu.async_copy(rows_v, out_hbm.at[i2_v], sem2)
            c1.wait()
            c2.wait()

    return pl.kernel(
        body,
        out_type=jax.ShapeDtypeStruct((n_slots, d), hn.dtype),
        mesh=_sc_mesh(),
        scratch_types=[pltpu.VMEM((SC_ROWS,), jnp.int32), pltpu.VMEM((SC_ROWS,), jnp.int32),
                       pltpu.VMEM((SC_ROWS, d), hn.dtype), pltpu.SemaphoreType.DMA, pltpu.SemaphoreType.DMA],
        name="dispatch",
    )(hn, pos1, pos2)


def _collect(ys, pos1, pos2):
    t = pos1.shape[0]
    d = ys.shape[1]
    per_worker = t // SC_WORKERS
    assert per_worker * SC_WORKERS == t and per_worker % SC_ROWS == 0

    def body(ys_hbm, p1_hbm, p2_hbm, y1_hbm, y2_hbm, i_v, rows_v, sem):
        base0 = _sc_worker_base(per_worker)

        @pl.loop(0, per_worker // SC_ROWS)
        def _(c):
            base = pl.multiple_of(base0 + c * SC_ROWS, SC_ROWS)
            for p_hbm, y_hbm in ((p1_hbm, y1_hbm), (p2_hbm, y2_hbm)):
                pltpu.sync_copy(p_hbm.at[pl.ds(base, SC_ROWS)], i_v)
                pltpu.async_copy(ys_hbm.at[i_v], rows_v, sem).wait()
                pltpu.sync_copy(rows_v, y_hbm.at[pl.ds(base, SC_ROWS)])

    out = jax.ShapeDtypeStruct((t, d), ys.dtype)
    return pl.kernel(
        body,
        out_type=(out, out),
        mesh=_sc_mesh(),
        scratch_types=[pltpu.VMEM((SC_ROWS,), jnp.int32), pltpu.VMEM((SC_ROWS, d), ys.dtype),
                       pltpu.SemaphoreType.DMA],
        name="collect",
    )(ys, pos1, pos2)


def _experts_kernel(te_ref, rows_ref, hs_ref, wg_ref, wu_ref, wd_ref, ys_ref):
    i = pl.program_id(0)

    @pl.when(rows_ref[i] > 0)
    def _():
        x = hs_ref[...].astype(BF16)
        hg = jnp.dot(x, wg_ref[...], preferred_element_type=F32)
        hu = jnp.dot(x, wu_ref[...], preferred_element_type=F32)
        hid = (hg * jax.nn.sigmoid(hg) * hu).astype(BF16)
        ys_ref[...] = jnp.dot(hid, wd_ref[...], preferred_element_type=F32)

    @pl.when(rows_ref[i] == 0)
    def _():
        ys_ref[...] = jnp.zeros_like(ys_ref)


def _experts(hs, tile_expert, tile_rows, w_gate, w_up, w_down):
    n_tiles = tile_expert.shape[0]
    wsel = lambda i, te, rows: (te[i], 0, 0)
    slot = lambda i, te, rows: (i, 0)
    grid_spec = pltpu.PrefetchScalarGridSpec(
        num_scalar_prefetch=2,
        grid=(n_tiles,),
        in_specs=[pl.BlockSpec((TM_EXP, D_MODEL), slot),
                  pl.BlockSpec((None, D_MODEL, D_EXPERT), wsel),
                  pl.BlockSpec((None, D_MODEL, D_EXPERT), wsel),
                  pl.BlockSpec((None, D_EXPERT, D_MODEL), wsel)],
        out_specs=pl.BlockSpec((TM_EXP, D_MODEL), slot),
    )
    return pl.pallas_call(
        _experts_kernel,
        grid_spec=grid_spec,
        out_shape=jax.ShapeDtypeStruct((n_tiles * TM_EXP, D_MODEL), F32),
        compiler_params=_cparams(),
        name="experts",
    )(tile_expert, tile_rows, hs, w_gate.astype(BF16), w_up.astype(BF16), w_down.astype(BF16))


def _combine_kernel(x1_ref, meta_ref, y1_ref, y2_ref, gfin_ref, outp_ref, outs_ref, *, n_first):
    i = pl.program_id(0)
    meta = meta_ref[...]
    x2 = x1_ref[...] + meta[:, 2:3] * y1_ref[...] + meta[:, 3:4] * y2_ref[...]
    out = _rms(x2, gfin_ref[...])

    @pl.when(i < n_first)
    def _():
        outp_ref[...] = out

    @pl.when(i >= n_first)
    def _():
        outs_ref[...] = out


def _combine(x1, meta, y1, y2, g_final, tp):
    t = x1.shape[0]
    ts = t - tp
    assert tp % TM_OUT == 0 and ts % TM_OUT == 0
    tok = pl.BlockSpec((TM_OUT, D_MODEL), lambda i: (i, 0))
    outp_spec, outs_spec = _two_batch_specs((TM_OUT, D_MODEL), tp // TM_OUT, ts // TM_OUT)
    return pl.pallas_call(
        functools.partial(_combine_kernel, n_first=tp // TM_OUT),
        grid=(t // TM_OUT,),
        in_specs=[tok, pl.BlockSpec((TM_OUT, META_LANES), lambda i: (i, 0)), tok, tok,
                  pl.BlockSpec((1, D_MODEL), lambda i: (0, 0))],
        out_specs=[outp_spec, outs_spec],
        out_shape=[jax.ShapeDtypeStruct((tp, D_MODEL), F32), jax.ShapeDtypeStruct((ts, D_MODEL), F32)],
        compiler_params=_cparams(),
        name="combine",
    )(x1, meta, y1, y2, g_final.reshape(1, D_MODEL))


def _slot_layout(meta, counts_row, n_tiles):
    eid = meta[:, 0:2].astype(jnp.int32)
    rank = meta[:, 4:6].astype(jnp.int32)
    counts = counts_row[0, EXPERT_LANE0:EXPERT_LANE0 + N_EXPERTS].astype(jnp.int32)
    padded = (counts + TM_EXP - 1) // TM_EXP * TM_EXP
    ends = jnp.cumsum(padded)
    offs = ends - padded
    pos = offs[eid] + rank
    tile_start = jnp.arange(n_tiles, dtype=jnp.int32) * TM_EXP
    tile_expert = jnp.minimum(jnp.sum(tile_start[:, None] >= ends[None, :], axis=1), N_EXPERTS - 1)
    tile_expert = tile_expert.astype(jnp.int32)
    tile_rows = jnp.clip(counts[tile_expert] - (tile_start - offs[tile_expert]), 0, TM_EXP)
    tile_rows = jnp.where(tile_start < ends[-1], tile_rows, 0).astype(jnp.int32)
    return pos[:, 0], pos[:, 1], tile_expert, tile_rows


def kernel(x_prompt, x_sample, g_mix, w_in, b_gate, conv_w, conv_b, ln_g, ln_b, w_out, g_ffn,
           w_router_group, b_router_group, w_router_expert, b_router_expert, w_gate, w_up, w_down, g_final):
    assert g_mix.shape[0] == 1, "one layer"
    bp, lp, _ = x_prompt.shape
    bs, ls, _ = x_sample.shape
    seq_lens = [lp] * bp + [ls] * bs
    tp, ts = bp * lp, bs * ls
    t = tp + ts
    xp = x_prompt.reshape(tp, D_MODEL)
    xs = x_sample.reshape(ts, D_MODEL)

    q, v, o, u, kt, gr = _inproj(xp, xs, g_mix[0], w_in[0], b_gate[0])
    hf, hb = _mlstm(q, kt, v, gr, seq_lens)
    x1, hn, meta, counts = _mixout(xp, xs, hf, hb, o, u, seq_lens, conv_w[0], conv_b[0], ln_g[0], ln_b[0],
                                   w_out[0], g_ffn[0], w_router_group[0], b_router_group[0],
                                   w_router_expert[0], b_router_expert[0])
    n_tiles = (2 * t + N_EXPERTS * (TM_EXP - 1)) // TM_EXP + 1
    pos1, pos2, tile_expert, tile_rows = _slot_layout(meta, counts, n_tiles)
    hs = _dispatch(hn, pos1, pos2, n_tiles * TM_EXP)
    ys = _experts(hs, tile_expert, tile_rows, w_gate[0], w_up[0], w_down[0])
    y1, y2 = _collect(ys, pos1, pos2)
    outp, outs = _combine(x1, meta, y1, y2, g_final, tp)
    return outp.reshape(bp, lp, D_MODEL), outs.reshape(bs, ls, D_MODEL)
```

```python
import functools

import jax
import jax.numpy as jnp
from jax import lax
from jax.experimental import pallas as pl
from jax.experimental.pallas import tpu as pltpu
from jax.experimental.pallas import tpu_sc as plsc

F32 = jnp.float32
BF16 = jnp.bfloat16

D_MODEL = 1024
N_HEADS = 4
HEAD_DIM = 128
D_MLSTM = N_HEADS * HEAD_DIM
D_CONV = D_MODEL - D_MLSTM
CONV_WIDTH = 31
CONV_HALO = 16
N_DIR = 2
N_GROUPS = 4
EXPERTS_PER_GROUP = 4
N_EXPERTS = N_GROUPS * EXPERTS_PER_GROUP
D_EXPERT = 512
EPS = 1e-6
K_SCALE = HEAD_DIM ** -0.5

GATE_ROWS = 16
QROWS = 24
ROUTER_LANES = 128
EXPERT_LANE0 = N_GROUPS
META_LANES = 8

TM_IN = 512
CHUNK = 256
GP_CHUNKS = 8
TM_MIX = 512
CONV_ROWS = 128
TM_EXP = 512
TM_OUT = 256
VMEM_LIMIT = 48 * 1024 * 1024

SC_CORES = 2
SC_SUBCORES = 16
SC_WORKERS = SC_CORES * SC_SUBCORES
SC_ROWS = 128


def _cparams(n_axes=1):
    return pltpu.CompilerParams(dimension_semantics=("arbitrary",) * n_axes,
                                vmem_limit_bytes=VMEM_LIMIT)


def _nt_dot(a, b):
    return lax.dot_general(a, b, (((1,), (1,)), ((), ())), preferred_element_type=F32)


def _rms(x, g):
    return x * lax.rsqrt(jnp.mean(x * x, axis=-1, keepdims=True) + EPS) * g


def _pack_rows(x):
    n = x.shape[1] // 2
    hi = lax.bitcast_convert_type(x[:, :n].astype(jnp.bfloat16).astype(F32), jnp.int32)
    lo = lax.bitcast_convert_type(x[:, n:].astype(jnp.bfloat16).astype(F32), jnp.int32)
    return hi | lax.shift_right_logical(lo, 16)


def _unpack_rows(p):
    hi = lax.bitcast_convert_type(p & jnp.int32(-65536), F32)
    lo = lax.bitcast_convert_type(lax.shift_left(p, 16), F32)
    return jnp.concatenate([hi, lo], axis=1)


def _two_batch_specs(block, n_first, n_second):
    first = pl.BlockSpec(block, lambda i, *_: (jnp.minimum(i, n_first - 1), 0))
    second = pl.BlockSpec(block, lambda i, *_: (jnp.maximum(i - n_first, 0), 0))
    return first, second


def _inproj_kernel(xp_ref, xs_ref, g_ref, wq_ref, wv_ref, wo_ref, wa_ref, wb_ref, wkt_ref, wgt_ref, bg_ref,
                   q_ref, v_ref, o_ref, u_ref, kt_ref, gr_ref, *, n_first):
    x = jnp.where(pl.program_id(0) < n_first, xp_ref[...], xs_ref[...])
    xn = _rms(x, g_ref[...]).astype(BF16)
    q_ref[...] = jnp.dot(xn, wq_ref[...], preferred_element_type=F32).astype(BF16)
    v_ref[...] = jnp.dot(xn, wv_ref[...], preferred_element_type=F32).astype(BF16)
    o_ref[...] = jnp.dot(xn, wo_ref[...], preferred_element_type=F32).astype(BF16)
    a = jnp.dot(xn, wa_ref[...], preferred_element_type=F32)
    b = jnp.dot(xn, wb_ref[...], preferred_element_type=F32)
    u_ref[...] = (a * jax.nn.sigmoid(b)).astype(BF16)
    kt_ref[...] = (_nt_dot(wkt_ref[...], xn) * K_SCALE).astype(BF16)
    gr = _nt_dot(wgt_ref[...], xn) + bg_ref[...]
    for c in range(gr_ref.shape[0]):
        gr_ref[c] = gr[:, c * CHUNK:(c + 1) * CHUNK]


def _inproj(xp, xs, g_mix, w_in, b_gate):
    tp, ts = xp.shape[0], xs.shape[0]
    t = tp + ts
    assert tp % TM_IN == 0 and ts % TM_IN == 0
    off_k, off_v, off_o, off_g = D_MLSTM, 2 * D_MLSTM, 3 * D_MLSTM, 4 * D_MLSTM
    off_a = off_g + 2 * N_DIR * N_HEADS
    off_b = off_a + D_CONV
    wq = w_in[:, 0:off_k].astype(BF16)
    wkt = w_in[:, off_k:off_v].T.astype(BF16)
    wv = w_in[:, off_v:off_o].astype(BF16)
    wo = w_in[:, off_o:off_g].astype(BF16)
    wa = w_in[:, off_a:off_b].astype(BF16)
    wb = w_in[:, off_b:off_b + D_CONV].astype(BF16)
    wg = w_in[:, off_g:off_a].T.reshape(N_DIR, 2, N_HEADS, D_MODEL)
    wgt = jnp.zeros((N_DIR, 2, GATE_ROWS // 2, D_MODEL), F32).at[:, :, :N_HEADS].set(wg)
    wgt = wgt.reshape(N_DIR * GATE_ROWS, D_MODEL).astype(BF16)
    bg = jnp.zeros((N_DIR, 2, GATE_ROWS // 2), F32).at[:, :, :N_HEADS].set(
        b_gate.reshape(N_DIR, 2, N_HEADS)).reshape(N_DIR * GATE_ROWS, 1)

    tok = lambda i: (i, 0)
    fixed = lambda i: (0, 0)
    wspec = pl.BlockSpec((D_MODEL, D_MLSTM), fixed)
    xp_spec, xs_spec = _two_batch_specs((TM_IN, D_MODEL), tp // TM_IN, ts // TM_IN)
    cpt = TM_IN // CHUNK
    return pl.pallas_call(
        functools.partial(_inproj_kernel, n_first=tp // TM_IN),
        grid=(t // TM_IN,),
        in_specs=[xp_spec, xs_spec, pl.BlockSpec((1, D_MODEL), fixed),
                  wspec, wspec, wspec, wspec, wspec,
                  pl.BlockSpec((D_MLSTM, D_MODEL), fixed),
                  pl.BlockSpec((N_DIR * GATE_ROWS, D_MODEL), fixed),
                  pl.BlockSpec((N_DIR * GATE_ROWS, 1), fixed)],
        out_specs=[pl.BlockSpec((TM_IN, D_MLSTM), tok)] * 4 + [
            pl.BlockSpec((D_MLSTM, TM_IN), lambda i: (0, i)),
            pl.BlockSpec((cpt, N_DIR * GATE_ROWS, CHUNK), lambda i: (i, 0, 0))],
        out_shape=[jax.ShapeDtypeStruct((t, D_MLSTM), BF16)] * 4 + [
            jax.ShapeDtypeStruct((D_MLSTM, t), BF16),
            jax.ShapeDtypeStruct((t // CHUNK, N_DIR * GATE_ROWS, CHUNK), F32)],
        compiler_params=_cparams(),
        name="inproj",
    )(xp, xs, g_mix.reshape(1, D_MODEL), wq, wv, wo, wa, wb, wkt, wgt, bg)


def _log_sigmoid(x):
    return jnp.minimum(x, 0.0) - jnp.log1p(jnp.exp(-jnp.abs(x)))


def _gateprep_kernel(reset_ref, g_ref, rowq_ref, colq_ref, m_ref, *, rev):
    n, _, c = g_ref.shape
    step = pl.program_id(0)
    blk = pl.num_programs(0) - 1 - step if rev else step

    @pl.when(step == 0)
    def _():
        m_ref[...] = jnp.zeros_like(m_ref)

    ig = g_ref[:, 0:8, :]
    lf = _log_sigmoid(g_ref[:, 8:16, :])
    lane = lax.broadcasted_iota(jnp.int32, (n, 8, c), 2)

    def scan(x, op, ident):
        k = 1
        while k < c:
            if rev:
                shifted, valid = pltpu.roll(x, c - k, axis=2), lane < c - k
            else:
                shifted, valid = pltpu.roll(x, k, axis=2), lane >= k
            x = op(x, jnp.where(valid, shifted, ident))
            k *= 2
        return x

    bc = scan(lf, jnp.add, 0.0)
    a = ig - bc
    cm = scan(a, jnp.maximum, -jnp.inf)
    b_tot = jnp.sum(lf, axis=2, keepdims=True)
    a_max = jnp.max(a, axis=2, keepdims=True)

    m = m_ref[...]
    m_in = [None] * n
    for j in (range(n - 1, -1, -1) if rev else range(n)):
        m = jnp.where(reset_ref[blk * n + j] == 1, 0.0, m)
        m_in[j] = m
        m = b_tot[j] + jnp.maximum(m, a_max[j])
    m_ref[...] = m
    m_old = jnp.stack(m_in)

    mx = jnp.maximum(m_old, cm)
    mx_last = jnp.maximum(m_old, a_max)
    rowq_ref[:, 0:8, :] = a
    rowq_ref[:, 8:16, :] = jnp.exp(a - mx_last)
    rowq_ref[:, 16:24, :] = jnp.exp(m_old - mx_last)
    e1 = jnp.exp(m_old - mx)
    fl = jnp.exp(-(mx + bc))
    for j in range(n):
        colq_ref[j] = jnp.concatenate([mx[j], e1[j], fl[j]], axis=0).T


def _gateprep(gr, reset, rev):
    n_chunks = gr.shape[0]
    assert n_chunks % GP_CHUNKS == 0
    nb = n_chunks // GP_CHUNKS
    d = 1 if rev else 0
    bidx = (lambda s: nb - 1 - s) if rev else (lambda s: s)
    grid_spec = pltpu.PrefetchScalarGridSpec(
        num_scalar_prefetch=1,
        grid=(nb,),
        in_specs=[pl.BlockSpec((GP_CHUNKS, GATE_ROWS, CHUNK), lambda s, r: (bidx(s), d, 0))],
        out_specs=[pl.BlockSpec((GP_CHUNKS, QROWS, CHUNK), lambda s, r: (bidx(s), 0, 0)),
                   pl.BlockSpec((GP_CHUNKS, CHUNK, QROWS), lambda s, r: (bidx(s), 0, 0))],
        scratch_shapes=[pltpu.VMEM((8, CHUNK), F32)],
    )
    return pl.pallas_call(
        functools.partial(_gateprep_kernel, rev=rev),
        grid_spec=grid_spec,
        out_shape=[jax.ShapeDtypeStruct((n_chunks, QROWS, CHUNK), F32),
                   jax.ShapeDtypeStruct((n_chunks, CHUNK, QROWS), F32)],
        compiler_params=_cparams(),
        name="gateprep_bwd" if rev else "gateprep_fwd",
    )(reset, gr)


def _mlstm_kernel(rf_ref, rb_ref,
                  qf_ref, ktf_ref, vf_ref, rowf_ref, colf_ref,
                  qb_ref, ktb_ref, vb_ref, rowb_ref, colb_ref,
                  hf_ref, hb_ref, cst_ref):
    c = qf_ref.shape[0]
    step = pl.program_id(0)
    last = pl.num_programs(0) - 1
    row_i = lax.broadcasted_iota(jnp.int32, (c, c), 0)
    col_i = lax.broadcasted_iota(jnp.int32, (c, c), 1)
    ones = jnp.ones((c, HEAD_DIM), BF16)

    dirs = ((rf_ref[step], col_i <= row_i, qf_ref, ktf_ref, vf_ref, rowf_ref, colf_ref, hf_ref),
            (rb_ref[last - step], col_i >= row_i, qb_ref, ktb_ref, vb_ref, rowb_ref, colb_ref, hb_ref))
    for d, (reset, mask, q_ref, kt_ref, v_ref, row_ref, col_ref, h_ref) in enumerate(dirs):
        @pl.when(reset == 1)
        def _():
            cst_ref[d] = jnp.zeros(cst_ref.shape[1:], F32)

        rowq = row_ref[...]
        colq = col_ref[...]
        for h in range(N_HEADS):
            hs = slice(h * HEAD_DIM, (h + 1) * HEAD_DIM)
            qh = q_ref[:, hs]
            kth = kt_ref[hs, :]
            vext = jnp.concatenate([v_ref[:, hs], ones], axis=1)
            s = jnp.dot(qh, kth, preferred_element_type=F32)
            e = jnp.exp(jnp.where(mask, rowq[h:h + 1, :] - colq[:, h:h + 1], -jnp.inf))
            r1 = jnp.dot((s * e).astype(BF16), vext, preferred_element_type=F32)
            cst = cst_ref[d, h]
            r2 = jnp.dot(qh, cst.astype(BF16), preferred_element_type=F32)
            e1 = colq[:, 8 + h:9 + h]
            num = r1[:, :HEAD_DIM] + e1 * r2[:, :HEAD_DIM]
            den = r1[:, HEAD_DIM:] + e1 * r2[:, HEAD_DIM:]
            h_ref[:, hs] = num / jnp.maximum(jnp.abs(den), colq[:, 16 + h:17 + h])
            kw = (kth.astype(F32) * rowq[8 + h:9 + h, :]).astype(BF16)
            cst_ref[d, h] = rowq[16 + h:17 + h, 0:1] * cst + jnp.dot(kw, vext, preferred_element_type=F32)


def _mlstm(q, kt, v, gr, seq_lens):
    t = q.shape[0]
    n = t // CHUNK
    starts, ends, pos = [], [], 0
    for ln in seq_lens:
        assert ln % CHUNK == 0
        starts.append(pos // CHUNK)
        ends.append((pos + ln) // CHUNK - 1)
        pos += ln
    reset_f = jnp.zeros((n,), jnp.int32).at[jnp.array(starts)].set(1)
    reset_b = jnp.zeros((n,), jnp.int32).at[jnp.array(ends)].set(1)
    rowf, colf = _gateprep(gr, reset_f, rev=False)
    rowb, colb = _gateprep(gr, reset_b, rev=True)

    def specs(cidx):
        return [pl.BlockSpec((CHUNK, D_MLSTM), lambda s, rf, rb: (cidx(s), 0)),
                pl.BlockSpec((D_MLSTM, CHUNK), lambda s, rf, rb: (0, cidx(s))),
                pl.BlockSpec((CHUNK, D_MLSTM), lambda s, rf, rb: (cidx(s), 0)),
                pl.BlockSpec((None, QROWS, CHUNK), lambda s, rf, rb: (cidx(s), 0, 0)),
                pl.BlockSpec((None, CHUNK, QROWS), lambda s, rf, rb: (cidx(s), 0, 0))]

    fwd = lambda s: s
    bwd = lambda s: n - 1 - s
    grid_spec = pltpu.PrefetchScalarGridSpec(
        num_scalar_prefetch=2,
        grid=(n,),
        in_specs=specs(fwd) + specs(bwd),
        out_specs=[pl.BlockSpec((CHUNK, D_MLSTM), lambda s, rf, rb: (fwd(s), 0)),
                   pl.BlockSpec((CHUNK, D_MLSTM), lambda s, rf, rb: (bwd(s), 0))],
        scratch_shapes=[pltpu.VMEM((N_DIR, N_HEADS, HEAD_DIM, 2 * HEAD_DIM), F32)],
    )
    return pl.pallas_call(
        _mlstm_kernel,
        grid_spec=grid_spec,
        out_shape=[jax.ShapeDtypeStruct((t, D_MLSTM), F32)] * 2,
        compiler_params=_cparams(),
        name="mlstm",
    )(reset_f, reset_b, q, kt, v, rowf, colf, q, kt, v, rowb, colb)


def _mixout_kernel(first_ref, last_ref,
                   xp_ref, xs_ref, hf_ref, hb_ref, o_ref, u_ref, up_ref, un_ref, cw_ref, cb_ref, lng_ref, lnb_ref,
                   wout_ref, gffn_ref, wr_ref, br_ref, tri_ref,
                   x1_ref, hn_ref, meta_ref, metat_ref, cnt_ref,
                   win_ref, y_ref, run_ref, *, n_first):
    i = pl.program_id(0)
    tm = hf_ref.shape[0]

    @pl.when(i == 0)
    def _():
        run_ref[...] = jnp.zeros_like(run_ref)

    ym = (jax.nn.sigmoid(o_ref[...].astype(F32)) * (hf_ref[...] + hb_ref[...])).astype(BF16)
    x = jnp.where(i < n_first, xp_ref[...], xs_ref[...])
    x1m = x + jnp.dot(ym, wout_ref[:D_MLSTM, :], preferred_element_type=F32)

    win_ref[0:CONV_HALO, :] = jnp.where(first_ref[i] == 1, 0.0, up_ref[...].astype(F32))
    win_ref[CONV_HALO:CONV_HALO + tm, :] = u_ref[...].astype(F32)
    win_ref[CONV_HALO + tm:, :] = jnp.where(last_ref[i] == 1, 0.0, un_ref[...].astype(F32))

    off0 = CONV_HALO - CONV_WIDTH // 2
    for r0 in range(0, tm, CONV_ROWS):
        tiles = []
        for lt in range(D_CONV // 128):
            ls = slice(lt * 128, (lt + 1) * 128)
            acc = jnp.broadcast_to(cb_ref[:, ls], (CONV_ROWS, 128))
            for s in range(8):
                part = None
                for j in range(CONV_WIDTH):
                    if (off0 + j) % 8 != s:
                        continue
                    base = (off0 + j) // 8 * 8
                    term = win_ref[r0 + base:r0 + base + CONV_ROWS + 8, ls] * cw_ref[j:j + 1, ls]
                    part = term if part is None else part + term
                acc = acc + part[s:s + CONV_ROWS, :]
            tiles.append(acc)
        cv = jnp.concatenate(tiles, axis=1)
        xc = cv - jnp.mean(cv, axis=-1, keepdims=True)
        yc = xc * lax.rsqrt(jnp.mean(xc * xc, axis=-1, keepdims=True) + EPS) * lng_ref[...] + lnb_ref[...]
        y_ref[r0:r0 + CONV_ROWS, :] = (yc * jax.nn.sigmoid(yc)).astype(BF16)

    x1 = x1m + jnp.dot(y_ref[...], wout_ref[D_MLSTM:, :], preferred_element_type=F32)
    x1_ref[...] = x1
    hn = _rms(x1, gffn_ref[...])
    hn_ref[...] = _pack_rows(hn)

    logits = jnp.dot(hn.astype(BF16), wr_ref[...], preferred_element_type=F32) + br_ref[...]
    lane = lax.broadcasted_iota(jnp.int32, (tm, ROUTER_LANES), 1).astype(F32)
    neg = -jnp.inf
    no_lane = float(ROUTER_LANES)
    gl = jnp.where(lane < N_GROUPS, logits, neg)
    gmax = jnp.max(gl, axis=1, keepdims=True)
    p_top = 1.0 / jnp.sum(jnp.exp(gl - gmax), axis=1, keepdims=True)
    g_idx = jnp.min(jnp.where(gl == gmax, lane, no_lane), axis=1, keepdims=True)
    lo = EXPERT_LANE0 + EXPERTS_PER_GROUP * g_idx
    in_grp = (lane >= lo) & (lane < lo + EXPERTS_PER_GROUP)
    el = jnp.where(in_grp, logits, neg)
    ee = jnp.exp(el - jnp.max(el, axis=1, keepdims=True))
    pe = jnp.where(in_grp, ee / jnp.sum(ee, axis=1, keepdims=True), -1.0)
    v1 = jnp.max(pe, axis=1, keepdims=True)
    i1 = jnp.min(jnp.where(pe == v1, lane, no_lane), axis=1, keepdims=True)
    pe2 = jnp.where(lane == i1, -1.0, pe)
    v2 = jnp.max(pe2, axis=1, keepdims=True)
    i2 = jnp.min(jnp.where(pe2 == v2, lane, no_lane), axis=1, keepdims=True)
    wsum = v1 + v2
    gate1 = p_top * (v1 / wsum)
    gate2 = p_top * (v2 / wsum)

    oh1 = (lane == i1).astype(F32)
    oh2 = (lane == i2).astype(F32)
    c1 = jnp.dot(tri_ref[...], oh1.astype(BF16), preferred_element_type=F32)
    c2 = jnp.dot(tri_ref[...], oh2.astype(BF16), preferred_element_type=F32)
    run = run_ref[...]
    tot1 = jnp.sum(oh1, axis=0, keepdims=True)
    tot2 = jnp.sum(oh2, axis=0, keepdims=True)
    rank1 = jnp.sum(oh1 * (run + c1), axis=1, keepdims=True)
    rank2 = jnp.sum(oh2 * (run + tot1 + c2), axis=1, keepdims=True)
    run = run + tot1 + tot2
    run_ref[...] = run
    cnt_ref[...] = run

    ml = lax.broadcasted_iota(jnp.int32, (tm, META_LANES), 1)
    meta = jnp.where(ml == 0, i1 - EXPERT_LANE0, 0.0)
    meta = jnp.where(ml == 1, i2 - EXPERT_LANE0, meta)
    meta = jnp.where(ml == 2, gate1, meta)
    meta = jnp.where(ml == 3, gate2, meta)
    meta = jnp.where(ml == 4, rank1, meta)
    meta = jnp.where(ml == 5, rank2, meta)
    meta_ref[...] = meta
    metat_ref[...] = meta.T


def _mixout(xp, xs, hf, hb, o, u, seq_lens, conv_w, conv_b, ln_g, ln_b, w_out, g_ffn, w_rg, b_rg, w_re, b_re):
    tp, ts = xp.shape[0], xs.shape[0]
    t = tp + ts
    n_tiles = t // TM_MIX
    hpt = TM_MIX // CONV_HALO
    n_halo = t // CONV_HALO
    firsts, lasts, pos = [], [], 0
    for ln in seq_lens:
        assert ln % TM_MIX == 0
        firsts.append(pos // TM_MIX)
        lasts.append((pos + ln) // TM_MIX - 1)
        pos += ln
    first = jnp.zeros((n_tiles,), jnp.int32).at[jnp.array(firsts)].set(1)
    last = jnp.zeros((n_tiles,), jnp.int32).at[jnp.array(lasts)].set(1)

    cw = jnp.zeros((32, D_CONV), F32).at[:CONV_WIDTH].set(conv_w.reshape(CONV_WIDTH, D_CONV))
    wr = jnp.zeros((D_MODEL, ROUTER_LANES), F32)
    wr = wr.at[:, :N_GROUPS].set(w_rg).at[:, EXPERT_LANE0:EXPERT_LANE0 + N_EXPERTS].set(w_re).astype(BF16)
    br = jnp.zeros((1, ROUTER_LANES), F32)
    br = br.at[0, :N_GROUPS].set(b_rg).at[0, EXPERT_LANE0:EXPERT_LANE0 + N_EXPERTS].set(b_re)
    tri = (lax.broadcasted_iota(jnp.int32, (TM_MIX, TM_MIX), 0)
           > lax.broadcasted_iota(jnp.int32, (TM_MIX, TM_MIX), 1)).astype(BF16)

    tok = lambda i, f, l: (i, 0)
    fixed = lambda i, f, l: (0, 0)
    row = lambda n: pl.BlockSpec((1, n), fixed)
    xp_spec, xs_spec = _two_batch_specs((TM_MIX, D_MODEL), tp // TM_MIX, ts // TM_MIX)
    grid_spec = pltpu.PrefetchScalarGridSpec(
        num_scalar_prefetch=2,
        grid=(n_tiles,),
        in_specs=[xp_spec, xs_spec,
                  pl.BlockSpec((TM_MIX, D_MLSTM), tok), pl.BlockSpec((TM_MIX, D_MLSTM), tok),
                  pl.BlockSpec((TM_MIX, D_MLSTM), tok), pl.BlockSpec((TM_MIX, D_CONV), tok),
                  pl.BlockSpec((CONV_HALO, D_CONV), lambda i, f, l: (jnp.maximum(i * hpt - 1, 0), 0)),
                  pl.BlockSpec((CONV_HALO, D_CONV), lambda i, f, l: (jnp.minimum((i + 1) * hpt, n_halo - 1), 0)),
                  pl.BlockSpec((32, D_CONV), fixed), row(D_CONV), row(D_CONV), row(D_CONV),
                  pl.BlockSpec((D_MODEL, D_MODEL), fixed), row(D_MODEL),
                  pl.BlockSpec((D_MODEL, ROUTER_LANES), fixed), row(ROUTER_LANES),
                  pl.BlockSpec((TM_MIX, TM_MIX), fixed)],
        out_specs=[pl.BlockSpec((TM_MIX, D_MODEL), tok), pl.BlockSpec((TM_MIX, D_MODEL // 2), tok),
                   pl.BlockSpec((TM_MIX, META_LANES), tok),
                   pl.BlockSpec((META_LANES, TM_MIX), lambda i, f, l: (0, i)), row(ROUTER_LANES)],
        scratch_shapes=[pltpu.VMEM((TM_MIX + 2 * CONV_HALO, D_CONV), F32),
                        pltpu.VMEM((TM_MIX, D_CONV), BF16),
                        pltpu.VMEM((1, ROUTER_LANES), F32)],
    )
    return pl.pallas_call(
        functools.partial(_mixout_kernel, n_first=tp // TM_MIX),
        grid_spec=grid_spec,
        out_shape=[jax.ShapeDtypeStruct((t, D_MODEL), F32), jax.ShapeDtypeStruct((t, D_MODEL // 2), jnp.int32),
                   jax.ShapeDtypeStruct((t, META_LANES), F32), jax.ShapeDtypeStruct((META_LANES, t), F32),
                   jax.ShapeDtypeStruct((1, ROUTER_LANES), F32)],
        compiler_params=_cparams(),
        name="mixout",
    )(first, last, xp, xs, hf, hb, o, u, u, u, cw, conv_b.reshape(1, D_CONV), ln_g.reshape(1, D_CONV),
      ln_b.reshape(1, D_CONV), w_out.astype(BF16), g_ffn.reshape(1, D_MODEL), wr, br, tri)


def _sc_mesh():
    return plsc.VectorSubcoreMesh(core_axis_name="c", subcore_axis_name="s")


def _sc_worker_base(per_worker):
    wid = lax.axis_index("s") * SC_CORES + lax.axis_index("c")
    return wid * per_worker


def _dispatch(hn, pos1, pos2, n_slots):
    t, d = hn.shape
    per_worker = t // SC_WORKERS
    assert per_worker * SC_WORKERS == t and per_worker % SC_ROWS == 0

    def body(h_hbm, p1_hbm, p2_hbm, out_hbm, i1_v, i2_v, rows_v, sem1, sem2):
        base0 = _sc_worker_base(per_worker)

        @pl.loop(0, per_worker // SC_ROWS)
        def _(c):
            base = pl.multiple_of(base0 + c * SC_ROWS, SC_ROWS)
            pltpu.sync_copy(p1_hbm.at[pl.ds(base, SC_ROWS)], i1_v)
            pltpu.sync_copy(p2_hbm.at[pl.ds(base, SC_ROWS)], i2_v)
            pltpu.sync_copy(h_hbm.at[pl.ds(base, SC_ROWS)], rows_v)
            c1 = pltpu.async_copy(rows_v, out_hbm.at[i1_v], sem1)
            c2 = pltpu.async_copy(rows_v, out_hbm.at[i2_v], sem2)
            c1.wait()
            c2.wait()

    return pl.kernel(
        body,
        out_type=jax.ShapeDtypeStruct((n_slots, d), hn.dtype),
        mesh=_sc_mesh(),
        scratch_types=[pltpu.VMEM((SC_ROWS,), jnp.int32), pltpu.VMEM((SC_ROWS,), jnp.int32),
                       pltpu.VMEM((SC_ROWS, d), hn.dtype), pltpu.SemaphoreType.DMA, pltpu.SemaphoreType.DMA],
        name="dispatch",
    )(hn, pos1, pos2)


def _collect(ys, pos1, pos2):
    t = pos1.shape[0]
    d = ys.shape[1]
    per_worker = t // SC_WORKERS
    assert per_worker * SC_WORKERS == t and per_worker % SC_ROWS == 0

    def body(ys_hbm, p1_hbm, p2_hbm, y1_hbm, y2_hbm, i_v, rows_v, sem):
        base0 = _sc_worker_base(per_worker)

        @pl.loop(0, per_worker // SC_ROWS)
        def _(c):
            base = pl.multiple_of(base0 + c * SC_ROWS, SC_ROWS)
            for p_hbm, y_hbm in ((p1_hbm, y1_hbm), (p2_hbm, y2_hbm)):
                pltpu.sync_copy(p_hbm.at[pl.ds(base, SC_ROWS)], i_v)
                pltpu.async_copy(ys_hbm.at[i_v], rows_v, sem).wait()
                pltpu.sync_copy(rows_v, y_hbm.at[pl.ds(base, SC_ROWS)])

    out = jax.ShapeDtypeStruct((t, d), ys.dtype)
    return pl.kernel(
        body,
        out_type=(out, out),
        mesh=_sc_mesh(),
        scratch_types=[pltpu.VMEM((SC_ROWS,), jnp.int32), pltpu.VMEM((SC_ROWS, d), ys.dtype),
                       pltpu.SemaphoreType.DMA],
        name="collect",
    )(ys, pos1, pos2)


def _experts_kernel(te_ref, rows_ref, hs_ref, wg_ref, wu_ref, wd_ref, ys_ref):
    i = pl.program_id(0)

    @pl.when(rows_ref[i] > 0)
    def _():
        x = _unpack_rows(hs_ref[...]).astype(BF16)
        hg = jnp.dot(x, wg_ref[...], preferred_element_type=F32)
        hu = jnp.dot(x, wu_ref[...], preferred_element_type=F32)
        hid = (hg * jax.nn.sigmoid(hg) * hu).astype(BF16)
        ys_ref[...] = _pack_rows(jnp.dot(hid, wd_ref[...], preferred_element_type=F32))

    @pl.when(rows_ref[i] == 0)
    def _():
        ys_ref[...] = jnp.zeros_like(ys_ref)


def _experts(hs, tile_expert, tile_rows, w_gate, w_up, w_down):
    n_tiles = tile_expert.shape[0]
    wsel = lambda i, te, rows: (te[i], 0, 0)
    slot = lambda i, te, rows: (i, 0)
    grid_spec = pltpu.PrefetchScalarGridSpec(
        num_scalar_prefetch=2,
        grid=(n_tiles,),
        in_specs=[pl.BlockSpec((TM_EXP, D_MODEL // 2), slot),
                  pl.BlockSpec((None, D_MODEL, D_EXPERT), wsel),
                  pl.BlockSpec((None, D_MODEL, D_EXPERT), wsel),
                  pl.BlockSpec((None, D_EXPERT, D_MODEL), wsel)],
        out_specs=pl.BlockSpec((TM_EXP, D_MODEL // 2), slot),
    )
    return pl.pallas_call(
        _experts_kernel,
        grid_spec=grid_spec,
        out_shape=jax.ShapeDtypeStruct((n_tiles * TM_EXP, D_MODEL // 2), jnp.int32),
        compiler_params=_cparams(),
        name="experts",
    )(tile_expert, tile_rows, hs, w_gate.astype(BF16), w_up.astype(BF16), w_down.astype(BF16))


def _combine_kernel(x1_ref, meta_ref, y1_ref, y2_ref, gfin_ref, outp_ref, outs_ref, *, n_first):
    i = pl.program_id(0)
    meta = meta_ref[...]
    x2 = x1_ref[...] + meta[:, 2:3] * _unpack_rows(y1_ref[...]) + meta[:, 3:4] * _unpack_rows(y2_ref[...])
    out = _rms(x2, gfin_ref[...])

    @pl.when(i < n_first)
    def _():
        outp_ref[...] = out

    @pl.when(i >= n_first)
    def _():
        outs_ref[...] = out


def _combine(x1, meta, y1, y2, g_final, tp):
    t = x1.shape[0]
    ts = t - tp
    assert tp % TM_OUT == 0 and ts % TM_OUT == 0
    tok = pl.BlockSpec((TM_OUT, D_MODEL), lambda i: (i, 0))
    packed = pl.BlockSpec((TM_OUT, D_MODEL // 2), lambda i: (i, 0))
    outp_spec, outs_spec = _two_batch_specs((TM_OUT, D_MODEL), tp // TM_OUT, ts // TM_OUT)
    return pl.pallas_call(
        functools.partial(_combine_kernel, n_first=tp // TM_OUT),
        grid=(t // TM_OUT,),
        in_specs=[tok, pl.BlockSpec((TM_OUT, META_LANES), lambda i: (i, 0)), packed, packed,
                  pl.BlockSpec((1, D_MODEL), lambda i: (0, 0))],
        out_specs=[outp_spec, outs_spec],
        out_shape=[jax.ShapeDtypeStruct((tp, D_MODEL), F32), jax.ShapeDtypeStruct((ts, D_MODEL), F32)],
        compiler_params=_cparams(),
        name="combine",
    )(x1, meta, y1, y2, g_final.reshape(1, D_MODEL))


def _slot_layout(metat, counts_row, n_tiles):
    eid = metat[0:2].astype(jnp.int32)
    rank = metat[4:6].astype(jnp.int32)
    counts = counts_row[0, EXPERT_LANE0:EXPERT_LANE0 + N_EXPERTS].astype(jnp.int32)
    padded = (counts + TM_EXP - 1) // TM_EXP * TM_EXP
    ends = jnp.cumsum(padded)
    offs = ends - padded
    pos = offs[eid] + rank
    tile_start = jnp.arange(n_tiles, dtype=jnp.int32) * TM_EXP
    tile_expert = jnp.minimum(jnp.sum(tile_start[:, None] >= ends[None, :], axis=1), N_EXPERTS - 1)
    tile_expert = tile_expert.astype(jnp.int32)
    tile_rows = jnp.clip(counts[tile_expert] - (tile_start - offs[tile_expert]), 0, TM_EXP)
    tile_rows = jnp.where(tile_start < ends[-1], tile_rows, 0).astype(jnp.int32)
    return pos[0], pos[1], tile_expert, tile_rows


def kernel(x_prompt, x_sample, g_mix, w_in, b_gate, conv_w, conv_b, ln_g, ln_b, w_out, g_ffn,
           w_router_group, b_router_group, w_router_expert, b_router_expert, w_gate, w_up, w_down, g_final):
    assert g_mix.shape[0] == 1, "one layer"
    bp, lp, _ = x_prompt.shape
    bs, ls, _ = x_sample.shape
    seq_lens = [lp] * bp + [ls] * bs
    tp, ts = bp * lp, bs * ls
    t = tp + ts
    xp = x_prompt.reshape(tp, D_MODEL)
    xs = x_sample.reshape(ts, D_MODEL)

    q, v, o, u, kt, gr = _inproj(xp, xs, g_mix[0], w_in[0], b_gate[0])
    hf, hb = _mlstm(q, kt, v, gr, seq_lens)
    x1, hn, meta, metat, counts = _mixout(xp, xs, hf, hb, o, u, seq_lens, conv_w[0], conv_b[0], ln_g[0], ln_b[0],
                                   w_out[0], g_ffn[0], w_router_group[0], b_router_group[0],
                                   w_router_expert[0], b_router_expert[0])
    n_tiles = (2 * t + N_EXPERTS * (TM_EXP - 1)) // TM_EXP + 1
    pos1, pos2, tile_expert, tile_rows = _slot_layout(metat, counts, n_tiles)
    hs = _dispatch(hn, pos1, pos2, n_tiles * TM_EXP)
    ys = _experts(hs, tile_expert, tile_rows, w_gate[0], w_up[0], w_down[0])
    y1, y2 = _collect(ys, pos1, pos2)
    outp, outs = _combine(x1, meta, y1, y2, g_final, tp)
    return outp.reshape(bp, lp, D_MODEL), outs.reshape(bs, ls, D_MODEL)
```

```python
import functools

import jax
import jax.numpy as jnp
from jax import lax
from jax.experimental import pallas as pl
from jax.experimental.pallas import tpu as pltpu
from jax.experimental.pallas import tpu_sc as plsc

F32 = jnp.float32
BF16 = jnp.bfloat16

D_MODEL = 1024
N_HEADS = 4
HEAD_DIM = 128
D_MLSTM = N_HEADS * HEAD_DIM
D_CONV = D_MODEL - D_MLSTM
CONV_WIDTH = 31
CONV_HALO = 16
N_DIR = 2
N_GROUPS = 4
EXPERTS_PER_GROUP = 4
N_EXPERTS = N_GROUPS * EXPERTS_PER_GROUP
D_EXPERT = 512
EPS = 1e-6
K_SCALE = HEAD_DIM ** -0.5

GATE_ROWS = 16
QROWS = 24
ROUTER_LANES = 128
EXPERT_LANE0 = N_GROUPS
META_LANES = 8

TM_IN = 512
CHUNK = 256
GP_CHUNKS = 8
TM_MIX = 512
CONV_ROWS = 128
TM_EXP = 512
TM_OUT = 256
VMEM_LIMIT = 48 * 1024 * 1024

SC_CORES = 2
SC_SUBCORES = 16
SC_WORKERS = SC_CORES * SC_SUBCORES
SC_ROWS = 128


def _cparams(n_axes=1):
    return pltpu.CompilerParams(dimension_semantics=("arbitrary",) * n_axes,
                                vmem_limit_bytes=VMEM_LIMIT)


def _nt_dot(a, b):
    return lax.dot_general(a, b, (((1,), (1,)), ((), ())), preferred_element_type=F32)


def _rms(x, g):
    return x * lax.rsqrt(jnp.mean(x * x, axis=-1, keepdims=True) + EPS) * g


def _pack_rows(x):
    n = x.shape[1] // 2
    hi = lax.bitcast_convert_type(x[:, :n].astype(jnp.bfloat16).astype(F32), jnp.int32)
    lo = lax.bitcast_convert_type(x[:, n:].astype(jnp.bfloat16).astype(F32), jnp.int32)
    return hi | lax.shift_right_logical(lo, 16)


def _unpack_rows(p):
    hi = lax.bitcast_convert_type(p & jnp.int32(-65536), F32)
    lo = lax.bitcast_convert_type(lax.shift_left(p, 16), F32)
    return jnp.concatenate([hi, lo], axis=1)


def _two_batch_specs(block, n_first, n_second):
    first = pl.BlockSpec(block, lambda i, *_: (jnp.minimum(i, n_first - 1), 0))
    second = pl.BlockSpec(block, lambda i, *_: (jnp.maximum(i - n_first, 0), 0))
    return first, second


def _inproj_kernel(xp_ref, xs_ref, g_ref, wq_ref, wv_ref, wo_ref, wa_ref, wb_ref, wkt_ref, wgt_ref, bg_ref,
                   q_ref, v_ref, o_ref, u_ref, kt_ref, gr_ref, *, n_first):
    x = jnp.where(pl.program_id(0) < n_first, xp_ref[...], xs_ref[...])
    xn = _rms(x, g_ref[...]).astype(BF16)
    q_ref[...] = jnp.dot(xn, wq_ref[...], preferred_element_type=F32).astype(BF16)
    v_ref[...] = jnp.dot(xn, wv_ref[...], preferred_element_type=F32).astype(BF16)
    o_ref[...] = jnp.dot(xn, wo_ref[...], preferred_element_type=F32).astype(BF16)
    a = jnp.dot(xn, wa_ref[...], preferred_element_type=F32)
    b = jnp.dot(xn, wb_ref[...], preferred_element_type=F32)
    u_ref[...] = (a * jax.nn.sigmoid(b)).astype(BF16)
    kt_ref[...] = (_nt_dot(wkt_ref[...], xn) * K_SCALE).astype(BF16)
    gr = _nt_dot(wgt_ref[...], xn) + bg_ref[...]
    for c in range(gr_ref.shape[0]):
        gr_ref[c] = gr[:, c * CHUNK:(c + 1) * CHUNK]


def _inproj(xp, xs, g_mix, w_in, b_gate):
    tp, ts = xp.shape[0], xs.shape[0]
    t = tp + ts
    assert tp % TM_IN == 0 and ts % TM_IN == 0
    off_k, off_v, off_o, off_g = D_MLSTM, 2 * D_MLSTM, 3 * D_MLSTM, 4 * D_MLSTM
    off_a = off_g + 2 * N_DIR * N_HEADS
    off_b = off_a + D_CONV
    wq = w_in[:, 0:off_k].astype(BF16)
    wkt = w_in[:, off_k:off_v].T.astype(BF16)
    wv = w_in[:, off_v:off_o].astype(BF16)
    wo = w_in[:, off_o:off_g].astype(BF16)
    wa = w_in[:, off_a:off_b].astype(BF16)
    wb = w_in[:, off_b:off_b + D_CONV].astype(BF16)
    wg = w_in[:, off_g:off_a].T.reshape(N_DIR, 2, N_HEADS, D_MODEL)
    wgt = jnp.zeros((N_DIR, 2, GATE_ROWS // 2, D_MODEL), F32).at[:, :, :N_HEADS].set(wg)
    wgt = wgt.reshape(N_DIR * GATE_ROWS, D_MODEL).astype(BF16)
    bg = jnp.zeros((N_DIR, 2, GATE_ROWS // 2), F32).at[:, :, :N_HEADS].set(
        b_gate.reshape(N_DIR, 2, N_HEADS)).reshape(N_DIR * GATE_ROWS, 1)

    tok = lambda i: (i, 0)
    fixed = lambda i: (0, 0)
    wspec = pl.BlockSpec((D_MODEL, D_MLSTM), fixed)
    xp_spec, xs_spec = _two_batch_specs((TM_IN, D_MODEL), tp // TM_IN, ts // TM_IN)
    cpt = TM_IN // CHUNK
    return pl.pallas_call(
        functools.partial(_inproj_kernel, n_first=tp // TM_IN),
        grid=(t // TM_IN,),
        in_specs=[xp_spec, xs_spec, pl.BlockSpec((1, D_MODEL), fixed),
                  wspec, wspec, wspec, wspec, wspec,
                  pl.BlockSpec((D_MLSTM, D_MODEL), fixed),
                  pl.BlockSpec((N_DIR * GATE_ROWS, D_MODEL), fixed),
                  pl.BlockSpec((N_DIR * GATE_ROWS, 1), fixed)],
        out_specs=[pl.BlockSpec((TM_IN, D_MLSTM), tok)] * 4 + [
            pl.BlockSpec((D_MLSTM, TM_IN), lambda i: (0, i)),
            pl.BlockSpec((cpt, N_DIR * GATE_ROWS, CHUNK), lambda i: (i, 0, 0))],
        out_shape=[jax.ShapeDtypeStruct((t, D_MLSTM), BF16)] * 4 + [
            jax.ShapeDtypeStruct((D_MLSTM, t), BF16),
            jax.ShapeDtypeStruct((t // CHUNK, N_DIR * GATE_ROWS, CHUNK), F32)],
        compiler_params=_cparams(),
        name="inproj",
    )(xp, xs, g_mix.reshape(1, D_MODEL), wq, wv, wo, wa, wb, wkt, wgt, bg)


def _log_sigmoid(x):
    return jnp.minimum(x, 0.0) - jnp.log1p(jnp.exp(-jnp.abs(x)))


def _gateprep_kernel(reset_ref, g_ref, rowq_ref, colq_ref, m_ref, *, rev):
    n, _, c = g_ref.shape
    step = pl.program_id(0)
    blk = pl.num_programs(0) - 1 - step if rev else step

    @pl.when(step == 0)
    def _():
        m_ref[...] = jnp.zeros_like(m_ref)

    ig = g_ref[:, 0:8, :]
    lf = _log_sigmoid(g_ref[:, 8:16, :])
    lane = lax.broadcasted_iota(jnp.int32, (n, 8, c), 2)

    def scan(x, op, ident):
        k = 1
        while k < c:
            if rev:
                shifted, valid = pltpu.roll(x, c - k, axis=2), lane < c - k
            else:
                shifted, valid = pltpu.roll(x, k, axis=2), lane >= k
            x = op(x, jnp.where(valid, shifted, ident))
            k *= 2
        return x

    bc = scan(lf, jnp.add, 0.0)
    a = ig - bc
    cm = scan(a, jnp.maximum, -jnp.inf)
    b_tot = jnp.sum(lf, axis=2, keepdims=True)
    a_max = jnp.max(a, axis=2, keepdims=True)

    m = m_ref[...]
    m_in = [None] * n
    for j in (range(n - 1, -1, -1) if rev else range(n)):
        m = jnp.where(reset_ref[blk * n + j] == 1, 0.0, m)
        m_in[j] = m
        m = b_tot[j] + jnp.maximum(m, a_max[j])
    m_ref[...] = m
    m_old = jnp.stack(m_in)

    mx = jnp.maximum(m_old, cm)
    mx_last = jnp.maximum(m_old, a_max)
    rowq_ref[:, 0:8, :] = a
    rowq_ref[:, 8:16, :] = jnp.exp(a - mx_last)
    rowq_ref[:, 16:24, :] = jnp.exp(m_old - mx_last)
    e1 = jnp.exp(m_old - mx)
    fl = jnp.exp(-(mx + bc))
    for j in range(n):
        colq_ref[j] = jnp.concatenate([mx[j], e1[j], fl[j]], axis=0).T


def _gateprep(gr, reset, rev):
    n_chunks = gr.shape[0]
    assert n_chunks % GP_CHUNKS == 0
    nb = n_chunks // GP_CHUNKS
    d = 1 if rev else 0
    bidx = (lambda s: nb - 1 - s) if rev else (lambda s: s)
    grid_spec = pltpu.PrefetchScalarGridSpec(
        num_scalar_prefetch=1,
        grid=(nb,),
        in_specs=[pl.BlockSpec((GP_CHUNKS, GATE_ROWS, CHUNK), lambda s, r: (bidx(s), d, 0))],
        out_specs=[pl.BlockSpec((GP_CHUNKS, QROWS, CHUNK), lambda s, r: (bidx(s), 0, 0)),
                   pl.BlockSpec((GP_CHUNKS, CHUNK, QROWS), lambda s, r: (bidx(s), 0, 0))],
        scratch_shapes=[pltpu.VMEM((8, CHUNK), F32)],
    )
    return pl.pallas_call(
        functools.partial(_gateprep_kernel, rev=rev),
        grid_spec=grid_spec,
        out_shape=[jax.ShapeDtypeStruct((n_chunks, QROWS, CHUNK), F32),
                   jax.ShapeDtypeStruct((n_chunks, CHUNK, QROWS), F32)],
        compiler_params=_cparams(),
        name="gateprep_bwd" if rev else "gateprep_fwd",
    )(reset, gr)


def _mlstm_kernel(rf_ref, rb_ref,
                  qf_ref, ktf_ref, vf_ref, rowf_ref, colf_ref,
                  qb_ref, ktb_ref, vb_ref, rowb_ref, colb_ref,
                  hf_ref, hb_ref, cst_ref):
    c = qf_ref.shape[0]
    step = pl.program_id(0)
    last = pl.num_programs(0) - 1
    row_i = lax.broadcasted_iota(jnp.int32, (c, c), 0)
    col_i = lax.broadcasted_iota(jnp.int32, (c, c), 1)
    ones = jnp.ones((c, HEAD_DIM), BF16)

    dirs = ((rf_ref[step], col_i <= row_i, qf_ref, ktf_ref, vf_ref, rowf_ref, colf_ref, hf_ref),
            (rb_ref[last - step], col_i >= row_i, qb_ref, ktb_ref, vb_ref, rowb_ref, colb_ref, hb_ref))
    for d, (reset, mask, q_ref, kt_ref, v_ref, row_ref, col_ref, h_ref) in enumerate(dirs):
        @pl.when(reset == 1)
        def _():
            cst_ref[d] = jnp.zeros(cst_ref.shape[1:], F32)

        rowq = row_ref[...]
        colq = col_ref[...]
        for h in range(N_HEADS):
            hs = slice(h * HEAD_DIM, (h + 1) * HEAD_DIM)
            qh = q_ref[:, hs]
            kth = kt_ref[hs, :]
            vext = jnp.concatenate([v_ref[:, hs], ones], axis=1)
            s = jnp.dot(qh, kth, preferred_element_type=F32)
            e = jnp.exp(jnp.where(mask, rowq[h:h + 1, :] - colq[:, h:h + 1], -jnp.inf))
            r1 = jnp.dot((s * e).astype(BF16), vext, preferred_element_type=F32)
            cst = cst_ref[d, h]
            r2 = jnp.dot(qh, cst.astype(BF16), preferred_element_type=F32)
            e1 = colq[:, 8 + h:9 + h]
            num = r1[:, :HEAD_DIM] + e1 * r2[:, :HEAD_DIM]
            den = r1[:, HEAD_DIM:] + e1 * r2[:, HEAD_DIM:]
            h_ref[:, hs] = num / jnp.maximum(jnp.abs(den), colq[:, 16 + h:17 + h])
            kw = (kth.astype(F32) * rowq[8 + h:9 + h, :]).astype(BF16)
            cst_ref[d, h] = rowq[16 + h:17 + h, 0:1] * cst + jnp.dot(kw, vext, preferred_element_type=F32)


def _mlstm(q, kt, v, gr, seq_lens):
    t = q.shape[0]
    n = t // CHUNK
    starts, ends, pos = [], [], 0
    for ln in seq_lens:
        assert ln % CHUNK == 0
        starts.append(pos // CHUNK)
        ends.append((pos + ln) // CHUNK - 1)
        pos += ln
    reset_f = jnp.zeros((n,), jnp.int32).at[jnp.array(starts)].set(1)
    reset_b = jnp.zeros((n,), jnp.int32).at[jnp.array(ends)].set(1)
    rowf, colf = _gateprep(gr, reset_f, rev=False)
    rowb, colb = _gateprep(gr, reset_b, rev=True)

    def specs(cidx):
        return [pl.BlockSpec((CHUNK, D_MLSTM), lambda s, rf, rb: (cidx(s), 0)),
                pl.BlockSpec((D_MLSTM, CHUNK), lambda s, rf, rb: (0, cidx(s))),
                pl.BlockSpec((CHUNK, D_MLSTM), lambda s, rf, rb: (cidx(s), 0)),
                pl.BlockSpec((None, QROWS, CHUNK), lambda s, rf, rb: (cidx(s), 0, 0)),
                pl.BlockSpec((None, CHUNK, QROWS), lambda s, rf, rb: (cidx(s), 0, 0))]

    fwd = lambda s: s
    bwd = lambda s: n - 1 - s
    grid_spec = pltpu.PrefetchScalarGridSpec(
        num_scalar_prefetch=2,
        grid=(n,),
        in_specs=specs(fwd) + specs(bwd),
        out_specs=[pl.BlockSpec((CHUNK, D_MLSTM), lambda s, rf, rb: (fwd(s), 0)),
                   pl.BlockSpec((CHUNK, D_MLSTM), lambda s, rf, rb: (bwd(s), 0))],
        scratch_shapes=[pltpu.VMEM((N_DIR, N_HEADS, HEAD_DIM, 2 * HEAD_DIM), F32)],
    )
    return pl.pallas_call(
        _mlstm_kernel,
        grid_spec=grid_spec,
        out_shape=[jax.ShapeDtypeStruct((t, D_MLSTM), F32)] * 2,
        compiler_params=_cparams(),
        name="mlstm",
    )(reset_f, reset_b, q, kt, v, rowf, colf, q, kt, v, rowb, colb)


def _mixout_kernel(first_ref, last_ref,
                   xp_ref, xs_ref, hf_ref, hb_ref, o_ref, u_ref, up_ref, un_ref, cw_ref, cb_ref, lng_ref, lnb_ref,
                   wout_ref, gffn_ref, wr_ref, br_ref, tri_ref,
                   x1_ref, hn_ref, meta_ref, metat_ref, cnt_ref,
                   win_ref, y_ref, run_ref, *, n_first):
    i = pl.program_id(0)
    tm = hf_ref.shape[0]

    @pl.when(i == 0)
    def _():
        run_ref[...] = jnp.zeros_like(run_ref)

    ym = (jax.nn.sigmoid(o_ref[...].astype(F32)) * (hf_ref[...] + hb_ref[...])).astype(BF16)
    x = jnp.where(i < n_first, xp_ref[...], xs_ref[...])
    x1m = x + jnp.dot(ym, wout_ref[:D_MLSTM, :], preferred_element_type=F32)

    win_ref[0:CONV_HALO, :] = jnp.where(first_ref[i] == 1, 0.0, up_ref[...].astype(F32))
    win_ref[CONV_HALO:CONV_HALO + tm, :] = u_ref[...].astype(F32)
    win_ref[CONV_HALO + tm:, :] = jnp.where(last_ref[i] == 1, 0.0, un_ref[...].astype(F32))

    off0 = CONV_HALO - CONV_WIDTH // 2
    for r0 in range(0, tm, CONV_ROWS):
        tiles = []
        for lt in range(D_CONV // 128):
            ls = slice(lt * 128, (lt + 1) * 128)
            acc = jnp.broadcast_to(cb_ref[:, ls], (CONV_ROWS, 128))
            for s in range(8):
                part = None
                for j in range(CONV_WIDTH):
                    if (off0 + j) % 8 != s:
                        continue
                    base = (off0 + j) // 8 * 8
                    term = win_ref[r0 + base:r0 + base + CONV_ROWS + 8, ls] * cw_ref[j:j + 1, ls]
                    part = term if part is None else part + term
                acc = acc + part[s:s + CONV_ROWS, :]
            tiles.append(acc)
        cv = jnp.concatenate(tiles, axis=1)
        xc = cv - jnp.mean(cv, axis=-1, keepdims=True)
        yc = xc * lax.rsqrt(jnp.mean(xc * xc, axis=-1, keepdims=True) + EPS) * lng_ref[...] + lnb_ref[...]
        y_ref[r0:r0 + CONV_ROWS, :] = (yc * jax.nn.sigmoid(yc)).astype(BF16)

    x1 = x1m + jnp.dot(y_ref[...], wout_ref[D_MLSTM:, :], preferred_element_type=F32)
    x1_ref[...] = x1
    hn = _rms(x1, gffn_ref[...])
    hn_ref[...] = _pack_rows(hn)

    logits = jnp.dot(hn.astype(BF16), wr_ref[...], preferred_element_type=F32) + br_ref[...]
    lane = lax.broadcasted_iota(jnp.int32, (tm, ROUTER_LANES), 1).astype(F32)
    neg = -jnp.inf
    no_lane = float(ROUTER_LANES)
    gl = jnp.where(lane < N_GROUPS, logits, neg)
    gmax = jnp.max(gl, axis=1, keepdims=True)
    p_top = 1.0 / jnp.sum(jnp.exp(gl - gmax), axis=1, keepdims=True)
    g_idx = jnp.min(jnp.where(gl == gmax, lane, no_lane), axis=1, keepdims=True)
    lo = EXPERT_LANE0 + EXPERTS_PER_GROUP * g_idx
    in_grp = (lane >= lo) & (lane < lo + EXPERTS_PER_GROUP)
    el = jnp.where(in_grp, logits, neg)
    ee = jnp.exp(el - jnp.max(el, axis=1, keepdims=True))
    pe = jnp.where(in_grp, ee / jnp.sum(ee, axis=1, keepdims=True), -1.0)
    v1 = jnp.max(pe, axis=1, keepdims=True)
    i1 = jnp.min(jnp.where(pe == v1, lane, no_lane), axis=1, keepdims=True)
    pe2 = jnp.where(lane == i1, -1.0, pe)
    v2 = jnp.max(pe2, axis=1, keepdims=True)
    i2 = jnp.min(jnp.where(pe2 == v2, lane, no_lane), axis=1, keepdims=True)
    wsum = v1 + v2
    gate1 = p_top * (v1 / wsum)
    gate2 = p_top * (v2 / wsum)

    oh1 = (lane == i1).astype(F32)
    oh2 = (lane == i2).astype(F32)
    c1 = jnp.dot(tri_ref[...], oh1.astype(BF16), preferred_element_type=F32)
    c2 = jnp.dot(tri_ref[...], oh2.astype(BF16), preferred_element_type=F32)
    run = run_ref[...]
    tot1 = jnp.sum(oh1, axis=0, keepdims=True)
    tot2 = jnp.sum(oh2, axis=0, keepdims=True)
    rank1 = jnp.sum(oh1 * (run + c1), axis=1, keepdims=True)
    rank2 = jnp.sum(oh2 * (run + tot1 + c2), axis=1, keepdims=True)
    run = run + tot1 + tot2
    run_ref[...] = run
    cnt_ref[...] = run

    ml = lax.broadcasted_iota(jnp.int32, (tm, META_LANES), 1)
    meta = jnp.where(ml == 0, i1 - EXPERT_LANE0, 0.0)
    meta = jnp.where(ml == 1, i2 - EXPERT_LANE0, meta)
    meta = jnp.where(ml == 2, gate1, meta)
    meta = jnp.where(ml == 3, gate2, meta)
    meta = jnp.where(ml == 4, rank1, meta)
    meta = jnp.where(ml == 5, rank2, meta)
    meta_ref[...] = meta
    metat_ref[...] = meta.T


def _mixout(xp, xs, hf, hb, o, u, seq_lens, conv_w, conv_b, ln_g, ln_b, w_out, g_ffn, w_rg, b_rg, w_re, b_re):
    tp, ts = xp.shape[0], xs.shape[0]
    t = tp + ts
    n_tiles = t // TM_MIX
    hpt = TM_MIX // CONV_HALO
    n_halo = t // CONV_HALO
    firsts, lasts, pos = [], [], 0
    for ln in seq_lens:
        assert ln % TM_MIX == 0
        firsts.append(pos // TM_MIX)
        lasts.append((pos + ln) // TM_MIX - 1)
        pos += ln
    first = jnp.zeros((n_tiles,), jnp.int32).at[jnp.array(firsts)].set(1)
    last = jnp.zeros((n_tiles,), jnp.int32).at[jnp.array(lasts)].set(1)

    cw = jnp.zeros((32, D_CONV), F32).at[:CONV_WIDTH].set(conv_w.reshape(CONV_WIDTH, D_CONV))
    wr = jnp.zeros((D_MODEL, ROUTER_LANES), F32)
    wr = wr.at[:, :N_GROUPS].set(w_rg).at[:, EXPERT_LANE0:EXPERT_LANE0 + N_EXPERTS].set(w_re).astype(BF16)
    br = jnp.zeros((1, ROUTER_LANES), F32)
    br = br.at[0, :N_GROUPS].set(b_rg).at[0, EXPERT_LANE0:EXPERT_LANE0 + N_EXPERTS].set(b_re)
    tri = (lax.broadcasted_iota(jnp.int32, (TM_MIX, TM_MIX), 0)
           > lax.broadcasted_iota(jnp.int32, (TM_MIX, TM_MIX), 1)).astype(BF16)

    tok = lambda i, f, l: (i, 0)
    fixed = lambda i, f, l: (0, 0)
    row = lambda n: pl.BlockSpec((1, n), fixed)
    xp_spec, xs_spec = _two_batch_specs((TM_MIX, D_MODEL), tp // TM_MIX, ts // TM_MIX)
    grid_spec = pltpu.PrefetchScalarGridSpec(
        num_scalar_prefetch=2,
        grid=(n_tiles,),
        in_specs=[xp_spec, xs_spec,
                  pl.BlockSpec((TM_MIX, D_MLSTM), tok), pl.BlockSpec((TM_MIX, D_MLSTM), tok),
                  pl.BlockSpec((TM_MIX, D_MLSTM), tok), pl.BlockSpec((TM_MIX, D_CONV), tok),
                  pl.BlockSpec((CONV_HALO, D_CONV), lambda i, f, l: (jnp.maximum(i * hpt - 1, 0), 0)),
                  pl.BlockSpec((CONV_HALO, D_CONV), lambda i, f, l: (jnp.minimum((i + 1) * hpt, n_halo - 1), 0)),
                  pl.BlockSpec((32, D_CONV), fixed), row(D_CONV), row(D_CONV), row(D_CONV),
                  pl.BlockSpec((D_MODEL, D_MODEL), fixed), row(D_MODEL),
                  pl.BlockSpec((D_MODEL, ROUTER_LANES), fixed), row(ROUTER_LANES),
                  pl.BlockSpec((TM_MIX, TM_MIX), fixed)],
        out_specs=[pl.BlockSpec((TM_MIX, D_MODEL), tok), pl.BlockSpec((TM_MIX, D_MODEL // 2), tok),
                   pl.BlockSpec((TM_MIX, META_LANES), tok),
                   pl.BlockSpec((META_LANES, TM_MIX), lambda i, f, l: (0, i)), row(ROUTER_LANES)],
        scratch_shapes=[pltpu.VMEM((TM_MIX + 2 * CONV_HALO, D_CONV), F32),
                        pltpu.VMEM((TM_MIX, D_CONV), BF16),
                        pltpu.VMEM((1, ROUTER_LANES), F32)],
    )
    return pl.pallas_call(
        functools.partial(_mixout_kernel, n_first=tp // TM_MIX),
        grid_spec=grid_spec,
        out_shape=[jax.ShapeDtypeStruct((t, D_MODEL), F32), jax.ShapeDtypeStruct((t, D_MODEL // 2), jnp.int32),
                   jax.ShapeDtypeStruct((t, META_LANES), F32), jax.ShapeDtypeStruct((META_LANES, t), F32),
                   jax.ShapeDtypeStruct((1, ROUTER_LANES), F32)],
        compiler_params=_cparams(),
        name="mixout",
    )(first, last, xp, xs, hf, hb, o, u, u, u, cw, conv_b.reshape(1, D_CONV), ln_g.reshape(1, D_CONV),
      ln_b.reshape(1, D_CONV), w_out.astype(BF16), g_ffn.reshape(1, D_MODEL), wr, br, tri)


def _sc_mesh():
    return plsc.VectorSubcoreMesh(core_axis_name="c", subcore_axis_name="s")


def _sc_worker_base(per_worker):
    wid = lax.axis_index("s") * SC_CORES + lax.axis_index("c")
    return wid * per_worker


def _dispatch(hn, pos1, pos2, n_slots):
    t, d = hn.shape
    per_worker = t // SC_WORKERS
    assert per_worker * SC_WORKERS == t and per_worker % SC_ROWS == 0

    def body(h_hbm, p1_hbm, p2_hbm, out_hbm, i1_v, i2_v, rows_v, sem1, sem2):
        base0 = _sc_worker_base(per_worker)

        @pl.loop(0, per_worker // SC_ROWS)
        def _(c):
            base = pl.multiple_of(base0 + c * SC_ROWS, SC_ROWS)
            pltpu.sync_copy(p1_hbm.at[pl.ds(base, SC_ROWS)], i1_v)
            pltpu.sync_copy(p2_hbm.at[pl.ds(base, SC_ROWS)], i2_v)
            pltpu.sync_copy(h_hbm.at[pl.ds(base, SC_ROWS)], rows_v)
            c1 = pltpu.async_copy(rows_v, out_hbm.at[i1_v], sem1)
            c2 = pltpu.async_copy(rows_v, out_hbm.at[i2_v], sem2)
            c1.wait()
            c2.wait()

    return pl.kernel(
        body,
        out_type=jax.ShapeDtypeStruct((n_slots, d), hn.dtype),
        mesh=_sc_mesh(),
        scratch_types=[pltpu.VMEM((SC_ROWS,), jnp.int32), pltpu.VMEM((SC_ROWS,), jnp.int32),
                       pltpu.VMEM((SC_ROWS, d), hn.dtype), pltpu.SemaphoreType.DMA, pltpu.SemaphoreType.DMA],
        name="dispatch",
    )(hn, pos1, pos2)


def _collect(ys, pos1, pos2):
    t = pos1.shape[0]
    d = ys.shape[1]
    per_worker = t // SC_WORKERS
    assert per_worker * SC_WORKERS == t and per_worker % SC_ROWS == 0

    def body(ys_hbm, p1_hbm, p2_hbm, y1_hbm, y2_hbm, i_v, rows_v, sem):
        base0 = _sc_worker_base(per_worker)

        @pl.loop(0, per_worker // SC_ROWS)
        def _(c):
            base = pl.multiple_of(base0 + c * SC_ROWS, SC_ROWS)
            for p_hbm, y_hbm in ((p1_hbm, y1_hbm), (p2_hbm, y2_hbm)):
                pltpu.sync_copy(p_hbm.at[pl.ds(base, SC_ROWS)], i_v)
                pltpu.async_copy(ys_hbm.at[i_v], rows_v, sem).wait()
                pltpu.sync_copy(rows_v, y_hbm.at[pl.ds(base, SC_ROWS)])

    out = jax.ShapeDtypeStruct((t, d), ys.dtype)
    return pl.kernel(
        body,
        out_type=(out, out),
        mesh=_sc_mesh(),
        scratch_types=[pltpu.VMEM((SC_ROWS,), jnp.int32), pltpu.VMEM((SC_ROWS, d), ys.dtype),
                       pltpu.SemaphoreType.DMA],
        name="collect",
    )(ys, pos1, pos2)


def _experts_kernel(te_ref, rows_ref, hs_ref, wg_ref, wu_ref, wd_ref, ys_ref, wgb_ref, wub_ref, wdb_ref):
    i = pl.program_id(0)

    @pl.when((i == 0) | (te_ref[i] != te_ref[jnp.maximum(i - 1, 0)]))
    def _():
        wgb_ref[...] = wg_ref[...].astype(BF16)
        wub_ref[...] = wu_ref[...].astype(BF16)
        wdb_ref[...] = wd_ref[...].astype(BF16)

    @pl.when(rows_ref[i] > 0)
    def _():
        x = _unpack_rows(hs_ref[...]).astype(BF16)
        hg = jnp.dot(x, wgb_ref[...], preferred_element_type=F32)
        hu = jnp.dot(x, wub_ref[...], preferred_element_type=F32)
        hid = (hg * jax.nn.sigmoid(hg) * hu).astype(BF16)
        ys_ref[...] = _pack_rows(jnp.dot(hid, wdb_ref[...], preferred_element_type=F32))

    @pl.when(rows_ref[i] == 0)
    def _():
        ys_ref[...] = jnp.zeros_like(ys_ref)


def _experts(hs, tile_expert, tile_rows, w_gate, w_up, w_down):
    n_tiles = tile_expert.shape[0]
    wsel = lambda i, te, rows: (te[i], 0, 0)
    slot = lambda i, te, rows: (i, 0)
    grid_spec = pltpu.PrefetchScalarGridSpec(
        num_scalar_prefetch=2,
        grid=(n_tiles,),
        in_specs=[pl.BlockSpec((TM_EXP, D_MODEL // 2), slot),
                  pl.BlockSpec((None, D_MODEL, D_EXPERT), wsel),
                  pl.BlockSpec((None, D_MODEL, D_EXPERT), wsel),
                  pl.BlockSpec((None, D_EXPERT, D_MODEL), wsel)],
        out_specs=pl.BlockSpec((TM_EXP, D_MODEL // 2), slot),
        scratch_shapes=[pltpu.VMEM((D_MODEL, D_EXPERT), BF16), pltpu.VMEM((D_MODEL, D_EXPERT), BF16),
                        pltpu.VMEM((D_EXPERT, D_MODEL), BF16)],
    )
    return pl.pallas_call(
        _experts_kernel,
        grid_spec=grid_spec,
        out_shape=jax.ShapeDtypeStruct((n_tiles * TM_EXP, D_MODEL // 2), jnp.int32),
        compiler_params=_cparams(),
        name="experts",
    )(tile_expert, tile_rows, hs, w_gate, w_up, w_down)


def _combine_kernel(x1_ref, meta_ref, y1_ref, y2_ref, gfin_ref, outp_ref, outs_ref, *, n_first):
    i = pl.program_id(0)
    meta = meta_ref[...]
    x2 = x1_ref[...] + meta[:, 2:3] * _unpack_rows(y1_ref[...]) + meta[:, 3:4] * _unpack_rows(y2_ref[...])
    out = _rms(x2, gfin_ref[...])

    @pl.when(i < n_first)
    def _():
        outp_ref[...] = out

    @pl.when(i >= n_first)
    def _():
        outs_ref[...] = out


def _combine(x1, meta, y1, y2, g_final, tp):
    t = x1.shape[0]
    ts = t - tp
    assert tp % TM_OUT == 0 and ts % TM_OUT == 0
    tok = pl.BlockSpec((TM_OUT, D_MODEL), lambda i: (i, 0))
    packed = pl.BlockSpec((TM_OUT, D_MODEL // 2), lambda i: (i, 0))
    outp_spec, outs_spec = _two_batch_specs((TM_OUT, D_MODEL), tp // TM_OUT, ts // TM_OUT)
    return pl.pallas_call(
        functools.partial(_combine_kernel, n_first=tp // TM_OUT),
        grid=(t // TM_OUT,),
        in_specs=[tok, pl.BlockSpec((TM_OUT, META_LANES), lambda i: (i, 0)), packed, packed,
                  pl.BlockSpec((1, D_MODEL), lambda i: (0, 0))],
        out_specs=[outp_spec, outs_spec],
        out_shape=[jax.ShapeDtypeStruct((tp, D_MODEL), F32), jax.ShapeDtypeStruct((ts, D_MODEL), F32)],
        compiler_params=_cparams(),
        name="combine",
    )(x1, meta, y1, y2, g_final.reshape(1, D_MODEL))


def _slot_layout(metat, counts_row, n_tiles):
    eid = metat[0:2].astype(jnp.int32)
    rank = metat[4:6].astype(jnp.int32)
    counts = counts_row[0, EXPERT_LANE0:EXPERT_LANE0 + N_EXPERTS].astype(jnp.int32)
    padded = (counts + TM_EXP - 1) // TM_EXP * TM_EXP
    ends = jnp.cumsum(padded)
    offs = ends - padded
    pos = rank
    for e in range(N_EXPERTS):
        pos = pos + jnp.where(eid == e, offs[e], 0)
    tile_start = jnp.arange(n_tiles, dtype=jnp.int32) * TM_EXP
    tile_expert = jnp.minimum(jnp.sum(tile_start[:, None] >= ends[None, :], axis=1), N_EXPERTS - 1)
    tile_expert = tile_expert.astype(jnp.int32)
    tile_rows = jnp.clip(counts[tile_expert] - (tile_start - offs[tile_expert]), 0, TM_EXP)
    tile_rows = jnp.where(tile_start < ends[-1], tile_rows, 0).astype(jnp.int32)
    return pos[0], pos[1], tile_expert, tile_rows


def kernel(x_prompt, x_sample, g_mix, w_in, b_gate, conv_w, conv_b, ln_g, ln_b, w_out, g_ffn,
           w_router_group, b_router_group, w_router_expert, b_router_expert, w_gate, w_up, w_down, g_final):
    assert g_mix.shape[0] == 1, "one layer"
    bp, lp, _ = x_prompt.shape
    bs, ls, _ = x_sample.shape
    seq_lens = [lp] * bp + [ls] * bs
    tp, ts = bp * lp, bs * ls
    t = tp + ts
    xp = x_prompt.reshape(tp, D_MODEL)
    xs = x_sample.reshape(ts, D_MODEL)

    q, v, o, u, kt, gr = _inproj(xp, xs, g_mix[0], w_in[0], b_gate[0])
    hf, hb = _mlstm(q, kt, v, gr, seq_lens)
    x1, hn, meta, metat, counts = _mixout(xp, xs, hf, hb, o, u, seq_lens, conv_w[0], conv_b[0], ln_g[0], ln_b[0],
                                   w_out[0], g_ffn[0], w_router_group[0], b_router_group[0],
                                   w_router_expert[0], b_router_expert[0])
    n_tiles = (2 * t + N_EXPERTS * (TM_EXP - 1)) // TM_EXP + 1
    pos1, pos2, tile_expert, tile_rows = _slot_layout(metat, counts, n_tiles)
    hs = _dispatch(hn, pos1, pos2, n_tiles * TM_EXP)
    ys = _experts(hs, tile_expert, tile_rows, w_gate[0], w_up[0], w_down[0])
    y1, y2 = _collect(ys, pos1, pos2)
    outp, outs = _combine(x1, meta, y1, y2, g_final, tp)
    return outp.reshape(bp, lp, D_MODEL), outs.reshape(bs, ls, D_MODEL)
```

```python
import functools

import jax
import jax.numpy as jnp
from jax import lax
from jax.experimental import pallas as pl
from jax.experimental.pallas import tpu as pltpu
from jax.experimental.pallas import tpu_sc as plsc

F32 = jnp.float32
BF16 = jnp.bfloat16

D_MODEL = 1024
N_HEADS = 4
HEAD_DIM = 128
D_MLSTM = N_HEADS * HEAD_DIM
D_CONV = D_MODEL - D_MLSTM
CONV_WIDTH = 31
CONV_HALO = 16
N_DIR = 2
N_GROUPS = 4
EXPERTS_PER_GROUP = 4
N_EXPERTS = N_GROUPS * EXPERTS_PER_GROUP
D_EXPERT = 512
EPS = 1e-6
K_SCALE = HEAD_DIM ** -0.5

GATE_ROWS = 16
QROWS = 24
ROUTER_LANES = 128
EXPERT_LANE0 = N_GROUPS
META_LANES = 8

TM_IN = 512
CHUNK = 256
GP_CHUNKS = 8
TM_MIX = 512
CONV_ROWS = 128
TM_EXP = 512
TM_OUT = 256
VMEM_LIMIT = 48 * 1024 * 1024

SC_CORES = 2
SC_SUBCORES = 16
SC_WORKERS = SC_CORES * SC_SUBCORES
SC_ROWS = 128


def _cparams(n_axes=1):
    return pltpu.CompilerParams(dimension_semantics=("arbitrary",) * n_axes,
                                vmem_limit_bytes=VMEM_LIMIT)


def _nt_dot(a, b):
    return lax.dot_general(a, b, (((1,), (1,)), ((), ())), preferred_element_type=F32)


def _rms(x, g):
    return x * lax.rsqrt(jnp.mean(x * x, axis=-1, keepdims=True) + EPS) * g


def _pack_rows(x):
    n = x.shape[1] // 2
    hi = lax.bitcast_convert_type(x[:, :n].astype(jnp.bfloat16).astype(F32), jnp.int32)
    lo = lax.bitcast_convert_type(x[:, n:].astype(jnp.bfloat16).astype(F32), jnp.int32)
    return hi | lax.shift_right_logical(lo, 16)


def _unpack_rows(p):
    hi = lax.bitcast_convert_type(p & jnp.int32(-65536), F32)
    lo = lax.bitcast_convert_type(lax.shift_left(p, 16), F32)
    return jnp.concatenate([hi, lo], axis=1)


def _two_batch_specs(block, n_first, n_second):
    first = pl.BlockSpec(block, lambda i, *_: (jnp.minimum(i, n_first - 1), 0))
    second = pl.BlockSpec(block, lambda i, *_: (jnp.maximum(i - n_first, 0), 0))
    return first, second


def _inproj_kernel(xp_ref, xs_ref, g_ref, wq_ref, wv_ref, wo_ref, wa_ref, wb_ref, wkt_ref, wgt_ref, bg_ref,
                   q_ref, v_ref, o_ref, u_ref, kt_ref, gr_ref, *, n_first):
    x = jnp.where(pl.program_id(0) < n_first, xp_ref[...], xs_ref[...])
    xn = _rms(x, g_ref[...]).astype(BF16)
    q_ref[...] = jnp.dot(xn, wq_ref[...], preferred_element_type=F32).astype(BF16)
    v_ref[...] = jnp.dot(xn, wv_ref[...], preferred_element_type=F32).astype(BF16)
    o_ref[...] = jnp.dot(xn, wo_ref[...], preferred_element_type=F32).astype(BF16)
    a = jnp.dot(xn, wa_ref[...], preferred_element_type=F32)
    b = jnp.dot(xn, wb_ref[...], preferred_element_type=F32)
    u_ref[...] = (a * jax.nn.sigmoid(b)).astype(BF16)
    kt_ref[...] = (_nt_dot(wkt_ref[...], xn) * K_SCALE).astype(BF16)
    gr = _nt_dot(wgt_ref[...], xn) + bg_ref[...]
    for c in range(gr_ref.shape[0]):
        gr_ref[c] = gr[:, c * CHUNK:(c + 1) * CHUNK]


def _inproj(xp, xs, g_mix, w_in, b_gate):
    tp, ts = xp.shape[0], xs.shape[0]
    t = tp + ts
    assert tp % TM_IN == 0 and ts % TM_IN == 0
    off_k, off_v, off_o, off_g = D_MLSTM, 2 * D_MLSTM, 3 * D_MLSTM, 4 * D_MLSTM
    off_a = off_g + 2 * N_DIR * N_HEADS
    off_b = off_a + D_CONV
    wq = w_in[:, 0:off_k].astype(BF16)
    wkt = w_in[:, off_k:off_v].T.astype(BF16)
    wv = w_in[:, off_v:off_o].astype(BF16)
    wo = w_in[:, off_o:off_g].astype(BF16)
    wa = w_in[:, off_a:off_b].astype(BF16)
    wb = w_in[:, off_b:off_b + D_CONV].astype(BF16)
    wg = w_in[:, off_g:off_a].T.reshape(N_DIR, 2, N_HEADS, D_MODEL)
    wgt = jnp.zeros((N_DIR, 2, GATE_ROWS // 2, D_MODEL), F32).at[:, :, :N_HEADS].set(wg)
    wgt = wgt.reshape(N_DIR * GATE_ROWS, D_MODEL).astype(BF16)
    bg = jnp.zeros((N_DIR, 2, GATE_ROWS // 2), F32).at[:, :, :N_HEADS].set(
        b_gate.reshape(N_DIR, 2, N_HEADS)).reshape(N_DIR * GATE_ROWS, 1)

    tok = lambda i: (i, 0)
    fixed = lambda i: (0, 0)
    wspec = pl.BlockSpec((D_MODEL, D_MLSTM), fixed)
    xp_spec, xs_spec = _two_batch_specs((TM_IN, D_MODEL), tp // TM_IN, ts // TM_IN)
    cpt = TM_IN // CHUNK
    return pl.pallas_call(
        functools.partial(_inproj_kernel, n_first=tp // TM_IN),
        grid=(t // TM_IN,),
        in_specs=[xp_spec, xs_spec, pl.BlockSpec((1, D_MODEL), fixed),
                  wspec, wspec, wspec, wspec, wspec,
                  pl.BlockSpec((D_MLSTM, D_MODEL), fixed),
                  pl.BlockSpec((N_DIR * GATE_ROWS, D_MODEL), fixed),
                  pl.BlockSpec((N_DIR * GATE_ROWS, 1), fixed)],
        out_specs=[pl.BlockSpec((TM_IN, D_MLSTM), tok)] * 4 + [
            pl.BlockSpec((D_MLSTM, TM_IN), lambda i: (0, i)),
            pl.BlockSpec((cpt, N_DIR * GATE_ROWS, CHUNK), lambda i: (i, 0, 0))],
        out_shape=[jax.ShapeDtypeStruct((t, D_MLSTM), BF16)] * 4 + [
            jax.ShapeDtypeStruct((D_MLSTM, t), BF16),
            jax.ShapeDtypeStruct((t // CHUNK, N_DIR * GATE_ROWS, CHUNK), F32)],
        compiler_params=_cparams(),
        name="inproj",
    )(xp, xs, g_mix.reshape(1, D_MODEL), wq, wv, wo, wa, wb, wkt, wgt, bg)


def _log_sigmoid(x):
    return jnp.minimum(x, 0.0) - jnp.log1p(jnp.exp(-jnp.abs(x)))


def _gateprep_kernel(reset_ref, g_ref, rowq_ref, colq_ref, m_ref, *, rev):
    n, _, c = g_ref.shape
    step = pl.program_id(0)
    blk = pl.num_programs(0) - 1 - step if rev else step

    @pl.when(step == 0)
    def _():
        m_ref[...] = jnp.zeros_like(m_ref)

    ig = g_ref[:, 0:8, :]
    lf = _log_sigmoid(g_ref[:, 8:16, :])
    lane = lax.broadcasted_iota(jnp.int32, (n, 8, c), 2)

    def scan(x, op, ident):
        k = 1
        while k < c:
            if rev:
                shifted, valid = pltpu.roll(x, c - k, axis=2), lane < c - k
            else:
                shifted, valid = pltpu.roll(x, k, axis=2), lane >= k
            x = op(x, jnp.where(valid, shifted, ident))
            k *= 2
        return x

    bc = scan(lf, jnp.add, 0.0)
    a = ig - bc
    cm = scan(a, jnp.maximum, -jnp.inf)
    b_tot = jnp.sum(lf, axis=2, keepdims=True)
    a_max = jnp.max(a, axis=2, keepdims=True)

    m = m_ref[...]
    m_in = [None] * n
    for j in (range(n - 1, -1, -1) if rev else range(n)):
        m = jnp.where(reset_ref[blk * n + j] == 1, 0.0, m)
        m_in[j] = m
        m = b_tot[j] + jnp.maximum(m, a_max[j])
    m_ref[...] = m
    m_old = jnp.stack(m_in)

    mx = jnp.maximum(m_old, cm)
    mx_last = jnp.maximum(m_old, a_max)
    rowq_ref[:, 0:8, :] = a
    rowq_ref[:, 8:16, :] = jnp.exp(a - mx_last)
    rowq_ref[:, 16:24, :] = jnp.exp(m_old - mx_last)
    e1 = jnp.exp(m_old - mx)
    fl = jnp.exp(-(mx + bc))
    for j in range(n):
        colq_ref[j] = jnp.concatenate([mx[j], e1[j], fl[j]], axis=0).T


def _gateprep(gr, reset, rev):
    n_chunks = gr.shape[0]
    assert n_chunks % GP_CHUNKS == 0
    nb = n_chunks // GP_CHUNKS
    d = 1 if rev else 0
    bidx = (lambda s: nb - 1 - s) if rev else (lambda s: s)
    grid_spec = pltpu.PrefetchScalarGridSpec(
        num_scalar_prefetch=1,
        grid=(nb,),
        in_specs=[pl.BlockSpec((GP_CHUNKS, GATE_ROWS, CHUNK), lambda s, r: (bidx(s), d, 0))],
        out_specs=[pl.BlockSpec((GP_CHUNKS, QROWS, CHUNK), lambda s, r: (bidx(s), 0, 0)),
                   pl.BlockSpec((GP_CHUNKS, CHUNK, QROWS), lambda s, r: (bidx(s), 0, 0))],
        scratch_shapes=[pltpu.VMEM((8, CHUNK), F32)],
    )
    return pl.pallas_call(
        functools.partial(_gateprep_kernel, rev=rev),
        grid_spec=grid_spec,
        out_shape=[jax.ShapeDtypeStruct((n_chunks, QROWS, CHUNK), F32),
                   jax.ShapeDtypeStruct((n_chunks, CHUNK, QROWS), F32)],
        compiler_params=_cparams(),
        name="gateprep_bwd" if rev else "gateprep_fwd",
    )(reset, gr)


def _mlstm_kernel(rf_ref, rb_ref,
                  qf_ref, ktf_ref, vf_ref, rowf_ref, colf_ref,
                  qb_ref, ktb_ref, vb_ref, rowb_ref, colb_ref,
                  hf_ref, hb_ref, cst_ref):
    c = qf_ref.shape[0]
    step = pl.program_id(0)
    last = pl.num_programs(0) - 1
    row_i = lax.broadcasted_iota(jnp.int32, (c, c), 0)
    col_i = lax.broadcasted_iota(jnp.int32, (c, c), 1)
    ones = jnp.ones((c, HEAD_DIM), BF16)

    dirs = ((rf_ref[step], col_i <= row_i, qf_ref, ktf_ref, vf_ref, rowf_ref, colf_ref, hf_ref),
            (rb_ref[last - step], col_i >= row_i, qb_ref, ktb_ref, vb_ref, rowb_ref, colb_ref, hb_ref))
    for d, (reset, mask, q_ref, kt_ref, v_ref, row_ref, col_ref, h_ref) in enumerate(dirs):
        @pl.when(reset == 1)
        def _():
            cst_ref[d] = jnp.zeros(cst_ref.shape[1:], F32)

        rowq = row_ref[...]
        colq = col_ref[...]
        for h in range(N_HEADS):
            hs = slice(h * HEAD_DIM, (h + 1) * HEAD_DIM)
            qh = q_ref[:, hs]
            kth = kt_ref[hs, :]
            vext = jnp.concatenate([v_ref[:, hs], ones], axis=1)
            s = jnp.dot(qh, kth, preferred_element_type=F32)
            e = jnp.exp(jnp.where(mask, rowq[h:h + 1, :] - colq[:, h:h + 1], -jnp.inf))
            r1 = jnp.dot((s * e).astype(BF16), vext, preferred_element_type=F32)
            cst = cst_ref[d, h]
            r2 = jnp.dot(qh, cst.astype(BF16), preferred_element_type=F32)
            e1 = colq[:, 8 + h:9 + h]
            num = r1[:, :HEAD_DIM] + e1 * r2[:, :HEAD_DIM]
            den = r1[:, HEAD_DIM:] + e1 * r2[:, HEAD_DIM:]
            h_ref[:, hs] = num / jnp.maximum(jnp.abs(den), colq[:, 16 + h:17 + h])
            kw = (kth.astype(F32) * rowq[8 + h:9 + h, :]).astype(BF16)
            cst_ref[d, h] = rowq[16 + h:17 + h, 0:1] * cst + jnp.dot(kw, vext, preferred_element_type=F32)


def _mlstm(q, kt, v, gr, seq_lens):
    t = q.shape[0]
    n = t // CHUNK
    starts, ends, pos = [], [], 0
    for ln in seq_lens:
        assert ln % CHUNK == 0
        starts.append(pos // CHUNK)
        ends.append((pos + ln) // CHUNK - 1)
        pos += ln
    reset_f = jnp.zeros((n,), jnp.int32).at[jnp.array(starts)].set(1)
    reset_b = jnp.zeros((n,), jnp.int32).at[jnp.array(ends)].set(1)
    rowf, colf = _gateprep(gr, reset_f, rev=False)
    rowb, colb = _gateprep(gr, reset_b, rev=True)

    def specs(cidx):
        return [pl.BlockSpec((CHUNK, D_MLSTM), lambda s, rf, rb: (cidx(s), 0)),
                pl.BlockSpec((D_MLSTM, CHUNK), lambda s, rf, rb: (0, cidx(s))),
                pl.BlockSpec((CHUNK, D_MLSTM), lambda s, rf, rb: (cidx(s), 0)),
                pl.BlockSpec((None, QROWS, CHUNK), lambda s, rf, rb: (cidx(s), 0, 0)),
                pl.BlockSpec((None, CHUNK, QROWS), lambda s, rf, rb: (cidx(s), 0, 0))]

    fwd = lambda s: s
    bwd = lambda s: n - 1 - s
    grid_spec = pltpu.PrefetchScalarGridSpec(
        num_scalar_prefetch=2,
        grid=(n,),
        in_specs=specs(fwd) + specs(bwd),
        out_specs=[pl.BlockSpec((CHUNK, D_MLSTM), lambda s, rf, rb: (fwd(s), 0)),
                   pl.BlockSpec((CHUNK, D_MLSTM), lambda s, rf, rb: (bwd(s), 0))],
        scratch_shapes=[pltpu.VMEM((N_DIR, N_HEADS, HEAD_DIM, 2 * HEAD_DIM), F32)],
    )
    return pl.pallas_call(
        _mlstm_kernel,
        grid_spec=grid_spec,
        out_shape=[jax.ShapeDtypeStruct((t, D_MLSTM), F32)] * 2,
        compiler_params=_cparams(),
        name="mlstm",
    )(reset_f, reset_b, q, kt, v, rowf, colf, q, kt, v, rowb, colb)


def _mixout_kernel(first_ref, last_ref,
                   x_ref, hf_ref, hb_ref, o_ref, u_ref, up_ref, un_ref, cw_ref, cb_ref, lng_ref, lnb_ref,
                   wout_ref, gffn_ref, wr_ref, br_ref, tri_ref,
                   x1_ref, hn_ref, meta_ref, metat_ref, cnt_ref,
                   win_ref, y_ref, run_ref):
    i = pl.program_id(0)
    tm = hf_ref.shape[0]

    @pl.when(i == 0)
    def _():
        run_ref[...] = jnp.zeros_like(run_ref)

    ym = (jax.nn.sigmoid(o_ref[...].astype(F32)) * (hf_ref[...] + hb_ref[...])).astype(BF16)
    x1m = x_ref[...] + jnp.dot(ym, wout_ref[:D_MLSTM, :], preferred_element_type=F32)

    win_ref[0:CONV_HALO, :] = jnp.where(first_ref[i] == 1, 0.0, up_ref[...].astype(F32))
    win_ref[CONV_HALO:CONV_HALO + tm, :] = u_ref[...].astype(F32)
    win_ref[CONV_HALO + tm:, :] = jnp.where(last_ref[i] == 1, 0.0, un_ref[...].astype(F32))

    off0 = CONV_HALO - CONV_WIDTH // 2
    for r0 in range(0, tm, CONV_ROWS):
        tiles = []
        for lt in range(D_CONV // 128):
            ls = slice(lt * 128, (lt + 1) * 128)
            acc = jnp.broadcast_to(cb_ref[:, ls], (CONV_ROWS, 128))
            for s in range(8):
                part = None
                for j in range(CONV_WIDTH):
                    if (off0 + j) % 8 != s:
                        continue
                    base = (off0 + j) // 8 * 8
                    term = win_ref[r0 + base:r0 + base + CONV_ROWS + 8, ls] * cw_ref[j:j + 1, ls]
                    part = term if part is None else part + term
                acc = acc + part[s:s + CONV_ROWS, :]
            tiles.append(acc)
        cv = jnp.concatenate(tiles, axis=1)
        xc = cv - jnp.mean(cv, axis=-1, keepdims=True)
        yc = xc * lax.rsqrt(jnp.mean(xc * xc, axis=-1, keepdims=True) + EPS) * lng_ref[...] + lnb_ref[...]
        y_ref[r0:r0 + CONV_ROWS, :] = (yc * jax.nn.sigmoid(yc)).astype(BF16)

    x1 = x1m + jnp.dot(y_ref[...], wout_ref[D_MLSTM:, :], preferred_element_type=F32)
    x1_ref[...] = x1
    hn = _rms(x1, gffn_ref[...])
    hn_ref[...] = _pack_rows(hn)

    logits = jnp.dot(hn.astype(BF16), wr_ref[...], preferred_element_type=F32) + br_ref[...]
    lane = lax.broadcasted_iota(jnp.int32, (tm, ROUTER_LANES), 1).astype(F32)
    neg = -jnp.inf
    no_lane = float(ROUTER_LANES)
    gl = jnp.where(lane < N_GROUPS, logits, neg)
    gmax = jnp.max(gl, axis=1, keepdims=True)
    p_top = 1.0 / jnp.sum(jnp.exp(gl - gmax), axis=1, keepdims=True)
    g_idx = jnp.min(jnp.where(gl == gmax, lane, no_lane), axis=1, keepdims=True)
    lo = EXPERT_LANE0 + EXPERTS_PER_GROUP * g_idx
    in_grp = (lane >= lo) & (lane < lo + EXPERTS_PER_GROUP)
    el = jnp.where(in_grp, logits, neg)
    ee = jnp.exp(el - jnp.max(el, axis=1, keepdims=True))
    pe = jnp.where(in_grp, ee / jnp.sum(ee, axis=1, keepdims=True), -1.0)
    v1 = jnp.max(pe, axis=1, keepdims=True)
    i1 = jnp.min(jnp.where(pe == v1, lane, no_lane), axis=1, keepdims=True)
    pe2 = jnp.where(lane == i1, -1.0, pe)
    v2 = jnp.max(pe2, axis=1, keepdims=True)
    i2 = jnp.min(jnp.where(pe2 == v2, lane, no_lane), axis=1, keepdims=True)
    wsum = v1 + v2
    gate1 = p_top * (v1 / wsum)
    gate2 = p_top * (v2 / wsum)

    oh1 = (lane == i1).astype(F32)
    oh2 = (lane == i2).astype(F32)
    c1 = jnp.dot(tri_ref[...], oh1.astype(BF16), preferred_element_type=F32)
    c2 = jnp.dot(tri_ref[...], oh2.astype(BF16), preferred_element_type=F32)
    run = run_ref[...]
    tot1 = jnp.sum(oh1, axis=0, keepdims=True)
    tot2 = jnp.sum(oh2, axis=0, keepdims=True)
    rank1 = jnp.sum(oh1 * (run + c1), axis=1, keepdims=True)
    rank2 = jnp.sum(oh2 * (run + tot1 + c2), axis=1, keepdims=True)
    run = run + tot1 + tot2
    run_ref[...] = run
    cnt_ref[...] = run

    ml = lax.broadcasted_iota(jnp.int32, (tm, META_LANES), 1)
    meta = jnp.where(ml == 0, i1 - EXPERT_LANE0, 0.0)
    meta = jnp.where(ml == 1, i2 - EXPERT_LANE0, meta)
    meta = jnp.where(ml == 2, gate1, meta)
    meta = jnp.where(ml == 3, gate2, meta)
    meta = jnp.where(ml == 4, rank1, meta)
    meta = jnp.where(ml == 5, rank2, meta)
    meta_ref[...] = meta
    metat_ref[...] = meta.T


def _mixout(x, tile0, hf, hb, o, u, seq_lens, conv_w, conv_b, ln_g, ln_b, w_out, g_ffn, w_rg, b_rg, w_re, b_re):
    t = x.shape[0]
    n_tiles = t // TM_MIX
    hpt = TM_MIX // CONV_HALO
    n_halo = u.shape[0] // CONV_HALO
    firsts, lasts, pos = [], [], 0
    for ln in seq_lens:
        assert ln % TM_MIX == 0
        firsts.append(pos // TM_MIX)
        lasts.append((pos + ln) // TM_MIX - 1)
        pos += ln
    assert pos == t
    first = jnp.zeros((n_tiles,), jnp.int32).at[jnp.array(firsts)].set(1)
    last = jnp.zeros((n_tiles,), jnp.int32).at[jnp.array(lasts)].set(1)

    cw = jnp.zeros((32, D_CONV), F32).at[:CONV_WIDTH].set(conv_w.reshape(CONV_WIDTH, D_CONV))
    wr = jnp.zeros((D_MODEL, ROUTER_LANES), F32)
    wr = wr.at[:, :N_GROUPS].set(w_rg).at[:, EXPERT_LANE0:EXPERT_LANE0 + N_EXPERTS].set(w_re).astype(BF16)
    br = jnp.zeros((1, ROUTER_LANES), F32)
    br = br.at[0, :N_GROUPS].set(b_rg).at[0, EXPERT_LANE0:EXPERT_LANE0 + N_EXPERTS].set(b_re)
    tri = (lax.broadcasted_iota(jnp.int32, (TM_MIX, TM_MIX), 0)
           > lax.broadcasted_iota(jnp.int32, (TM_MIX, TM_MIX), 1)).astype(BF16)

    tok = lambda i, f, l: (i, 0)
    flat = lambda i, f, l: (tile0 + i, 0)
    fixed = lambda i, f, l: (0, 0)
    row = lambda n: pl.BlockSpec((1, n), fixed)
    half = pl.BlockSpec((TM_MIX, D_MLSTM), flat)
    grid_spec = pltpu.PrefetchScalarGridSpec(
        num_scalar_prefetch=2,
        grid=(n_tiles,),
        in_specs=[pl.BlockSpec((TM_MIX, D_MODEL), tok), half, half, half,
                  pl.BlockSpec((TM_MIX, D_CONV), flat),
                  pl.BlockSpec((CONV_HALO, D_CONV), lambda i, f, l: (jnp.maximum((tile0 + i) * hpt - 1, 0), 0)),
                  pl.BlockSpec((CONV_HALO, D_CONV),
                               lambda i, f, l: (jnp.minimum((tile0 + i + 1) * hpt, n_halo - 1), 0)),
                  pl.BlockSpec((32, D_CONV), fixed), row(D_CONV), row(D_CONV), row(D_CONV),
                  pl.BlockSpec((D_MODEL, D_MODEL), fixed), row(D_MODEL),
                  pl.BlockSpec((D_MODEL, ROUTER_LANES), fixed), row(ROUTER_LANES),
                  pl.BlockSpec((TM_MIX, TM_MIX), fixed)],
        out_specs=[pl.BlockSpec((TM_MIX, D_MODEL), tok), pl.BlockSpec((TM_MIX, D_MODEL // 2), tok),
                   pl.BlockSpec((TM_MIX, META_LANES), tok),
                   pl.BlockSpec((META_LANES, TM_MIX), lambda i, f, l: (0, i)), row(ROUTER_LANES)],
        scratch_shapes=[pltpu.VMEM((TM_MIX + 2 * CONV_HALO, D_CONV), F32),
                        pltpu.VMEM((TM_MIX, D_CONV), BF16),
                        pltpu.VMEM((1, ROUTER_LANES), F32)],
    )
    return pl.pallas_call(
        _mixout_kernel,
        grid_spec=grid_spec,
        out_shape=[jax.ShapeDtypeStruct((t, D_MODEL), F32), jax.ShapeDtypeStruct((t, D_MODEL // 2), jnp.int32),
                   jax.ShapeDtypeStruct((t, META_LANES), F32), jax.ShapeDtypeStruct((META_LANES, t), F32),
                   jax.ShapeDtypeStruct((1, ROUTER_LANES), F32)],
        compiler_params=_cparams(),
        name="mixout",
    )(first, last, x, hf, hb, o, u, u, u, cw, conv_b.reshape(1, D_CONV), ln_g.reshape(1, D_CONV),
      ln_b.reshape(1, D_CONV), w_out.astype(BF16), g_ffn.reshape(1, D_MODEL), wr, br, tri)


def _sc_mesh():
    return plsc.VectorSubcoreMesh(core_axis_name="c", subcore_axis_name="s")


def _sc_worker_base(per_worker):
    wid = lax.axis_index("s") * SC_CORES + lax.axis_index("c")
    return wid * per_worker


def _dispatch(hn, pos1, pos2, n_slots):
    t, d = hn.shape
    per_worker = t // SC_WORKERS
    assert per_worker * SC_WORKERS == t and per_worker % SC_ROWS == 0

    def body(h_hbm, p1_hbm, p2_hbm, out_hbm, i1_v, i2_v, rows_v, sem1, sem2):
        base0 = _sc_worker_base(per_worker)

        @pl.loop(0, per_worker // SC_ROWS)
        def _(c):
            base = pl.multiple_of(base0 + c * SC_ROWS, SC_ROWS)
            pltpu.sync_copy(p1_hbm.at[pl.ds(base, SC_ROWS)], i1_v)
            pltpu.sync_copy(p2_hbm.at[pl.ds(base, SC_ROWS)], i2_v)
            pltpu.sync_copy(h_hbm.at[pl.ds(base, SC_ROWS)], rows_v)
            c1 = pltpu.async_copy(rows_v, out_hbm.at[i1_v], sem1)
            c2 = pltpu.async_copy(rows_v, out_hbm.at[i2_v], sem2)
            c1.wait()
            c2.wait()

    return pl.kernel(
        body,
        out_type=jax.ShapeDtypeStruct((n_slots, d), hn.dtype),
        mesh=_sc_mesh(),
        scratch_types=[pltpu.VMEM((SC_ROWS,), jnp.int32), pltpu.VMEM((SC_ROWS,), jnp.int32),
                       pltpu.VMEM((SC_ROWS, d), hn.dtype), pltpu.SemaphoreType.DMA, pltpu.SemaphoreType.DMA],
        name="dispatch",
    )(hn, pos1, pos2)


def _collect(ys, pos1, pos2):
    t = pos1.shape[0]
    d = ys.shape[1]
    per_worker = t // SC_WORKERS
    assert per_worker * SC_WORKERS == t and per_worker % SC_ROWS == 0

    def body(ys_hbm, p1_hbm, p2_hbm, y1_hbm, y2_hbm, i_v, rows_v, sem):
        base0 = _sc_worker_base(per_worker)

        @pl.loop(0, per_worker // SC_ROWS)
        def _(c):
            base = pl.multiple_of(base0 + c * SC_ROWS, SC_ROWS)
            for p_hbm, y_hbm in ((p1_hbm, y1_hbm), (p2_hbm, y2_hbm)):
                pltpu.sync_copy(p_hbm.at[pl.ds(base, SC_ROWS)], i_v)
                pltpu.async_copy(ys_hbm.at[i_v], rows_v, sem).wait()
                pltpu.sync_copy(rows_v, y_hbm.at[pl.ds(base, SC_ROWS)])

    out = jax.ShapeDtypeStruct((t, d), ys.dtype)
    return pl.kernel(
        body,
        out_type=(out, out),
        mesh=_sc_mesh(),
        scratch_types=[pltpu.VMEM((SC_ROWS,), jnp.int32), pltpu.VMEM((SC_ROWS, d), ys.dtype),
                       pltpu.SemaphoreType.DMA],
        name="collect",
    )(ys, pos1, pos2)


def _experts_kernel(te_ref, rows_ref, hs_ref, wg_ref, wu_ref, wd_ref, ys_ref, wgb_ref, wub_ref, wdb_ref):
    i = pl.program_id(0)

    @pl.when((i == 0) | (te_ref[i] != te_ref[jnp.maximum(i - 1, 0)]))
    def _():
        wgb_ref[...] = wg_ref[...].astype(BF16)
        wub_ref[...] = wu_ref[...].astype(BF16)
        wdb_ref[...] = wd_ref[...].astype(BF16)

    @pl.when(rows_ref[i] > 0)
    def _():
        x = _unpack_rows(hs_ref[...]).astype(BF16)
        hg = jnp.dot(x, wgb_ref[...], preferred_element_type=F32)
        hu = jnp.dot(x, wub_ref[...], preferred_element_type=F32)
        hid = (hg * jax.nn.sigmoid(hg) * hu).astype(BF16)
        ys_ref[...] = _pack_rows(jnp.dot(hid, wdb_ref[...], preferred_element_type=F32))

    @pl.when(rows_ref[i] == 0)
    def _():
        ys_ref[...] = jnp.zeros_like(ys_ref)


def _experts(hs, tile_expert, tile_rows, w_gate, w_up, w_down):
    n_tiles = tile_expert.shape[0]
    wsel = lambda i, te, rows: (te[i], 0, 0)
    slot = lambda i, te, rows: (i, 0)
    grid_spec = pltpu.PrefetchScalarGridSpec(
        num_scalar_prefetch=2,
        grid=(n_tiles,),
        in_specs=[pl.BlockSpec((TM_EXP, D_MODEL // 2), slot),
                  pl.BlockSpec((None, D_MODEL, D_EXPERT), wsel),
                  pl.BlockSpec((None, D_MODEL, D_EXPERT), wsel),
                  pl.BlockSpec((None, D_EXPERT, D_MODEL), wsel)],
        out_specs=pl.BlockSpec((TM_EXP, D_MODEL // 2), slot),
        scratch_shapes=[pltpu.VMEM((D_MODEL, D_EXPERT), BF16), pltpu.VMEM((D_MODEL, D_EXPERT), BF16),
                        pltpu.VMEM((D_EXPERT, D_MODEL), BF16)],
    )
    return pl.pallas_call(
        _experts_kernel,
        grid_spec=grid_spec,
        out_shape=jax.ShapeDtypeStruct((n_tiles * TM_EXP, D_MODEL // 2), jnp.int32),
        compiler_params=_cparams(),
        name="experts",
    )(tile_expert, tile_rows, hs, w_gate, w_up, w_down)


def _combine_kernel(x1_ref, meta_ref, y1_ref, y2_ref, gfin_ref, out_ref):
    meta = meta_ref[...]
    x2 = x1_ref[...] + meta[:, 2:3] * _unpack_rows(y1_ref[...]) + meta[:, 3:4] * _unpack_rows(y2_ref[...])
    out_ref[...] = _rms(x2, gfin_ref[...])


def _combine(x1, meta, y1, y2, g_final):
    t = x1.shape[0]
    assert t % TM_OUT == 0
    tok = pl.BlockSpec((TM_OUT, D_MODEL), lambda i: (i, 0))
    packed = pl.BlockSpec((TM_OUT, D_MODEL // 2), lambda i: (i, 0))
    return pl.pallas_call(
        _combine_kernel,
        grid=(t // TM_OUT,),
        in_specs=[tok, pl.BlockSpec((TM_OUT, META_LANES), lambda i: (i, 0)), packed, packed,
                  pl.BlockSpec((1, D_MODEL), lambda i: (0, 0))],
        out_specs=tok,
        out_shape=jax.ShapeDtypeStruct((t, D_MODEL), F32),
        compiler_params=_cparams(),
        name="combine",
    )(x1, meta, y1, y2, g_final.reshape(1, D_MODEL))


def _slot_layout(metat, counts_row, n_tiles):
    eid = metat[0:2].astype(jnp.int32)
    rank = metat[4:6].astype(jnp.int32)
    counts = counts_row[0, EXPERT_LANE0:EXPERT_LANE0 + N_EXPERTS].astype(jnp.int32)
    padded = (counts + TM_EXP - 1) // TM_EXP * TM_EXP
    ends = jnp.cumsum(padded)
    offs = ends - padded
    pos = rank
    for e in range(N_EXPERTS):
        pos = pos + jnp.where(eid == e, offs[e], 0)
    tile_start = jnp.arange(n_tiles, dtype=jnp.int32) * TM_EXP
    tile_expert = jnp.minimum(jnp.sum(tile_start[:, None] >= ends[None, :], axis=1), N_EXPERTS - 1)
    tile_expert = tile_expert.astype(jnp.int32)
    tile_rows = jnp.clip(counts[tile_expert] - (tile_start - offs[tile_expert]), 0, TM_EXP)
    tile_rows = jnp.where(tile_start < ends[-1], tile_rows, 0).astype(jnp.int32)
    return pos[0], pos[1], tile_expert, tile_rows


def kernel(x_prompt, x_sample, g_mix, w_in, b_gate, conv_w, conv_b, ln_g, ln_b, w_out, g_ffn,
           w_router_group, b_router_group, w_router_expert, b_router_expert, w_gate, w_up, w_down, g_final):
    assert g_mix.shape[0] == 1, "one layer"
    bp, lp, _ = x_prompt.shape
    bs, ls, _ = x_sample.shape
    seq_lens = [lp] * bp + [ls] * bs
    tp, ts = bp * lp, bs * ls
    xp = x_prompt.reshape(tp, D_MODEL)
    xs = x_sample.reshape(ts, D_MODEL)

    q, v, o, u, kt, gr = _inproj(xp, xs, g_mix[0], w_in[0], b_gate[0])
    hf, hb = _mlstm(q, kt, v, gr, seq_lens)

    outs = []
    for x, tile0, lens in ((xp, 0, [lp] * bp), (xs, tp // TM_MIX, [ls] * bs)):
        tg = x.shape[0]
        x1, hn, meta, metat, counts = _mixout(x, tile0, hf, hb, o, u, lens, conv_w[0], conv_b[0], ln_g[0], ln_b[0],
                                              w_out[0], g_ffn[0], w_router_group[0], b_router_group[0],
                                              w_router_expert[0], b_router_expert[0])
        n_tiles = (2 * tg + N_EXPERTS * (TM_EXP - 1)) // TM_EXP + 1
        pos1, pos2, tile_expert, tile_rows = _slot_layout(metat, counts, n_tiles)
        hs = _dispatch(hn, pos1, pos2, n_tiles * TM_EXP)
        ys = _experts(hs, tile_expert, tile_rows, w_gate[0], w_up[0], w_down[0])
        y1, y2 = _collect(ys, pos1, pos2)
        outs.append(_combine(x1, meta, y1, y2, g_final))
    return outs[0].reshape(bp, lp, D_MODEL), outs[1].reshape(bs, ls, D_MODEL)
```

```python
import functools

import jax
import jax.numpy as jnp
from jax import lax
from jax.experimental import pallas as pl
from jax.experimental.pallas import tpu as pltpu
from jax.experimental.pallas import tpu_sc as plsc

F32 = jnp.float32
BF16 = jnp.bfloat16

D_MODEL = 1024
N_HEADS = 4
HEAD_DIM = 128
D_MLSTM = N_HEADS * HEAD_DIM
D_CONV = D_MODEL - D_MLSTM
CONV_WIDTH = 31
CONV_HALO = 16
N_DIR = 2
N_GROUPS = 4
EXPERTS_PER_GROUP = 4
N_EXPERTS = N_GROUPS * EXPERTS_PER_GROUP
D_EXPERT = 512
EPS = 1e-6
K_SCALE = HEAD_DIM ** -0.5

GATE_ROWS = 16
QROWS = 24
ROUTER_LANES = 128
EXPERT_LANE0 = N_GROUPS
META_LANES = 8

TM_IN = 512
CHUNK = 256
GP_CHUNKS = 32
TM_MIX = 512
CONV_ROWS = 128
TM_EXP = 512
TM_OUT = 1024
VMEM_LIMIT = 48 * 1024 * 1024

SC_CORES = 2
SC_SUBCORES = 16
SC_WORKERS = SC_CORES * SC_SUBCORES
SC_ROWS = 128


def _cparams(n_axes=1):
    return pltpu.CompilerParams(dimension_semantics=("arbitrary",) * n_axes,
                                vmem_limit_bytes=VMEM_LIMIT)


def _nt_dot(a, b):
    return lax.dot_general(a, b, (((1,), (1,)), ((), ())), preferred_element_type=F32)


def _rms(x, g):
    return x * lax.rsqrt(jnp.mean(x * x, axis=-1, keepdims=True) + EPS) * g


def _pack_rows(x):
    n = x.shape[1] // 2
    hi = lax.bitcast_convert_type(x[:, :n].astype(jnp.bfloat16).astype(F32), jnp.int32)
    lo = lax.bitcast_convert_type(x[:, n:].astype(jnp.bfloat16).astype(F32), jnp.int32)
    return hi | lax.shift_right_logical(lo, 16)


def _unpack_rows(p):
    hi = lax.bitcast_convert_type(p & jnp.int32(-65536), F32)
    lo = lax.bitcast_convert_type(lax.shift_left(p, 16), F32)
    return jnp.concatenate([hi, lo], axis=1)


def _two_batch_specs(block, n_first, n_second):
    first = pl.BlockSpec(block, lambda i, *_: (jnp.minimum(i, n_first - 1), 0))
    second = pl.BlockSpec(block, lambda i, *_: (jnp.maximum(i - n_first, 0), 0))
    return first, second


def _inproj_kernel(xp_ref, xs_ref, g_ref, wq_ref, wv_ref, wo_ref, wa_ref, wb_ref, wkt_ref, wgt_ref, bg_ref,
                   q_ref, v_ref, o_ref, u_ref, kt_ref, gr_ref, *, n_first):
    x = jnp.where(pl.program_id(0) < n_first, xp_ref[...], xs_ref[...])
    xn = _rms(x, g_ref[...]).astype(BF16)
    q_ref[...] = jnp.dot(xn, wq_ref[...], preferred_element_type=F32).astype(BF16)
    v_ref[...] = jnp.dot(xn, wv_ref[...], preferred_element_type=F32).astype(BF16)
    o_ref[...] = jnp.dot(xn, wo_ref[...], preferred_element_type=F32).astype(BF16)
    a = jnp.dot(xn, wa_ref[...], preferred_element_type=F32)
    b = jnp.dot(xn, wb_ref[...], preferred_element_type=F32)
    u_ref[...] = (a * jax.nn.sigmoid(b)).astype(BF16)
    kt_ref[...] = (_nt_dot(wkt_ref[...], xn) * K_SCALE).astype(BF16)
    gr = _nt_dot(wgt_ref[...], xn) + bg_ref[...]
    for c in range(gr_ref.shape[0]):
        gr_ref[c] = gr[:, c * CHUNK:(c + 1) * CHUNK]


def _inproj(xp, xs, g_mix, w_in, b_gate):
    tp, ts = xp.shape[0], xs.shape[0]
    t = tp + ts
    assert tp % TM_IN == 0 and ts % TM_IN == 0
    off_k, off_v, off_o, off_g = D_MLSTM, 2 * D_MLSTM, 3 * D_MLSTM, 4 * D_MLSTM
    off_a = off_g + 2 * N_DIR * N_HEADS
    off_b = off_a + D_CONV
    wq = w_in[:, 0:off_k].astype(BF16)
    wkt = w_in[:, off_k:off_v].T.astype(BF16)
    wv = w_in[:, off_v:off_o].astype(BF16)
    wo = w_in[:, off_o:off_g].astype(BF16)
    wa = w_in[:, off_a:off_b].astype(BF16)
    wb = w_in[:, off_b:off_b + D_CONV].astype(BF16)
    wg = w_in[:, off_g:off_a].T.reshape(N_DIR, 2, N_HEADS, D_MODEL)
    wgt = jnp.zeros((N_DIR, 2, GATE_ROWS // 2, D_MODEL), F32).at[:, :, :N_HEADS].set(wg)
    wgt = wgt.reshape(N_DIR * GATE_ROWS, D_MODEL).astype(BF16)
    bg = jnp.zeros((N_DIR, 2, GATE_ROWS // 2), F32).at[:, :, :N_HEADS].set(
        b_gate.reshape(N_DIR, 2, N_HEADS)).reshape(N_DIR * GATE_ROWS, 1)

    tok = lambda i: (i, 0)
    fixed = lambda i: (0, 0)
    wspec = pl.BlockSpec((D_MODEL, D_MLSTM), fixed)
    xp_spec, xs_spec = _two_batch_specs((TM_IN, D_MODEL), tp // TM_IN, ts // TM_IN)
    cpt = TM_IN // CHUNK
    return pl.pallas_call(
        functools.partial(_inproj_kernel, n_first=tp // TM_IN),
        grid=(t // TM_IN,),
        in_specs=[xp_spec, xs_spec, pl.BlockSpec((1, D_MODEL), fixed),
                  wspec, wspec, wspec, wspec, wspec,
                  pl.BlockSpec((D_MLSTM, D_MODEL), fixed),
                  pl.BlockSpec((N_DIR * GATE_ROWS, D_MODEL), fixed),
                  pl.BlockSpec((N_DIR * GATE_ROWS, 1), fixed)],
        out_specs=[pl.BlockSpec((TM_IN, D_MLSTM), tok)] * 4 + [
            pl.BlockSpec((D_MLSTM, TM_IN), lambda i: (0, i)),
            pl.BlockSpec((cpt, N_DIR * GATE_ROWS, CHUNK), lambda i: (i, 0, 0))],
        out_shape=[jax.ShapeDtypeStruct((t, D_MLSTM), BF16)] * 4 + [
            jax.ShapeDtypeStruct((D_MLSTM, t), BF16),
            jax.ShapeDtypeStruct((t // CHUNK, N_DIR * GATE_ROWS, CHUNK), F32)],
        compiler_params=_cparams(),
        name="inproj",
    )(xp, xs, g_mix.reshape(1, D_MODEL), wq, wv, wo, wa, wb, wkt, wgt, bg)


def _log_sigmoid(x):
    return jnp.minimum(x, 0.0) - jnp.log1p(jnp.exp(-jnp.abs(x)))


def _gateprep_kernel(reset_ref, g_ref, rowq_ref, colq_ref, m_ref, *, rev):
    n, _, c = g_ref.shape
    step = pl.program_id(0)
    blk = pl.num_programs(0) - 1 - step if rev else step

    @pl.when(step == 0)
    def _():
        m_ref[...] = jnp.zeros_like(m_ref)

    ig = g_ref[:, 0:8, :]
    lf = _log_sigmoid(g_ref[:, 8:16, :])
    lane = lax.broadcasted_iota(jnp.int32, (n, 8, c), 2)

    def scan(x, op, ident):
        k = 1
        while k < c:
            if rev:
                shifted, valid = pltpu.roll(x, c - k, axis=2), lane < c - k
            else:
                shifted, valid = pltpu.roll(x, k, axis=2), lane >= k
            x = op(x, jnp.where(valid, shifted, ident))
            k *= 2
        return x

    bc = scan(lf, jnp.add, 0.0)
    a = ig - bc
    cm = scan(a, jnp.maximum, -jnp.inf)
    b_tot = jnp.sum(lf, axis=2, keepdims=True)
    a_max = jnp.max(a, axis=2, keepdims=True)

    m = m_ref[...]
    m_in = [None] * n
    for j in (range(n - 1, -1, -1) if rev else range(n)):
        m = jnp.where(reset_ref[blk * n + j] == 1, 0.0, m)
        m_in[j] = m
        m = b_tot[j] + jnp.maximum(m, a_max[j])
    m_ref[...] = m
    m_old = jnp.stack(m_in)

    mx = jnp.maximum(m_old, cm)
    mx_last = jnp.maximum(m_old, a_max)
    rowq_ref[:, 0:8, :] = a
    rowq_ref[:, 8:16, :] = jnp.exp(a - mx_last)
    rowq_ref[:, 16:24, :] = jnp.exp(m_old - mx_last)
    e1 = jnp.exp(m_old - mx)
    fl = jnp.exp(-(mx + bc))
    for j in range(n):
        colq_ref[j] = jnp.concatenate([mx[j], e1[j], fl[j]], axis=0).T


def _gateprep(gr, reset, rev):
    n_chunks = gr.shape[0]
    gp = min(GP_CHUNKS, n_chunks)
    assert n_chunks % gp == 0
    nb = n_chunks // gp
    d = 1 if rev else 0
    bidx = (lambda s: nb - 1 - s) if rev else (lambda s: s)
    grid_spec = pltpu.PrefetchScalarGridSpec(
        num_scalar_prefetch=1,
        grid=(nb,),
        in_specs=[pl.BlockSpec((gp, GATE_ROWS, CHUNK), lambda s, r: (bidx(s), d, 0))],
        out_specs=[pl.BlockSpec((gp, QROWS, CHUNK), lambda s, r: (bidx(s), 0, 0)),
                   pl.BlockSpec((gp, CHUNK, QROWS), lambda s, r: (bidx(s), 0, 0))],
        scratch_shapes=[pltpu.VMEM((8, CHUNK), F32)],
    )
    return pl.pallas_call(
        functools.partial(_gateprep_kernel, rev=rev),
        grid_spec=grid_spec,
        out_shape=[jax.ShapeDtypeStruct((n_chunks, QROWS, CHUNK), F32),
                   jax.ShapeDtypeStruct((n_chunks, CHUNK, QROWS), F32)],
        compiler_params=_cparams(),
        name="gateprep_bwd" if rev else "gateprep_fwd",
    )(reset, gr)


def _mlstm_kernel(rf_ref, rb_ref,
                  qf_ref, ktf_ref, vf_ref, rowf_ref, colf_ref,
                  qb_ref, ktb_ref, vb_ref, rowb_ref, colb_ref,
                  hf_ref, hb_ref, cst_ref):
    c = qf_ref.shape[0]
    step = pl.program_id(0)
    last = pl.num_programs(0) - 1
    row_i = lax.broadcasted_iota(jnp.int32, (c, c), 0)
    col_i = lax.broadcasted_iota(jnp.int32, (c, c), 1)
    ones = jnp.ones((c, HEAD_DIM), BF16)

    dirs = ((rf_ref[step], col_i <= row_i, qf_ref, ktf_ref, vf_ref, rowf_ref, colf_ref, hf_ref),
            (rb_ref[last - step], col_i >= row_i, qb_ref, ktb_ref, vb_ref, rowb_ref, colb_ref, hb_ref))
    for d, (reset, mask, q_ref, kt_ref, v_ref, row_ref, col_ref, h_ref) in enumerate(dirs):
        @pl.when(reset == 1)
        def _():
            cst_ref[d] = jnp.zeros(cst_ref.shape[1:], F32)

        rowq = row_ref[...]
        colq = col_ref[...]
        for h in range(N_HEADS):
            hs = slice(h * HEAD_DIM, (h + 1) * HEAD_DIM)
            qh = q_ref[:, hs]
            kth = kt_ref[hs, :]
            vext = jnp.concatenate([v_ref[:, hs], ones], axis=1)
            s = jnp.dot(qh, kth, preferred_element_type=F32)
            e = jnp.exp(jnp.where(mask, rowq[h:h + 1, :] - colq[:, h:h + 1], -jnp.inf))
            r1 = jnp.dot((s * e).astype(BF16), vext, preferred_element_type=F32)
            cst = cst_ref[d, h]
            r2 = jnp.dot(qh, cst.astype(BF16), preferred_element_type=F32)
            e1 = colq[:, 8 + h:9 + h]
            num = r1[:, :HEAD_DIM] + e1 * r2[:, :HEAD_DIM]
            den = r1[:, HEAD_DIM:] + e1 * r2[:, HEAD_DIM:]
            h_ref[:, hs] = (num / jnp.maximum(jnp.abs(den), colq[:, 16 + h:17 + h])).astype(h_ref.dtype)
            kw = (kth.astype(F32) * rowq[8 + h:9 + h, :]).astype(BF16)
            cst_ref[d, h] = rowq[16 + h:17 + h, 0:1] * cst + jnp.dot(kw, vext, preferred_element_type=F32)


def _mlstm(q, kt, v, gr, seq_lens):
    t = q.shape[0]
    n = t // CHUNK
    starts, ends, pos = [], [], 0
    for ln in seq_lens:
        assert ln % CHUNK == 0
        starts.append(pos // CHUNK)
        ends.append((pos + ln) // CHUNK - 1)
        pos += ln
    reset_f = jnp.zeros((n,), jnp.int32).at[jnp.array(starts)].set(1)
    reset_b = jnp.zeros((n,), jnp.int32).at[jnp.array(ends)].set(1)
    rowf, colf = _gateprep(gr, reset_f, rev=False)
    rowb, colb = _gateprep(gr, reset_b, rev=True)

    def specs(cidx):
        return [pl.BlockSpec((CHUNK, D_MLSTM), lambda s, rf, rb: (cidx(s), 0)),
                pl.BlockSpec((D_MLSTM, CHUNK), lambda s, rf, rb: (0, cidx(s))),
                pl.BlockSpec((CHUNK, D_MLSTM), lambda s, rf, rb: (cidx(s), 0)),
                pl.BlockSpec((None, QROWS, CHUNK), lambda s, rf, rb: (cidx(s), 0, 0)),
                pl.BlockSpec((None, CHUNK, QROWS), lambda s, rf, rb: (cidx(s), 0, 0))]

    fwd = lambda s: s
    bwd = lambda s: n - 1 - s
    grid_spec = pltpu.PrefetchScalarGridSpec(
        num_scalar_prefetch=2,
        grid=(n,),
        in_specs=specs(fwd) + specs(bwd),
        out_specs=[pl.BlockSpec((CHUNK, D_MLSTM), lambda s, rf, rb: (fwd(s), 0)),
                   pl.BlockSpec((CHUNK, D_MLSTM), lambda s, rf, rb: (bwd(s), 0))],
        scratch_shapes=[pltpu.VMEM((N_DIR, N_HEADS, HEAD_DIM, 2 * HEAD_DIM), F32)],
    )
    return pl.pallas_call(
        _mlstm_kernel,
        grid_spec=grid_spec,
        out_shape=[jax.ShapeDtypeStruct((t, D_MLSTM), BF16)] * 2,
        compiler_params=_cparams(),
        name="mlstm",
    )(reset_f, reset_b, q, kt, v, rowf, colf, q, kt, v, rowb, colb)


def _mixout_kernel(first_ref, last_ref,
                   x_ref, hf_ref, hb_ref, o_ref, u_ref, up_ref, un_ref, cw_ref, cb_ref, lng_ref, lnb_ref,
                   wout_ref, gffn_ref, wr_ref, br_ref, tri_ref,
                   x1_ref, hn_ref, meta_ref, metat_ref, cnt_ref,
                   win_ref, y_ref, run_ref):
    i = pl.program_id(0)
    tm = hf_ref.shape[0]

    @pl.when(i == 0)
    def _():
        run_ref[...] = jnp.zeros_like(run_ref)

    h_sum = hf_ref[...].astype(F32) + hb_ref[...].astype(F32)
    ym = (jax.nn.sigmoid(o_ref[...].astype(F32)) * h_sum).astype(BF16)
    x1m = x_ref[...] + jnp.dot(ym, wout_ref[:D_MLSTM, :], preferred_element_type=F32)

    win_ref[0:CONV_HALO, :] = jnp.where(first_ref[i] == 1, 0.0, up_ref[...].astype(F32))
    win_ref[CONV_HALO:CONV_HALO + tm, :] = u_ref[...].astype(F32)
    win_ref[CONV_HALO + tm:, :] = jnp.where(last_ref[i] == 1, 0.0, un_ref[...].astype(F32))

    off0 = CONV_HALO - CONV_WIDTH // 2
    for r0 in range(0, tm, CONV_ROWS):
        tiles = []
        for lt in range(D_CONV // 128):
            ls = slice(lt * 128, (lt + 1) * 128)
            acc = jnp.broadcast_to(cb_ref[:, ls], (CONV_ROWS, 128))
            for s in range(8):
                part = None
                for j in range(CONV_WIDTH):
                    if (off0 + j) % 8 != s:
                        continue
                    base = (off0 + j) // 8 * 8
                    term = win_ref[r0 + base:r0 + base + CONV_ROWS + 8, ls] * cw_ref[j:j + 1, ls]
                    part = term if part is None else part + term
                acc = acc + part[s:s + CONV_ROWS, :]
            tiles.append(acc)
        cv = jnp.concatenate(tiles, axis=1)
        xc = cv - jnp.mean(cv, axis=-1, keepdims=True)
        yc = xc * lax.rsqrt(jnp.mean(xc * xc, axis=-1, keepdims=True) + EPS) * lng_ref[...] + lnb_ref[...]
        y_ref[r0:r0 + CONV_ROWS, :] = (yc * jax.nn.sigmoid(yc)).astype(BF16)

    x1 = x1m + jnp.dot(y_ref[...], wout_ref[D_MLSTM:, :], preferred_element_type=F32)
    x1_ref[...] = x1
    hn = _rms(x1, gffn_ref[...])
    hn_ref[...] = _pack_rows(hn)

    logits = jnp.dot(hn.astype(BF16), wr_ref[...], preferred_element_type=F32) + br_ref[...]
    lane = lax.broadcasted_iota(jnp.int32, (tm, ROUTER_LANES), 1).astype(F32)
    neg = -jnp.inf
    no_lane = float(ROUTER_LANES)
    gl = jnp.where(lane < N_GROUPS, logits, neg)
    gmax = jnp.max(gl, axis=1, keepdims=True)
    p_top = 1.0 / jnp.sum(jnp.exp(gl - gmax), axis=1, keepdims=True)
    g_idx = jnp.min(jnp.where(gl == gmax, lane, no_lane), axis=1, keepdims=True)
    lo = EXPERT_LANE0 + EXPERTS_PER_GROUP * g_idx
    in_grp = (lane >= lo) & (lane < lo + EXPERTS_PER_GROUP)
    el = jnp.where(in_grp, logits, neg)
    ee = jnp.exp(el - jnp.max(el, axis=1, keepdims=True))
    pe = jnp.where(in_grp, ee / jnp.sum(ee, axis=1, keepdims=True), -1.0)
    v1 = jnp.max(pe, axis=1, keepdims=True)
    i1 = jnp.min(jnp.where(pe == v1, lane, no_lane), axis=1, keepdims=True)
    pe2 = jnp.where(lane == i1, -1.0, pe)
    v2 = jnp.max(pe2, axis=1, keepdims=True)
    i2 = jnp.min(jnp.where(pe2 == v2, lane, no_lane), axis=1, keepdims=True)
    wsum = v1 + v2
    gate1 = p_top * (v1 / wsum)
    gate2 = p_top * (v2 / wsum)

    oh1 = (lane == i1).astype(F32)
    oh2 = (lane == i2).astype(F32)
    c1 = jnp.dot(tri_ref[...], oh1.astype(BF16), preferred_element_type=F32)
    c2 = jnp.dot(tri_ref[...], oh2.astype(BF16), preferred_element_type=F32)
    run = run_ref[...]
    tot1 = jnp.sum(oh1, axis=0, keepdims=True)
    tot2 = jnp.sum(oh2, axis=0, keepdims=True)
    rank1 = jnp.sum(oh1 * (run + c1), axis=1, keepdims=True)
    rank2 = jnp.sum(oh2 * (run + tot1 + c2), axis=1, keepdims=True)
    run = run + tot1 + tot2
    run_ref[...] = run
    cnt_ref[...] = run

    ml = lax.broadcasted_iota(jnp.int32, (tm, META_LANES), 1)
    meta = jnp.where(ml == 0, i1 - EXPERT_LANE0, 0.0)
    meta = jnp.where(ml == 1, i2 - EXPERT_LANE0, meta)
    meta = jnp.where(ml == 2, gate1, meta)
    meta = jnp.where(ml == 3, gate2, meta)
    meta = jnp.where(ml == 4, rank1, meta)
    meta = jnp.where(ml == 5, rank2, meta)
    meta_ref[...] = meta
    metat_ref[...] = meta.T


def _mixout(x, tile0, hf, hb, o, u, seq_lens, conv_w, conv_b, ln_g, ln_b, w_out, g_ffn, w_rg, b_rg, w_re, b_re):
    t = x.shape[0]
    n_tiles = t // TM_MIX
    hpt = TM_MIX // CONV_HALO
    n_halo = u.shape[0] // CONV_HALO
    firsts, lasts, pos = [], [], 0
    for ln in seq_lens:
        assert ln % TM_MIX == 0
        firsts.append(pos // TM_MIX)
        lasts.append((pos + ln) // TM_MIX - 1)
        pos += ln
    assert pos == t
    first = jnp.zeros((n_tiles,), jnp.int32).at[jnp.array(firsts)].set(1)
    last = jnp.zeros((n_tiles,), jnp.int32).at[jnp.array(lasts)].set(1)

    cw = jnp.zeros((32, D_CONV), F32).at[:CONV_WIDTH].set(conv_w.reshape(CONV_WIDTH, D_CONV))
    wr = jnp.zeros((D_MODEL, ROUTER_LANES), F32)
    wr = wr.at[:, :N_GROUPS].set(w_rg).at[:, EXPERT_LANE0:EXPERT_LANE0 + N_EXPERTS].set(w_re).astype(BF16)
    br = jnp.zeros((1, ROUTER_LANES), F32)
    br = br.at[0, :N_GROUPS].set(b_rg).at[0, EXPERT_LANE0:EXPERT_LANE0 + N_EXPERTS].set(b_re)
    tri = (lax.broadcasted_iota(jnp.int32, (TM_MIX, TM_MIX), 0)
           > lax.broadcasted_iota(jnp.int32, (TM_MIX, TM_MIX), 1)).astype(BF16)

    tok = lambda i, f, l: (i, 0)
    flat = lambda i, f, l: (tile0 + i, 0)
    fixed = lambda i, f, l: (0, 0)
    row = lambda n: pl.BlockSpec((1, n), fixed)
    half = pl.BlockSpec((TM_MIX, D_MLSTM), flat)
    grid_spec = pltpu.PrefetchScalarGridSpec(
        num_scalar_prefetch=2,
        grid=(n_tiles,),
        in_specs=[pl.BlockSpec((TM_MIX, D_MODEL), tok), half, half, half,
                  pl.BlockSpec((TM_MIX, D_CONV), flat),
                  pl.BlockSpec((CONV_HALO, D_CONV), lambda i, f, l: (jnp.maximum((tile0 + i) * hpt - 1, 0), 0)),
                  pl.BlockSpec((CONV_HALO, D_CONV),
                               lambda i, f, l: (jnp.minimum((tile0 + i + 1) * hpt, n_halo - 1), 0)),
                  pl.BlockSpec((32, D_CONV), fixed), row(D_CONV), row(D_CONV), row(D_CONV),
                  pl.BlockSpec((D_MODEL, D_MODEL), fixed), row(D_MODEL),
                  pl.BlockSpec((D_MODEL, ROUTER_LANES), fixed), row(ROUTER_LANES),
                  pl.BlockSpec((TM_MIX, TM_MIX), fixed)],
        out_specs=[pl.BlockSpec((TM_MIX, D_MODEL), tok), pl.BlockSpec((TM_MIX, D_MODEL // 2), tok),
                   pl.BlockSpec((TM_MIX, META_LANES), tok),
                   pl.BlockSpec((META_LANES, TM_MIX), lambda i, f, l: (0, i)), row(ROUTER_LANES)],
        scratch_shapes=[pltpu.VMEM((TM_MIX + 2 * CONV_HALO, D_CONV), F32),
                        pltpu.VMEM((TM_MIX, D_CONV), BF16),
                        pltpu.VMEM((1, ROUTER_LANES), F32)],
    )
    return pl.pallas_call(
        _mixout_kernel,
        grid_spec=grid_spec,
        out_shape=[jax.ShapeDtypeStruct((t, D_MODEL), F32), jax.ShapeDtypeStruct((t, D_MODEL // 2), jnp.int32),
                   jax.ShapeDtypeStruct((t, META_LANES), F32), jax.ShapeDtypeStruct((META_LANES, t), F32),
                   jax.ShapeDtypeStruct((1, ROUTER_LANES), F32)],
        compiler_params=_cparams(),
        name="mixout",
    )(first, last, x, hf, hb, o, u, u, u, cw, conv_b.reshape(1, D_CONV), ln_g.reshape(1, D_CONV),
      ln_b.reshape(1, D_CONV), w_out.astype(BF16), g_ffn.reshape(1, D_MODEL), wr, br, tri)


def _sc_mesh():
    return plsc.VectorSubcoreMesh(core_axis_name="c", subcore_axis_name="s")


def _sc_worker_base(per_worker):
    wid = lax.axis_index("s") * SC_CORES + lax.axis_index("c")
    return wid * per_worker


def _dispatch(hn, pos1, pos2, n_slots):
    t, d = hn.shape
    per_worker = t // SC_WORKERS
    assert per_worker * SC_WORKERS == t and per_worker % SC_ROWS == 0

    def body(h_hbm, p1_hbm, p2_hbm, out_hbm, i1_v, i2_v, rows_v, sem1, sem2):
        base0 = _sc_worker_base(per_worker)

        @pl.loop(0, per_worker // SC_ROWS)
        def _(c):
            base = pl.multiple_of(base0 + c * SC_ROWS, SC_ROWS)
            pltpu.sync_copy(p1_hbm.at[pl.ds(base, SC_ROWS)], i1_v)
            pltpu.sync_copy(p2_hbm.at[pl.ds(base, SC_ROWS)], i2_v)
            pltpu.sync_copy(h_hbm.at[pl.ds(base, SC_ROWS)], rows_v)
            c1 = pltpu.async_copy(rows_v, out_hbm.at[i1_v], sem1)
            c2 = pltpu.async_copy(rows_v, out_hbm.at[i2_v], sem2)
            c1.wait()
            c2.wait()

    return pl.kernel(
        body,
        out_type=jax.ShapeDtypeStruct((n_slots, d), hn.dtype),
        mesh=_sc_mesh(),
        scratch_types=[pltpu.VMEM((SC_ROWS,), jnp.int32), pltpu.VMEM((SC_ROWS,), jnp.int32),
                       pltpu.VMEM((SC_ROWS, d), hn.dtype), pltpu.SemaphoreType.DMA, pltpu.SemaphoreType.DMA],
        name="dispatch",
    )(hn, pos1, pos2)


def _collect(ys, pos1, pos2):
    t = pos1.shape[0]
    d = ys.shape[1]
    per_worker = t // SC_WORKERS
    assert per_worker * SC_WORKERS == t and per_worker % SC_ROWS == 0

    def body(ys_hbm, p1_hbm, p2_hbm, y1_hbm, y2_hbm, i_v, rows_v, sem):
        base0 = _sc_worker_base(per_worker)

        @pl.loop(0, per_worker // SC_ROWS)
        def _(c):
            base = pl.multiple_of(base0 + c * SC_ROWS, SC_ROWS)
            for p_hbm, y_hbm in ((p1_hbm, y1_hbm), (p2_hbm, y2_hbm)):
                pltpu.sync_copy(p_hbm.at[pl.ds(base, SC_ROWS)], i_v)
                pltpu.async_copy(ys_hbm.at[i_v], rows_v, sem).wait()
                pltpu.sync_copy(rows_v, y_hbm.at[pl.ds(base, SC_ROWS)])

    out = jax.ShapeDtypeStruct((t, d), ys.dtype)
    return pl.kernel(
        body,
        out_type=(out, out),
        mesh=_sc_mesh(),
        scratch_types=[pltpu.VMEM((SC_ROWS,), jnp.int32), pltpu.VMEM((SC_ROWS, d), ys.dtype),
                       pltpu.SemaphoreType.DMA],
        name="collect",
    )(ys, pos1, pos2)


def _experts_kernel(te_ref, rows_ref, hs_ref, wg_ref, wu_ref, wd_ref, ys_ref, wgb_ref, wub_ref, wdb_ref):
    i = pl.program_id(0)

    @pl.when((i == 0) | (te_ref[i] != te_ref[jnp.maximum(i - 1, 0)]))
    def _():
        wgb_ref[...] = wg_ref[...].astype(BF16)
        wub_ref[...] = wu_ref[...].astype(BF16)
        wdb_ref[...] = wd_ref[...].astype(BF16)

    @pl.when(rows_ref[i] > 0)
    def _():
        x = _unpack_rows(hs_ref[...]).astype(BF16)
        hg = jnp.dot(x, wgb_ref[...], preferred_element_type=F32)
        hu = jnp.dot(x, wub_ref[...], preferred_element_type=F32)
        hid = (hg * jax.nn.sigmoid(hg) * hu).astype(BF16)
        ys_ref[...] = _pack_rows(jnp.dot(hid, wdb_ref[...], preferred_element_type=F32))

    @pl.when(rows_ref[i] == 0)
    def _():
        ys_ref[...] = jnp.zeros_like(ys_ref)


def _experts(hs, tile_expert, tile_rows, w_gate, w_up, w_down):
    n_tiles = tile_expert.shape[0]
    wsel = lambda i, te, rows: (te[i], 0, 0)
    slot = lambda i, te, rows: (i, 0)
    grid_spec = pltpu.PrefetchScalarGridSpec(
        num_scalar_prefetch=2,
        grid=(n_tiles,),
        in_specs=[pl.BlockSpec((TM_EXP, D_MODEL // 2), slot),
                  pl.BlockSpec((None, D_MODEL, D_EXPERT), wsel),
                  pl.BlockSpec((None, D_MODEL, D_EXPERT), wsel),
                  pl.BlockSpec((None, D_EXPERT, D_MODEL), wsel)],
        out_specs=pl.BlockSpec((TM_EXP, D_MODEL // 2), slot),
        scratch_shapes=[pltpu.VMEM((D_MODEL, D_EXPERT), BF16), pltpu.VMEM((D_MODEL, D_EXPERT), BF16),
                        pltpu.VMEM((D_EXPERT, D_MODEL), BF16)],
    )
    return pl.pallas_call(
        _experts_kernel,
        grid_spec=grid_spec,
        out_shape=jax.ShapeDtypeStruct((n_tiles * TM_EXP, D_MODEL // 2), jnp.int32),
        compiler_params=_cparams(),
        name="experts",
    )(tile_expert, tile_rows, hs, w_gate, w_up, w_down)


def _combine_kernel(x1_ref, meta_ref, y1_ref, y2_ref, gfin_ref, out_ref):
    meta = meta_ref[...]
    x2 = x1_ref[...] + meta[:, 2:3] * _unpack_rows(y1_ref[...]) + meta[:, 3:4] * _unpack_rows(y2_ref[...])
    out_ref[...] = _rms(x2, gfin_ref[...])


def _combine(x1, meta, y1, y2, g_final):
    t = x1.shape[0]
    assert t % TM_OUT == 0
    tok = pl.BlockSpec((TM_OUT, D_MODEL), lambda i: (i, 0))
    packed = pl.BlockSpec((TM_OUT, D_MODEL // 2), lambda i: (i, 0))
    return pl.pallas_call(
        _combine_kernel,
        grid=(t // TM_OUT,),
        in_specs=[tok, pl.BlockSpec((TM_OUT, META_LANES), lambda i: (i, 0)), packed, packed,
                  pl.BlockSpec((1, D_MODEL), lambda i: (0, 0))],
        out_specs=tok,
        out_shape=jax.ShapeDtypeStruct((t, D_MODEL), F32),
        compiler_params=_cparams(),
        name="combine",
    )(x1, meta, y1, y2, g_final.reshape(1, D_MODEL))


def _slot_layout(metat, counts_row, n_tiles):
    eid = metat[0:2].astype(jnp.int32)
    rank = metat[4:6].astype(jnp.int32)
    counts = counts_row[0, EXPERT_LANE0:EXPERT_LANE0 + N_EXPERTS].astype(jnp.int32)
    padded = (counts + TM_EXP - 1) // TM_EXP * TM_EXP
    ends = jnp.cumsum(padded)
    offs = ends - padded
    pos = rank
    for e in range(N_EXPERTS):
        pos = pos + jnp.where(eid == e, offs[e], 0)
    tile_start = jnp.arange(n_tiles, dtype=jnp.int32) * TM_EXP
    tile_expert = jnp.minimum(jnp.sum(tile_start[:, None] >= ends[None, :], axis=1), N_EXPERTS - 1)
    tile_expert = tile_expert.astype(jnp.int32)
    tile_rows = jnp.clip(counts[tile_expert] - (tile_start - offs[tile_expert]), 0, TM_EXP)
    tile_rows = jnp.where(tile_start < ends[-1], tile_rows, 0).astype(jnp.int32)
    return pos[0], pos[1], tile_expert, tile_rows


def kernel(x_prompt, x_sample, g_mix, w_in, b_gate, conv_w, conv_b, ln_g, ln_b, w_out, g_ffn,
           w_router_group, b_router_group, w_router_expert, b_router_expert, w_gate, w_up, w_down, g_final):
    assert g_mix.shape[0] == 1, "one layer"
    bp, lp, _ = x_prompt.shape
    bs, ls, _ = x_sample.shape
    seq_lens = [lp] * bp + [ls] * bs
    tp, ts = bp * lp, bs * ls
    xp = x_prompt.reshape(tp, D_MODEL)
    xs = x_sample.reshape(ts, D_MODEL)

    q, v, o, u, kt, gr = _inproj(xp, xs, g_mix[0], w_in[0], b_gate[0])
    hf, hb = _mlstm(q, kt, v, gr, seq_lens)

    outs = []
    for x, tile0, lens in ((xp, 0, [lp] * bp), (xs, tp // TM_MIX, [ls] * bs)):
        tg = x.shape[0]
        x1, hn, meta, metat, counts = _mixout(x, tile0, hf, hb, o, u, lens, conv_w[0], conv_b[0], ln_g[0], ln_b[0],
                                              w_out[0], g_ffn[0], w_router_group[0], b_router_group[0],
                                              w_router_expert[0], b_router_expert[0])
        n_tiles = (2 * tg + N_EXPERTS * (TM_EXP - 1)) // TM_EXP + 1
        pos1, pos2, tile_expert, tile_rows = _slot_layout(metat, counts, n_tiles)
        hs = _dispatch(hn, pos1, pos2, n_tiles * TM_EXP)
        ys = _experts(hs, tile_expert, tile_rows, w_gate[0], w_up[0], w_down[0])
        y1, y2 = _collect(ys, pos1, pos2)
        outs.append(_combine(x1, meta, y1, y2, g_final))
    return outs[0].reshape(bp, lp, D_MODEL), outs[1].reshape(bs, ls, D_MODEL)
```

```python
import functools

import jax
import jax.numpy as jnp
from jax import lax
from jax.experimental import pallas as pl
from jax.experimental.pallas import tpu as pltpu
from jax.experimental.pallas import tpu_sc as plsc

F32 = jnp.float32
BF16 = jnp.bfloat16

D_MODEL = 1024
N_HEADS = 4
HEAD_DIM = 128
D_MLSTM = N_HEADS * HEAD_DIM
D_CONV = D_MODEL - D_MLSTM
CONV_WIDTH = 31
CONV_HALO = 16
N_DIR = 2
N_GROUPS = 4
EXPERTS_PER_GROUP = 4
N_EXPERTS = N_GROUPS * EXPERTS_PER_GROUP
D_EXPERT = 512
EPS = 1e-6
K_SCALE = HEAD_DIM ** -0.5

GATE_ROWS = 16
QROWS = 24
ROUTER_LANES = 128
EXPERT_LANE0 = N_GROUPS
META_LANES = 8

TM_IN = 1024
CHUNK = 256
GP_CHUNKS = 32
TM_MIX = 512
CONV_ROWS = 128
TM_EXP = 512
TM_OUT = 1024
VMEM_LIMIT = 48 * 1024 * 1024

SC_CORES = 2
SC_SUBCORES = 16
SC_WORKERS = SC_CORES * SC_SUBCORES
SC_ROWS = 128


def _cparams(n_axes=1):
    return pltpu.CompilerParams(dimension_semantics=("arbitrary",) * n_axes,
                                vmem_limit_bytes=VMEM_LIMIT)


def _nt_dot(a, b):
    return lax.dot_general(a, b, (((1,), (1,)), ((), ())), preferred_element_type=F32)


def _rms(x, g):
    return x * lax.rsqrt(jnp.mean(x * x, axis=-1, keepdims=True) + EPS) * g


def _pack_rows(x):
    n = x.shape[1] // 2
    hi = lax.bitcast_convert_type(x[:, :n].astype(jnp.bfloat16).astype(F32), jnp.int32)
    lo = lax.bitcast_convert_type(x[:, n:].astype(jnp.bfloat16).astype(F32), jnp.int32)
    return hi | lax.shift_right_logical(lo, 16)


def _unpack_rows(p):
    hi = lax.bitcast_convert_type(p & jnp.int32(-65536), F32)
    lo = lax.bitcast_convert_type(lax.shift_left(p, 16), F32)
    return jnp.concatenate([hi, lo], axis=1)


def _two_batch_specs(block, n_first, n_second):
    first = pl.BlockSpec(block, lambda i, *_: (jnp.minimum(i, n_first - 1), 0))
    second = pl.BlockSpec(block, lambda i, *_: (jnp.maximum(i - n_first, 0), 0))
    return first, second


def _inproj_kernel(xp_ref, xs_ref, g_ref, wq_ref, wv_ref, wo_ref, wa_ref, wb_ref, wkt_ref, wgt_ref, bg_ref,
                   q_ref, v_ref, o_ref, u_ref, kt_ref, gr_ref, *, n_first):
    x = jnp.where(pl.program_id(0) < n_first, xp_ref[...], xs_ref[...])
    xn = _rms(x, g_ref[...]).astype(BF16)
    q_ref[...] = jnp.dot(xn, wq_ref[...], preferred_element_type=F32).astype(BF16)
    v_ref[...] = jnp.dot(xn, wv_ref[...], preferred_element_type=F32).astype(BF16)
    o_ref[...] = jnp.dot(xn, wo_ref[...], preferred_element_type=F32).astype(BF16)
    a = jnp.dot(xn, wa_ref[...], preferred_element_type=F32)
    b = jnp.dot(xn, wb_ref[...], preferred_element_type=F32)
    u_ref[...] = (a * jax.nn.sigmoid(b)).astype(BF16)
    kt_ref[...] = (_nt_dot(wkt_ref[...], xn) * K_SCALE).astype(BF16)
    gr = _nt_dot(wgt_ref[...], xn) + bg_ref[...]
    for c in range(gr_ref.shape[0]):
        gr_ref[c] = gr[:, c * CHUNK:(c + 1) * CHUNK]


def _inproj(xp, xs, g_mix, w_in, b_gate):
    tp, ts = xp.shape[0], xs.shape[0]
    t = tp + ts
    assert tp % TM_IN == 0 and ts % TM_IN == 0
    off_k, off_v, off_o, off_g = D_MLSTM, 2 * D_MLSTM, 3 * D_MLSTM, 4 * D_MLSTM
    off_a = off_g + 2 * N_DIR * N_HEADS
    off_b = off_a + D_CONV
    wq = w_in[:, 0:off_k].astype(BF16)
    wkt = w_in[:, off_k:off_v].T.astype(BF16)
    wv = w_in[:, off_v:off_o].astype(BF16)
    wo = w_in[:, off_o:off_g].astype(BF16)
    wa = w_in[:, off_a:off_b].astype(BF16)
    wb = w_in[:, off_b:off_b + D_CONV].astype(BF16)
    wg = w_in[:, off_g:off_a].T.reshape(N_DIR, 2, N_HEADS, D_MODEL)
    wgt = jnp.zeros((N_DIR, 2, GATE_ROWS // 2, D_MODEL), F32).at[:, :, :N_HEADS].set(wg)
    wgt = wgt.reshape(N_DIR * GATE_ROWS, D_MODEL).astype(BF16)
    bg = jnp.zeros((N_DIR, 2, GATE_ROWS // 2), F32).at[:, :, :N_HEADS].set(
        b_gate.reshape(N_DIR, 2, N_HEADS)).reshape(N_DIR * GATE_ROWS, 1)

    tok = lambda i: (i, 0)
    fixed = lambda i: (0, 0)
    wspec = pl.BlockSpec((D_MODEL, D_MLSTM), fixed)
    xp_spec, xs_spec = _two_batch_specs((TM_IN, D_MODEL), tp // TM_IN, ts // TM_IN)
    cpt = TM_IN // CHUNK
    return pl.pallas_call(
        functools.partial(_inproj_kernel, n_first=tp // TM_IN),
        grid=(t // TM_IN,),
        in_specs=[xp_spec, xs_spec, pl.BlockSpec((1, D_MODEL), fixed),
                  wspec, wspec, wspec, wspec, wspec,
                  pl.BlockSpec((D_MLSTM, D_MODEL), fixed),
                  pl.BlockSpec((N_DIR * GATE_ROWS, D_MODEL), fixed),
                  pl.BlockSpec((N_DIR * GATE_ROWS, 1), fixed)],
        out_specs=[pl.BlockSpec((TM_IN, D_MLSTM), tok)] * 4 + [
            pl.BlockSpec((D_MLSTM, TM_IN), lambda i: (0, i)),
            pl.BlockSpec((cpt, N_DIR * GATE_ROWS, CHUNK), lambda i: (i, 0, 0))],
        out_shape=[jax.ShapeDtypeStruct((t, D_MLSTM), BF16)] * 4 + [
            jax.ShapeDtypeStruct((D_MLSTM, t), BF16),
            jax.ShapeDtypeStruct((t // CHUNK, N_DIR * GATE_ROWS, CHUNK), F32)],
        compiler_params=_cparams(),
        name="inproj",
    )(xp, xs, g_mix.reshape(1, D_MODEL), wq, wv, wo, wa, wb, wkt, wgt, bg)


def _log_sigmoid(x):
    return jnp.minimum(x, 0.0) - jnp.log1p(jnp.exp(-jnp.abs(x)))


def _gateprep_kernel(reset_ref, g_ref, rowq_ref, colq_ref, m_ref, *, rev):
    n, _, c = g_ref.shape
    step = pl.program_id(0)
    blk = pl.num_programs(0) - 1 - step if rev else step

    @pl.when(step == 0)
    def _():
        m_ref[...] = jnp.zeros_like(m_ref)

    ig = g_ref[:, 0:8, :]
    lf = _log_sigmoid(g_ref[:, 8:16, :])
    lane = lax.broadcasted_iota(jnp.int32, (n, 8, c), 2)

    def scan(x, op, ident):
        k = 1
        while k < c:
            if rev:
                shifted, valid = pltpu.roll(x, c - k, axis=2), lane < c - k
            else:
                shifted, valid = pltpu.roll(x, k, axis=2), lane >= k
            x = op(x, jnp.where(valid, shifted, ident))
            k *= 2
        return x

    bc = scan(lf, jnp.add, 0.0)
    a = ig - bc
    cm = scan(a, jnp.maximum, -jnp.inf)
    b_tot = jnp.sum(lf, axis=2, keepdims=True)
    a_max = jnp.max(a, axis=2, keepdims=True)

    m = m_ref[...]
    m_in = [None] * n
    for j in (range(n - 1, -1, -1) if rev else range(n)):
        m = jnp.where(reset_ref[blk * n + j] == 1, 0.0, m)
        m_in[j] = m
        m = b_tot[j] + jnp.maximum(m, a_max[j])
    m_ref[...] = m
    m_old = jnp.stack(m_in)

    mx = jnp.maximum(m_old, cm)
    mx_last = jnp.maximum(m_old, a_max)
    rowq_ref[:, 0:8, :] = a
    rowq_ref[:, 8:16, :] = jnp.exp(a - mx_last)
    rowq_ref[:, 16:24, :] = jnp.exp(m_old - mx_last)
    e1 = jnp.exp(m_old - mx)
    fl = jnp.exp(-(mx + bc))
    for j in range(n):
        colq_ref[j] = jnp.concatenate([mx[j], e1[j], fl[j]], axis=0).T


def _gateprep(gr, reset, rev):
    n_chunks = gr.shape[0]
    gp = min(GP_CHUNKS, n_chunks)
    assert n_chunks % gp == 0
    nb = n_chunks // gp
    d = 1 if rev else 0
    bidx = (lambda s: nb - 1 - s) if rev else (lambda s: s)
    grid_spec = pltpu.PrefetchScalarGridSpec(
        num_scalar_prefetch=1,
        grid=(nb,),
        in_specs=[pl.BlockSpec((gp, GATE_ROWS, CHUNK), lambda s, r: (bidx(s), d, 0))],
        out_specs=[pl.BlockSpec((gp, QROWS, CHUNK), lambda s, r: (bidx(s), 0, 0)),
                   pl.BlockSpec((gp, CHUNK, QROWS), lambda s, r: (bidx(s), 0, 0))],
        scratch_shapes=[pltpu.VMEM((8, CHUNK), F32)],
    )
    return pl.pallas_call(
        functools.partial(_gateprep_kernel, rev=rev),
        grid_spec=grid_spec,
        out_shape=[jax.ShapeDtypeStruct((n_chunks, QROWS, CHUNK), F32),
                   jax.ShapeDtypeStruct((n_chunks, CHUNK, QROWS), F32)],
        compiler_params=_cparams(),
        name="gateprep_bwd" if rev else "gateprep_fwd",
    )(reset, gr)


def _mlstm_kernel(rf_ref, rb_ref,
                  qf_ref, ktf_ref, vf_ref, rowf_ref, colf_ref,
                  qb_ref, ktb_ref, vb_ref, rowb_ref, colb_ref,
                  hf_ref, hb_ref, cst_ref):
    c = qf_ref.shape[0]
    step = pl.program_id(0)
    last = pl.num_programs(0) - 1
    row_i = lax.broadcasted_iota(jnp.int32, (c, c), 0)
    col_i = lax.broadcasted_iota(jnp.int32, (c, c), 1)
    ones = jnp.ones((c, HEAD_DIM), BF16)

    dirs = ((rf_ref[step], col_i <= row_i, qf_ref, ktf_ref, vf_ref, rowf_ref, colf_ref, hf_ref),
            (rb_ref[last - step], col_i >= row_i, qb_ref, ktb_ref, vb_ref, rowb_ref, colb_ref, hb_ref))
    for d, (reset, mask, q_ref, kt_ref, v_ref, row_ref, col_ref, h_ref) in enumerate(dirs):
        @pl.when(reset == 1)
        def _():
            cst_ref[d] = jnp.zeros(cst_ref.shape[1:], F32)

        rowq = row_ref[...]
        colq = col_ref[...]
        for h in range(N_HEADS):
            hs = slice(h * HEAD_DIM, (h + 1) * HEAD_DIM)
            qh = q_ref[:, hs]
            kth = kt_ref[hs, :]
            vext = jnp.concatenate([v_ref[:, hs], ones], axis=1)
            s = jnp.dot(qh, kth, preferred_element_type=F32)
            e = jnp.exp(jnp.where(mask, rowq[h:h + 1, :] - colq[:, h:h + 1], -jnp.inf))
            r1 = jnp.dot((s * e).astype(BF16), vext, preferred_element_type=F32)
            cst = cst_ref[d, h]
            r2 = jnp.dot(qh, cst.astype(BF16), preferred_element_type=F32)
            e1 = colq[:, 8 + h:9 + h]
            num = r1[:, :HEAD_DIM] + e1 * r2[:, :HEAD_DIM]
            den = r1[:, HEAD_DIM:] + e1 * r2[:, HEAD_DIM:]
            h_ref[:, hs] = (num / jnp.maximum(jnp.abs(den), colq[:, 16 + h:17 + h])).astype(h_ref.dtype)
            kw = (kth.astype(F32) * rowq[8 + h:9 + h, :]).astype(BF16)
            cst_ref[d, h] = rowq[16 + h:17 + h, 0:1] * cst + jnp.dot(kw, vext, preferred_element_type=F32)


def _mlstm(q, kt, v, gr, seq_lens):
    t = q.shape[0]
    n = t // CHUNK
    starts, ends, pos = [], [], 0
    for ln in seq_lens:
        assert ln % CHUNK == 0
        starts.append(pos // CHUNK)
        ends.append((pos + ln) // CHUNK - 1)
        pos += ln
    reset_f = jnp.zeros((n,), jnp.int32).at[jnp.array(starts)].set(1)
    reset_b = jnp.zeros((n,), jnp.int32).at[jnp.array(ends)].set(1)
    rowf, colf = _gateprep(gr, reset_f, rev=False)
    rowb, colb = _gateprep(gr, reset_b, rev=True)

    def specs(cidx):
        return [pl.BlockSpec((CHUNK, D_MLSTM), lambda s, rf, rb: (cidx(s), 0)),
                pl.BlockSpec((D_MLSTM, CHUNK), lambda s, rf, rb: (0, cidx(s))),
                pl.BlockSpec((CHUNK, D_MLSTM), lambda s, rf, rb: (cidx(s), 0)),
                pl.BlockSpec((None, QROWS, CHUNK), lambda s, rf, rb: (cidx(s), 0, 0)),
                pl.BlockSpec((None, CHUNK, QROWS), lambda s, rf, rb: (cidx(s), 0, 0))]

    fwd = lambda s: s
    bwd = lambda s: n - 1 - s
    grid_spec = pltpu.PrefetchScalarGridSpec(
        num_scalar_prefetch=2,
        grid=(n,),
        in_specs=specs(fwd) + specs(bwd),
        out_specs=[pl.BlockSpec((CHUNK, D_MLSTM), lambda s, rf, rb: (fwd(s), 0)),
                   pl.BlockSpec((CHUNK, D_MLSTM), lambda s, rf, rb: (bwd(s), 0))],
        scratch_shapes=[pltpu.VMEM((N_DIR, N_HEADS, HEAD_DIM, 2 * HEAD_DIM), F32)],
    )
    return pl.pallas_call(
        _mlstm_kernel,
        grid_spec=grid_spec,
        out_shape=[jax.ShapeDtypeStruct((t, D_MLSTM), BF16)] * 2,
        compiler_params=_cparams(),
        name="mlstm",
    )(reset_f, reset_b, q, kt, v, rowf, colf, q, kt, v, rowb, colb)


def _mixout_kernel(first_ref, last_ref,
                   x_ref, hf_ref, hb_ref, o_ref, u_ref, up_ref, un_ref, cw_ref, cb_ref, lng_ref, lnb_ref,
                   wout_ref, gffn_ref, wr_ref, br_ref, tri_ref,
                   x1_ref, hn_ref, meta_ref, metat_ref, cnt_ref,
                   win_ref, y_ref, run_ref):
    i = pl.program_id(0)
    tm = hf_ref.shape[0]

    @pl.when(i == 0)
    def _():
        run_ref[...] = jnp.zeros_like(run_ref)

    h_sum = hf_ref[...].astype(F32) + hb_ref[...].astype(F32)
    ym = (jax.nn.sigmoid(o_ref[...].astype(F32)) * h_sum).astype(BF16)
    x1m = x_ref[...] + jnp.dot(ym, wout_ref[:D_MLSTM, :], preferred_element_type=F32)

    win_ref[0:CONV_HALO, :] = jnp.where(first_ref[i] == 1, 0.0, up_ref[...].astype(F32))
    win_ref[CONV_HALO:CONV_HALO + tm, :] = u_ref[...].astype(F32)
    win_ref[CONV_HALO + tm:, :] = jnp.where(last_ref[i] == 1, 0.0, un_ref[...].astype(F32))

    off0 = CONV_HALO - CONV_WIDTH // 2
    for r0 in range(0, tm, CONV_ROWS):
        tiles = []
        for lt in range(D_CONV // 128):
            ls = slice(lt * 128, (lt + 1) * 128)
            acc = jnp.broadcast_to(cb_ref[:, ls], (CONV_ROWS, 128))
            for s in range(8):
                part = None
                for j in range(CONV_WIDTH):
                    if (off0 + j) % 8 != s:
                        continue
                    base = (off0 + j) // 8 * 8
                    term = win_ref[r0 + base:r0 + base + CONV_ROWS + 8, ls] * cw_ref[j:j + 1, ls]
                    part = term if part is None else part + term
                acc = acc + part[s:s + CONV_ROWS, :]
            tiles.append(acc)
        cv = jnp.concatenate(tiles, axis=1)
        xc = cv - jnp.mean(cv, axis=-1, keepdims=True)
        yc = xc * lax.rsqrt(jnp.mean(xc * xc, axis=-1, keepdims=True) + EPS) * lng_ref[...] + lnb_ref[...]
        y_ref[r0:r0 + CONV_ROWS, :] = (yc * jax.nn.sigmoid(yc)).astype(BF16)

    x1 = x1m + jnp.dot(y_ref[...], wout_ref[D_MLSTM:, :], preferred_element_type=F32)
    x1_ref[...] = x1
    hn = _rms(x1, gffn_ref[...])
    hn_ref[...] = _pack_rows(hn)

    logits = jnp.dot(hn.astype(BF16), wr_ref[...], preferred_element_type=F32) + br_ref[...]
    lane = lax.broadcasted_iota(jnp.int32, (tm, ROUTER_LANES), 1).astype(F32)
    neg = -jnp.inf
    no_lane = float(ROUTER_LANES)
    gl = jnp.where(lane < N_GROUPS, logits, neg)
    gmax = jnp.max(gl, axis=1, keepdims=True)
    p_top = 1.0 / jnp.sum(jnp.exp(gl - gmax), axis=1, keepdims=True)
    g_idx = jnp.min(jnp.where(gl == gmax, lane, no_lane), axis=1, keepdims=True)
    lo = EXPERT_LANE0 + EXPERTS_PER_GROUP * g_idx
    in_grp = (lane >= lo) & (lane < lo + EXPERTS_PER_GROUP)
    el = jnp.where(in_grp, logits, neg)
    ee = jnp.exp(el - jnp.max(el, axis=1, keepdims=True))
    pe = jnp.where(in_grp, ee / jnp.sum(ee, axis=1, keepdims=True), -1.0)
    v1 = jnp.max(pe, axis=1, keepdims=True)
    i1 = jnp.min(jnp.where(pe == v1, lane, no_lane), axis=1, keepdims=True)
    pe2 = jnp.where(lane == i1, -1.0, pe)
    v2 = jnp.max(pe2, axis=1, keepdims=True)
    i2 = jnp.min(jnp.where(pe2 == v2, lane, no_lane), axis=1, keepdims=True)
    wsum = v1 + v2
    gate1 = p_top * (v1 / wsum)
    gate2 = p_top * (v2 / wsum)

    oh1 = (lane == i1).astype(F32)
    oh2 = (lane == i2).astype(F32)
    c1 = jnp.dot(tri_ref[...], oh1.astype(BF16), preferred_element_type=F32)
    c2 = jnp.dot(tri_ref[...], oh2.astype(BF16), preferred_element_type=F32)
    run = run_ref[...]
    tot1 = jnp.sum(oh1, axis=0, keepdims=True)
    tot2 = jnp.sum(oh2, axis=0, keepdims=True)
    rank1 = jnp.sum(oh1 * (run + c1), axis=1, keepdims=True)
    rank2 = jnp.sum(oh2 * (run + tot1 + c2), axis=1, keepdims=True)
    run = run + tot1 + tot2
    run_ref[...] = run
    cnt_ref[...] = run

    ml = lax.broadcasted_iota(jnp.int32, (tm, META_LANES), 1)
    meta = jnp.where(ml == 0, i1 - EXPERT_LANE0, 0.0)
    meta = jnp.where(ml == 1, i2 - EXPERT_LANE0, meta)
    meta = jnp.where(ml == 2, gate1, meta)
    meta = jnp.where(ml == 3, gate2, meta)
    meta = jnp.where(ml == 4, rank1, meta)
    meta = jnp.where(ml == 5, rank2, meta)
    meta_ref[...] = meta
    metat_ref[...] = meta.T


def _mixout(x, tile0, hf, hb, o, u, seq_lens, conv_w, conv_b, ln_g, ln_b, w_out, g_ffn, w_rg, b_rg, w_re, b_re):
    t = x.shape[0]
    n_tiles = t // TM_MIX
    hpt = TM_MIX // CONV_HALO
    n_halo = u.shape[0] // CONV_HALO
    firsts, lasts, pos = [], [], 0
    for ln in seq_lens:
        assert ln % TM_MIX == 0
        firsts.append(pos // TM_MIX)
        lasts.append((pos + ln) // TM_MIX - 1)
        pos += ln
    assert pos == t
    first = jnp.zeros((n_tiles,), jnp.int32).at[jnp.array(firsts)].set(1)
    last = jnp.zeros((n_tiles,), jnp.int32).at[jnp.array(lasts)].set(1)

    cw = jnp.zeros((32, D_CONV), F32).at[:CONV_WIDTH].set(conv_w.reshape(CONV_WIDTH, D_CONV))
    wr = jnp.zeros((D_MODEL, ROUTER_LANES), F32)
    wr = wr.at[:, :N_GROUPS].set(w_rg).at[:, EXPERT_LANE0:EXPERT_LANE0 + N_EXPERTS].set(w_re).astype(BF16)
    br = jnp.zeros((1, ROUTER_LANES), F32)
    br = br.at[0, :N_GROUPS].set(b_rg).at[0, EXPERT_LANE0:EXPERT_LANE0 + N_EXPERTS].set(b_re)
    tri = (lax.broadcasted_iota(jnp.int32, (TM_MIX, TM_MIX), 0)
           > lax.broadcasted_iota(jnp.int32, (TM_MIX, TM_MIX), 1)).astype(BF16)

    tok = lambda i, f, l: (i, 0)
    flat = lambda i, f, l: (tile0 + i, 0)
    fixed = lambda i, f, l: (0, 0)
    row = lambda n: pl.BlockSpec((1, n), fixed)
    half = pl.BlockSpec((TM_MIX, D_MLSTM), flat)
    grid_spec = pltpu.PrefetchScalarGridSpec(
        num_scalar_prefetch=2,
        grid=(n_tiles,),
        in_specs=[pl.BlockSpec((TM_MIX, D_MODEL), tok), half, half, half,
                  pl.BlockSpec((TM_MIX, D_CONV), flat),
                  pl.BlockSpec((CONV_HALO, D_CONV), lambda i, f, l: (jnp.maximum((tile0 + i) * hpt - 1, 0), 0)),
                  pl.BlockSpec((CONV_HALO, D_CONV),
                               lambda i, f, l: (jnp.minimum((tile0 + i + 1) * hpt, n_halo - 1), 0)),
                  pl.BlockSpec((32, D_CONV), fixed), row(D_CONV), row(D_CONV), row(D_CONV),
                  pl.BlockSpec((D_MODEL, D_MODEL), fixed), row(D_MODEL),
                  pl.BlockSpec((D_MODEL, ROUTER_LANES), fixed), row(ROUTER_LANES),
                  pl.BlockSpec((TM_MIX, TM_MIX), fixed)],
        out_specs=[pl.BlockSpec((TM_MIX, D_MODEL), tok), pl.BlockSpec((TM_MIX, D_MODEL // 2), tok),
                   pl.BlockSpec((TM_MIX, META_LANES), tok),
                   pl.BlockSpec((META_LANES, TM_MIX), lambda i, f, l: (0, i)), row(ROUTER_LANES)],
        scratch_shapes=[pltpu.VMEM((TM_MIX + 2 * CONV_HALO, D_CONV), F32),
                        pltpu.VMEM((TM_MIX, D_CONV), BF16),
                        pltpu.VMEM((1, ROUTER_LANES), F32)],
    )
    return pl.pallas_call(
        _mixout_kernel,
        grid_spec=grid_spec,
        out_shape=[jax.ShapeDtypeStruct((t, D_MODEL), F32), jax.ShapeDtypeStruct((t, D_MODEL // 2), jnp.int32),
                   jax.ShapeDtypeStruct((t, META_LANES), F32), jax.ShapeDtypeStruct((META_LANES, t), F32),
                   jax.ShapeDtypeStruct((1, ROUTER_LANES), F32)],
        compiler_params=_cparams(),
        name="mixout",
    )(first, last, x, hf, hb, o, u, u, u, cw, conv_b.reshape(1, D_CONV), ln_g.reshape(1, D_CONV),
      ln_b.reshape(1, D_CONV), w_out.astype(BF16), g_ffn.reshape(1, D_MODEL), wr, br, tri)


def _sc_mesh():
    return plsc.VectorSubcoreMesh(core_axis_name="c", subcore_axis_name="s")


def _sc_worker_base(per_worker):
    wid = lax.axis_index("s") * SC_CORES + lax.axis_index("c")
    return wid * per_worker


def _dispatch(hn, pos1, pos2, n_slots):
    t, d = hn.shape
    per_worker = t // SC_WORKERS
    assert per_worker * SC_WORKERS == t and per_worker % SC_ROWS == 0

    def body(h_hbm, p1_hbm, p2_hbm, out_hbm, i1_v, i2_v, rows_v, sem1, sem2):
        base0 = _sc_worker_base(per_worker)

        @pl.loop(0, per_worker // SC_ROWS)
        def _(c):
            base = pl.multiple_of(base0 + c * SC_ROWS, SC_ROWS)
            pltpu.sync_copy(p1_hbm.at[pl.ds(base, SC_ROWS)], i1_v)
            pltpu.sync_copy(p2_hbm.at[pl.ds(base, SC_ROWS)], i2_v)
            pltpu.sync_copy(h_hbm.at[pl.ds(base, SC_ROWS)], rows_v)
            c1 = pltpu.async_copy(rows_v, out_hbm.at[i1_v], sem1)
            c2 = pltpu.async_copy(rows_v, out_hbm.at[i2_v], sem2)
            c1.wait()
            c2.wait()

    return pl.kernel(
        body,
        out_type=jax.ShapeDtypeStruct((n_slots, d), hn.dtype),
        mesh=_sc_mesh(),
        scratch_types=[pltpu.VMEM((SC_ROWS,), jnp.int32), pltpu.VMEM((SC_ROWS,), jnp.int32),
                       pltpu.VMEM((SC_ROWS, d), hn.dtype), pltpu.SemaphoreType.DMA, pltpu.SemaphoreType.DMA],
        name="dispatch",
    )(hn, pos1, pos2)


def _collect(ys, pos1, pos2):
    t = pos1.shape[0]
    d = ys.shape[1]
    per_worker = t // SC_WORKERS
    assert per_worker * SC_WORKERS == t and per_worker % SC_ROWS == 0

    def body(ys_hbm, p1_hbm, p2_hbm, y1_hbm, y2_hbm, i_v, rows_v, sem):
        base0 = _sc_worker_base(per_worker)

        @pl.loop(0, per_worker // SC_ROWS)
        def _(c):
            base = pl.multiple_of(base0 + c * SC_ROWS, SC_ROWS)
            for p_hbm, y_hbm in ((p1_hbm, y1_hbm), (p2_hbm, y2_hbm)):
                pltpu.sync_copy(p_hbm.at[pl.ds(base, SC_ROWS)], i_v)
                pltpu.async_copy(ys_hbm.at[i_v], rows_v, sem).wait()
                pltpu.sync_copy(rows_v, y_hbm.at[pl.ds(base, SC_ROWS)])

    out = jax.ShapeDtypeStruct((t, d), ys.dtype)
    return pl.kernel(
        body,
        out_type=(out, out),
        mesh=_sc_mesh(),
        scratch_types=[pltpu.VMEM((SC_ROWS,), jnp.int32), pltpu.VMEM((SC_ROWS, d), ys.dtype),
                       pltpu.SemaphoreType.DMA],
        name="collect",
    )(ys, pos1, pos2)


def _experts_kernel(te_ref, rows_ref, hs_ref, wg_ref, wu_ref, wd_ref, ys_ref, wgb_ref, wub_ref, wdb_ref):
    i = pl.program_id(0)

    @pl.when((i == 0) | (te_ref[i] != te_ref[jnp.maximum(i - 1, 0)]))
    def _():
        wgb_ref[...] = wg_ref[...].astype(BF16)
        wub_ref[...] = wu_ref[...].astype(BF16)
        wdb_ref[...] = wd_ref[...].astype(BF16)

    @pl.when(rows_ref[i] > 0)
    def _():
        x = _unpack_rows(hs_ref[...]).astype(BF16)
        hg = jnp.dot(x, wgb_ref[...], preferred_element_type=F32)
        hu = jnp.dot(x, wub_ref[...], preferred_element_type=F32)
        hid = (hg * jax.nn.sigmoid(hg) * hu).astype(BF16)
        ys_ref[...] = _pack_rows(jnp.dot(hid, wdb_ref[...], preferred_element_type=F32))

    @pl.when(rows_ref[i] == 0)
    def _():
        ys_ref[...] = jnp.zeros_like(ys_ref)


def _experts(hs, tile_expert, tile_rows, w_gate, w_up, w_down):
    n_tiles = tile_expert.shape[0]
    wsel = lambda i, te, rows: (te[i], 0, 0)
    slot = lambda i, te, rows: (i, 0)
    grid_spec = pltpu.PrefetchScalarGridSpec(
        num_scalar_prefetch=2,
        grid=(n_tiles,),
        in_specs=[pl.BlockSpec((TM_EXP, D_MODEL // 2), slot),
                  pl.BlockSpec((None, D_MODEL, D_EXPERT), wsel),
                  pl.BlockSpec((None, D_MODEL, D_EXPERT), wsel),
                  pl.BlockSpec((None, D_EXPERT, D_MODEL), wsel)],
        out_specs=pl.BlockSpec((TM_EXP, D_MODEL // 2), slot),
        scratch_shapes=[pltpu.VMEM((D_MODEL, D_EXPERT), BF16), pltpu.VMEM((D_MODEL, D_EXPERT), BF16),
                        pltpu.VMEM((D_EXPERT, D_MODEL), BF16)],
    )
    return pl.pallas_call(
        _experts_kernel,
        grid_spec=grid_spec,
        out_shape=jax.ShapeDtypeStruct((n_tiles * TM_EXP, D_MODEL // 2), jnp.int32),
        compiler_params=_cparams(),
        name="experts",
    )(tile_expert, tile_rows, hs, w_gate, w_up, w_down)


def _combine_kernel(x1_ref, meta_ref, y1_ref, y2_ref, gfin_ref, out_ref):
    meta = meta_ref[...]
    x2 = x1_ref[...] + meta[:, 2:3] * _unpack_rows(y1_ref[...]) + meta[:, 3:4] * _unpack_rows(y2_ref[...])
    out_ref[...] = _rms(x2, gfin_ref[...])


def _combine(x1, meta, y1, y2, g_final):
    t = x1.shape[0]
    assert t % TM_OUT == 0
    tok = pl.BlockSpec((TM_OUT, D_MODEL), lambda i: (i, 0))
    packed = pl.BlockSpec((TM_OUT, D_MODEL // 2), lambda i: (i, 0))
    return pl.pallas_call(
        _combine_kernel,
        grid=(t // TM_OUT,),
        in_specs=[tok, pl.BlockSpec((TM_OUT, META_LANES), lambda i: (i, 0)), packed, packed,
                  pl.BlockSpec((1, D_MODEL), lambda i: (0, 0))],
        out_specs=tok,
        out_shape=jax.ShapeDtypeStruct((t, D_MODEL), F32),
        compiler_params=_cparams(),
        name="combine",
    )(x1, meta, y1, y2, g_final.reshape(1, D_MODEL))


def _slot_layout(metat, counts_row, n_tiles):
    counts = counts_row[0, EXPERT_LANE0:EXPERT_LANE0 + N_EXPERTS].astype(jnp.int32)
    padded = (counts + TM_EXP - 1) // TM_EXP * TM_EXP
    ends = jnp.cumsum(padded)
    offs = ends - padded
    offs_f = offs.astype(F32)
    pos = []
    for k in range(2):
        eid, p = metat[k], metat[4 + k]
        for e in range(N_EXPERTS):
            p = p + jnp.where(eid == e, offs_f[e], 0.0)
        pos.append(p.astype(jnp.int32))
    tile_start = jnp.arange(n_tiles, dtype=jnp.int32) * TM_EXP
    tile_expert = jnp.minimum(jnp.sum(tile_start[:, None] >= ends[None, :], axis=1), N_EXPERTS - 1)
    tile_expert = tile_expert.astype(jnp.int32)
    tile_rows = jnp.clip(counts[tile_expert] - (tile_start - offs[tile_expert]), 0, TM_EXP)
    tile_rows = jnp.where(tile_start < ends[-1], tile_rows, 0).astype(jnp.int32)
    return pos[0], pos[1], tile_expert, tile_rows


def kernel(x_prompt, x_sample, g_mix, w_in, b_gate, conv_w, conv_b, ln_g, ln_b, w_out, g_ffn,
           w_router_group, b_router_group, w_router_expert, b_router_expert, w_gate, w_up, w_down, g_final):
    assert g_mix.shape[0] == 1, "one layer"
    bp, lp, _ = x_prompt.shape
    bs, ls, _ = x_sample.shape
    seq_lens = [lp] * bp + [ls] * bs
    tp, ts = bp * lp, bs * ls
    xp = x_prompt.reshape(tp, D_MODEL)
    xs = x_sample.reshape(ts, D_MODEL)

    q, v, o, u, kt, gr = _inproj(xp, xs, g_mix[0], w_in[0], b_gate[0])
    hf, hb = _mlstm(q, kt, v, gr, seq_lens)

    outs = []
    for x, tile0, lens in ((xp, 0, [lp] * bp), (xs, tp // TM_MIX, [ls] * bs)):
        tg = x.shape[0]
        x1, hn, meta, metat, counts = _mixout(x, tile0, hf, hb, o, u, lens, conv_w[0], conv_b[0], ln_g[0], ln_b[0],
                                              w_out[0], g_ffn[0], w_router_group[0], b_router_group[0],
                                              w_router_expert[0], b_router_expert[0])
        n_tiles = (2 * tg + N_EXPERTS * (TM_EXP - 1)) // TM_EXP + 1
        pos1, pos2, tile_expert, tile_rows = _slot_layout(metat, counts, n_tiles)
        hs = _dispatch(hn, pos1, pos2, n_tiles * TM_EXP)
        ys = _experts(hs, tile_expert, tile_rows, w_gate[0], w_up[0], w_down[0])
        y1, y2 = _collect(ys, pos1, pos2)
        outs.append(_combine(x1, meta, y1, y2, g_final))
    return outs[0].reshape(bp, lp, D_MODEL), outs[1].reshape(bs, ls, D_MODEL)
```

```python
import functools

import jax
import jax.numpy as jnp
from jax import lax
from jax.experimental import pallas as pl
from jax.experimental.pallas import tpu as pltpu
from jax.experimental.pallas import tpu_sc as plsc

F32 = jnp.float32
BF16 = jnp.bfloat16

D_MODEL = 1024
N_HEADS = 4
HEAD_DIM = 128
D_MLSTM = N_HEADS * HEAD_DIM
D_CONV = D_MODEL - D_MLSTM
CONV_WIDTH = 31
CONV_HALO = 16
N_DIR = 2
N_GROUPS = 4
EXPERTS_PER_GROUP = 4
N_EXPERTS = N_GROUPS * EXPERTS_PER_GROUP
D_EXPERT = 512
EPS = 1e-6
K_SCALE = HEAD_DIM ** -0.5

GATE_ROWS = 16
QROWS = 24
ROUTER_LANES = 128
EXPERT_LANE0 = N_GROUPS
META_LANES = 8

TM_IN = 1024
CHUNK = 512
GP_CHUNKS = 32
TM_MIX = 512
CONV_ROWS = 128
TM_EXP = 512
TM_OUT = 1024
VMEM_LIMIT = 48 * 1024 * 1024

SC_CORES = 2
SC_SUBCORES = 16
SC_WORKERS = SC_CORES * SC_SUBCORES
SC_ROWS = 128


def _cparams(n_axes=1):
    return pltpu.CompilerParams(dimension_semantics=("arbitrary",) * n_axes,
                                vmem_limit_bytes=VMEM_LIMIT)


def _nt_dot(a, b):
    return lax.dot_general(a, b, (((1,), (1,)), ((), ())), preferred_element_type=F32)


def _rms(x, g):
    return x * lax.rsqrt(jnp.mean(x * x, axis=-1, keepdims=True) + EPS) * g


def _pack_rows(x):
    n = x.shape[1] // 2
    hi = lax.bitcast_convert_type(x[:, :n].astype(jnp.bfloat16).astype(F32), jnp.int32)
    lo = lax.bitcast_convert_type(x[:, n:].astype(jnp.bfloat16).astype(F32), jnp.int32)
    return hi | lax.shift_right_logical(lo, 16)


def _unpack_rows(p):
    hi = lax.bitcast_convert_type(p & jnp.int32(-65536), F32)
    lo = lax.bitcast_convert_type(lax.shift_left(p, 16), F32)
    return jnp.concatenate([hi, lo], axis=1)


def _two_batch_specs(block, n_first, n_second):
    first = pl.BlockSpec(block, lambda i, *_: (jnp.minimum(i, n_first - 1), 0))
    second = pl.BlockSpec(block, lambda i, *_: (jnp.maximum(i - n_first, 0), 0))
    return first, second


def _inproj_kernel(xp_ref, xs_ref, g_ref, wq_ref, wv_ref, wo_ref, wa_ref, wb_ref, wkt_ref, wgt_ref, bg_ref,
                   q_ref, v_ref, o_ref, u_ref, kt_ref, gr_ref, *, n_first):
    x = jnp.where(pl.program_id(0) < n_first, xp_ref[...], xs_ref[...])
    xn = _rms(x, g_ref[...]).astype(BF16)
    q_ref[...] = jnp.dot(xn, wq_ref[...], preferred_element_type=F32).astype(BF16)
    v_ref[...] = jnp.dot(xn, wv_ref[...], preferred_element_type=F32).astype(BF16)
    o_ref[...] = jnp.dot(xn, wo_ref[...], preferred_element_type=F32).astype(BF16)
    a = jnp.dot(xn, wa_ref[...], preferred_element_type=F32)
    b = jnp.dot(xn, wb_ref[...], preferred_element_type=F32)
    u_ref[...] = (a * jax.nn.sigmoid(b)).astype(BF16)
    kt_ref[...] = (_nt_dot(wkt_ref[...], xn) * K_SCALE).astype(BF16)
    gr = _nt_dot(wgt_ref[...], xn) + bg_ref[...]
    for c in range(gr_ref.shape[0]):
        gr_ref[c] = gr[:, c * CHUNK:(c + 1) * CHUNK]


def _inproj(xp, xs, g_mix, w_in, b_gate):
    tp, ts = xp.shape[0], xs.shape[0]
    t = tp + ts
    assert tp % TM_IN == 0 and ts % TM_IN == 0
    off_k, off_v, off_o, off_g = D_MLSTM, 2 * D_MLSTM, 3 * D_MLSTM, 4 * D_MLSTM
    off_a = off_g + 2 * N_DIR * N_HEADS
    off_b = off_a + D_CONV
    wq = w_in[:, 0:off_k].astype(BF16)
    wkt = w_in[:, off_k:off_v].T.astype(BF16)
    wv = w_in[:, off_v:off_o].astype(BF16)
    wo = w_in[:, off_o:off_g].astype(BF16)
    wa = w_in[:, off_a:off_b].astype(BF16)
    wb = w_in[:, off_b:off_b + D_CONV].astype(BF16)
    wg = w_in[:, off_g:off_a].T.reshape(N_DIR, 2, N_HEADS, D_MODEL)
    wgt = jnp.zeros((N_DIR, 2, GATE_ROWS // 2, D_MODEL), F32).at[:, :, :N_HEADS].set(wg)
    wgt = wgt.reshape(N_DIR * GATE_ROWS, D_MODEL).astype(BF16)
    bg = jnp.zeros((N_DIR, 2, GATE_ROWS // 2), F32).at[:, :, :N_HEADS].set(
        b_gate.reshape(N_DIR, 2, N_HEADS)).reshape(N_DIR * GATE_ROWS, 1)

    tok = lambda i: (i, 0)
    fixed = lambda i: (0, 0)
    wspec = pl.BlockSpec((D_MODEL, D_MLSTM), fixed)
    xp_spec, xs_spec = _two_batch_specs((TM_IN, D_MODEL), tp // TM_IN, ts // TM_IN)
    cpt = TM_IN // CHUNK
    return pl.pallas_call(
        functools.partial(_inproj_kernel, n_first=tp // TM_IN),
        grid=(t // TM_IN,),
        in_specs=[xp_spec, xs_spec, pl.BlockSpec((1, D_MODEL), fixed),
                  wspec, wspec, wspec, wspec, wspec,
                  pl.BlockSpec((D_MLSTM, D_MODEL), fixed),
                  pl.BlockSpec((N_DIR * GATE_ROWS, D_MODEL), fixed),
                  pl.BlockSpec((N_DIR * GATE_ROWS, 1), fixed)],
        out_specs=[pl.BlockSpec((TM_IN, D_MLSTM), tok)] * 4 + [
            pl.BlockSpec((D_MLSTM, TM_IN), lambda i: (0, i)),
            pl.BlockSpec((cpt, N_DIR * GATE_ROWS, CHUNK), lambda i: (i, 0, 0))],
        out_shape=[jax.ShapeDtypeStruct((t, D_MLSTM), BF16)] * 4 + [
            jax.ShapeDtypeStruct((D_MLSTM, t), BF16),
            jax.ShapeDtypeStruct((t // CHUNK, N_DIR * GATE_ROWS, CHUNK), F32)],
        compiler_params=_cparams(),
        name="inproj",
    )(xp, xs, g_mix.reshape(1, D_MODEL), wq, wv, wo, wa, wb, wkt, wgt, bg)


def _log_sigmoid(x):
    return jnp.minimum(x, 0.0) - jnp.log1p(jnp.exp(-jnp.abs(x)))


def _gateprep_kernel(reset_ref, g_ref, rowq_ref, colq_ref, m_ref, *, rev):
    n, _, c = g_ref.shape
    step = pl.program_id(0)
    blk = pl.num_programs(0) - 1 - step if rev else step

    @pl.when(step == 0)
    def _():
        m_ref[...] = jnp.zeros_like(m_ref)

    ig = g_ref[:, 0:8, :]
    lf = _log_sigmoid(g_ref[:, 8:16, :])
    lane = lax.broadcasted_iota(jnp.int32, (n, 8, c), 2)

    def scan(x, op, ident):
        k = 1
        while k < c:
            if rev:
                shifted, valid = pltpu.roll(x, c - k, axis=2), lane < c - k
            else:
                shifted, valid = pltpu.roll(x, k, axis=2), lane >= k
            x = op(x, jnp.where(valid, shifted, ident))
            k *= 2
        return x

    bc = scan(lf, jnp.add, 0.0)
    a = ig - bc
    cm = scan(a, jnp.maximum, -jnp.inf)
    b_tot = jnp.sum(lf, axis=2, keepdims=True)
    a_max = jnp.max(a, axis=2, keepdims=True)

    m = m_ref[...]
    m_in = [None] * n
    for j in (range(n - 1, -1, -1) if rev else range(n)):
        m = jnp.where(reset_ref[blk * n + j] == 1, 0.0, m)
        m_in[j] = m
        m = b_tot[j] + jnp.maximum(m, a_max[j])
    m_ref[...] = m
    m_old = jnp.stack(m_in)

    mx = jnp.maximum(m_old, cm)
    mx_last = jnp.maximum(m_old, a_max)
    rowq_ref[:, 0:8, :] = a
    rowq_ref[:, 8:16, :] = jnp.exp(a - mx_last)
    rowq_ref[:, 16:24, :] = jnp.exp(m_old - mx_last)
    e1 = jnp.exp(m_old - mx)
    fl = jnp.exp(-(mx + bc))
    for j in range(n):
        colq_ref[j] = jnp.concatenate([mx[j], e1[j], fl[j]], axis=0).T


def _gateprep(gr, reset, rev):
    n_chunks = gr.shape[0]
    gp = min(GP_CHUNKS, n_chunks)
    assert n_chunks % gp == 0
    nb = n_chunks // gp
    d = 1 if rev else 0
    bidx = (lambda s: nb - 1 - s) if rev else (lambda s: s)
    grid_spec = pltpu.PrefetchScalarGridSpec(
        num_scalar_prefetch=1,
        grid=(nb,),
        in_specs=[pl.BlockSpec((gp, GATE_ROWS, CHUNK), lambda s, r: (bidx(s), d, 0))],
        out_specs=[pl.BlockSpec((gp, QROWS, CHUNK), lambda s, r: (bidx(s), 0, 0)),
                   pl.BlockSpec((gp, CHUNK, QROWS), lambda s, r: (bidx(s), 0, 0))],
        scratch_shapes=[pltpu.VMEM((8, CHUNK), F32)],
    )
    return pl.pallas_call(
        functools.partial(_gateprep_kernel, rev=rev),
        grid_spec=grid_spec,
        out_shape=[jax.ShapeDtypeStruct((n_chunks, QROWS, CHUNK), F32),
                   jax.ShapeDtypeStruct((n_chunks, CHUNK, QROWS), F32)],
        compiler_params=_cparams(),
        name="gateprep_bwd" if rev else "gateprep_fwd",
    )(reset, gr)


def _mlstm_kernel(rf_ref, rb_ref,
                  qf_ref, ktf_ref, vf_ref, rowf_ref, colf_ref,
                  qb_ref, ktb_ref, vb_ref, rowb_ref, colb_ref,
                  hf_ref, hb_ref, cst_ref):
    c = qf_ref.shape[0]
    step = pl.program_id(0)
    last = pl.num_programs(0) - 1
    row_i = lax.broadcasted_iota(jnp.int32, (c, c), 0)
    col_i = lax.broadcasted_iota(jnp.int32, (c, c), 1)
    ones = jnp.ones((c, HEAD_DIM), BF16)

    dirs = ((rf_ref[step], col_i <= row_i, qf_ref, ktf_ref, vf_ref, rowf_ref, colf_ref, hf_ref),
            (rb_ref[last - step], col_i >= row_i, qb_ref, ktb_ref, vb_ref, rowb_ref, colb_ref, hb_ref))
    for d, (reset, mask, q_ref, kt_ref, v_ref, row_ref, col_ref, h_ref) in enumerate(dirs):
        @pl.when(reset == 1)
        def _():
            cst_ref[d] = jnp.zeros(cst_ref.shape[1:], F32)

        rowq = row_ref[...]
        colq = col_ref[...]
        for h in range(N_HEADS):
            hs = slice(h * HEAD_DIM, (h + 1) * HEAD_DIM)
            qh = q_ref[:, hs]
            kth = kt_ref[hs, :]
            vext = jnp.concatenate([v_ref[:, hs], ones], axis=1)
            s = jnp.dot(qh, kth, preferred_element_type=F32)
            e = jnp.exp(jnp.where(mask, rowq[h:h + 1, :] - colq[:, h:h + 1], -jnp.inf))
            r1 = jnp.dot((s * e).astype(BF16), vext, preferred_element_type=F32)
            cst = cst_ref[d, h]
            r2 = jnp.dot(qh, cst.astype(BF16), preferred_element_type=F32)
            e1 = colq[:, 8 + h:9 + h]
            num = r1[:, :HEAD_DIM] + e1 * r2[:, :HEAD_DIM]
            den = r1[:, HEAD_DIM:] + e1 * r2[:, HEAD_DIM:]
            h_ref[:, hs] = (num / jnp.maximum(jnp.abs(den), colq[:, 16 + h:17 + h])).astype(h_ref.dtype)
            kw = (kth.astype(F32) * rowq[8 + h:9 + h, :]).astype(BF16)
            cst_ref[d, h] = rowq[16 + h:17 + h, 0:1] * cst + jnp.dot(kw, vext, preferred_element_type=F32)


def _mlstm(q, kt, v, gr, seq_lens):
    t = q.shape[0]
    n = t // CHUNK
    starts, ends, pos = [], [], 0
    for ln in seq_lens:
        assert ln % CHUNK == 0
        starts.append(pos // CHUNK)
        ends.append((pos + ln) // CHUNK - 1)
        pos += ln
    reset_f = jnp.zeros((n,), jnp.int32).at[jnp.array(starts)].set(1)
    reset_b = jnp.zeros((n,), jnp.int32).at[jnp.array(ends)].set(1)
    rowf, colf = _gateprep(gr, reset_f, rev=False)
    rowb, colb = _gateprep(gr, reset_b, rev=True)

    def specs(cidx):
        return [pl.BlockSpec((CHUNK, D_MLSTM), lambda s, rf, rb: (cidx(s), 0)),
                pl.BlockSpec((D_MLSTM, CHUNK), lambda s, rf, rb: (0, cidx(s))),
                pl.BlockSpec((CHUNK, D_MLSTM), lambda s, rf, rb: (cidx(s), 0)),
                pl.BlockSpec((None, QROWS, CHUNK), lambda s, rf, rb: (cidx(s), 0, 0)),
                pl.BlockSpec((None, CHUNK, QROWS), lambda s, rf, rb: (cidx(s), 0, 0))]

    fwd = lambda s: s
    bwd = lambda s: n - 1 - s
    grid_spec = pltpu.PrefetchScalarGridSpec(
        num_scalar_prefetch=2,
        grid=(n,),
        in_specs=specs(fwd) + specs(bwd),
        out_specs=[pl.BlockSpec((CHUNK, D_MLSTM), lambda s, rf, rb: (fwd(s), 0)),
                   pl.BlockSpec((CHUNK, D_MLSTM), lambda s, rf, rb: (bwd(s), 0))],
        scratch_shapes=[pltpu.VMEM((N_DIR, N_HEADS, HEAD_DIM, 2 * HEAD_DIM), F32)],
    )
    return pl.pallas_call(
        _mlstm_kernel,
        grid_spec=grid_spec,
        out_shape=[jax.ShapeDtypeStruct((t, D_MLSTM), BF16)] * 2,
        compiler_params=_cparams(),
        name="mlstm",
    )(reset_f, reset_b, q, kt, v, rowf, colf, q, kt, v, rowb, colb)


def _mixout_kernel(first_ref, last_ref,
                   x_ref, hf_ref, hb_ref, o_ref, u_ref, up_ref, un_ref, cw_ref, cb_ref, lng_ref, lnb_ref,
                   wout_ref, gffn_ref, wr_ref, br_ref, tri_ref,
                   x1_ref, hn_ref, meta_ref, metat_ref, cnt_ref,
                   win_ref, y_ref, run_ref):
    i = pl.program_id(0)
    tm = hf_ref.shape[0]

    @pl.when(i == 0)
    def _():
        run_ref[...] = jnp.zeros_like(run_ref)

    h_sum = hf_ref[...].astype(F32) + hb_ref[...].astype(F32)
    ym = (jax.nn.sigmoid(o_ref[...].astype(F32)) * h_sum).astype(BF16)
    x1m = x_ref[...] + jnp.dot(ym, wout_ref[:D_MLSTM, :], preferred_element_type=F32)

    win_ref[0:CONV_HALO, :] = jnp.where(first_ref[i] == 1, 0.0, up_ref[...].astype(F32))
    win_ref[CONV_HALO:CONV_HALO + tm, :] = u_ref[...].astype(F32)
    win_ref[CONV_HALO + tm:, :] = jnp.where(last_ref[i] == 1, 0.0, un_ref[...].astype(F32))

    off0 = CONV_HALO - CONV_WIDTH // 2
    for r0 in range(0, tm, CONV_ROWS):
        tiles = []
        for lt in range(D_CONV // 128):
            ls = slice(lt * 128, (lt + 1) * 128)
            acc = jnp.broadcast_to(cb_ref[:, ls], (CONV_ROWS, 128))
            for s in range(8):
                part = None
                for j in range(CONV_WIDTH):
                    if (off0 + j) % 8 != s:
                        continue
                    base = (off0 + j) // 8 * 8
                    term = win_ref[r0 + base:r0 + base + CONV_ROWS + 8, ls] * cw_ref[j:j + 1, ls]
                    part = term if part is None else part + term
                acc = acc + part[s:s + CONV_ROWS, :]
            tiles.append(acc)
        cv = jnp.concatenate(tiles, axis=1)
        xc = cv - jnp.mean(cv, axis=-1, keepdims=True)
        yc = xc * lax.rsqrt(jnp.mean(xc * xc, axis=-1, keepdims=True) + EPS) * lng_ref[...] + lnb_ref[...]
        y_ref[r0:r0 + CONV_ROWS, :] = (yc * jax.nn.sigmoid(yc)).astype(BF16)

    x1 = x1m + jnp.dot(y_ref[...], wout_ref[D_MLSTM:, :], preferred_element_type=F32)
    x1_ref[...] = x1
    hn = _rms(x1, gffn_ref[...])
    hn_ref[...] = _pack_rows(hn)

    logits = jnp.dot(hn.astype(BF16), wr_ref[...], preferred_element_type=F32) + br_ref[...]
    lane = lax.broadcasted_iota(jnp.int32, (tm, ROUTER_LANES), 1).astype(F32)
    neg = -jnp.inf
    no_lane = float(ROUTER_LANES)
    gl = jnp.where(lane < N_GROUPS, logits, neg)
    gmax = jnp.max(gl, axis=1, keepdims=True)
    p_top = 1.0 / jnp.sum(jnp.exp(gl - gmax), axis=1, keepdims=True)
    g_idx = jnp.min(jnp.where(gl == gmax, lane, no_lane), axis=1, keepdims=True)
    lo = EXPERT_LANE0 + EXPERTS_PER_GROUP * g_idx
    in_grp = (lane >= lo) & (lane < lo + EXPERTS_PER_GROUP)
    el = jnp.where(in_grp, logits, neg)
    ee = jnp.exp(el - jnp.max(el, axis=1, keepdims=True))
    pe = jnp.where(in_grp, ee / jnp.sum(ee, axis=1, keepdims=True), -1.0)
    v1 = jnp.max(pe, axis=1, keepdims=True)
    i1 = jnp.min(jnp.where(pe == v1, lane, no_lane), axis=1, keepdims=True)
    pe2 = jnp.where(lane == i1, -1.0, pe)
    v2 = jnp.max(pe2, axis=1, keepdims=True)
    i2 = jnp.min(jnp.where(pe2 == v2, lane, no_lane), axis=1, keepdims=True)
    wsum = v1 + v2
    gate1 = p_top * (v1 / wsum)
    gate2 = p_top * (v2 / wsum)

    oh1 = (lane == i1).astype(F32)
    oh2 = (lane == i2).astype(F32)
    c1 = jnp.dot(tri_ref[...], oh1.astype(BF16), preferred_element_type=F32)
    c2 = jnp.dot(tri_ref[...], oh2.astype(BF16), preferred_element_type=F32)
    run = run_ref[...]
    tot1 = jnp.sum(oh1, axis=0, keepdims=True)
    tot2 = jnp.sum(oh2, axis=0, keepdims=True)
    rank1 = jnp.sum(oh1 * (run + c1), axis=1, keepdims=True)
    rank2 = jnp.sum(oh2 * (run + tot1 + c2), axis=1, keepdims=True)
    run = run + tot1 + tot2
    run_ref[...] = run
    cnt_ref[...] = run

    ml = lax.broadcasted_iota(jnp.int32, (tm, META_LANES), 1)
    meta = jnp.where(ml == 0, i1 - EXPERT_LANE0, 0.0)
    meta = jnp.where(ml == 1, i2 - EXPERT_LANE0, meta)
    meta = jnp.where(ml == 2, gate1, meta)
    meta = jnp.where(ml == 3, gate2, meta)
    meta = jnp.where(ml == 4, rank1, meta)
    meta = jnp.where(ml == 5, rank2, meta)
    meta_ref[...] = meta
    metat_ref[...] = meta.T


def _mixout(x, tile0, hf, hb, o, u, seq_lens, conv_w, conv_b, ln_g, ln_b, w_out, g_ffn, w_rg, b_rg, w_re, b_re):
    t = x.shape[0]
    n_tiles = t // TM_MIX
    hpt = TM_MIX // CONV_HALO
    n_halo = u.shape[0] // CONV_HALO
    firsts, lasts, pos = [], [], 0
    for ln in seq_lens:
        assert ln % TM_MIX == 0
        firsts.append(pos // TM_MIX)
        lasts.append((pos + ln) // TM_MIX - 1)
        pos += ln
    assert pos == t
    first = jnp.zeros((n_tiles,), jnp.int32).at[jnp.array(firsts)].set(1)
    last = jnp.zeros((n_tiles,), jnp.int32).at[jnp.array(lasts)].set(1)

    cw = jnp.zeros((32, D_CONV), F32).at[:CONV_WIDTH].set(conv_w.reshape(CONV_WIDTH, D_CONV))
    wr = jnp.zeros((D_MODEL, ROUTER_LANES), F32)
    wr = wr.at[:, :N_GROUPS].set(w_rg).at[:, EXPERT_LANE0:EXPERT_LANE0 + N_EXPERTS].set(w_re).astype(BF16)
    br = jnp.zeros((1, ROUTER_LANES), F32)
    br = br.at[0, :N_GROUPS].set(b_rg).at[0, EXPERT_LANE0:EXPERT_LANE0 + N_EXPERTS].set(b_re)
    tri = (lax.broadcasted_iota(jnp.int32, (TM_MIX, TM_MIX), 0)
           > lax.broadcasted_iota(jnp.int32, (TM_MIX, TM_MIX), 1)).astype(BF16)

    tok = lambda i, f, l: (i, 0)
    flat = lambda i, f, l: (tile0 + i, 0)
    fixed = lambda i, f, l: (0, 0)
    row = lambda n: pl.BlockSpec((1, n), fixed)
    half = pl.BlockSpec((TM_MIX, D_MLSTM), flat)
    grid_spec = pltpu.PrefetchScalarGridSpec(
        num_scalar_prefetch=2,
        grid=(n_tiles,),
        in_specs=[pl.BlockSpec((TM_MIX, D_MODEL), tok), half, half, half,
                  pl.BlockSpec((TM_MIX, D_CONV), flat),
                  pl.BlockSpec((CONV_HALO, D_CONV), lambda i, f, l: (jnp.maximum((tile0 + i) * hpt - 1, 0), 0)),
                  pl.BlockSpec((CONV_HALO, D_CONV),
                               lambda i, f, l: (jnp.minimum((tile0 + i + 1) * hpt, n_halo - 1), 0)),
                  pl.BlockSpec((32, D_CONV), fixed), row(D_CONV), row(D_CONV), row(D_CONV),
                  pl.BlockSpec((D_MODEL, D_MODEL), fixed), row(D_MODEL),
                  pl.BlockSpec((D_MODEL, ROUTER_LANES), fixed), row(ROUTER_LANES),
                  pl.BlockSpec((TM_MIX, TM_MIX), fixed)],
        out_specs=[pl.BlockSpec((TM_MIX, D_MODEL), tok), pl.BlockSpec((TM_MIX, D_MODEL // 2), tok),
                   pl.BlockSpec((TM_MIX, META_LANES), tok),
                   pl.BlockSpec((META_LANES, TM_MIX), lambda i, f, l: (0, i)), row(ROUTER_LANES)],
        scratch_shapes=[pltpu.VMEM((TM_MIX + 2 * CONV_HALO, D_CONV), F32),
                        pltpu.VMEM((TM_MIX, D_CONV), BF16),
                        pltpu.VMEM((1, ROUTER_LANES), F32)],
    )
    return pl.pallas_call(
        _mixout_kernel,
        grid_spec=grid_spec,
        out_shape=[jax.ShapeDtypeStruct((t, D_MODEL), F32), jax.ShapeDtypeStruct((t, D_MODEL // 2), jnp.int32),
                   jax.ShapeDtypeStruct((t, META_LANES), F32), jax.ShapeDtypeStruct((META_LANES, t), F32),
                   jax.ShapeDtypeStruct((1, ROUTER_LANES), F32)],
        compiler_params=_cparams(),
        name="mixout",
    )(first, last, x, hf, hb, o, u, u, u, cw, conv_b.reshape(1, D_CONV), ln_g.reshape(1, D_CONV),
      ln_b.reshape(1, D_CONV), w_out.astype(BF16), g_ffn.reshape(1, D_MODEL), wr, br, tri)


def _sc_mesh():
    return plsc.VectorSubcoreMesh(core_axis_name="c", subcore_axis_name="s")


def _sc_worker_base(per_worker):
    wid = lax.axis_index("s") * SC_CORES + lax.axis_index("c")
    return wid * per_worker


def _dispatch(hn, pos1, pos2, n_slots):
    t, d = hn.shape
    per_worker = t // SC_WORKERS
    assert per_worker * SC_WORKERS == t and per_worker % SC_ROWS == 0

    def body(h_hbm, p1_hbm, p2_hbm, out_hbm, i1_v, i2_v, rows_v, sem1, sem2):
        base0 = _sc_worker_base(per_worker)

        @pl.loop(0, per_worker // SC_ROWS)
        def _(c):
            base = pl.multiple_of(base0 + c * SC_ROWS, SC_ROWS)
            pltpu.sync_copy(p1_hbm.at[pl.ds(base, SC_ROWS)], i1_v)
            pltpu.sync_copy(p2_hbm.at[pl.ds(base, SC_ROWS)], i2_v)
            pltpu.sync_copy(h_hbm.at[pl.ds(base, SC_ROWS)], rows_v)
            c1 = pltpu.async_copy(rows_v, out_hbm.at[i1_v], sem1)
            c2 = pltpu.async_copy(rows_v, out_hbm.at[i2_v], sem2)
            c1.wait()
            c2.wait()

    return pl.kernel(
        body,
        out_type=jax.ShapeDtypeStruct((n_slots, d), hn.dtype),
        mesh=_sc_mesh(),
        scratch_types=[pltpu.VMEM((SC_ROWS,), jnp.int32), pltpu.VMEM((SC_ROWS,), jnp.int32),
                       pltpu.VMEM((SC_ROWS, d), hn.dtype), pltpu.SemaphoreType.DMA, pltpu.SemaphoreType.DMA],
        name="dispatch",
    )(hn, pos1, pos2)


def _collect(ys, pos1, pos2):
    t = pos1.shape[0]
    d = ys.shape[1]
    per_worker = t // SC_WORKERS
    assert per_worker * SC_WORKERS == t and per_worker % SC_ROWS == 0

    def body(ys_hbm, p1_hbm, p2_hbm, y1_hbm, y2_hbm, i_v, rows_v, sem):
        base0 = _sc_worker_base(per_worker)

        @pl.loop(0, per_worker // SC_ROWS)
        def _(c):
            base = pl.multiple_of(base0 + c * SC_ROWS, SC_ROWS)
            for p_hbm, y_hbm in ((p1_hbm, y1_hbm), (p2_hbm, y2_hbm)):
                pltpu.sync_copy(p_hbm.at[pl.ds(base, SC_ROWS)], i_v)
                pltpu.async_copy(ys_hbm.at[i_v], rows_v, sem).wait()
                pltpu.sync_copy(rows_v, y_hbm.at[pl.ds(base, SC_ROWS)])

    out = jax.ShapeDtypeStruct((t, d), ys.dtype)
    return pl.kernel(
        body,
        out_type=(out, out),
        mesh=_sc_mesh(),
        scratch_types=[pltpu.VMEM((SC_ROWS,), jnp.int32), pltpu.VMEM((SC_ROWS, d), ys.dtype),
                       pltpu.SemaphoreType.DMA],
        name="collect",
    )(ys, pos1, pos2)


def _experts_kernel(te_ref, rows_ref, hs_ref, wg_ref, wu_ref, wd_ref, ys_ref, wgb_ref, wub_ref, wdb_ref):
    i = pl.program_id(0)

    @pl.when((i == 0) | (te_ref[i] != te_ref[jnp.maximum(i - 1, 0)]))
    def _():
        wgb_ref[...] = wg_ref[...].astype(BF16)
        wub_ref[...] = wu_ref[...].astype(BF16)
        wdb_ref[...] = wd_ref[...].astype(BF16)

    @pl.when(rows_ref[i] > 0)
    def _():
        x = _unpack_rows(hs_ref[...]).astype(BF16)
        hg = jnp.dot(x, wgb_ref[...], preferred_element_type=F32)
        hu = jnp.dot(x, wub_ref[...], preferred_element_type=F32)
        hid = (hg * jax.nn.sigmoid(hg) * hu).astype(BF16)
        ys_ref[...] = _pack_rows(jnp.dot(hid, wdb_ref[...], preferred_element_type=F32))

    @pl.when(rows_ref[i] == 0)
    def _():
        ys_ref[...] = jnp.zeros_like(ys_ref)


def _experts(hs, tile_expert, tile_rows, w_gate, w_up, w_down):
    n_tiles = tile_expert.shape[0]
    wsel = lambda i, te, rows: (te[i], 0, 0)
    slot = lambda i, te, rows: (i, 0)
    grid_spec = pltpu.PrefetchScalarGridSpec(
        num_scalar_prefetch=2,
        grid=(n_tiles,),
        in_specs=[pl.BlockSpec((TM_EXP, D_MODEL // 2), slot),
                  pl.BlockSpec((None, D_MODEL, D_EXPERT), wsel),
                  pl.BlockSpec((None, D_MODEL, D_EXPERT), wsel),
                  pl.BlockSpec((None, D_EXPERT, D_MODEL), wsel)],
        out_specs=pl.BlockSpec((TM_EXP, D_MODEL // 2), slot),
        scratch_shapes=[pltpu.VMEM((D_MODEL, D_EXPERT), BF16), pltpu.VMEM((D_MODEL, D_EXPERT), BF16),
                        pltpu.VMEM((D_EXPERT, D_MODEL), BF16)],
    )
    return pl.pallas_call(
        _experts_kernel,
        grid_spec=grid_spec,
        out_shape=jax.ShapeDtypeStruct((n_tiles * TM_EXP, D_MODEL // 2), jnp.int32),
        compiler_params=_cparams(),
        name="experts",
    )(tile_expert, tile_rows, hs, w_gate, w_up, w_down)


def _combine_kernel(x1_ref, meta_ref, y1_ref, y2_ref, gfin_ref, out_ref):
    meta = meta_ref[...]
    x2 = x1_ref[...] + meta[:, 2:3] * _unpack_rows(y1_ref[...]) + meta[:, 3:4] * _unpack_rows(y2_ref[...])
    out_ref[...] = _rms(x2, gfin_ref[...])


def _combine(x1, meta, y1, y2, g_final):
    t = x1.shape[0]
    assert t % TM_OUT == 0
    tok = pl.BlockSpec((TM_OUT, D_MODEL), lambda i: (i, 0))
    packed = pl.BlockSpec((TM_OUT, D_MODEL // 2), lambda i: (i, 0))
    return pl.pallas_call(
        _combine_kernel,
        grid=(t // TM_OUT,),
        in_specs=[tok, pl.BlockSpec((TM_OUT, META_LANES), lambda i: (i, 0)), packed, packed,
                  pl.BlockSpec((1, D_MODEL), lambda i: (0, 0))],
        out_specs=tok,
        out_shape=jax.ShapeDtypeStruct((t, D_MODEL), F32),
        compiler_params=_cparams(),
        name="combine",
    )(x1, meta, y1, y2, g_final.reshape(1, D_MODEL))


def _slot_layout(metat, counts_row, n_tiles):
    counts = counts_row[0, EXPERT_LANE0:EXPERT_LANE0 + N_EXPERTS].astype(jnp.int32)
    padded = (counts + TM_EXP - 1) // TM_EXP * TM_EXP
    ends = jnp.cumsum(padded)
    offs = ends - padded
    offs_f = offs.astype(F32)
    pos = []
    for k in range(2):
        eid, p = metat[k], metat[4 + k]
        for e in range(N_EXPERTS):
            p = p + jnp.where(eid == e, offs_f[e], 0.0)
        pos.append(p.astype(jnp.int32))
    tile_start = jnp.arange(n_tiles, dtype=jnp.int32) * TM_EXP
    tile_expert = jnp.minimum(jnp.sum(tile_start[:, None] >= ends[None, :], axis=1), N_EXPERTS - 1)
    tile_expert = tile_expert.astype(jnp.int32)
    tile_rows = jnp.clip(counts[tile_expert] - (tile_start - offs[tile_expert]), 0, TM_EXP)
    tile_rows = jnp.where(tile_start < ends[-1], tile_rows, 0).astype(jnp.int32)
    return pos[0], pos[1], tile_expert, tile_rows


def kernel(x_prompt, x_sample, g_mix, w_in, b_gate, conv_w, conv_b, ln_g, ln_b, w_out, g_ffn,
           w_router_group, b_router_group, w_router_expert, b_router_expert, w_gate, w_up, w_down, g_final):
    assert g_mix.shape[0] == 1, "one layer"
    bp, lp, _ = x_prompt.shape
    bs, ls, _ = x_sample.shape
    seq_lens = [lp] * bp + [ls] * bs
    tp, ts = bp * lp, bs * ls
    xp = x_prompt.reshape(tp, D_MODEL)
    xs = x_sample.reshape(ts, D_MODEL)

    q, v, o, u, kt, gr = _inproj(xp, xs, g_mix[0], w_in[0], b_gate[0])
    hf, hb = _mlstm(q, kt, v, gr, seq_lens)

    outs = []
    for x, tile0, lens in ((xp, 0, [lp] * bp), (xs, tp // TM_MIX, [ls] * bs)):
        tg = x.shape[0]
        x1, hn, meta, metat, counts = _mixout(x, tile0, hf, hb, o, u, lens, conv_w[0], conv_b[0], ln_g[0], ln_b[0],
                                              w_out[0], g_ffn[0], w_router_group[0], b_router_group[0],
                                              w_router_expert[0], b_router_expert[0])
        n_tiles = (2 * tg + N_EXPERTS * (TM_EXP - 1)) // TM_EXP + 1
        pos1, pos2, tile_expert, tile_rows = _slot_layout(metat, counts, n_tiles)
        hs = _dispatch(hn, pos1, pos2, n_tiles * TM_EXP)
        ys = _experts(hs, tile_expert, tile_rows, w_gate[0], w_up[0], w_down[0])
        y1, y2 = _collect(ys, pos1, pos2)
        outs.append(_combine(x1, meta, y1, y2, g_final))
    return outs[0].reshape(bp, lp, D_MODEL), outs[1].reshape(bs, ls, D_MODEL)
```

```python
import functools

import jax
import jax.numpy as jnp
from jax import lax
from jax.experimental import pallas as pl
from jax.experimental.pallas import tpu as pltpu
from jax.experimental.pallas import tpu_sc as plsc

F32 = jnp.float32
BF16 = jnp.bfloat16

D_MODEL = 1024
N_HEADS = 4
HEAD_DIM = 128
D_MLSTM = N_HEADS * HEAD_DIM
D_CONV = D_MODEL - D_MLSTM
CONV_WIDTH = 31
CONV_HALO = 16
N_DIR = 2
N_GROUPS = 4
EXPERTS_PER_GROUP = 4
N_EXPERTS = N_GROUPS * EXPERTS_PER_GROUP
D_EXPERT = 512
EPS = 1e-6
K_SCALE = HEAD_DIM ** -0.5

GATE_ROWS = 16
QROWS = 24
ROUTER_ROWS = 32
EXPERT_ROW0 = N_GROUPS
META_LANES = 8

TM_IN = 1024
CHUNK = 512
GP_CHUNKS = 32
TM_MIX = 512
CONV_ROWS = 128
TM_EXP = 512
TM_OUT = 1024
VMEM_LIMIT = 48 * 1024 * 1024

SC_CORES = 2
SC_SUBCORES = 16
SC_WORKERS = SC_CORES * SC_SUBCORES
SC_ROWS = 128


def _cparams(n_axes=1):
    return pltpu.CompilerParams(dimension_semantics=("arbitrary",) * n_axes,
                                vmem_limit_bytes=VMEM_LIMIT)


def _nt_dot(a, b):
    return lax.dot_general(a, b, (((1,), (1,)), ((), ())), preferred_element_type=F32)


def _rms(x, g):
    return x * lax.rsqrt(jnp.mean(x * x, axis=-1, keepdims=True) + EPS) * g


def _pack_rows(x):
    n = x.shape[1] // 2
    hi = lax.bitcast_convert_type(x[:, :n].astype(jnp.bfloat16).astype(F32), jnp.int32)
    lo = lax.bitcast_convert_type(x[:, n:].astype(jnp.bfloat16).astype(F32), jnp.int32)
    return hi | lax.shift_right_logical(lo, 16)


def _unpack_rows(p):
    hi = lax.bitcast_convert_type(p & jnp.int32(-65536), F32)
    lo = lax.bitcast_convert_type(lax.shift_left(p, 16), F32)
    return jnp.concatenate([hi, lo], axis=1)


def _two_batch_specs(block, n_first, n_second):
    first = pl.BlockSpec(block, lambda i, *_: (jnp.minimum(i, n_first - 1), 0))
    second = pl.BlockSpec(block, lambda i, *_: (jnp.maximum(i - n_first, 0), 0))
    return first, second


def _inproj_kernel(xp_ref, xs_ref, g_ref, wq_ref, wv_ref, wo_ref, wa_ref, wb_ref, wkg_ref, bg_ref,
                   q_ref, v_ref, o_ref, u_ref, kt_ref, gr_ref, *, n_first):
    x = jnp.where(pl.program_id(0) < n_first, xp_ref[...], xs_ref[...])
    xn = _rms(x, g_ref[...]).astype(BF16)
    q_ref[...] = jnp.dot(xn, wq_ref[...], preferred_element_type=F32).astype(BF16)
    v_ref[...] = jnp.dot(xn, wv_ref[...], preferred_element_type=F32).astype(BF16)
    o_ref[...] = jnp.dot(xn, wo_ref[...], preferred_element_type=F32).astype(BF16)
    a = jnp.dot(xn, wa_ref[...], preferred_element_type=F32)
    b = jnp.dot(xn, wb_ref[...], preferred_element_type=F32)
    u_ref[...] = (a * jax.nn.sigmoid(b)).astype(BF16)
    kg = _nt_dot(wkg_ref[...], xn)
    kt_ref[...] = (kg[:D_MLSTM] * K_SCALE).astype(BF16)
    gr = kg[D_MLSTM:] + bg_ref[...]
    for c in range(gr_ref.shape[0]):
        gr_ref[c] = gr[:, c * CHUNK:(c + 1) * CHUNK]


def _inproj(xp, xs, g_mix, w_in, b_gate):
    tp, ts = xp.shape[0], xs.shape[0]
    t = tp + ts
    assert tp % TM_IN == 0 and ts % TM_IN == 0
    off_k, off_v, off_o, off_g = D_MLSTM, 2 * D_MLSTM, 3 * D_MLSTM, 4 * D_MLSTM
    off_a = off_g + 2 * N_DIR * N_HEADS
    off_b = off_a + D_CONV
    wq = w_in[:, 0:off_k].astype(BF16)
    wkt = w_in[:, off_k:off_v].T.astype(BF16)
    wv = w_in[:, off_v:off_o].astype(BF16)
    wo = w_in[:, off_o:off_g].astype(BF16)
    wa = w_in[:, off_a:off_b].astype(BF16)
    wb = w_in[:, off_b:off_b + D_CONV].astype(BF16)
    wg = w_in[:, off_g:off_a].T.reshape(N_DIR, 2, N_HEADS, D_MODEL)
    wgt = jnp.zeros((N_DIR, 2, GATE_ROWS // 2, D_MODEL), F32).at[:, :, :N_HEADS].set(wg)
    wkg = jnp.concatenate([wkt, wgt.reshape(N_DIR * GATE_ROWS, D_MODEL).astype(BF16)], axis=0)
    bg = jnp.zeros((N_DIR, 2, GATE_ROWS // 2), F32).at[:, :, :N_HEADS].set(
        b_gate.reshape(N_DIR, 2, N_HEADS)).reshape(N_DIR * GATE_ROWS, 1)

    tok = lambda i: (i, 0)
    fixed = lambda i: (0, 0)
    wspec = pl.BlockSpec((D_MODEL, D_MLSTM), fixed)
    xp_spec, xs_spec = _two_batch_specs((TM_IN, D_MODEL), tp // TM_IN, ts // TM_IN)
    cpt = TM_IN // CHUNK
    return pl.pallas_call(
        functools.partial(_inproj_kernel, n_first=tp // TM_IN),
        grid=(t // TM_IN,),
        in_specs=[xp_spec, xs_spec, pl.BlockSpec((1, D_MODEL), fixed),
                  wspec, wspec, wspec, wspec, wspec,
                  pl.BlockSpec((D_MLSTM + N_DIR * GATE_ROWS, D_MODEL), fixed),
                  pl.BlockSpec((N_DIR * GATE_ROWS, 1), fixed)],
        out_specs=[pl.BlockSpec((TM_IN, D_MLSTM), tok)] * 4 + [
            pl.BlockSpec((D_MLSTM, TM_IN), lambda i: (0, i)),
            pl.BlockSpec((cpt, N_DIR * GATE_ROWS, CHUNK), lambda i: (i, 0, 0))],
        out_shape=[jax.ShapeDtypeStruct((t, D_MLSTM), BF16)] * 4 + [
            jax.ShapeDtypeStruct((D_MLSTM, t), BF16),
            jax.ShapeDtypeStruct((t // CHUNK, N_DIR * GATE_ROWS, CHUNK), F32)],
        compiler_params=_cparams(),
        name="inproj",
    )(xp, xs, g_mix.reshape(1, D_MODEL), wq, wv, wo, wa, wb, wkg, bg)


def _log_sigmoid(x):
    return jnp.minimum(x, 0.0) - jnp.log1p(jnp.exp(-jnp.abs(x)))


def _gateprep_kernel(reset_ref, g_ref, rowq_ref, colq_ref, m_ref, *, rev):
    n, _, c = g_ref.shape
    step = pl.program_id(0)
    blk = pl.num_programs(0) - 1 - step if rev else step

    @pl.when(step == 0)
    def _():
        m_ref[...] = jnp.zeros_like(m_ref)

    ig = g_ref[:, 0:8, :]
    lf = _log_sigmoid(g_ref[:, 8:16, :])
    lane = lax.broadcasted_iota(jnp.int32, (n, 8, c), 2)

    def scan(x, op, ident):
        k = 1
        while k < c:
            if rev:
                shifted, valid = pltpu.roll(x, c - k, axis=2), lane < c - k
            else:
                shifted, valid = pltpu.roll(x, k, axis=2), lane >= k
            x = op(x, jnp.where(valid, shifted, ident))
            k *= 2
        return x

    bc = scan(lf, jnp.add, 0.0)
    a = ig - bc
    cm = scan(a, jnp.maximum, -jnp.inf)
    b_tot = jnp.sum(lf, axis=2, keepdims=True)
    a_max = jnp.max(a, axis=2, keepdims=True)

    m = m_ref[...]
    m_in = [None] * n
    for j in (range(n - 1, -1, -1) if rev else range(n)):
        m = jnp.where(reset_ref[blk * n + j] == 1, 0.0, m)
        m_in[j] = m
        m = b_tot[j] + jnp.maximum(m, a_max[j])
    m_ref[...] = m
    m_old = jnp.stack(m_in)

    mx = jnp.maximum(m_old, cm)
    mx_last = jnp.maximum(m_old, a_max)
    rowq_ref[:, 0:8, :] = a
    rowq_ref[:, 8:16, :] = jnp.exp(a - mx_last)
    rowq_ref[:, 16:24, :] = jnp.exp(m_old - mx_last)
    e1 = jnp.exp(m_old - mx)
    fl = jnp.exp(-(mx + bc))
    for j in range(n):
        colq_ref[j] = jnp.concatenate([mx[j], e1[j], fl[j]], axis=0).T


def _gateprep(gr, reset, rev):
    n_chunks = gr.shape[0]
    gp = min(GP_CHUNKS, n_chunks)
    assert n_chunks % gp == 0
    nb = n_chunks // gp
    d = 1 if rev else 0
    bidx = (lambda s: nb - 1 - s) if rev else (lambda s: s)
    grid_spec = pltpu.PrefetchScalarGridSpec(
        num_scalar_prefetch=1,
        grid=(nb,),
        in_specs=[pl.BlockSpec((gp, GATE_ROWS, CHUNK), lambda s, r: (bidx(s), d, 0))],
        out_specs=[pl.BlockSpec((gp, QROWS, CHUNK), lambda s, r: (bidx(s), 0, 0)),
                   pl.BlockSpec((gp, CHUNK, QROWS), lambda s, r: (bidx(s), 0, 0))],
        scratch_shapes=[pltpu.VMEM((8, CHUNK), F32)],
    )
    return pl.pallas_call(
        functools.partial(_gateprep_kernel, rev=rev),
        grid_spec=grid_spec,
        out_shape=[jax.ShapeDtypeStruct((n_chunks, QROWS, CHUNK), F32),
                   jax.ShapeDtypeStruct((n_chunks, CHUNK, QROWS), F32)],
        compiler_params=_cparams(),
        name="gateprep_bwd" if rev else "gateprep_fwd",
    )(reset, gr)


def _mlstm_kernel(rf_ref, rb_ref,
                  qf_ref, ktf_ref, vf_ref, rowf_ref, colf_ref,
                  qb_ref, ktb_ref, vb_ref, rowb_ref, colb_ref,
                  hf_ref, hb_ref, cst_ref):
    c = qf_ref.shape[0]
    step = pl.program_id(0)
    last = pl.num_programs(0) - 1
    row_i = lax.broadcasted_iota(jnp.int32, (c, c), 0)
    col_i = lax.broadcasted_iota(jnp.int32, (c, c), 1)
    ones = jnp.ones((c, HEAD_DIM), BF16)

    dirs = ((rf_ref[step], col_i <= row_i, qf_ref, ktf_ref, vf_ref, rowf_ref, colf_ref, hf_ref),
            (rb_ref[last - step], col_i >= row_i, qb_ref, ktb_ref, vb_ref, rowb_ref, colb_ref, hb_ref))
    for d, (reset, mask, q_ref, kt_ref, v_ref, row_ref, col_ref, h_ref) in enumerate(dirs):
        @pl.when(reset == 1)
        def _():
            cst_ref[d] = jnp.zeros(cst_ref.shape[1:], F32)

        rowq = row_ref[...]
        colq = col_ref[...]
        for h in range(N_HEADS):
            hs = slice(h * HEAD_DIM, (h + 1) * HEAD_DIM)
            qh = q_ref[:, hs]
            kth = kt_ref[hs, :]
            vext = jnp.concatenate([v_ref[:, hs], ones], axis=1)
            s = jnp.dot(qh, kth, preferred_element_type=F32)
            e = jnp.exp(jnp.where(mask, rowq[h:h + 1, :] - colq[:, h:h + 1], -jnp.inf))
            r1 = jnp.dot((s * e).astype(BF16), vext, preferred_element_type=F32)
            cst = cst_ref[d, h]
            r2 = jnp.dot(qh, cst.astype(BF16), preferred_element_type=F32)
            e1 = colq[:, 8 + h:9 + h]
            num = r1[:, :HEAD_DIM] + e1 * r2[:, :HEAD_DIM]
            den = r1[:, HEAD_DIM:] + e1 * r2[:, HEAD_DIM:]
            h_ref[:, hs] = (num / jnp.maximum(jnp.abs(den), colq[:, 16 + h:17 + h])).astype(h_ref.dtype)
            kw = (kth.astype(F32) * rowq[8 + h:9 + h, :]).astype(BF16)
            cst_ref[d, h] = rowq[16 + h:17 + h, 0:1] * cst + jnp.dot(kw, vext, preferred_element_type=F32)


def _mlstm(q, kt, v, gr, seq_lens):
    t = q.shape[0]
    n = t // CHUNK
    starts, ends, pos = [], [], 0
    for ln in seq_lens:
        assert ln % CHUNK == 0
        starts.append(pos // CHUNK)
        ends.append((pos + ln) // CHUNK - 1)
        pos += ln
    reset_f = jnp.zeros((n,), jnp.int32).at[jnp.array(starts)].set(1)
    reset_b = jnp.zeros((n,), jnp.int32).at[jnp.array(ends)].set(1)
    rowf, colf = _gateprep(gr, reset_f, rev=False)
    rowb, colb = _gateprep(gr, reset_b, rev=True)

    def specs(cidx):
        return [pl.BlockSpec((CHUNK, D_MLSTM), lambda s, rf, rb: (cidx(s), 0)),
                pl.BlockSpec((D_MLSTM, CHUNK), lambda s, rf, rb: (0, cidx(s))),
                pl.BlockSpec((CHUNK, D_MLSTM), lambda s, rf, rb: (cidx(s), 0)),
                pl.BlockSpec((None, QROWS, CHUNK), lambda s, rf, rb: (cidx(s), 0, 0)),
                pl.BlockSpec((None, CHUNK, QROWS), lambda s, rf, rb: (cidx(s), 0, 0))]

    fwd = lambda s: s
    bwd = lambda s: n - 1 - s
    grid_spec = pltpu.PrefetchScalarGridSpec(
        num_scalar_prefetch=2,
        grid=(n,),
        in_specs=specs(fwd) + specs(bwd),
        out_specs=[pl.BlockSpec((CHUNK, D_MLSTM), lambda s, rf, rb: (fwd(s), 0)),
                   pl.BlockSpec((CHUNK, D_MLSTM), lambda s, rf, rb: (bwd(s), 0))],
        scratch_shapes=[pltpu.VMEM((N_DIR, N_HEADS, HEAD_DIM, 2 * HEAD_DIM), F32)],
    )
    return pl.pallas_call(
        _mlstm_kernel,
        grid_spec=grid_spec,
        out_shape=[jax.ShapeDtypeStruct((t, D_MLSTM), BF16)] * 2,
        compiler_params=_cparams(),
        name="mlstm",
    )(reset_f, reset_b, q, kt, v, rowf, colf, q, kt, v, rowb, colb)


def _mixout_kernel(first_ref, last_ref,
                   x_ref, hf_ref, hb_ref, o_ref, u_ref, up_ref, un_ref, cw_ref, cb_ref, lng_ref, lnb_ref,
                   wout_ref, gffn_ref, wr_ref, br_ref, tri_ref,
                   x1_ref, hn_ref, meta_ref, metat_ref, cnt_ref,
                   win_ref, y_ref, run_ref):
    i = pl.program_id(0)
    tm = hf_ref.shape[0]

    @pl.when(i == 0)
    def _():
        run_ref[...] = jnp.zeros_like(run_ref)

    h_sum = hf_ref[...].astype(F32) + hb_ref[...].astype(F32)
    ym = (jax.nn.sigmoid(o_ref[...].astype(F32)) * h_sum).astype(BF16)
    x1m = x_ref[...] + jnp.dot(ym, wout_ref[:D_MLSTM, :], preferred_element_type=F32)

    win_ref[0:CONV_HALO, :] = jnp.where(first_ref[i] == 1, 0.0, up_ref[...].astype(F32))
    win_ref[CONV_HALO:CONV_HALO + tm, :] = u_ref[...].astype(F32)
    win_ref[CONV_HALO + tm:, :] = jnp.where(last_ref[i] == 1, 0.0, un_ref[...].astype(F32))

    off0 = CONV_HALO - CONV_WIDTH // 2
    for r0 in range(0, tm, CONV_ROWS):
        tiles = []
        for lt in range(D_CONV // 128):
            ls = slice(lt * 128, (lt + 1) * 128)
            acc = jnp.broadcast_to(cb_ref[:, ls], (CONV_ROWS, 128))
            for s in range(8):
                part = None
                for j in range(CONV_WIDTH):
                    if (off0 + j) % 8 != s:
                        continue
                    base = (off0 + j) // 8 * 8
                    term = win_ref[r0 + base:r0 + base + CONV_ROWS + 8, ls] * cw_ref[j:j + 1, ls]
                    part = term if part is None else part + term
                acc = acc + part[s:s + CONV_ROWS, :]
            tiles.append(acc)
        cv = jnp.concatenate(tiles, axis=1)
        xc = cv - jnp.mean(cv, axis=-1, keepdims=True)
        yc = xc * lax.rsqrt(jnp.mean(xc * xc, axis=-1, keepdims=True) + EPS) * lng_ref[...] + lnb_ref[...]
        y_ref[r0:r0 + CONV_ROWS, :] = (yc * jax.nn.sigmoid(yc)).astype(BF16)

    x1 = x1m + jnp.dot(y_ref[...], wout_ref[D_MLSTM:, :], preferred_element_type=F32)
    x1_ref[...] = x1
    hn = _rms(x1, gffn_ref[...])
    hn_ref[...] = _pack_rows(hn)

    logits = _nt_dot(wr_ref[...], hn.astype(BF16)) + br_ref[...]
    row = lax.broadcasted_iota(jnp.int32, (ROUTER_ROWS, tm), 0).astype(F32)
    neg = -jnp.inf
    no_row = float(ROUTER_ROWS)
    gl = jnp.where(row < N_GROUPS, logits, neg)
    gmax = jnp.max(gl, axis=0, keepdims=True)
    p_top = 1.0 / jnp.sum(jnp.exp(gl - gmax), axis=0, keepdims=True)
    g_idx = jnp.min(jnp.where(gl == gmax, row, no_row), axis=0, keepdims=True)
    lo = EXPERT_ROW0 + EXPERTS_PER_GROUP * g_idx
    in_grp = (row >= lo) & (row < lo + EXPERTS_PER_GROUP)
    el = jnp.where(in_grp, logits, neg)
    ee = jnp.exp(el - jnp.max(el, axis=0, keepdims=True))
    pe = jnp.where(in_grp, ee / jnp.sum(ee, axis=0, keepdims=True), -1.0)
    v1 = jnp.max(pe, axis=0, keepdims=True)
    i1 = jnp.min(jnp.where(pe == v1, row, no_row), axis=0, keepdims=True)
    pe2 = jnp.where(row == i1, -1.0, pe)
    v2 = jnp.max(pe2, axis=0, keepdims=True)
    i2 = jnp.min(jnp.where(pe2 == v2, row, no_row), axis=0, keepdims=True)
    wsum = v1 + v2
    gate1 = p_top * (v1 / wsum)
    gate2 = p_top * (v2 / wsum)

    oh1 = (row == i1).astype(F32)
    oh2 = (row == i2).astype(F32)
    cnt = jnp.dot(jnp.concatenate([oh1, oh2], axis=0).astype(BF16), tri_ref[...], preferred_element_type=F32)
    run = run_ref[...]
    tot1 = jnp.sum(oh1, axis=1, keepdims=True)
    tot2 = jnp.sum(oh2, axis=1, keepdims=True)
    rank1 = jnp.sum(oh1 * (run + cnt[:ROUTER_ROWS]), axis=0, keepdims=True)
    rank2 = jnp.sum(oh2 * (run + tot1 + cnt[ROUTER_ROWS:]), axis=0, keepdims=True)
    run = run + tot1 + tot2
    run_ref[...] = run
    cnt_ref[...] = jnp.broadcast_to(run, cnt_ref.shape)

    mr = lax.broadcasted_iota(jnp.int32, (META_LANES, tm), 0)
    metat = jnp.where(mr == 0, i1 - EXPERT_ROW0, 0.0)
    metat = jnp.where(mr == 1, i2 - EXPERT_ROW0, metat)
    metat = jnp.where(mr == 2, gate1, metat)
    metat = jnp.where(mr == 3, gate2, metat)
    metat = jnp.where(mr == 4, rank1, metat)
    metat = jnp.where(mr == 5, rank2, metat)
    metat_ref[...] = metat
    meta_ref[...] = metat.T


def _mixout(x, tile0, hf, hb, o, u, seq_lens, conv_w, conv_b, ln_g, ln_b, w_out, g_ffn, w_rg, b_rg, w_re, b_re):
    t = x.shape[0]
    n_tiles = t // TM_MIX
    hpt = TM_MIX // CONV_HALO
    n_halo = u.shape[0] // CONV_HALO
    firsts, lasts, pos = [], [], 0
    for ln in seq_lens:
        assert ln % TM_MIX == 0
        firsts.append(pos // TM_MIX)
        lasts.append((pos + ln) // TM_MIX - 1)
        pos += ln
    assert pos == t
    first = jnp.zeros((n_tiles,), jnp.int32).at[jnp.array(firsts)].set(1)
    last = jnp.zeros((n_tiles,), jnp.int32).at[jnp.array(lasts)].set(1)

    cw = jnp.zeros((32, D_CONV), F32).at[:CONV_WIDTH].set(conv_w.reshape(CONV_WIDTH, D_CONV))
    wr = jnp.zeros((ROUTER_ROWS, D_MODEL), F32)
    wr = wr.at[:N_GROUPS].set(w_rg.T).at[EXPERT_ROW0:EXPERT_ROW0 + N_EXPERTS].set(w_re.T).astype(BF16)
    br = jnp.zeros((ROUTER_ROWS, 1), F32)
    br = br.at[:N_GROUPS, 0].set(b_rg).at[EXPERT_ROW0:EXPERT_ROW0 + N_EXPERTS, 0].set(b_re)
    tri = (lax.broadcasted_iota(jnp.int32, (TM_MIX, TM_MIX), 0)
           < lax.broadcasted_iota(jnp.int32, (TM_MIX, TM_MIX), 1)).astype(BF16)

    tok = lambda i, f, l: (i, 0)
    flat = lambda i, f, l: (tile0 + i, 0)
    fixed = lambda i, f, l: (0, 0)
    row = lambda n: pl.BlockSpec((1, n), fixed)
    half = pl.BlockSpec((TM_MIX, D_MLSTM), flat)
    grid_spec = pltpu.PrefetchScalarGridSpec(
        num_scalar_prefetch=2,
        grid=(n_tiles,),
        in_specs=[pl.BlockSpec((TM_MIX, D_MODEL), tok), half, half, half,
                  pl.BlockSpec((TM_MIX, D_CONV), flat),
                  pl.BlockSpec((CONV_HALO, D_CONV), lambda i, f, l: (jnp.maximum((tile0 + i) * hpt - 1, 0), 0)),
                  pl.BlockSpec((CONV_HALO, D_CONV),
                               lambda i, f, l: (jnp.minimum((tile0 + i + 1) * hpt, n_halo - 1), 0)),
                  pl.BlockSpec((32, D_CONV), fixed), row(D_CONV), row(D_CONV), row(D_CONV),
                  pl.BlockSpec((D_MODEL, D_MODEL), fixed), row(D_MODEL),
                  pl.BlockSpec((ROUTER_ROWS, D_MODEL), fixed), pl.BlockSpec((ROUTER_ROWS, 1), fixed),
                  pl.BlockSpec((TM_MIX, TM_MIX), fixed)],
        out_specs=[pl.BlockSpec((TM_MIX, D_MODEL), tok), pl.BlockSpec((TM_MIX, D_MODEL // 2), tok),
                   pl.BlockSpec((TM_MIX, META_LANES), tok),
                   pl.BlockSpec((META_LANES, TM_MIX), lambda i, f, l: (0, i)),
                   pl.BlockSpec((ROUTER_ROWS, 128), fixed)],
        scratch_shapes=[pltpu.VMEM((TM_MIX + 2 * CONV_HALO, D_CONV), F32),
                        pltpu.VMEM((TM_MIX, D_CONV), BF16),
                        pltpu.VMEM((ROUTER_ROWS, 1), F32)],
    )
    return pl.pallas_call(
        _mixout_kernel,
        grid_spec=grid_spec,
        out_shape=[jax.ShapeDtypeStruct((t, D_MODEL), F32), jax.ShapeDtypeStruct((t, D_MODEL // 2), jnp.int32),
                   jax.ShapeDtypeStruct((t, META_LANES), F32), jax.ShapeDtypeStruct((META_LANES, t), F32),
                   jax.ShapeDtypeStruct((ROUTER_ROWS, 128), F32)],
        compiler_params=_cparams(),
        name="mixout",
    )(first, last, x, hf, hb, o, u, u, u, cw, conv_b.reshape(1, D_CONV), ln_g.reshape(1, D_CONV),
      ln_b.reshape(1, D_CONV), w_out.astype(BF16), g_ffn.reshape(1, D_MODEL), wr, br, tri)


def _sc_mesh():
    return plsc.VectorSubcoreMesh(core_axis_name="c", subcore_axis_name="s")


def _sc_worker_base(per_worker):
    wid = lax.axis_index("s") * SC_CORES + lax.axis_index("c")
    return wid * per_worker


def _dispatch(hn, pos1, pos2, n_slots):
    t, d = hn.shape
    per_worker = t // SC_WORKERS
    assert per_worker * SC_WORKERS == t and per_worker % SC_ROWS == 0

    def body(h_hbm, p1_hbm, p2_hbm, out_hbm, i1_v, i2_v, rows_v, sem1, sem2):
        base0 = _sc_worker_base(per_worker)

        @pl.loop(0, per_worker // SC_ROWS)
        def _(c):
            base = pl.multiple_of(base0 + c * SC_ROWS, SC_ROWS)
            pltpu.sync_copy(p1_hbm.at[pl.ds(base, SC_ROWS)], i1_v)
            pltpu.sync_copy(p2_hbm.at[pl.ds(base, SC_ROWS)], i2_v)
            pltpu.sync_copy(h_hbm.at[pl.ds(base, SC_ROWS)], rows_v)
            c1 = pltpu.async_copy(rows_v, out_hbm.at[i1_v], sem1)
            c2 = pltpu.async_copy(rows_v, out_hbm.at[i2_v], sem2)
            c1.wait()
            c2.wait()

    return pl.kernel(
        body,
        out_type=jax.ShapeDtypeStruct((n_slots, d), hn.dtype),
        mesh=_sc_mesh(),
        scratch_types=[pltpu.VMEM((SC_ROWS,), jnp.int32), pltpu.VMEM((SC_ROWS,), jnp.int32),
                       pltpu.VMEM((SC_ROWS, d), hn.dtype), pltpu.SemaphoreType.DMA, pltpu.SemaphoreType.DMA],
        name="dispatch",
    )(hn, pos1, pos2)


def _collect(ys, pos1, pos2):
    t = pos1.shape[0]
    d = ys.shape[1]
    per_worker = t // SC_WORKERS
    assert per_worker * SC_WORKERS == t and per_worker % SC_ROWS == 0

    def body(ys_hbm, p1_hbm, p2_hbm, y1_hbm, y2_hbm, i_v, rows_v, sem):
        base0 = _sc_worker_base(per_worker)

        @pl.loop(0, per_worker // SC_ROWS)
        def _(c):
            base = pl.multiple_of(base0 + c * SC_ROWS, SC_ROWS)
            for p_hbm, y_hbm in ((p1_hbm, y1_hbm), (p2_hbm, y2_hbm)):
                pltpu.sync_copy(p_hbm.at[pl.ds(base, SC_ROWS)], i_v)
                pltpu.async_copy(ys_hbm.at[i_v], rows_v, sem).wait()
                pltpu.sync_copy(rows_v, y_hbm.at[pl.ds(base, SC_ROWS)])

    out = jax.ShapeDtypeStruct((t, d), ys.dtype)
    return pl.kernel(
        body,
        out_type=(out, out),
        mesh=_sc_mesh(),
        scratch_types=[pltpu.VMEM((SC_ROWS,), jnp.int32), pltpu.VMEM((SC_ROWS, d), ys.dtype),
                       pltpu.SemaphoreType.DMA],
        name="collect",
    )(ys, pos1, pos2)


def _experts_kernel(te_ref, rows_ref, hs_ref, wg_ref, wu_ref, wd_ref, ys_ref, wgb_ref, wub_ref, wdb_ref):
    i = pl.program_id(0)

    @pl.when((i == 0) | (te_ref[i] != te_ref[jnp.maximum(i - 1, 0)]))
    def _():
        wgb_ref[...] = wg_ref[...].astype(BF16)
        wub_ref[...] = wu_ref[...].astype(BF16)
        wdb_ref[...] = wd_ref[...].astype(BF16)

    @pl.when(rows_ref[i] > 0)
    def _():
        x = _unpack_rows(hs_ref[...]).astype(BF16)
        hg = jnp.dot(x, wgb_ref[...], preferred_element_type=F32)
        hu = jnp.dot(x, wub_ref[...], preferred_element_type=F32)
        hid = (hg * jax.nn.sigmoid(hg) * hu).astype(BF16)
        ys_ref[...] = _pack_rows(jnp.dot(hid, wdb_ref[...], preferred_element_type=F32))

    @pl.when(rows_ref[i] == 0)
    def _():
        ys_ref[...] = jnp.zeros_like(ys_ref)


def _experts(hs, tile_expert, tile_rows, w_gate, w_up, w_down):
    n_tiles = tile_expert.shape[0]
    wsel = lambda i, te, rows: (te[i], 0, 0)
    slot = lambda i, te, rows: (i, 0)
    grid_spec = pltpu.PrefetchScalarGridSpec(
        num_scalar_prefetch=2,
        grid=(n_tiles,),
        in_specs=[pl.BlockSpec((TM_EXP, D_MODEL // 2), slot),
                  pl.BlockSpec((None, D_MODEL, D_EXPERT), wsel),
                  pl.BlockSpec((None, D_MODEL, D_EXPERT), wsel),
                  pl.BlockSpec((None, D_EXPERT, D_MODEL), wsel)],
        out_specs=pl.BlockSpec((TM_EXP, D_MODEL // 2), slot),
        scratch_shapes=[pltpu.VMEM((D_MODEL, D_EXPERT), BF16), pltpu.VMEM((D_MODEL, D_EXPERT), BF16),
                        pltpu.VMEM((D_EXPERT, D_MODEL), BF16)],
    )
    return pl.pallas_call(
        _experts_kernel,
        grid_spec=grid_spec,
        out_shape=jax.ShapeDtypeStruct((n_tiles * TM_EXP, D_MODEL // 2), jnp.int32),
        compiler_params=_cparams(),
        name="experts",
    )(tile_expert, tile_rows, hs, w_gate, w_up, w_down)


def _combine_kernel(x1_ref, meta_ref, y1_ref, y2_ref, gfin_ref, out_ref):
    meta = meta_ref[...]
    x2 = x1_ref[...] + meta[:, 2:3] * _unpack_rows(y1_ref[...]) + meta[:, 3:4] * _unpack_rows(y2_ref[...])
    out_ref[...] = _rms(x2, gfin_ref[...])


def _combine(x1, meta, y1, y2, g_final):
    t = x1.shape[0]
    assert t % TM_OUT == 0
    tok = pl.BlockSpec((TM_OUT, D_MODEL), lambda i: (i, 0))
    packed = pl.BlockSpec((TM_OUT, D_MODEL // 2), lambda i: (i, 0))
    return pl.pallas_call(
        _combine_kernel,
        grid=(t // TM_OUT,),
        in_specs=[tok, pl.BlockSpec((TM_OUT, META_LANES), lambda i: (i, 0)), packed, packed,
                  pl.BlockSpec((1, D_MODEL), lambda i: (0, 0))],
        out_specs=tok,
        out_shape=jax.ShapeDtypeStruct((t, D_MODEL), F32),
        compiler_params=_cparams(),
        name="combine",
    )(x1, meta, y1, y2, g_final.reshape(1, D_MODEL))


def _slot_layout(metat, counts_rows, n_tiles):
    counts = counts_rows[EXPERT_ROW0:EXPERT_ROW0 + N_EXPERTS, 0].astype(jnp.int32)
    padded = (counts + TM_EXP - 1) // TM_EXP * TM_EXP
    ends = jnp.cumsum(padded)
    offs = ends - padded
    offs_f = offs.astype(F32)
    pos = []
    for k in range(2):
        eid, p = metat[k], metat[4 + k]
        for e in range(N_EXPERTS):
            p = p + jnp.where(eid == e, offs_f[e], 0.0)
        pos.append(p.astype(jnp.int32))
    tile_start = jnp.arange(n_tiles, dtype=jnp.int32) * TM_EXP
    tile_expert = jnp.minimum(jnp.sum(tile_start[:, None] >= ends[None, :], axis=1), N_EXPERTS - 1)
    tile_expert = tile_expert.astype(jnp.int32)
    tile_rows = jnp.clip(counts[tile_expert] - (tile_start - offs[tile_expert]), 0, TM_EXP)
    tile_rows = jnp.where(tile_start < ends[-1], tile_rows, 0).astype(jnp.int32)
    return pos[0], pos[1], tile_expert, tile_rows


def kernel(x_prompt, x_sample, g_mix, w_in, b_gate, conv_w, conv_b, ln_g, ln_b, w_out, g_ffn,
           w_router_group, b_router_group, w_router_expert, b_router_expert, w_gate, w_up, w_down, g_final):
    assert g_mix.shape[0] == 1, "one layer"
    bp, lp, _ = x_prompt.shape
    bs, ls, _ = x_sample.shape
    seq_lens = [lp] * bp + [ls] * bs
    tp, ts = bp * lp, bs * ls
    xp = x_prompt.reshape(tp, D_MODEL)
    xs = x_sample.reshape(ts, D_MODEL)

    q, v, o, u, kt, gr = _inproj(xp, xs, g_mix[0], w_in[0], b_gate[0])
    hf, hb = _mlstm(q, kt, v, gr, seq_lens)

    outs = []
    for x, tile0, lens in ((xp, 0, [lp] * bp), (xs, tp // TM_MIX, [ls] * bs)):
        tg = x.shape[0]
        x1, hn, meta, metat, counts = _mixout(x, tile0, hf, hb, o, u, lens, conv_w[0], conv_b[0], ln_g[0], ln_b[0],
                                              w_out[0], g_ffn[0], w_router_group[0], b_router_group[0],
                                              w_router_expert[0], b_router_expert[0])
        n_tiles = (2 * tg + N_EXPERTS * (TM_EXP - 1)) // TM_EXP + 1
        pos1, pos2, tile_expert, tile_rows = _slot_layout(metat, counts, n_tiles)
        hs = _dispatch(hn, pos1, pos2, n_tiles * TM_EXP)
        ys = _experts(hs, tile_expert, tile_rows, w_gate[0], w_up[0], w_down[0])
        y1, y2 = _collect(ys, pos1, pos2)
        outs.append(_combine(x1, meta, y1, y2, g_final))
    return outs[0].reshape(bp, lp, D_MODEL), outs[1].reshape(bs, ls, D_MODEL)
```

```python
import functools

import jax
import jax.numpy as jnp
from jax import lax
from jax.experimental import pallas as pl
from jax.experimental.pallas import tpu as pltpu
from jax.experimental.pallas import tpu_sc as plsc

F32 = jnp.float32
BF16 = jnp.bfloat16

D_MODEL = 1024
N_HEADS = 4
HEAD_DIM = 128
D_MLSTM = N_HEADS * HEAD_DIM
D_CONV = D_MODEL - D_MLSTM
CONV_WIDTH = 31
CONV_HALO = 16
N_DIR = 2
N_GROUPS = 4
EXPERTS_PER_GROUP = 4
N_EXPERTS = N_GROUPS * EXPERTS_PER_GROUP
D_EXPERT = 512
EPS = 1e-6
K_SCALE = HEAD_DIM ** -0.5

GATE_ROWS = 16
QROWS = 24
ROUTER_ROWS = 32
EXPERT_ROW0 = N_GROUPS
META_LANES = 8

TM_IN = 1024
CHUNK = 512
GP_CHUNKS = 32
TM_MIX = 512
CONV_ROWS = 256
TM_EXP = 512
TM_OUT = 1024
TM_POS = 8192
VMEM_LIMIT = 48 * 1024 * 1024

SC_CORES = 2
SC_SUBCORES = 16
SC_WORKERS = SC_CORES * SC_SUBCORES
SC_ROWS = 128


def _cparams(n_axes=1):
    return pltpu.CompilerParams(dimension_semantics=("arbitrary",) * n_axes,
                                vmem_limit_bytes=VMEM_LIMIT)


def _nt_dot(a, b):
    return lax.dot_general(a, b, (((1,), (1,)), ((), ())), preferred_element_type=F32)


def _rms(x, g):
    return x * lax.rsqrt(jnp.mean(x * x, axis=-1, keepdims=True) + EPS) * g


def _pack_rows(x):
    n = x.shape[1] // 2
    hi = lax.bitcast_convert_type(x[:, :n].astype(jnp.bfloat16).astype(F32), jnp.int32)
    lo = lax.bitcast_convert_type(x[:, n:].astype(jnp.bfloat16).astype(F32), jnp.int32)
    return hi | lax.shift_right_logical(lo, 16)


def _unpack_rows(p):
    hi = lax.bitcast_convert_type(p & jnp.int32(-65536), F32)
    lo = lax.bitcast_convert_type(lax.shift_left(p, 16), F32)
    return jnp.concatenate([hi, lo], axis=1)


def _two_batch_specs(block, n_first, n_second):
    first = pl.BlockSpec(block, lambda i, *_: (jnp.minimum(i, n_first - 1), 0))
    second = pl.BlockSpec(block, lambda i, *_: (jnp.maximum(i - n_first, 0), 0))
    return first, second


def _inproj_kernel(xp_ref, xs_ref, g_ref, wq_ref, wv_ref, wo_ref, wa_ref, wb_ref, wkg_ref, bg_ref,
                   q_ref, v_ref, o_ref, u_ref, kt_ref, gr_ref, *, n_first):
    x = jnp.where(pl.program_id(0) < n_first, xp_ref[...], xs_ref[...])
    xn = _rms(x, g_ref[...]).astype(BF16)
    q_ref[...] = jnp.dot(xn, wq_ref[...], preferred_element_type=F32).astype(BF16)
    v_ref[...] = jnp.dot(xn, wv_ref[...], preferred_element_type=F32).astype(BF16)
    o_ref[...] = jnp.dot(xn, wo_ref[...], preferred_element_type=F32).astype(BF16)
    a = jnp.dot(xn, wa_ref[...], preferred_element_type=F32)
    b = jnp.dot(xn, wb_ref[...], preferred_element_type=F32)
    u_ref[...] = (a * jax.nn.sigmoid(b)).astype(BF16)
    kg = _nt_dot(wkg_ref[...], xn)
    kt_ref[...] = (kg[:D_MLSTM] * K_SCALE).astype(BF16)
    gr = kg[D_MLSTM:] + bg_ref[...]
    for c in range(gr_ref.shape[0]):
        gr_ref[c] = gr[:, c * CHUNK:(c + 1) * CHUNK]


def _inproj(xp, xs, g_mix, w_in, b_gate):
    tp, ts = xp.shape[0], xs.shape[0]
    t = tp + ts
    assert tp % TM_IN == 0 and ts % TM_IN == 0
    off_k, off_v, off_o, off_g = D_MLSTM, 2 * D_MLSTM, 3 * D_MLSTM, 4 * D_MLSTM
    off_a = off_g + 2 * N_DIR * N_HEADS
    off_b = off_a + D_CONV
    wq = w_in[:, 0:off_k].astype(BF16)
    wkt = w_in[:, off_k:off_v].T.astype(BF16)
    wv = w_in[:, off_v:off_o].astype(BF16)
    wo = w_in[:, off_o:off_g].astype(BF16)
    wa = w_in[:, off_a:off_b].astype(BF16)
    wb = w_in[:, off_b:off_b + D_CONV].astype(BF16)
    wg = w_in[:, off_g:off_a].T.reshape(N_DIR, 2, N_HEADS, D_MODEL)
    wgt = jnp.zeros((N_DIR, 2, GATE_ROWS // 2, D_MODEL), F32).at[:, :, :N_HEADS].set(wg)
    wkg = jnp.concatenate([wkt, wgt.reshape(N_DIR * GATE_ROWS, D_MODEL).astype(BF16)], axis=0)
    bg = jnp.zeros((N_DIR, 2, GATE_ROWS // 2), F32).at[:, :, :N_HEADS].set(
        b_gate.reshape(N_DIR, 2, N_HEADS)).reshape(N_DIR * GATE_ROWS, 1)

    tok = lambda i: (i, 0)
    fixed = lambda i: (0, 0)
    wspec = pl.BlockSpec((D_MODEL, D_MLSTM), fixed)
    xp_spec, xs_spec = _two_batch_specs((TM_IN, D_MODEL), tp // TM_IN, ts // TM_IN)
    cpt = TM_IN // CHUNK
    return pl.pallas_call(
        functools.partial(_inproj_kernel, n_first=tp // TM_IN),
        grid=(t // TM_IN,),
        in_specs=[xp_spec, xs_spec, pl.BlockSpec((1, D_MODEL), fixed),
                  wspec, wspec, wspec, wspec, wspec,
                  pl.BlockSpec((D_MLSTM + N_DIR * GATE_ROWS, D_MODEL), fixed),
                  pl.BlockSpec((N_DIR * GATE_ROWS, 1), fixed)],
        out_specs=[pl.BlockSpec((TM_IN, D_MLSTM), tok)] * 4 + [
            pl.BlockSpec((D_MLSTM, TM_IN), lambda i: (0, i)),
            pl.BlockSpec((cpt, N_DIR * GATE_ROWS, CHUNK), lambda i: (i, 0, 0))],
        out_shape=[jax.ShapeDtypeStruct((t, D_MLSTM), BF16)] * 4 + [
            jax.ShapeDtypeStruct((D_MLSTM, t), BF16),
            jax.ShapeDtypeStruct((t // CHUNK, N_DIR * GATE_ROWS, CHUNK), F32)],
        compiler_params=_cparams(),
        name="inproj",
    )(xp, xs, g_mix.reshape(1, D_MODEL), wq, wv, wo, wa, wb, wkg, bg)


def _log_sigmoid(x):
    return jnp.minimum(x, 0.0) - jnp.log1p(jnp.exp(-jnp.abs(x)))


def _gateprep_kernel(reset_ref, g_ref, rowq_ref, colq_ref, m_ref, *, rev):
    n, _, c = g_ref.shape
    step = pl.program_id(0)
    blk = pl.num_programs(0) - 1 - step if rev else step

    @pl.when(step == 0)
    def _():
        m_ref[...] = jnp.zeros_like(m_ref)

    ig = g_ref[:, 0:8, :]
    lf = _log_sigmoid(g_ref[:, 8:16, :])
    lane = lax.broadcasted_iota(jnp.int32, (n, 8, c), 2)

    def scan(x, op, ident):
        k = 1
        while k < c:
            if rev:
                shifted, valid = pltpu.roll(x, c - k, axis=2), lane < c - k
            else:
                shifted, valid = pltpu.roll(x, k, axis=2), lane >= k
            x = op(x, jnp.where(valid, shifted, ident))
            k *= 2
        return x

    bc = scan(lf, jnp.add, 0.0)
    a = ig - bc
    cm = scan(a, jnp.maximum, -jnp.inf)
    b_tot = jnp.sum(lf, axis=2, keepdims=True)
    a_max = jnp.max(a, axis=2, keepdims=True)

    m = m_ref[...]
    m_in = [None] * n
    for j in (range(n - 1, -1, -1) if rev else range(n)):
        m = jnp.where(reset_ref[blk * n + j] == 1, 0.0, m)
        m_in[j] = m
        m = b_tot[j] + jnp.maximum(m, a_max[j])
    m_ref[...] = m
    m_old = jnp.stack(m_in)

    mx = jnp.maximum(m_old, cm)
    mx_last = jnp.maximum(m_old, a_max)
    rowq_ref[:, 0:8, :] = a
    rowq_ref[:, 8:16, :] = jnp.exp(a - mx_last)
    rowq_ref[:, 16:24, :] = jnp.exp(m_old - mx_last)
    e1 = jnp.exp(m_old - mx)
    fl = jnp.exp(-(mx + bc))
    for j in range(n):
        colq_ref[j] = jnp.concatenate([mx[j], e1[j], fl[j]], axis=0).T


def _gateprep(gr, reset, rev):
    n_chunks = gr.shape[0]
    gp = min(GP_CHUNKS, n_chunks)
    assert n_chunks % gp == 0
    nb = n_chunks // gp
    d = 1 if rev else 0
    bidx = (lambda s: nb - 1 - s) if rev else (lambda s: s)
    grid_spec = pltpu.PrefetchScalarGridSpec(
        num_scalar_prefetch=1,
        grid=(nb,),
        in_specs=[pl.BlockSpec((gp, GATE_ROWS, CHUNK), lambda s, r: (bidx(s), d, 0))],
        out_specs=[pl.BlockSpec((gp, QROWS, CHUNK), lambda s, r: (bidx(s), 0, 0)),
                   pl.BlockSpec((gp, CHUNK, QROWS), lambda s, r: (bidx(s), 0, 0))],
        scratch_shapes=[pltpu.VMEM((8, CHUNK), F32)],
    )
    return pl.pallas_call(
        functools.partial(_gateprep_kernel, rev=rev),
        grid_spec=grid_spec,
        out_shape=[jax.ShapeDtypeStruct((n_chunks, QROWS, CHUNK), F32),
                   jax.ShapeDtypeStruct((n_chunks, CHUNK, QROWS), F32)],
        compiler_params=_cparams(),
        name="gateprep_bwd" if rev else "gateprep_fwd",
    )(reset, gr)


def _mlstm_kernel(rf_ref, rb_ref,
                  qf_ref, ktf_ref, vf_ref, rowf_ref, colf_ref,
                  qb_ref, ktb_ref, vb_ref, rowb_ref, colb_ref,
                  hf_ref, hb_ref, cst_ref):
    c = qf_ref.shape[0]
    step = pl.program_id(0)
    last = pl.num_programs(0) - 1
    row_i = lax.broadcasted_iota(jnp.int32, (c, c), 0)
    col_i = lax.broadcasted_iota(jnp.int32, (c, c), 1)
    ones = jnp.ones((c, HEAD_DIM), BF16)

    dirs = ((rf_ref[step], col_i <= row_i, qf_ref, ktf_ref, vf_ref, rowf_ref, colf_ref, hf_ref),
            (rb_ref[last - step], col_i >= row_i, qb_ref, ktb_ref, vb_ref, rowb_ref, colb_ref, hb_ref))
    for d, (reset, mask, q_ref, kt_ref, v_ref, row_ref, col_ref, h_ref) in enumerate(dirs):
        @pl.when(reset == 1)
        def _():
            cst_ref[d] = jnp.zeros(cst_ref.shape[1:], F32)

        rowq = row_ref[...]
        colq = col_ref[...]
        for h in range(N_HEADS):
            hs = slice(h * HEAD_DIM, (h + 1) * HEAD_DIM)
            qh = q_ref[:, hs]
            kth = kt_ref[hs, :]
            vext = jnp.concatenate([v_ref[:, hs], ones], axis=1)
            s = jnp.dot(qh, kth, preferred_element_type=F32)
            e = jnp.exp(jnp.where(mask, rowq[h:h + 1, :] - colq[:, h:h + 1], -jnp.inf))
            r1 = jnp.dot((s * e).astype(BF16), vext, preferred_element_type=F32)
            cst = cst_ref[d, h]
            r2 = jnp.dot(qh, cst.astype(BF16), preferred_element_type=F32)
            e1 = colq[:, 8 + h:9 + h]
            num = r1[:, :HEAD_DIM] + e1 * r2[:, :HEAD_DIM]
            den = r1[:, HEAD_DIM:] + e1 * r2[:, HEAD_DIM:]
            h_ref[:, hs] = (num / jnp.maximum(jnp.abs(den), colq[:, 16 + h:17 + h])).astype(h_ref.dtype)
            kw = (kth.astype(F32) * rowq[8 + h:9 + h, :]).astype(BF16)
            cst_ref[d, h] = rowq[16 + h:17 + h, 0:1] * cst + jnp.dot(kw, vext, preferred_element_type=F32)


def _mlstm(q, kt, v, gr, seq_lens):
    t = q.shape[0]
    n = t // CHUNK
    starts, ends, pos = [], [], 0
    for ln in seq_lens:
        assert ln % CHUNK == 0
        starts.append(pos // CHUNK)
        ends.append((pos + ln) // CHUNK - 1)
        pos += ln
    reset_f = jnp.zeros((n,), jnp.int32).at[jnp.array(starts)].set(1)
    reset_b = jnp.zeros((n,), jnp.int32).at[jnp.array(ends)].set(1)
    rowf, colf = _gateprep(gr, reset_f, rev=False)
    rowb, colb = _gateprep(gr, reset_b, rev=True)

    def specs(cidx):
        return [pl.BlockSpec((CHUNK, D_MLSTM), lambda s, rf, rb: (cidx(s), 0)),
                pl.BlockSpec((D_MLSTM, CHUNK), lambda s, rf, rb: (0, cidx(s))),
                pl.BlockSpec((CHUNK, D_MLSTM), lambda s, rf, rb: (cidx(s), 0)),
                pl.BlockSpec((None, QROWS, CHUNK), lambda s, rf, rb: (cidx(s), 0, 0)),
                pl.BlockSpec((None, CHUNK, QROWS), lambda s, rf, rb: (cidx(s), 0, 0))]

    fwd = lambda s: s
    bwd = lambda s: n - 1 - s
    grid_spec = pltpu.PrefetchScalarGridSpec(
        num_scalar_prefetch=2,
        grid=(n,),
        in_specs=specs(fwd) + specs(bwd),
        out_specs=[pl.BlockSpec((CHUNK, D_MLSTM), lambda s, rf, rb: (fwd(s), 0)),
                   pl.BlockSpec((CHUNK, D_MLSTM), lambda s, rf, rb: (bwd(s), 0))],
        scratch_shapes=[pltpu.VMEM((N_DIR, N_HEADS, HEAD_DIM, 2 * HEAD_DIM), F32)],
    )
    return pl.pallas_call(
        _mlstm_kernel,
        grid_spec=grid_spec,
        out_shape=[jax.ShapeDtypeStruct((t, D_MLSTM), BF16)] * 2,
        compiler_params=_cparams(),
        name="mlstm",
    )(reset_f, reset_b, q, kt, v, rowf, colf, q, kt, v, rowb, colb)


def _mixout_kernel(first_ref, last_ref,
                   x_ref, hf_ref, hb_ref, o_ref, u_ref, up_ref, un_ref, cw_ref, cb_ref, lng_ref, lnb_ref,
                   wout_ref, gffn_ref, wr_ref, br_ref, tri_ref,
                   x1_ref, hn_ref, meta_ref, metat_ref, cnt_ref,
                   win_ref, y_ref, run_ref):
    i = pl.program_id(0)
    tm = hf_ref.shape[0]

    @pl.when(i == 0)
    def _():
        run_ref[...] = jnp.zeros_like(run_ref)

    h_sum = hf_ref[...].astype(F32) + hb_ref[...].astype(F32)
    ym = (jax.nn.sigmoid(o_ref[...].astype(F32)) * h_sum).astype(BF16)
    x1m = x_ref[...] + jnp.dot(ym, wout_ref[:D_MLSTM, :], preferred_element_type=F32)

    win_ref[0:CONV_HALO, :] = jnp.where(first_ref[i] == 1, 0.0, up_ref[...].astype(F32))
    win_ref[CONV_HALO:CONV_HALO + tm, :] = u_ref[...].astype(F32)
    win_ref[CONV_HALO + tm:, :] = jnp.where(last_ref[i] == 1, 0.0, un_ref[...].astype(F32))

    off0 = CONV_HALO - CONV_WIDTH // 2
    for r0 in range(0, tm, CONV_ROWS):
        tiles = []
        for lt in range(D_CONV // 128):
            ls = slice(lt * 128, (lt + 1) * 128)
            acc = jnp.broadcast_to(cb_ref[:, ls], (CONV_ROWS, 128))
            for s in range(8):
                part = None
                for j in range(CONV_WIDTH):
                    if (off0 + j) % 8 != s:
                        continue
                    base = (off0 + j) // 8 * 8
                    term = win_ref[r0 + base:r0 + base + CONV_ROWS + 8, ls] * cw_ref[j:j + 1, ls]
                    part = term if part is None else part + term
                acc = acc + part[s:s + CONV_ROWS, :]
            tiles.append(acc)
        cv = jnp.concatenate(tiles, axis=1)
        xc = cv - jnp.mean(cv, axis=-1, keepdims=True)
        yc = xc * lax.rsqrt(jnp.mean(xc * xc, axis=-1, keepdims=True) + EPS) * lng_ref[...] + lnb_ref[...]
        y_ref[r0:r0 + CONV_ROWS, :] = (yc * jax.nn.sigmoid(yc)).astype(BF16)

    x1 = x1m + jnp.dot(y_ref[...], wout_ref[D_MLSTM:, :], preferred_element_type=F32)
    x1_ref[...] = x1
    hn = _rms(x1, gffn_ref[...])
    hn_ref[...] = _pack_rows(hn)

    logits = _nt_dot(wr_ref[...], hn.astype(BF16)) + br_ref[...]
    row = lax.broadcasted_iota(jnp.int32, (ROUTER_ROWS, tm), 0).astype(F32)
    neg = -jnp.inf
    no_row = float(ROUTER_ROWS)
    gl = jnp.where(row < N_GROUPS, logits, neg)
    gmax = jnp.max(gl, axis=0, keepdims=True)
    p_top = 1.0 / jnp.sum(jnp.exp(gl - gmax), axis=0, keepdims=True)
    g_idx = jnp.min(jnp.where(gl == gmax, row, no_row), axis=0, keepdims=True)
    lo = EXPERT_ROW0 + EXPERTS_PER_GROUP * g_idx
    in_grp = (row >= lo) & (row < lo + EXPERTS_PER_GROUP)
    el = jnp.where(in_grp, logits, neg)
    ee = jnp.exp(el - jnp.max(el, axis=0, keepdims=True))
    pe = jnp.where(in_grp, ee / jnp.sum(ee, axis=0, keepdims=True), -1.0)
    v1 = jnp.max(pe, axis=0, keepdims=True)
    i1 = jnp.min(jnp.where(pe == v1, row, no_row), axis=0, keepdims=True)
    pe2 = jnp.where(row == i1, -1.0, pe)
    v2 = jnp.max(pe2, axis=0, keepdims=True)
    i2 = jnp.min(jnp.where(pe2 == v2, row, no_row), axis=0, keepdims=True)
    wsum = v1 + v2
    gate1 = p_top * (v1 / wsum)
    gate2 = p_top * (v2 / wsum)

    oh1 = (row == i1).astype(F32)
    oh2 = (row == i2).astype(F32)
    cnt = jnp.dot(jnp.concatenate([oh1, oh2], axis=0).astype(BF16), tri_ref[...], preferred_element_type=F32)
    run = run_ref[...]
    tot1 = jnp.sum(oh1, axis=1, keepdims=True)
    tot2 = jnp.sum(oh2, axis=1, keepdims=True)
    rank1 = jnp.sum(oh1 * (run + cnt[:ROUTER_ROWS]), axis=0, keepdims=True)
    rank2 = jnp.sum(oh2 * (run + tot1 + cnt[ROUTER_ROWS:]), axis=0, keepdims=True)
    run = run + tot1 + tot2
    run_ref[...] = run
    cnt_ref[...] = jnp.broadcast_to(run, cnt_ref.shape)

    mr = lax.broadcasted_iota(jnp.int32, (META_LANES, tm), 0)
    metat = jnp.where(mr == 0, i1 - EXPERT_ROW0, 0.0)
    metat = jnp.where(mr == 1, i2 - EXPERT_ROW0, metat)
    metat = jnp.where(mr == 2, gate1, metat)
    metat = jnp.where(mr == 3, gate2, metat)
    metat = jnp.where(mr == 4, rank1, metat)
    metat = jnp.where(mr == 5, rank2, metat)
    metat_ref[...] = metat
    meta_ref[...] = metat.T


def _mixout(x, tile0, hf, hb, o, u, seq_lens, conv_w, conv_b, ln_g, ln_b, w_out, g_ffn, w_rg, b_rg, w_re, b_re):
    t = x.shape[0]
    n_tiles = t // TM_MIX
    hpt = TM_MIX // CONV_HALO
    n_halo = u.shape[0] // CONV_HALO
    firsts, lasts, pos = [], [], 0
    for ln in seq_lens:
        assert ln % TM_MIX == 0
        firsts.append(pos // TM_MIX)
        lasts.append((pos + ln) // TM_MIX - 1)
        pos += ln
    assert pos == t
    first = jnp.zeros((n_tiles,), jnp.int32).at[jnp.array(firsts)].set(1)
    last = jnp.zeros((n_tiles,), jnp.int32).at[jnp.array(lasts)].set(1)

    cw = jnp.zeros((32, D_CONV), F32).at[:CONV_WIDTH].set(conv_w.reshape(CONV_WIDTH, D_CONV))
    wr = jnp.zeros((ROUTER_ROWS, D_MODEL), F32)
    wr = wr.at[:N_GROUPS].set(w_rg.T).at[EXPERT_ROW0:EXPERT_ROW0 + N_EXPERTS].set(w_re.T).astype(BF16)
    br = jnp.zeros((ROUTER_ROWS, 1), F32)
    br = br.at[:N_GROUPS, 0].set(b_rg).at[EXPERT_ROW0:EXPERT_ROW0 + N_EXPERTS, 0].set(b_re)
    tri = (lax.broadcasted_iota(jnp.int32, (TM_MIX, TM_MIX), 0)
           < lax.broadcasted_iota(jnp.int32, (TM_MIX, TM_MIX), 1)).astype(BF16)

    tok = lambda i, f, l: (i, 0)
    flat = lambda i, f, l: (tile0 + i, 0)
    fixed = lambda i, f, l: (0, 0)
    row = lambda n: pl.BlockSpec((1, n), fixed)
    half = pl.BlockSpec((TM_MIX, D_MLSTM), flat)
    grid_spec = pltpu.PrefetchScalarGridSpec(
        num_scalar_prefetch=2,
        grid=(n_tiles,),
        in_specs=[pl.BlockSpec((TM_MIX, D_MODEL), tok), half, half, half,
                  pl.BlockSpec((TM_MIX, D_CONV), flat),
                  pl.BlockSpec((CONV_HALO, D_CONV), lambda i, f, l: (jnp.maximum((tile0 + i) * hpt - 1, 0), 0)),
                  pl.BlockSpec((CONV_HALO, D_CONV),
                               lambda i, f, l: (jnp.minimum((tile0 + i + 1) * hpt, n_halo - 1), 0)),
                  pl.BlockSpec((32, D_CONV), fixed), row(D_CONV), row(D_CONV), row(D_CONV),
                  pl.BlockSpec((D_MODEL, D_MODEL), fixed), row(D_MODEL),
                  pl.BlockSpec((ROUTER_ROWS, D_MODEL), fixed), pl.BlockSpec((ROUTER_ROWS, 1), fixed),
                  pl.BlockSpec((TM_MIX, TM_MIX), fixed)],
        out_specs=[pl.BlockSpec((TM_MIX, D_MODEL), tok), pl.BlockSpec((TM_MIX, D_MODEL // 2), tok),
                   pl.BlockSpec((TM_MIX, META_LANES), tok),
                   pl.BlockSpec((META_LANES, TM_MIX), lambda i, f, l: (0, i)),
                   pl.BlockSpec((ROUTER_ROWS, 128), fixed)],
        scratch_shapes=[pltpu.VMEM((TM_MIX + 2 * CONV_HALO, D_CONV), F32),
                        pltpu.VMEM((TM_MIX, D_CONV), BF16),
                        pltpu.VMEM((ROUTER_ROWS, 1), F32)],
    )
    return pl.pallas_call(
        _mixout_kernel,
        grid_spec=grid_spec,
        out_shape=[jax.ShapeDtypeStruct((t, D_MODEL), F32), jax.ShapeDtypeStruct((t, D_MODEL // 2), jnp.int32),
                   jax.ShapeDtypeStruct((t, META_LANES), F32), jax.ShapeDtypeStruct((META_LANES, t), F32),
                   jax.ShapeDtypeStruct((ROUTER_ROWS, 128), F32)],
        compiler_params=_cparams(),
        name="mixout",
    )(first, last, x, hf, hb, o, u, u, u, cw, conv_b.reshape(1, D_CONV), ln_g.reshape(1, D_CONV),
      ln_b.reshape(1, D_CONV), w_out.astype(BF16), g_ffn.reshape(1, D_MODEL), wr, br, tri)


def _sc_mesh():
    return plsc.VectorSubcoreMesh(core_axis_name="c", subcore_axis_name="s")


def _sc_worker_base(per_worker):
    wid = lax.axis_index("s") * SC_CORES + lax.axis_index("c")
    return wid * per_worker


def _dispatch(hn, pos1, pos2, n_slots):
    t, d = hn.shape
    per_worker = t // SC_WORKERS
    assert per_worker * SC_WORKERS == t and per_worker % SC_ROWS == 0

    def body(h_hbm, p1_hbm, p2_hbm, out_hbm, i1_v, i2_v, rows_v, sem1, sem2):
        base0 = _sc_worker_base(per_worker)

        @pl.loop(0, per_worker // SC_ROWS)
        def _(c):
            base = pl.multiple_of(base0 + c * SC_ROWS, SC_ROWS)
            pltpu.sync_copy(p1_hbm.at[pl.ds(base, SC_ROWS)], i1_v)
            pltpu.sync_copy(p2_hbm.at[pl.ds(base, SC_ROWS)], i2_v)
            pltpu.sync_copy(h_hbm.at[pl.ds(base, SC_ROWS)], rows_v)
            c1 = pltpu.async_copy(rows_v, out_hbm.at[i1_v], sem1)
            c2 = pltpu.async_copy(rows_v, out_hbm.at[i2_v], sem2)
            c1.wait()
            c2.wait()

    return pl.kernel(
        body,
        out_type=jax.ShapeDtypeStruct((n_slots, d), hn.dtype),
        mesh=_sc_mesh(),
        scratch_types=[pltpu.VMEM((SC_ROWS,), jnp.int32), pltpu.VMEM((SC_ROWS,), jnp.int32),
                       pltpu.VMEM((SC_ROWS, d), hn.dtype), pltpu.SemaphoreType.DMA, pltpu.SemaphoreType.DMA],
        name="dispatch",
    )(hn, pos1, pos2)


def _collect(ys, pos1, pos2):
    t = pos1.shape[0]
    d = ys.shape[1]
    per_worker = t // SC_WORKERS
    assert per_worker * SC_WORKERS == t and per_worker % SC_ROWS == 0

    def body(ys_hbm, p1_hbm, p2_hbm, y1_hbm, y2_hbm, i_v, rows_v, sem):
        base0 = _sc_worker_base(per_worker)

        @pl.loop(0, per_worker // SC_ROWS)
        def _(c):
            base = pl.multiple_of(base0 + c * SC_ROWS, SC_ROWS)
            for p_hbm, y_hbm in ((p1_hbm, y1_hbm), (p2_hbm, y2_hbm)):
                pltpu.sync_copy(p_hbm.at[pl.ds(base, SC_ROWS)], i_v)
                pltpu.async_copy(ys_hbm.at[i_v], rows_v, sem).wait()
                pltpu.sync_copy(rows_v, y_hbm.at[pl.ds(base, SC_ROWS)])

    out = jax.ShapeDtypeStruct((t, d), ys.dtype)
    return pl.kernel(
        body,
        out_type=(out, out),
        mesh=_sc_mesh(),
        scratch_types=[pltpu.VMEM((SC_ROWS,), jnp.int32), pltpu.VMEM((SC_ROWS, d), ys.dtype),
                       pltpu.SemaphoreType.DMA],
        name="collect",
    )(ys, pos1, pos2)


def _experts_kernel(te_ref, rows_ref, hs_ref, wg_ref, wu_ref, wd_ref, ys_ref, wgb_ref, wub_ref, wdb_ref):
    i = pl.program_id(0)

    @pl.when((i == 0) | (te_ref[i] != te_ref[jnp.maximum(i - 1, 0)]))
    def _():
        wgb_ref[...] = wg_ref[...].astype(BF16)
        wub_ref[...] = wu_ref[...].astype(BF16)
        wdb_ref[...] = wd_ref[...].astype(BF16)

    @pl.when(rows_ref[i] > 0)
    def _():
        x = _unpack_rows(hs_ref[...]).astype(BF16)
        hg = jnp.dot(x, wgb_ref[...], preferred_element_type=F32)
        hu = jnp.dot(x, wub_ref[...], preferred_element_type=F32)
        hid = (hg * jax.nn.sigmoid(hg) * hu).astype(BF16)
        ys_ref[...] = _pack_rows(jnp.dot(hid, wdb_ref[...], preferred_element_type=F32))

    @pl.when(rows_ref[i] == 0)
    def _():
        ys_ref[...] = jnp.zeros_like(ys_ref)


def _experts(hs, tile_expert, tile_rows, w_gate, w_up, w_down):
    n_tiles = tile_expert.shape[0]
    wsel = lambda i, te, rows: (te[i], 0, 0)
    slot = lambda i, te, rows: (i, 0)
    grid_spec = pltpu.PrefetchScalarGridSpec(
        num_scalar_prefetch=2,
        grid=(n_tiles,),
        in_specs=[pl.BlockSpec((TM_EXP, D_MODEL // 2), slot),
                  pl.BlockSpec((None, D_MODEL, D_EXPERT), wsel),
                  pl.BlockSpec((None, D_MODEL, D_EXPERT), wsel),
                  pl.BlockSpec((None, D_EXPERT, D_MODEL), wsel)],
        out_specs=pl.BlockSpec((TM_EXP, D_MODEL // 2), slot),
        scratch_shapes=[pltpu.VMEM((D_MODEL, D_EXPERT), BF16), pltpu.VMEM((D_MODEL, D_EXPERT), BF16),
                        pltpu.VMEM((D_EXPERT, D_MODEL), BF16)],
    )
    return pl.pallas_call(
        _experts_kernel,
        grid_spec=grid_spec,
        out_shape=jax.ShapeDtypeStruct((n_tiles * TM_EXP, D_MODEL // 2), jnp.int32),
        compiler_params=_cparams(),
        name="experts",
    )(tile_expert, tile_rows, hs, w_gate, w_up, w_down)


def _combine_kernel(x1_ref, meta_ref, y1_ref, y2_ref, gfin_ref, out_ref):
    meta = meta_ref[...]
    x2 = x1_ref[...] + meta[:, 2:3] * _unpack_rows(y1_ref[...]) + meta[:, 3:4] * _unpack_rows(y2_ref[...])
    out_ref[...] = _rms(x2, gfin_ref[...])


def _combine(x1, meta, y1, y2, g_final):
    t = x1.shape[0]
    assert t % TM_OUT == 0
    tok = pl.BlockSpec((TM_OUT, D_MODEL), lambda i: (i, 0))
    packed = pl.BlockSpec((TM_OUT, D_MODEL // 2), lambda i: (i, 0))
    return pl.pallas_call(
        _combine_kernel,
        grid=(t // TM_OUT,),
        in_specs=[tok, pl.BlockSpec((TM_OUT, META_LANES), lambda i: (i, 0)), packed, packed,
                  pl.BlockSpec((1, D_MODEL), lambda i: (0, 0))],
        out_specs=tok,
        out_shape=jax.ShapeDtypeStruct((t, D_MODEL), F32),
        compiler_params=_cparams(),
        name="combine",
    )(x1, meta, y1, y2, g_final.reshape(1, D_MODEL))


def _slotpos_kernel(offs_ref, metat_ref, pos_ref):
    m = metat_ref[...]
    eid = m[0:2]
    p = m[4:6]
    for e in range(N_EXPERTS):
        p = p + jnp.where(eid == e, offs_ref[e].astype(F32), 0.0)
    pos_ref[...] = p.astype(jnp.int32)


def _slot_layout(metat, counts_rows, n_tiles):
    t = metat.shape[1]
    counts = counts_rows[EXPERT_ROW0:EXPERT_ROW0 + N_EXPERTS, 0].astype(jnp.int32)
    padded = (counts + TM_EXP - 1) // TM_EXP * TM_EXP
    ends = jnp.cumsum(padded)
    offs = ends - padded
    tp = min(TM_POS, t)
    assert t % tp == 0
    pos = pl.pallas_call(
        _slotpos_kernel,
        grid_spec=pltpu.PrefetchScalarGridSpec(
            num_scalar_prefetch=1,
            grid=(t // tp,),
            in_specs=[pl.BlockSpec((META_LANES, tp), lambda i, offs: (0, i))],
            out_specs=pl.BlockSpec((2, tp), lambda i, offs: (0, i)),
        ),
        out_shape=jax.ShapeDtypeStruct((2, t), jnp.int32),
        compiler_params=_cparams(),
        name="slotpos",
    )(offs, metat)
    tile_start = jnp.arange(n_tiles, dtype=jnp.int32) * TM_EXP
    tile_expert = jnp.minimum(jnp.sum(tile_start[:, None] >= ends[None, :], axis=1), N_EXPERTS - 1)
    tile_expert = tile_expert.astype(jnp.int32)
    tile_rows = jnp.clip(counts[tile_expert] - (tile_start - offs[tile_expert]), 0, TM_EXP)
    tile_rows = jnp.where(tile_start < ends[-1], tile_rows, 0).astype(jnp.int32)
    return pos[0], pos[1], tile_expert, tile_rows


def kernel(x_prompt, x_sample, g_mix, w_in, b_gate, conv_w, conv_b, ln_g, ln_b, w_out, g_ffn,
           w_router_group, b_router_group, w_router_expert, b_router_expert, w_gate, w_up, w_down, g_final):
    assert g_mix.shape[0] == 1, "one layer"
    bp, lp, _ = x_prompt.shape
    bs, ls, _ = x_sample.shape
    seq_lens = [lp] * bp + [ls] * bs
    tp, ts = bp * lp, bs * ls
    xp = x_prompt.reshape(tp, D_MODEL)
    xs = x_sample.reshape(ts, D_MODEL)

    q, v, o, u, kt, gr = _inproj(xp, xs, g_mix[0], w_in[0], b_gate[0])
    hf, hb = _mlstm(q, kt, v, gr, seq_lens)

    outs = []
    for x, tile0, lens in ((xp, 0, [lp] * bp), (xs, tp // TM_MIX, [ls] * bs)):
        tg = x.shape[0]
        x1, hn, meta, metat, counts = _mixout(x, tile0, hf, hb, o, u, lens, conv_w[0], conv_b[0], ln_g[0], ln_b[0],
                                              w_out[0], g_ffn[0], w_router_group[0], b_router_group[0],
                                              w_router_expert[0], b_router_expert[0])
        n_tiles = (2 * tg + N_EXPERTS * (TM_EXP - 1)) // TM_EXP + 1
        pos1, pos2, tile_expert, tile_rows = _slot_layout(metat, counts, n_tiles)
        hs = _dispatch(hn, pos1, pos2, n_tiles * TM_EXP)
        ys = _experts(hs, tile_expert, tile_rows, w_gate[0], w_up[0], w_down[0])
        y1, y2 = _collect(ys, pos1, pos2)
        outs.append(_combine(x1, meta, y1, y2, g_final))
    return outs[0].reshape(bp, lp, D_MODEL), outs[1].reshape(bs, ls, D_MODEL)
```

```python
import functools

import jax
import jax.numpy as jnp
from jax import lax
from jax.experimental import pallas as pl
from jax.experimental.pallas import tpu as pltpu
from jax.experimental.pallas import tpu_sc as plsc

F32 = jnp.float32
BF16 = jnp.bfloat16

D_MODEL = 1024
N_HEADS = 4
HEAD_DIM = 128
D_MLSTM = N_HEADS * HEAD_DIM
D_CONV = D_MODEL - D_MLSTM
CONV_WIDTH = 31
CONV_HALO = 16
N_DIR = 2
N_GROUPS = 4
EXPERTS_PER_GROUP = 4
N_EXPERTS = N_GROUPS * EXPERTS_PER_GROUP
D_EXPERT = 512
EPS = 1e-6
K_SCALE = HEAD_DIM ** -0.5

GATE_ROWS = 16
QROWS = 24
ROUTER_ROWS = 32
EXPERT_ROW0 = N_GROUPS
META_LANES = 8

TM_IN = 1024
CHUNK = 512
GP_CHUNKS = 32
TM_MIX = 512
CONV_ROWS = 256
TM_EXP = 512
TM_OUT = 1024
TM_POS = 8192
VMEM_LIMIT = 48 * 1024 * 1024

SC_CORES = 2
SC_SUBCORES = 16
SC_WORKERS = SC_CORES * SC_SUBCORES
SC_ROWS = 128


def _cparams(n_axes=1):
    return pltpu.CompilerParams(dimension_semantics=("arbitrary",) * n_axes,
                                vmem_limit_bytes=VMEM_LIMIT)


def _nt_dot(a, b):
    return lax.dot_general(a, b, (((1,), (1,)), ((), ())), preferred_element_type=F32)


def _rms(x, g):
    return x * lax.rsqrt(jnp.mean(x * x, axis=-1, keepdims=True) + EPS) * g


def _pack_rows(x):
    n = x.shape[1] // 2
    hi = lax.bitcast_convert_type(x[:, :n].astype(jnp.bfloat16).astype(F32), jnp.int32)
    lo = lax.bitcast_convert_type(x[:, n:].astype(jnp.bfloat16).astype(F32), jnp.int32)
    return hi | lax.shift_right_logical(lo, 16)


def _unpack_rows(p):
    hi = lax.bitcast_convert_type(p & jnp.int32(-65536), F32)
    lo = lax.bitcast_convert_type(lax.shift_left(p, 16), F32)
    return jnp.concatenate([hi, lo], axis=1)


def _two_batch_specs(block, n_first, n_second):
    first = pl.BlockSpec(block, lambda i, *_: (jnp.minimum(i, n_first - 1), 0))
    second = pl.BlockSpec(block, lambda i, *_: (jnp.maximum(i - n_first, 0), 0))
    return first, second


def _inproj_kernel(xp_ref, xs_ref, g_ref, wq_ref, wv_ref, wo_ref, wa_ref, wb_ref, wkg_ref, bg_ref,
                   q_ref, v_ref, o_ref, u_ref, kt_ref, gr_ref, *, n_first):
    x = jnp.where(pl.program_id(0) < n_first, xp_ref[...], xs_ref[...])
    xn = _rms(x, g_ref[...]).astype(BF16)
    q_ref[...] = jnp.dot(xn, wq_ref[...], preferred_element_type=F32).astype(BF16)
    v_ref[...] = jnp.dot(xn, wv_ref[...], preferred_element_type=F32).astype(BF16)
    o_ref[...] = jnp.dot(xn, wo_ref[...], preferred_element_type=F32).astype(BF16)
    a = jnp.dot(xn, wa_ref[...], preferred_element_type=F32)
    b = jnp.dot(xn, wb_ref[...], preferred_element_type=F32)
    u_ref[...] = (a * jax.nn.sigmoid(b)).astype(BF16)
    kg = _nt_dot(wkg_ref[...], xn)
    kt_ref[...] = (kg[:D_MLSTM] * K_SCALE).astype(BF16)
    gr = kg[D_MLSTM:] + bg_ref[...]
    for c in range(gr_ref.shape[0]):
        gr_ref[c] = gr[:, c * CHUNK:(c + 1) * CHUNK]


def _inproj(xp, xs, g_mix, w_in, b_gate):
    tp, ts = xp.shape[0], xs.shape[0]
    t = tp + ts
    assert tp % TM_IN == 0 and ts % TM_IN == 0
    off_k, off_v, off_o, off_g = D_MLSTM, 2 * D_MLSTM, 3 * D_MLSTM, 4 * D_MLSTM
    off_a = off_g + 2 * N_DIR * N_HEADS
    off_b = off_a + D_CONV
    wq = w_in[:, 0:off_k].astype(BF16)
    wkt = w_in[:, off_k:off_v].T.astype(BF16)
    wv = w_in[:, off_v:off_o].astype(BF16)
    wo = w_in[:, off_o:off_g].astype(BF16)
    wa = w_in[:, off_a:off_b].astype(BF16)
    wb = w_in[:, off_b:off_b + D_CONV].astype(BF16)
    wg = w_in[:, off_g:off_a].T.reshape(N_DIR, 2, N_HEADS, D_MODEL)
    wgt = jnp.zeros((N_DIR, 2, GATE_ROWS // 2, D_MODEL), F32).at[:, :, :N_HEADS].set(wg)
    wkg = jnp.concatenate([wkt, wgt.reshape(N_DIR * GATE_ROWS, D_MODEL).astype(BF16)], axis=0)
    bg = jnp.zeros((N_DIR, 2, GATE_ROWS // 2), F32).at[:, :, :N_HEADS].set(
        b_gate.reshape(N_DIR, 2, N_HEADS)).reshape(N_DIR * GATE_ROWS, 1)

    tok = lambda i: (i, 0)
    fixed = lambda i: (0, 0)
    wspec = pl.BlockSpec((D_MODEL, D_MLSTM), fixed)
    xp_spec, xs_spec = _two_batch_specs((TM_IN, D_MODEL), tp // TM_IN, ts // TM_IN)
    cpt = TM_IN // CHUNK
    return pl.pallas_call(
        functools.partial(_inproj_kernel, n_first=tp // TM_IN),
        grid=(t // TM_IN,),
        in_specs=[xp_spec, xs_spec, pl.BlockSpec((1, D_MODEL), fixed),
                  wspec, wspec, wspec, wspec, wspec,
                  pl.BlockSpec((D_MLSTM + N_DIR * GATE_ROWS, D_MODEL), fixed),
                  pl.BlockSpec((N_DIR * GATE_ROWS, 1), fixed)],
        out_specs=[pl.BlockSpec((TM_IN, D_MLSTM), tok)] * 4 + [
            pl.BlockSpec((D_MLSTM, TM_IN), lambda i: (0, i)),
            pl.BlockSpec((cpt, N_DIR * GATE_ROWS, CHUNK), lambda i: (i, 0, 0))],
        out_shape=[jax.ShapeDtypeStruct((t, D_MLSTM), BF16)] * 4 + [
            jax.ShapeDtypeStruct((D_MLSTM, t), BF16),
            jax.ShapeDtypeStruct((t // CHUNK, N_DIR * GATE_ROWS, CHUNK), F32)],
        compiler_params=_cparams(),
        name="inproj",
    )(xp, xs, g_mix.reshape(1, D_MODEL), wq, wv, wo, wa, wb, wkg, bg)


def _log_sigmoid(x):
    return jnp.minimum(x, 0.0) - jnp.log1p(jnp.exp(-jnp.abs(x)))


def _gateprep_kernel(reset_ref, g_ref, rowq_ref, colq_ref, m_ref, *, rev):
    n, _, c = g_ref.shape
    step = pl.program_id(0)
    blk = pl.num_programs(0) - 1 - step if rev else step

    @pl.when(step == 0)
    def _():
        m_ref[...] = jnp.zeros_like(m_ref)

    ig = g_ref[:, 0:8, :]
    lf = _log_sigmoid(g_ref[:, 8:16, :])
    lane = lax.broadcasted_iota(jnp.int32, (n, 8, c), 2)

    def scan(x, op, ident):
        k = 1
        while k < c:
            if rev:
                shifted, valid = pltpu.roll(x, c - k, axis=2), lane < c - k
            else:
                shifted, valid = pltpu.roll(x, k, axis=2), lane >= k
            x = op(x, jnp.where(valid, shifted, ident))
            k *= 2
        return x

    bc = scan(lf, jnp.add, 0.0)
    a = ig - bc
    cm = scan(a, jnp.maximum, -jnp.inf)
    b_tot = jnp.sum(lf, axis=2, keepdims=True)
    a_max = jnp.max(a, axis=2, keepdims=True)

    m = m_ref[...]
    m_in = [None] * n
    for j in (range(n - 1, -1, -1) if rev else range(n)):
        m = jnp.where(reset_ref[blk * n + j] == 1, 0.0, m)
        m_in[j] = m
        m = b_tot[j] + jnp.maximum(m, a_max[j])
    m_ref[...] = m
    m_old = jnp.stack(m_in)

    mx = jnp.maximum(m_old, cm)
    mx_last = jnp.maximum(m_old, a_max)
    rowq_ref[:, 0:8, :] = a
    rowq_ref[:, 8:16, :] = jnp.exp(a - mx_last)
    rowq_ref[:, 16:24, :] = jnp.exp(m_old - mx_last)
    e1 = jnp.exp(m_old - mx)
    fl = jnp.exp(-(mx + bc))
    for j in range(n):
        colq_ref[j] = jnp.concatenate([mx[j], e1[j], fl[j]], axis=0).T


def _gateprep(gr, reset, rev):
    n_chunks = gr.shape[0]
    gp = min(GP_CHUNKS, n_chunks)
    assert n_chunks % gp == 0
    nb = n_chunks // gp
    d = 1 if rev else 0
    bidx = (lambda s: nb - 1 - s) if rev else (lambda s: s)
    grid_spec = pltpu.PrefetchScalarGridSpec(
        num_scalar_prefetch=1,
        grid=(nb,),
        in_specs=[pl.BlockSpec((gp, GATE_ROWS, CHUNK), lambda s, r: (bidx(s), d, 0))],
        out_specs=[pl.BlockSpec((gp, QROWS, CHUNK), lambda s, r: (bidx(s), 0, 0)),
                   pl.BlockSpec((gp, CHUNK, QROWS), lambda s, r: (bidx(s), 0, 0))],
        scratch_shapes=[pltpu.VMEM((8, CHUNK), F32)],
    )
    return pl.pallas_call(
        functools.partial(_gateprep_kernel, rev=rev),
        grid_spec=grid_spec,
        out_shape=[jax.ShapeDtypeStruct((n_chunks, QROWS, CHUNK), F32),
                   jax.ShapeDtypeStruct((n_chunks, CHUNK, QROWS), F32)],
        compiler_params=_cparams(),
        name="gateprep_bwd" if rev else "gateprep_fwd",
    )(reset, gr)


def _mlstm_kernel(rf_ref, rb_ref,
                  qf_ref, ktf_ref, vf_ref, rowf_ref, colf_ref,
                  qb_ref, ktb_ref, vb_ref, rowb_ref, colb_ref,
                  hf_ref, hb_ref, cst_ref):
    c = qf_ref.shape[0]
    step = pl.program_id(0)
    last = pl.num_programs(0) - 1
    row_i = lax.broadcasted_iota(jnp.int32, (c, c), 0)
    col_i = lax.broadcasted_iota(jnp.int32, (c, c), 1)
    ones = jnp.ones((c, HEAD_DIM), BF16)

    dirs = ((rf_ref[step], col_i <= row_i, qf_ref, ktf_ref, vf_ref, rowf_ref, colf_ref, hf_ref),
            (rb_ref[last - step], col_i >= row_i, qb_ref, ktb_ref, vb_ref, rowb_ref, colb_ref, hb_ref))
    for d, (reset, mask, q_ref, kt_ref, v_ref, row_ref, col_ref, h_ref) in enumerate(dirs):
        @pl.when(reset == 1)
        def _():
            cst_ref[d] = jnp.zeros(cst_ref.shape[1:], F32)

        rowq = row_ref[...]
        colq = col_ref[...]
        for h in range(N_HEADS):
            hs = slice(h * HEAD_DIM, (h + 1) * HEAD_DIM)
            qh = q_ref[:, hs]
            kth = kt_ref[hs, :]
            vext = jnp.concatenate([v_ref[:, hs], ones], axis=1)
            s = jnp.dot(qh, kth, preferred_element_type=F32)
            e = jnp.exp(jnp.where(mask, rowq[h:h + 1, :] - colq[:, h:h + 1], -jnp.inf))
            r1 = jnp.dot((s * e).astype(BF16), vext, preferred_element_type=F32)
            cst = cst_ref[d, h]
            r2 = jnp.dot(qh, cst.astype(BF16), preferred_element_type=F32)
            e1 = colq[:, 8 + h:9 + h]
            num = r1[:, :HEAD_DIM] + e1 * r2[:, :HEAD_DIM]
            den = r1[:, HEAD_DIM:] + e1 * r2[:, HEAD_DIM:]
            h_ref[:, hs] = (num / jnp.maximum(jnp.abs(den), colq[:, 16 + h:17 + h])).astype(h_ref.dtype)
            kw = (kth.astype(F32) * rowq[8 + h:9 + h, :]).astype(BF16)
            cst_ref[d, h] = rowq[16 + h:17 + h, 0:1] * cst + jnp.dot(kw, vext, preferred_element_type=F32)


def _mlstm(q, kt, v, gr, seq_lens):
    t = q.shape[0]
    n = t // CHUNK
    starts, ends, pos = [], [], 0
    for ln in seq_lens:
        assert ln % CHUNK == 0
        starts.append(pos // CHUNK)
        ends.append((pos + ln) // CHUNK - 1)
        pos += ln
    reset_f = jnp.zeros((n,), jnp.int32).at[jnp.array(starts)].set(1)
    reset_b = jnp.zeros((n,), jnp.int32).at[jnp.array(ends)].set(1)
    rowf, colf = _gateprep(gr, reset_f, rev=False)
    rowb, colb = _gateprep(gr, reset_b, rev=True)

    def specs(cidx):
        return [pl.BlockSpec((CHUNK, D_MLSTM), lambda s, rf, rb: (cidx(s), 0)),
                pl.BlockSpec((D_MLSTM, CHUNK), lambda s, rf, rb: (0, cidx(s))),
                pl.BlockSpec((CHUNK, D_MLSTM), lambda s, rf, rb: (cidx(s), 0)),
                pl.BlockSpec((None, QROWS, CHUNK), lambda s, rf, rb: (cidx(s), 0, 0)),
                pl.BlockSpec((None, CHUNK, QROWS), lambda s, rf, rb: (cidx(s), 0, 0))]

    fwd = lambda s: s
    bwd = lambda s: n - 1 - s
    grid_spec = pltpu.PrefetchScalarGridSpec(
        num_scalar_prefetch=2,
        grid=(n,),
        in_specs=specs(fwd) + specs(bwd),
        out_specs=[pl.BlockSpec((CHUNK, D_MLSTM), lambda s, rf, rb: (fwd(s), 0)),
                   pl.BlockSpec((CHUNK, D_MLSTM), lambda s, rf, rb: (bwd(s), 0))],
        scratch_shapes=[pltpu.VMEM((N_DIR, N_HEADS, HEAD_DIM, 2 * HEAD_DIM), F32)],
    )
    return pl.pallas_call(
        _mlstm_kernel,
        grid_spec=grid_spec,
        out_shape=[jax.ShapeDtypeStruct((t, D_MLSTM), BF16)] * 2,
        compiler_params=_cparams(),
        name="mlstm",
    )(reset_f, reset_b, q, kt, v, rowf, colf, q, kt, v, rowb, colb)


def _mixout_kernel(first_ref, last_ref,
                   x_ref, hf_ref, hb_ref, o_ref, u_ref, up_ref, un_ref, cw_ref, cb_ref, lng_ref, lnb_ref,
                   wout_ref, gffn_ref, wr_ref, br_ref, tri_ref,
                   x1_ref, hn_ref, meta_ref, metat_ref, cnt_ref,
                   win_ref, y_ref, run_ref):
    i = pl.program_id(0)
    tm = hf_ref.shape[0]

    @pl.when(i == 0)
    def _():
        run_ref[...] = jnp.zeros_like(run_ref)

    h_sum = hf_ref[...].astype(F32) + hb_ref[...].astype(F32)
    ym = (jax.nn.sigmoid(o_ref[...].astype(F32)) * h_sum).astype(BF16)
    x1m = x_ref[...] + jnp.dot(ym, wout_ref[:D_MLSTM, :], preferred_element_type=F32)

    win_ref[0:CONV_HALO, :] = jnp.where(first_ref[i] == 1, 0.0, up_ref[...].astype(F32))
    win_ref[CONV_HALO:CONV_HALO + tm, :] = u_ref[...].astype(F32)
    win_ref[CONV_HALO + tm:, :] = jnp.where(last_ref[i] == 1, 0.0, un_ref[...].astype(F32))

    off0 = CONV_HALO - CONV_WIDTH // 2
    for r0 in range(0, tm, CONV_ROWS):
        tiles = []
        for lt in range(D_CONV // 128):
            ls = slice(lt * 128, (lt + 1) * 128)
            acc = jnp.broadcast_to(cb_ref[:, ls], (CONV_ROWS, 128))
            for s in range(8):
                part = None
                for j in range(CONV_WIDTH):
                    if (off0 + j) % 8 != s:
                        continue
                    base = (off0 + j) // 8 * 8
                    term = win_ref[r0 + base:r0 + base + CONV_ROWS + 8, ls] * cw_ref[j:j + 1, ls]
                    part = term if part is None else part + term
                acc = acc + part[s:s + CONV_ROWS, :]
            tiles.append(acc)
        cv = jnp.concatenate(tiles, axis=1)
        xc = cv - jnp.mean(cv, axis=-1, keepdims=True)
        yc = xc * lax.rsqrt(jnp.mean(xc * xc, axis=-1, keepdims=True) + EPS) * lng_ref[...] + lnb_ref[...]
        y_ref[r0:r0 + CONV_ROWS, :] = (yc * jax.nn.sigmoid(yc)).astype(BF16)

    x1 = x1m + jnp.dot(y_ref[...], wout_ref[D_MLSTM:, :], preferred_element_type=F32)
    x1_ref[...] = x1
    hn = _rms(x1, gffn_ref[...])
    hn_ref[...] = _pack_rows(hn)

    logits = _nt_dot(wr_ref[...], hn.astype(BF16)) + br_ref[...]
    row = lax.broadcasted_iota(jnp.int32, (ROUTER_ROWS, tm), 0).astype(F32)
    neg = -jnp.inf
    no_row = float(ROUTER_ROWS)
    gl = jnp.where(row < N_GROUPS, logits, neg)
    gmax = jnp.max(gl, axis=0, keepdims=True)
    p_top = 1.0 / jnp.sum(jnp.exp(gl - gmax), axis=0, keepdims=True)
    g_idx = jnp.min(jnp.where(gl == gmax, row, no_row), axis=0, keepdims=True)
    lo = EXPERT_ROW0 + EXPERTS_PER_GROUP * g_idx
    in_grp = (row >= lo) & (row < lo + EXPERTS_PER_GROUP)
    el = jnp.where(in_grp, logits, neg)
    ee = jnp.exp(el - jnp.max(el, axis=0, keepdims=True))
    pe = jnp.where(in_grp, ee / jnp.sum(ee, axis=0, keepdims=True), -1.0)
    v1 = jnp.max(pe, axis=0, keepdims=True)
    i1 = jnp.min(jnp.where(pe == v1, row, no_row), axis=0, keepdims=True)
    pe2 = jnp.where(row == i1, -1.0, pe)
    v2 = jnp.max(pe2, axis=0, keepdims=True)
    i2 = jnp.min(jnp.where(pe2 == v2, row, no_row), axis=0, keepdims=True)
    wsum = v1 + v2
    gate1 = p_top * (v1 / wsum)
    gate2 = p_top * (v2 / wsum)

    oh1 = (row == i1).astype(F32)
    oh2 = (row == i2).astype(F32)
    cnt = jnp.dot(jnp.concatenate([oh1, oh2], axis=0).astype(BF16), tri_ref[...], preferred_element_type=F32)
    run = run_ref[...]
    tot1 = jnp.sum(oh1, axis=1, keepdims=True)
    tot2 = jnp.sum(oh2, axis=1, keepdims=True)
    rank1 = jnp.sum(oh1 * (run + cnt[:ROUTER_ROWS]), axis=0, keepdims=True)
    rank2 = jnp.sum(oh2 * (run + tot1 + cnt[ROUTER_ROWS:]), axis=0, keepdims=True)
    run = run + tot1 + tot2
    run_ref[...] = run
    cnt_ref[...] = jnp.broadcast_to(run, cnt_ref.shape)

    mr = lax.broadcasted_iota(jnp.int32, (META_LANES, tm), 0)
    metat = jnp.where(mr == 0, i1 - EXPERT_ROW0, 0.0)
    metat = jnp.where(mr == 1, i2 - EXPERT_ROW0, metat)
    metat = jnp.where(mr == 2, gate1, metat)
    metat = jnp.where(mr == 3, gate2, metat)
    metat = jnp.where(mr == 4, rank1, metat)
    metat = jnp.where(mr == 5, rank2, metat)
    metat_ref[...] = metat
    meta_ref[...] = metat.T


def _mixout(x, tile0, hf, hb, o, u, seq_lens, conv_w, conv_b, ln_g, ln_b, w_out, g_ffn, w_rg, b_rg, w_re, b_re):
    t = x.shape[0]
    n_tiles = t // TM_MIX
    hpt = TM_MIX // CONV_HALO
    n_halo = u.shape[0] // CONV_HALO
    firsts, lasts, pos = [], [], 0
    for ln in seq_lens:
        assert ln % TM_MIX == 0
        firsts.append(pos // TM_MIX)
        lasts.append((pos + ln) // TM_MIX - 1)
        pos += ln
    assert pos == t
    first = jnp.zeros((n_tiles,), jnp.int32).at[jnp.array(firsts)].set(1)
    last = jnp.zeros((n_tiles,), jnp.int32).at[jnp.array(lasts)].set(1)

    cw = jnp.zeros((32, D_CONV), F32).at[:CONV_WIDTH].set(conv_w.reshape(CONV_WIDTH, D_CONV))
    wr = jnp.zeros((ROUTER_ROWS, D_MODEL), F32)
    wr = wr.at[:N_GROUPS].set(w_rg.T).at[EXPERT_ROW0:EXPERT_ROW0 + N_EXPERTS].set(w_re.T).astype(BF16)
    br = jnp.zeros((ROUTER_ROWS, 1), F32)
    br = br.at[:N_GROUPS, 0].set(b_rg).at[EXPERT_ROW0:EXPERT_ROW0 + N_EXPERTS, 0].set(b_re)
    tri = (lax.broadcasted_iota(jnp.int32, (TM_MIX, TM_MIX), 0)
           < lax.broadcasted_iota(jnp.int32, (TM_MIX, TM_MIX), 1)).astype(BF16)

    tok = lambda i, f, l: (i, 0)
    flat = lambda i, f, l: (tile0 + i, 0)
    fixed = lambda i, f, l: (0, 0)
    row = lambda n: pl.BlockSpec((1, n), fixed)
    half = pl.BlockSpec((TM_MIX, D_MLSTM), flat)
    grid_spec = pltpu.PrefetchScalarGridSpec(
        num_scalar_prefetch=2,
        grid=(n_tiles,),
        in_specs=[pl.BlockSpec((TM_MIX, D_MODEL), tok), half, half, half,
                  pl.BlockSpec((TM_MIX, D_CONV), flat),
                  pl.BlockSpec((CONV_HALO, D_CONV), lambda i, f, l: (jnp.maximum((tile0 + i) * hpt - 1, 0), 0)),
                  pl.BlockSpec((CONV_HALO, D_CONV),
                               lambda i, f, l: (jnp.minimum((tile0 + i + 1) * hpt, n_halo - 1), 0)),
                  pl.BlockSpec((32, D_CONV), fixed), row(D_CONV), row(D_CONV), row(D_CONV),
                  pl.BlockSpec((D_MODEL, D_MODEL), fixed), row(D_MODEL),
                  pl.BlockSpec((ROUTER_ROWS, D_MODEL), fixed), pl.BlockSpec((ROUTER_ROWS, 1), fixed),
                  pl.BlockSpec((TM_MIX, TM_MIX), fixed)],
        out_specs=[pl.BlockSpec((TM_MIX, D_MODEL), tok), pl.BlockSpec((TM_MIX, D_MODEL // 2), tok),
                   pl.BlockSpec((TM_MIX, META_LANES), tok),
                   pl.BlockSpec((META_LANES, TM_MIX), lambda i, f, l: (0, i)),
                   pl.BlockSpec((ROUTER_ROWS, 128), fixed)],
        scratch_shapes=[pltpu.VMEM((TM_MIX + 2 * CONV_HALO, D_CONV), F32),
                        pltpu.VMEM((TM_MIX, D_CONV), BF16),
                        pltpu.VMEM((ROUTER_ROWS, 1), F32)],
    )
    return pl.pallas_call(
        _mixout_kernel,
        grid_spec=grid_spec,
        out_shape=[jax.ShapeDtypeStruct((t, D_MODEL), F32), jax.ShapeDtypeStruct((t, D_MODEL // 2), jnp.int32),
                   jax.ShapeDtypeStruct((t, META_LANES), F32), jax.ShapeDtypeStruct((META_LANES, t), F32),
                   jax.ShapeDtypeStruct((ROUTER_ROWS, 128), F32)],
        compiler_params=_cparams(),
        name="mixout",
    )(first, last, x, hf, hb, o, u, u, u, cw, conv_b.reshape(1, D_CONV), ln_g.reshape(1, D_CONV),
      ln_b.reshape(1, D_CONV), w_out.astype(BF16), g_ffn.reshape(1, D_MODEL), wr, br, tri)


def _sc_mesh():
    return plsc.VectorSubcoreMesh(core_axis_name="c", subcore_axis_name="s")


def _sc_worker_base(per_worker):
    wid = lax.axis_index("s") * SC_CORES + lax.axis_index("c")
    return wid * per_worker


def _dispatch(hn, pos1, pos2, n_slots):
    t, d = hn.shape
    per_worker = t // SC_WORKERS
    assert per_worker * SC_WORKERS == t and per_worker % SC_ROWS == 0

    def body(h_hbm, p1_hbm, p2_hbm, out_hbm, i1_v, i2_v, rows_v, sem1, sem2):
        base0 = _sc_worker_base(per_worker)

        @pl.loop(0, per_worker // SC_ROWS)
        def _(c):
            base = pl.multiple_of(base0 + c * SC_ROWS, SC_ROWS)
            pltpu.sync_copy(p1_hbm.at[pl.ds(base, SC_ROWS)], i1_v)
            pltpu.sync_copy(p2_hbm.at[pl.ds(base, SC_ROWS)], i2_v)
            pltpu.sync_copy(h_hbm.at[pl.ds(base, SC_ROWS)], rows_v)
            c1 = pltpu.async_copy(rows_v, out_hbm.at[i1_v], sem1)
            c2 = pltpu.async_copy(rows_v, out_hbm.at[i2_v], sem2)
            c1.wait()
            c2.wait()

    return pl.kernel(
        body,
        out_type=jax.ShapeDtypeStruct((n_slots, d), hn.dtype),
        mesh=_sc_mesh(),
        scratch_types=[pltpu.VMEM((SC_ROWS,), jnp.int32), pltpu.VMEM((SC_ROWS,), jnp.int32),
                       pltpu.VMEM((SC_ROWS, d), hn.dtype), pltpu.SemaphoreType.DMA, pltpu.SemaphoreType.DMA],
        name="dispatch",
    )(hn, pos1, pos2)


def _collect(ys, pos1, pos2):
    t = pos1.shape[0]
    d = ys.shape[1]
    per_worker = t // SC_WORKERS
    assert per_worker * SC_WORKERS == t and per_worker % SC_ROWS == 0

    def body(ys_hbm, p1_hbm, p2_hbm, y1_hbm, y2_hbm, i_v, rows_v, sem):
        base0 = _sc_worker_base(per_worker)

        @pl.loop(0, per_worker // SC_ROWS)
        def _(c):
            base = pl.multiple_of(base0 + c * SC_ROWS, SC_ROWS)
            for p_hbm, y_hbm in ((p1_hbm, y1_hbm), (p2_hbm, y2_hbm)):
                pltpu.sync_copy(p_hbm.at[pl.ds(base, SC_ROWS)], i_v)
                pltpu.async_copy(ys_hbm.at[i_v], rows_v, sem).wait()
                pltpu.sync_copy(rows_v, y_hbm.at[pl.ds(base, SC_ROWS)])

    out = jax.ShapeDtypeStruct((t, d), ys.dtype)
    return pl.kernel(
        body,
        out_type=(out, out),
        mesh=_sc_mesh(),
        scratch_types=[pltpu.VMEM((SC_ROWS,), jnp.int32), pltpu.VMEM((SC_ROWS, d), ys.dtype),
                       pltpu.SemaphoreType.DMA],
        name="collect",
    )(ys, pos1, pos2)


def _experts_kernel(te_ref, rows_ref, hs_ref, wg_ref, wu_ref, wd_ref, ys_ref, wgb_ref, wub_ref, wdb_ref):
    i = pl.program_id(0)

    @pl.when((i == 0) | (te_ref[i] != te_ref[jnp.maximum(i - 1, 0)]))
    def _():
        wgb_ref[...] = wg_ref[...].astype(BF16)
        wub_ref[...] = wu_ref[...].astype(BF16)
        wdb_ref[...] = wd_ref[...].astype(BF16)

    @pl.when(rows_ref[i] > 0)
    def _():
        x = _unpack_rows(hs_ref[...]).astype(BF16)
        hg = jnp.dot(x, wgb_ref[...], preferred_element_type=F32)
        hu = jnp.dot(x, wub_ref[...], preferred_element_type=F32)
        hid = (hg * jax.nn.sigmoid(hg) * hu).astype(BF16)
        ys_ref[...] = _pack_rows(jnp.dot(hid, wdb_ref[...], preferred_element_type=F32))

    @pl.when(rows_ref[i] == 0)
    def _():
        ys_ref[...] = jnp.zeros_like(ys_ref)


def _experts(hs, tile_expert, tile_rows, w_gate, w_up, w_down):
    n_tiles = tile_expert.shape[0]
    wsel = lambda i, te, rows: (te[i], 0, 0)
    slot = lambda i, te, rows: (i, 0)
    grid_spec = pltpu.PrefetchScalarGridSpec(
        num_scalar_prefetch=2,
        grid=(n_tiles,),
        in_specs=[pl.BlockSpec((TM_EXP, D_MODEL // 2), slot),
                  pl.BlockSpec((None, D_MODEL, D_EXPERT), wsel),
                  pl.BlockSpec((None, D_MODEL, D_EXPERT), wsel),
                  pl.BlockSpec((None, D_EXPERT, D_MODEL), wsel)],
        out_specs=pl.BlockSpec((TM_EXP, D_MODEL // 2), slot),
        scratch_shapes=[pltpu.VMEM((D_MODEL, D_EXPERT), BF16), pltpu.VMEM((D_MODEL, D_EXPERT), BF16),
                        pltpu.VMEM((D_EXPERT, D_MODEL), BF16)],
    )
    return pl.pallas_call(
        _experts_kernel,
        grid_spec=grid_spec,
        out_shape=jax.ShapeDtypeStruct((n_tiles * TM_EXP, D_MODEL // 2), jnp.int32),
        compiler_params=_cparams(),
        name="experts",
    )(tile_expert, tile_rows, hs, w_gate, w_up, w_down)


def _combine_kernel(x1_ref, meta_ref, y1_ref, y2_ref, gfin_ref, out_ref):
    meta = meta_ref[...]
    x2 = x1_ref[...] + meta[:, 2:3] * _unpack_rows(y1_ref[...]) + meta[:, 3:4] * _unpack_rows(y2_ref[...])
    out_ref[...] = _rms(x2, gfin_ref[...])


def _combine(x1, meta, y1, y2, g_final):
    t = x1.shape[0]
    assert t % TM_OUT == 0
    tok = pl.BlockSpec((TM_OUT, D_MODEL), lambda i: (i, 0))
    packed = pl.BlockSpec((TM_OUT, D_MODEL // 2), lambda i: (i, 0))
    return pl.pallas_call(
        _combine_kernel,
        grid=(t // TM_OUT,),
        in_specs=[tok, pl.BlockSpec((TM_OUT, META_LANES), lambda i: (i, 0)), packed, packed,
                  pl.BlockSpec((1, D_MODEL), lambda i: (0, 0))],
        out_specs=tok,
        out_shape=jax.ShapeDtypeStruct((t, D_MODEL), F32),
        compiler_params=_cparams(),
        name="combine",
    )(x1, meta, y1, y2, g_final.reshape(1, D_MODEL))


def _layout_kernel(cnt_ref, metat_ref, pos_ref, tiles_ref):
    n_lanes = tiles_ref.shape[1]
    tile_start = lax.broadcasted_iota(jnp.int32, (1, n_lanes), 1).astype(F32) * TM_EXP
    m = metat_ref[...]
    eid = m[0:2]
    p = m[4:6]
    off = jnp.zeros((1, 1), F32)
    t_exp = jnp.full((1, n_lanes), N_EXPERTS - 1.0, F32)
    t_rows = jnp.zeros((1, n_lanes), F32)
    for e in range(N_EXPERTS):
        cnt = cnt_ref[EXPERT_ROW0 + e:EXPERT_ROW0 + e + 1, 0:1]
        padded = jnp.ceil(cnt * (1.0 / TM_EXP)) * TM_EXP
        p = p + jnp.where(eid == e, off, 0.0)
        mine = (tile_start >= off) & (tile_start < off + padded)
        t_exp = jnp.where(mine, float(e), t_exp)
        t_rows = jnp.where(mine, jnp.clip(cnt - (tile_start - off), 0.0, float(TM_EXP)), t_rows)
        off = off + padded
    pos_ref[...] = p.astype(jnp.int32)
    tiles_ref[...] = jnp.concatenate([t_exp, t_rows], axis=0).astype(jnp.int32)


def _slot_layout(metat, counts_rows, n_tiles):
    t = metat.shape[1]
    tp = min(TM_POS, t)
    assert t % tp == 0
    n_lanes = (n_tiles + 127) // 128 * 128
    pos, tiles = pl.pallas_call(
        _layout_kernel,
        grid=(t // tp,),
        in_specs=[pl.BlockSpec(counts_rows.shape, lambda i: (0, 0)),
                  pl.BlockSpec((META_LANES, tp), lambda i: (0, i))],
        out_specs=[pl.BlockSpec((2, tp), lambda i: (0, i)), pl.BlockSpec((2, n_lanes), lambda i: (0, 0))],
        out_shape=[jax.ShapeDtypeStruct((2, t), jnp.int32), jax.ShapeDtypeStruct((2, n_lanes), jnp.int32)],
        compiler_params=_cparams(),
        name="layout",
    )(counts_rows, metat)
    return pos[0], pos[1], tiles[0, :n_tiles], tiles[1, :n_tiles]


def kernel(x_prompt, x_sample, g_mix, w_in, b_gate, conv_w, conv_b, ln_g, ln_b, w_out, g_ffn,
           w_router_group, b_router_group, w_router_expert, b_router_expert, w_gate, w_up, w_down, g_final):
    assert g_mix.shape[0] == 1, "one layer"
    bp, lp, _ = x_prompt.shape
    bs, ls, _ = x_sample.shape
    seq_lens = [lp] * bp + [ls] * bs
    tp, ts = bp * lp, bs * ls
    xp = x_prompt.reshape(tp, D_MODEL)
    xs = x_sample.reshape(ts, D_MODEL)

    q, v, o, u, kt, gr = _inproj(xp, xs, g_mix[0], w_in[0], b_gate[0])
    hf, hb = _mlstm(q, kt, v, gr, seq_lens)

    outs = []
    for x, tile0, lens in ((xp, 0, [lp] * bp), (xs, tp // TM_MIX, [ls] * bs)):
        tg = x.shape[0]
        x1, hn, meta, metat, counts = _mixout(x, tile0, hf, hb, o, u, lens, conv_w[0], conv_b[0], ln_g[0], ln_b[0],
                                              w_out[0], g_ffn[0], w_router_group[0], b_router_group[0],
                                              w_router_expert[0], b_router_expert[0])
        n_tiles = (2 * tg + N_EXPERTS * (TM_EXP - 1)) // TM_EXP + 1
        pos1, pos2, tile_expert, tile_rows = _slot_layout(metat, counts, n_tiles)
        hs = _dispatch(hn, pos1, pos2, n_tiles * TM_EXP)
        ys = _experts(hs, tile_expert, tile_rows, w_gate[0], w_up[0], w_down[0])
        y1, y2 = _collect(ys, pos1, pos2)
        outs.append(_combine(x1, meta, y1, y2, g_final))
    return outs[0].reshape(bp, lp, D_MODEL), outs[1].reshape(bs, ls, D_MODEL)
```

```python
import functools

import jax
import jax.numpy as jnp
from jax import lax
from jax.experimental import pallas as pl
from jax.experimental.pallas import tpu as pltpu
from jax.experimental.pallas import tpu_sc as plsc

F32 = jnp.float32
BF16 = jnp.bfloat16

D_MODEL = 1024
N_HEADS = 4
HEAD_DIM = 128
D_MLSTM = N_HEADS * HEAD_DIM
D_CONV = D_MODEL - D_MLSTM
CONV_WIDTH = 31
CONV_HALO = 16
N_DIR = 2
N_GROUPS = 4
EXPERTS_PER_GROUP = 4
N_EXPERTS = N_GROUPS * EXPERTS_PER_GROUP
D_EXPERT = 512
EPS = 1e-6
K_SCALE = HEAD_DIM ** -0.5

GATE_ROWS = 16
QROWS = 24
ROUTER_ROWS = 32
EXPERT_ROW0 = N_GROUPS
META_LANES = 8

TM_IN = 1024
CHUNK = 512
GP_CHUNKS = 32
TM_MIX = 512
CONV_ROWS = 256
TM_EXP = 512
TM_OUT = 1024
TM_POS = 8192
VMEM_LIMIT = 48 * 1024 * 1024

SC_CORES = 2
SC_SUBCORES = 16
SC_WORKERS = SC_CORES * SC_SUBCORES
SC_ROWS = 128


def _cparams(n_axes=1):
    return pltpu.CompilerParams(dimension_semantics=("arbitrary",) * n_axes,
                                vmem_limit_bytes=VMEM_LIMIT)


def _nt_dot(a, b):
    return lax.dot_general(a, b, (((1,), (1,)), ((), ())), preferred_element_type=F32)


def _rms(x, g):
    return x * lax.rsqrt(jnp.mean(x * x, axis=-1, keepdims=True) + EPS) * g


def _pack_rows(x):
    n = x.shape[1] // 2
    hi = lax.bitcast_convert_type(x[:, :n].astype(jnp.bfloat16).astype(F32), jnp.int32)
    lo = lax.bitcast_convert_type(x[:, n:].astype(jnp.bfloat16).astype(F32), jnp.int32)
    return hi | lax.shift_right_logical(lo, 16)


def _unpack_rows(p):
    hi = lax.bitcast_convert_type(p & jnp.int32(-65536), F32)
    lo = lax.bitcast_convert_type(lax.shift_left(p, 16), F32)
    return jnp.concatenate([hi, lo], axis=1)


def _two_batch_specs(block, n_first, n_second):
    first = pl.BlockSpec(block, lambda i, *_: (jnp.minimum(i, n_first - 1), 0))
    second = pl.BlockSpec(block, lambda i, *_: (jnp.maximum(i - n_first, 0), 0))
    return first, second


def _inproj_kernel(xp_ref, xs_ref, g_ref, wq_ref, wv_ref, wo_ref, wa_ref, wb_ref, wkg_ref, bg_ref,
                   q_ref, v_ref, o_ref, u_ref, kt_ref, gr_ref, *, n_first):
    x = jnp.where(pl.program_id(0) < n_first, xp_ref[...], xs_ref[...])
    xn = _rms(x, g_ref[...]).astype(BF16)
    q_ref[...] = jnp.dot(xn, wq_ref[...], preferred_element_type=F32).astype(BF16)
    v_ref[...] = jnp.dot(xn, wv_ref[...], preferred_element_type=F32).astype(BF16)
    o_ref[...] = jax.nn.sigmoid(jnp.dot(xn, wo_ref[...], preferred_element_type=F32)).astype(BF16)
    a = jnp.dot(xn, wa_ref[...], preferred_element_type=F32)
    b = jnp.dot(xn, wb_ref[...], preferred_element_type=F32)
    u_ref[...] = (a * jax.nn.sigmoid(b)).astype(BF16)
    kg = _nt_dot(wkg_ref[...], xn)
    kt_ref[...] = (kg[:D_MLSTM] * K_SCALE).astype(BF16)
    gr = kg[D_MLSTM:] + bg_ref[...]
    for c in range(gr_ref.shape[0]):
        gr_ref[c] = gr[:, c * CHUNK:(c + 1) * CHUNK]


def _inproj(xp, xs, g_mix, w_in, b_gate):
    tp, ts = xp.shape[0], xs.shape[0]
    t = tp + ts
    assert tp % TM_IN == 0 and ts % TM_IN == 0
    off_k, off_v, off_o, off_g = D_MLSTM, 2 * D_MLSTM, 3 * D_MLSTM, 4 * D_MLSTM
    off_a = off_g + 2 * N_DIR * N_HEADS
    off_b = off_a + D_CONV
    wq = w_in[:, 0:off_k].astype(BF16)
    wkt = w_in[:, off_k:off_v].T.astype(BF16)
    wv = w_in[:, off_v:off_o].astype(BF16)
    wo = w_in[:, off_o:off_g].astype(BF16)
    wa = w_in[:, off_a:off_b].astype(BF16)
    wb = w_in[:, off_b:off_b + D_CONV].astype(BF16)
    wg = w_in[:, off_g:off_a].T.reshape(N_DIR, 2, N_HEADS, D_MODEL)
    wgt = jnp.zeros((N_DIR, 2, GATE_ROWS // 2, D_MODEL), F32).at[:, :, :N_HEADS].set(wg)
    wkg = jnp.concatenate([wkt, wgt.reshape(N_DIR * GATE_ROWS, D_MODEL).astype(BF16)], axis=0)
    bg = jnp.zeros((N_DIR, 2, GATE_ROWS // 2), F32).at[:, :, :N_HEADS].set(
        b_gate.reshape(N_DIR, 2, N_HEADS)).reshape(N_DIR * GATE_ROWS, 1)

    tok = lambda i: (i, 0)
    fixed = lambda i: (0, 0)
    wspec = pl.BlockSpec((D_MODEL, D_MLSTM), fixed)
    xp_spec, xs_spec = _two_batch_specs((TM_IN, D_MODEL), tp // TM_IN, ts // TM_IN)
    cpt = TM_IN // CHUNK
    return pl.pallas_call(
        functools.partial(_inproj_kernel, n_first=tp // TM_IN),
        grid=(t // TM_IN,),
        in_specs=[xp_spec, xs_spec, pl.BlockSpec((1, D_MODEL), fixed),
                  wspec, wspec, wspec, wspec, wspec,
                  pl.BlockSpec((D_MLSTM + N_DIR * GATE_ROWS, D_MODEL), fixed),
                  pl.BlockSpec((N_DIR * GATE_ROWS, 1), fixed)],
        out_specs=[pl.BlockSpec((TM_IN, D_MLSTM), tok)] * 4 + [
            pl.BlockSpec((D_MLSTM, TM_IN), lambda i: (0, i)),
            pl.BlockSpec((cpt, N_DIR * GATE_ROWS, CHUNK), lambda i: (i, 0, 0))],
        out_shape=[jax.ShapeDtypeStruct((t, D_MLSTM), BF16)] * 4 + [
            jax.ShapeDtypeStruct((D_MLSTM, t), BF16),
            jax.ShapeDtypeStruct((t // CHUNK, N_DIR * GATE_ROWS, CHUNK), F32)],
        compiler_params=_cparams(),
        name="inproj",
    )(xp, xs, g_mix.reshape(1, D_MODEL), wq, wv, wo, wa, wb, wkg, bg)


def _log_sigmoid(x):
    return jnp.minimum(x, 0.0) - jnp.log1p(jnp.exp(-jnp.abs(x)))


def _gateprep_kernel(reset_ref, g_ref, rowq_ref, colq_ref, m_ref, *, rev):
    n, _, c = g_ref.shape
    step = pl.program_id(0)
    blk = pl.num_programs(0) - 1 - step if rev else step

    @pl.when(step == 0)
    def _():
        m_ref[...] = jnp.zeros_like(m_ref)

    ig = g_ref[:, 0:8, :]
    lf = _log_sigmoid(g_ref[:, 8:16, :])
    lane = lax.broadcasted_iota(jnp.int32, (n, 8, c), 2)

    def scan(x, op, ident):
        k = 1
        while k < c:
            if rev:
                shifted, valid = pltpu.roll(x, c - k, axis=2), lane < c - k
            else:
                shifted, valid = pltpu.roll(x, k, axis=2), lane >= k
            x = op(x, jnp.where(valid, shifted, ident))
            k *= 2
        return x

    bc = scan(lf, jnp.add, 0.0)
    a = ig - bc
    cm = scan(a, jnp.maximum, -jnp.inf)
    b_tot = jnp.sum(lf, axis=2, keepdims=True)
    a_max = jnp.max(a, axis=2, keepdims=True)

    m = m_ref[...]
    m_in = [None] * n
    for j in (range(n - 1, -1, -1) if rev else range(n)):
        m = jnp.where(reset_ref[blk * n + j] == 1, 0.0, m)
        m_in[j] = m
        m = b_tot[j] + jnp.maximum(m, a_max[j])
    m_ref[...] = m
    m_old = jnp.stack(m_in)

    mx = jnp.maximum(m_old, cm)
    mx_last = jnp.maximum(m_old, a_max)
    rowq_ref[:, 0:8, :] = a
    rowq_ref[:, 8:16, :] = jnp.exp(a - mx_last)
    rowq_ref[:, 16:24, :] = jnp.exp(m_old - mx_last)
    e1 = jnp.exp(m_old - mx)
    fl = jnp.exp(-(mx + bc))
    for j in range(n):
        colq_ref[j] = jnp.concatenate([mx[j], e1[j], fl[j]], axis=0).T


def _gateprep(gr, reset, rev):
    n_chunks = gr.shape[0]
    gp = min(GP_CHUNKS, n_chunks)
    assert n_chunks % gp == 0
    nb = n_chunks // gp
    d = 1 if rev else 0
    bidx = (lambda s: nb - 1 - s) if rev else (lambda s: s)
    grid_spec = pltpu.PrefetchScalarGridSpec(
        num_scalar_prefetch=1,
        grid=(nb,),
        in_specs=[pl.BlockSpec((gp, GATE_ROWS, CHUNK), lambda s, r: (bidx(s), d, 0))],
        out_specs=[pl.BlockSpec((gp, QROWS, CHUNK), lambda s, r: (bidx(s), 0, 0)),
                   pl.BlockSpec((gp, CHUNK, QROWS), lambda s, r: (bidx(s), 0, 0))],
        scratch_shapes=[pltpu.VMEM((8, CHUNK), F32)],
    )
    return pl.pallas_call(
        functools.partial(_gateprep_kernel, rev=rev),
        grid_spec=grid_spec,
        out_shape=[jax.ShapeDtypeStruct((n_chunks, QROWS, CHUNK), F32),
                   jax.ShapeDtypeStruct((n_chunks, CHUNK, QROWS), F32)],
        compiler_params=_cparams(),
        name="gateprep_bwd" if rev else "gateprep_fwd",
    )(reset, gr)


def _mlstm_kernel(rf_ref, rb_ref,
                  qf_ref, ktf_ref, vf_ref, rowf_ref, colf_ref,
                  qb_ref, ktb_ref, vb_ref, rowb_ref, colb_ref,
                  hf_ref, hb_ref, cst_ref):
    c = qf_ref.shape[0]
    step = pl.program_id(0)
    last = pl.num_programs(0) - 1
    row_i = lax.broadcasted_iota(jnp.int32, (c, c), 0)
    col_i = lax.broadcasted_iota(jnp.int32, (c, c), 1)
    ones = jnp.ones((c, HEAD_DIM), BF16)

    dirs = ((rf_ref[step], col_i <= row_i, qf_ref, ktf_ref, vf_ref, rowf_ref, colf_ref, hf_ref),
            (rb_ref[last - step], col_i >= row_i, qb_ref, ktb_ref, vb_ref, rowb_ref, colb_ref, hb_ref))
    for d, (reset, mask, q_ref, kt_ref, v_ref, row_ref, col_ref, h_ref) in enumerate(dirs):
        @pl.when(reset == 1)
        def _():
            cst_ref[d] = jnp.zeros(cst_ref.shape[1:], F32)

        rowq = row_ref[...]
        colq = col_ref[...]
        for h in range(N_HEADS):
            hs = slice(h * HEAD_DIM, (h + 1) * HEAD_DIM)
            qh = q_ref[:, hs]
            kth = kt_ref[hs, :]
            vext = jnp.concatenate([v_ref[:, hs], ones], axis=1)
            s = jnp.dot(qh, kth, preferred_element_type=F32)
            e = jnp.exp(jnp.where(mask, rowq[h:h + 1, :] - colq[:, h:h + 1], -jnp.inf))
            r1 = jnp.dot((s * e).astype(BF16), vext, preferred_element_type=F32)
            cst = cst_ref[d, h]
            r2 = jnp.dot(qh, cst.astype(BF16), preferred_element_type=F32)
            e1 = colq[:, 8 + h:9 + h]
            num = r1[:, :HEAD_DIM] + e1 * r2[:, :HEAD_DIM]
            den = r1[:, HEAD_DIM:] + e1 * r2[:, HEAD_DIM:]
            h_ref[:, hs] = (num / jnp.maximum(jnp.abs(den), colq[:, 16 + h:17 + h])).astype(h_ref.dtype)
            kw = (kth.astype(F32) * rowq[8 + h:9 + h, :]).astype(BF16)
            cst_ref[d, h] = rowq[16 + h:17 + h, 0:1] * cst + jnp.dot(kw, vext, preferred_element_type=F32)


def _mlstm(q, kt, v, gr, seq_lens):
    t = q.shape[0]
    n = t // CHUNK
    starts, ends, pos = [], [], 0
    for ln in seq_lens:
        assert ln % CHUNK == 0
        starts.append(pos // CHUNK)
        ends.append((pos + ln) // CHUNK - 1)
        pos += ln
    reset_f = jnp.zeros((n,), jnp.int32).at[jnp.array(starts)].set(1)
    reset_b = jnp.zeros((n,), jnp.int32).at[jnp.array(ends)].set(1)
    rowf, colf = _gateprep(gr, reset_f, rev=False)
    rowb, colb = _gateprep(gr, reset_b, rev=True)

    def specs(cidx):
        return [pl.BlockSpec((CHUNK, D_MLSTM), lambda s, rf, rb: (cidx(s), 0)),
                pl.BlockSpec((D_MLSTM, CHUNK), lambda s, rf, rb: (0, cidx(s))),
                pl.BlockSpec((CHUNK, D_MLSTM), lambda s, rf, rb: (cidx(s), 0)),
                pl.BlockSpec((None, QROWS, CHUNK), lambda s, rf, rb: (cidx(s), 0, 0)),
                pl.BlockSpec((None, CHUNK, QROWS), lambda s, rf, rb: (cidx(s), 0, 0))]

    fwd = lambda s: s
    bwd = lambda s: n - 1 - s
    grid_spec = pltpu.PrefetchScalarGridSpec(
        num_scalar_prefetch=2,
        grid=(n,),
        in_specs=specs(fwd) + specs(bwd),
        out_specs=[pl.BlockSpec((CHUNK, D_MLSTM), lambda s, rf, rb: (fwd(s), 0)),
                   pl.BlockSpec((CHUNK, D_MLSTM), lambda s, rf, rb: (bwd(s), 0))],
        scratch_shapes=[pltpu.VMEM((N_DIR, N_HEADS, HEAD_DIM, 2 * HEAD_DIM), F32)],
    )
    return pl.pallas_call(
        _mlstm_kernel,
        grid_spec=grid_spec,
        out_shape=[jax.ShapeDtypeStruct((t, D_MLSTM), BF16)] * 2,
        compiler_params=_cparams(),
        name="mlstm",
    )(reset_f, reset_b, q, kt, v, rowf, colf, q, kt, v, rowb, colb)


def _mixout_kernel(first_ref, last_ref,
                   x_ref, hf_ref, hb_ref, o_ref, u_ref, up_ref, un_ref, cw_ref, cb_ref, lng_ref, lnb_ref,
                   wout_ref, gffn_ref, wr_ref, br_ref, tri_ref,
                   x1_ref, hn_ref, meta_ref, metat_ref, cnt_ref,
                   win_ref, y_ref, run_ref):
    i = pl.program_id(0)
    tm = hf_ref.shape[0]

    @pl.when(i == 0)
    def _():
        run_ref[...] = jnp.zeros_like(run_ref)

    h_sum = hf_ref[...].astype(F32) + hb_ref[...].astype(F32)
    ym = (o_ref[...].astype(F32) * h_sum).astype(BF16)
    x1m = x_ref[...] + jnp.dot(ym, wout_ref[:D_MLSTM, :], preferred_element_type=F32)

    win_ref[0:CONV_HALO, :] = jnp.where(first_ref[i] == 1, 0.0, up_ref[...].astype(F32))
    win_ref[CONV_HALO:CONV_HALO + tm, :] = u_ref[...].astype(F32)
    win_ref[CONV_HALO + tm:, :] = jnp.where(last_ref[i] == 1, 0.0, un_ref[...].astype(F32))

    off0 = CONV_HALO - CONV_WIDTH // 2
    for r0 in range(0, tm, CONV_ROWS):
        tiles = []
        for lt in range(D_CONV // 128):
            ls = slice(lt * 128, (lt + 1) * 128)
            acc = jnp.broadcast_to(cb_ref[:, ls], (CONV_ROWS, 128))
            for s in range(8):
                part = None
                for j in range(CONV_WIDTH):
                    if (off0 + j) % 8 != s:
                        continue
                    base = (off0 + j) // 8 * 8
                    term = win_ref[r0 + base:r0 + base + CONV_ROWS + 8, ls] * cw_ref[j:j + 1, ls]
                    part = term if part is None else part + term
                acc = acc + part[s:s + CONV_ROWS, :]
            tiles.append(acc)
        cv = jnp.concatenate(tiles, axis=1)
        xc = cv - jnp.mean(cv, axis=-1, keepdims=True)
        yc = xc * lax.rsqrt(jnp.mean(xc * xc, axis=-1, keepdims=True) + EPS) * lng_ref[...] + lnb_ref[...]
        y_ref[r0:r0 + CONV_ROWS, :] = (yc * jax.nn.sigmoid(yc)).astype(BF16)

    x1 = x1m + jnp.dot(y_ref[...], wout_ref[D_MLSTM:, :], preferred_element_type=F32)
    x1_ref[...] = x1
    hn = _rms(x1, gffn_ref[...])
    hn_ref[...] = _pack_rows(hn)

    logits = _nt_dot(wr_ref[...], hn.astype(BF16)) + br_ref[...]
    row = lax.broadcasted_iota(jnp.int32, (ROUTER_ROWS, tm), 0).astype(F32)
    neg = -jnp.inf
    no_row = float(ROUTER_ROWS)
    gl = jnp.where(row < N_GROUPS, logits, neg)
    gmax = jnp.max(gl, axis=0, keepdims=True)
    p_top = 1.0 / jnp.sum(jnp.exp(gl - gmax), axis=0, keepdims=True)
    g_idx = jnp.min(jnp.where(gl == gmax, row, no_row), axis=0, keepdims=True)
    lo = EXPERT_ROW0 + EXPERTS_PER_GROUP * g_idx
    in_grp = (row >= lo) & (row < lo + EXPERTS_PER_GROUP)
    el = jnp.where(in_grp, logits, neg)
    ee = jnp.exp(el - jnp.max(el, axis=0, keepdims=True))
    pe = jnp.where(in_grp, ee / jnp.sum(ee, axis=0, keepdims=True), -1.0)
    v1 = jnp.max(pe, axis=0, keepdims=True)
    i1 = jnp.min(jnp.where(pe == v1, row, no_row), axis=0, keepdims=True)
    pe2 = jnp.where(row == i1, -1.0, pe)
    v2 = jnp.max(pe2, axis=0, keepdims=True)
    i2 = jnp.min(jnp.where(pe2 == v2, row, no_row), axis=0, keepdims=True)
    wsum = v1 + v2
    gate1 = p_top * (v1 / wsum)
    gate2 = p_top * (v2 / wsum)

    oh1 = (row == i1).astype(F32)
    oh2 = (row == i2).astype(F32)
    cnt = jnp.dot(jnp.concatenate([oh1, oh2], axis=0).astype(BF16), tri_ref[...], preferred_element_type=F32)
    run = run_ref[...]
    tot1 = jnp.sum(oh1, axis=1, keepdims=True)
    tot2 = jnp.sum(oh2, axis=1, keepdims=True)
    rank1 = jnp.sum(oh1 * (run + cnt[:ROUTER_ROWS]), axis=0, keepdims=True)
    rank2 = jnp.sum(oh2 * (run + tot1 + cnt[ROUTER_ROWS:]), axis=0, keepdims=True)
    run = run + tot1 + tot2
    run_ref[...] = run
    cnt_ref[...] = jnp.broadcast_to(run, cnt_ref.shape)

    mr = lax.broadcasted_iota(jnp.int32, (META_LANES, tm), 0)
    metat = jnp.where(mr == 0, i1 - EXPERT_ROW0, 0.0)
    metat = jnp.where(mr == 1, i2 - EXPERT_ROW0, metat)
    metat = jnp.where(mr == 2, gate1, metat)
    metat = jnp.where(mr == 3, gate2, metat)
    metat = jnp.where(mr == 4, rank1, metat)
    metat = jnp.where(mr == 5, rank2, metat)
    metat_ref[...] = metat
    meta_ref[...] = metat.T


def _mixout(x, tile0, hf, hb, o, u, seq_lens, conv_w, conv_b, ln_g, ln_b, w_out, g_ffn, w_rg, b_rg, w_re, b_re):
    t = x.shape[0]
    n_tiles = t // TM_MIX
    hpt = TM_MIX // CONV_HALO
    n_halo = u.shape[0] // CONV_HALO
    firsts, lasts, pos = [], [], 0
    for ln in seq_lens:
        assert ln % TM_MIX == 0
        firsts.append(pos // TM_MIX)
        lasts.append((pos + ln) // TM_MIX - 1)
        pos += ln
    assert pos == t
    first = jnp.zeros((n_tiles,), jnp.int32).at[jnp.array(firsts)].set(1)
    last = jnp.zeros((n_tiles,), jnp.int32).at[jnp.array(lasts)].set(1)

    cw = jnp.zeros((32, D_CONV), F32).at[:CONV_WIDTH].set(conv_w.reshape(CONV_WIDTH, D_CONV))
    wr = jnp.zeros((ROUTER_ROWS, D_MODEL), F32)
    wr = wr.at[:N_GROUPS].set(w_rg.T).at[EXPERT_ROW0:EXPERT_ROW0 + N_EXPERTS].set(w_re.T).astype(BF16)
    br = jnp.zeros((ROUTER_ROWS, 1), F32)
    br = br.at[:N_GROUPS, 0].set(b_rg).at[EXPERT_ROW0:EXPERT_ROW0 + N_EXPERTS, 0].set(b_re)
    tri = (lax.broadcasted_iota(jnp.int32, (TM_MIX, TM_MIX), 0)
           < lax.broadcasted_iota(jnp.int32, (TM_MIX, TM_MIX), 1)).astype(BF16)

    tok = lambda i, f, l: (i, 0)
    flat = lambda i, f, l: (tile0 + i, 0)
    fixed = lambda i, f, l: (0, 0)
    row = lambda n: pl.BlockSpec((1, n), fixed)
    half = pl.BlockSpec((TM_MIX, D_MLSTM), flat)
    grid_spec = pltpu.PrefetchScalarGridSpec(
        num_scalar_prefetch=2,
        grid=(n_tiles,),
        in_specs=[pl.BlockSpec((TM_MIX, D_MODEL), tok), half, half, half,
                  pl.BlockSpec((TM_MIX, D_CONV), flat),
                  pl.BlockSpec((CONV_HALO, D_CONV), lambda i, f, l: (jnp.maximum((tile0 + i) * hpt - 1, 0), 0)),
                  pl.BlockSpec((CONV_HALO, D_CONV),
                               lambda i, f, l: (jnp.minimum((tile0 + i + 1) * hpt, n_halo - 1), 0)),
                  pl.BlockSpec((32, D_CONV), fixed), row(D_CONV), row(D_CONV), row(D_CONV),
                  pl.BlockSpec((D_MODEL, D_MODEL), fixed), row(D_MODEL),
                  pl.BlockSpec((ROUTER_ROWS, D_MODEL), fixed), pl.BlockSpec((ROUTER_ROWS, 1), fixed),
                  pl.BlockSpec((TM_MIX, TM_MIX), fixed)],
        out_specs=[pl.BlockSpec((TM_MIX, D_MODEL), tok), pl.BlockSpec((TM_MIX, D_MODEL // 2), tok),
                   pl.BlockSpec((TM_MIX, META_LANES), tok),
                   pl.BlockSpec((META_LANES, TM_MIX), lambda i, f, l: (0, i)),
                   pl.BlockSpec((ROUTER_ROWS, 128), fixed)],
        scratch_shapes=[pltpu.VMEM((TM_MIX + 2 * CONV_HALO, D_CONV), F32),
                        pltpu.VMEM((TM_MIX, D_CONV), BF16),
                        pltpu.VMEM((ROUTER_ROWS, 1), F32)],
    )
    return pl.pallas_call(
        _mixout_kernel,
        grid_spec=grid_spec,
        out_shape=[jax.ShapeDtypeStruct((t, D_MODEL), F32), jax.ShapeDtypeStruct((t, D_MODEL // 2), jnp.int32),
                   jax.ShapeDtypeStruct((t, META_LANES), F32), jax.ShapeDtypeStruct((META_LANES, t), F32),
                   jax.ShapeDtypeStruct((ROUTER_ROWS, 128), F32)],
        compiler_params=_cparams(),
        name="mixout",
    )(first, last, x, hf, hb, o, u, u, u, cw, conv_b.reshape(1, D_CONV), ln_g.reshape(1, D_CONV),
      ln_b.reshape(1, D_CONV), w_out.astype(BF16), g_ffn.reshape(1, D_MODEL), wr, br, tri)


def _sc_mesh():
    return plsc.VectorSubcoreMesh(core_axis_name="c", subcore_axis_name="s")


def _sc_worker_base(per_worker):
    wid = lax.axis_index("s") * SC_CORES + lax.axis_index("c")
    return wid * per_worker


def _dispatch(hn, pos1, pos2, n_slots):
    t, d = hn.shape
    per_worker = t // SC_WORKERS
    assert per_worker * SC_WORKERS == t and per_worker % SC_ROWS == 0

    def body(h_hbm, p1_hbm, p2_hbm, out_hbm, i1_v, i2_v, rows_v, sem1, sem2):
        base0 = _sc_worker_base(per_worker)

        @pl.loop(0, per_worker // SC_ROWS)
        def _(c):
            base = pl.multiple_of(base0 + c * SC_ROWS, SC_ROWS)
            pltpu.sync_copy(p1_hbm.at[pl.ds(base, SC_ROWS)], i1_v)
            pltpu.sync_copy(p2_hbm.at[pl.ds(base, SC_ROWS)], i2_v)
            pltpu.sync_copy(h_hbm.at[pl.ds(base, SC_ROWS)], rows_v)
            c1 = pltpu.async_copy(rows_v, out_hbm.at[i1_v], sem1)
            c2 = pltpu.async_copy(rows_v, out_hbm.at[i2_v], sem2)
            c1.wait()
            c2.wait()

    return pl.kernel(
        body,
        out_type=jax.ShapeDtypeStruct((n_slots, d), hn.dtype),
        mesh=_sc_mesh(),
        scratch_types=[pltpu.VMEM((SC_ROWS,), jnp.int32), pltpu.VMEM((SC_ROWS,), jnp.int32),
                       pltpu.VMEM((SC_ROWS, d), hn.dtype), pltpu.SemaphoreType.DMA, pltpu.SemaphoreType.DMA],
        name="dispatch",
    )(hn, pos1, pos2)


def _collect(ys, pos1, pos2):
    t = pos1.shape[0]
    d = ys.shape[1]
    per_worker = t // SC_WORKERS
    assert per_worker * SC_WORKERS == t and per_worker % SC_ROWS == 0

    def body(ys_hbm, p1_hbm, p2_hbm, y1_hbm, y2_hbm, i_v, rows_v, sem):
        base0 = _sc_worker_base(per_worker)

        @pl.loop(0, per_worker // SC_ROWS)
        def _(c):
            base = pl.multiple_of(base0 + c * SC_ROWS, SC_ROWS)
            for p_hbm, y_hbm in ((p1_hbm, y1_hbm), (p2_hbm, y2_hbm)):
                pltpu.sync_copy(p_hbm.at[pl.ds(base, SC_ROWS)], i_v)
                pltpu.async_copy(ys_hbm.at[i_v], rows_v, sem).wait()
                pltpu.sync_copy(rows_v, y_hbm.at[pl.ds(base, SC_ROWS)])

    out = jax.ShapeDtypeStruct((t, d), ys.dtype)
    return pl.kernel(
        body,
        out_type=(out, out),
        mesh=_sc_mesh(),
        scratch_types=[pltpu.VMEM((SC_ROWS,), jnp.int32), pltpu.VMEM((SC_ROWS, d), ys.dtype),
                       pltpu.SemaphoreType.DMA],
        name="collect",
    )(ys, pos1, pos2)


def _experts_kernel(te_ref, rows_ref, hs_ref, wg_ref, wu_ref, wd_ref, ys_ref, wgb_ref, wub_ref, wdb_ref):
    i = pl.program_id(0)

    @pl.when((i == 0) | (te_ref[i] != te_ref[jnp.maximum(i - 1, 0)]))
    def _():
        wgb_ref[...] = wg_ref[...].astype(BF16)
        wub_ref[...] = wu_ref[...].astype(BF16)
        wdb_ref[...] = wd_ref[...].astype(BF16)

    @pl.when(rows_ref[i] > 0)
    def _():
        x = _unpack_rows(hs_ref[...]).astype(BF16)
        hg = jnp.dot(x, wgb_ref[...], preferred_element_type=F32)
        hu = jnp.dot(x, wub_ref[...], preferred_element_type=F32)
        hid = (hg * jax.nn.sigmoid(hg) * hu).astype(BF16)
        ys_ref[...] = _pack_rows(jnp.dot(hid, wdb_ref[...], preferred_element_type=F32))

    @pl.when(rows_ref[i] == 0)
    def _():
        ys_ref[...] = jnp.zeros_like(ys_ref)


def _experts(hs, tile_expert, tile_rows, w_gate, w_up, w_down):
    n_tiles = tile_expert.shape[0]
    wsel = lambda i, te, rows: (te[i], 0, 0)
    slot = lambda i, te, rows: (i, 0)
    grid_spec = pltpu.PrefetchScalarGridSpec(
        num_scalar_prefetch=2,
        grid=(n_tiles,),
        in_specs=[pl.BlockSpec((TM_EXP, D_MODEL // 2), slot),
                  pl.BlockSpec((None, D_MODEL, D_EXPERT), wsel),
                  pl.BlockSpec((None, D_MODEL, D_EXPERT), wsel),
                  pl.BlockSpec((None, D_EXPERT, D_MODEL), wsel)],
        out_specs=pl.BlockSpec((TM_EXP, D_MODEL // 2), slot),
        scratch_shapes=[pltpu.VMEM((D_MODEL, D_EXPERT), BF16), pltpu.VMEM((D_MODEL, D_EXPERT), BF16),
                        pltpu.VMEM((D_EXPERT, D_MODEL), BF16)],
    )
    return pl.pallas_call(
        _experts_kernel,
        grid_spec=grid_spec,
        out_shape=jax.ShapeDtypeStruct((n_tiles * TM_EXP, D_MODEL // 2), jnp.int32),
        compiler_params=_cparams(),
        name="experts",
    )(tile_expert, tile_rows, hs, w_gate, w_up, w_down)


def _combine_kernel(x1_ref, meta_ref, y1_ref, y2_ref, gfin_ref, out_ref):
    meta = meta_ref[...]
    x2 = x1_ref[...] + meta[:, 2:3] * _unpack_rows(y1_ref[...]) + meta[:, 3:4] * _unpack_rows(y2_ref[...])
    out_ref[...] = _rms(x2, gfin_ref[...])


def _combine(x1, meta, y1, y2, g_final):
    t = x1.shape[0]
    assert t % TM_OUT == 0
    tok = pl.BlockSpec((TM_OUT, D_MODEL), lambda i: (i, 0))
    packed = pl.BlockSpec((TM_OUT, D_MODEL // 2), lambda i: (i, 0))
    return pl.pallas_call(
        _combine_kernel,
        grid=(t // TM_OUT,),
        in_specs=[tok, pl.BlockSpec((TM_OUT, META_LANES), lambda i: (i, 0)), packed, packed,
                  pl.BlockSpec((1, D_MODEL), lambda i: (0, 0))],
        out_specs=tok,
        out_shape=jax.ShapeDtypeStruct((t, D_MODEL), F32),
        compiler_params=_cparams(),
        name="combine",
    )(x1, meta, y1, y2, g_final.reshape(1, D_MODEL))


def _layout_kernel(cnt_ref, metat_ref, pos_ref, tiles_ref):
    n_lanes = tiles_ref.shape[1]
    tile_start = lax.broadcasted_iota(jnp.int32, (1, n_lanes), 1).astype(F32) * TM_EXP
    m = metat_ref[...]
    eid = m[0:2]
    p = m[4:6]
    off = jnp.zeros((1, 1), F32)
    t_exp = jnp.full((1, n_lanes), N_EXPERTS - 1.0, F32)
    t_rows = jnp.zeros((1, n_lanes), F32)
    for e in range(N_EXPERTS):
        cnt = cnt_ref[EXPERT_ROW0 + e:EXPERT_ROW0 + e + 1, 0:1]
        padded = jnp.ceil(cnt * (1.0 / TM_EXP)) * TM_EXP
        p = p + jnp.where(eid == e, off, 0.0)
        mine = (tile_start >= off) & (tile_start < off + padded)
        t_exp = jnp.where(mine, float(e), t_exp)
        t_rows = jnp.where(mine, jnp.clip(cnt - (tile_start - off), 0.0, float(TM_EXP)), t_rows)
        off = off + padded
    pos_ref[...] = p.astype(jnp.int32)
    tiles_ref[...] = jnp.concatenate([t_exp, t_rows], axis=0).astype(jnp.int32)


def _slot_layout(metat, counts_rows, n_tiles):
    t = metat.shape[1]
    tp = min(TM_POS, t)
    assert t % tp == 0
    n_lanes = (n_tiles + 127) // 128 * 128
    pos, tiles = pl.pallas_call(
        _layout_kernel,
        grid=(t // tp,),
        in_specs=[pl.BlockSpec(counts_rows.shape, lambda i: (0, 0)),
                  pl.BlockSpec((META_LANES, tp), lambda i: (0, i))],
        out_specs=[pl.BlockSpec((2, tp), lambda i: (0, i)), pl.BlockSpec((2, n_lanes), lambda i: (0, 0))],
        out_shape=[jax.ShapeDtypeStruct((2, t), jnp.int32), jax.ShapeDtypeStruct((2, n_lanes), jnp.int32)],
        compiler_params=_cparams(),
        name="layout",
    )(counts_rows, metat)
    return pos[0], pos[1], tiles[0, :n_tiles], tiles[1, :n_tiles]


def kernel(x_prompt, x_sample, g_mix, w_in, b_gate, conv_w, conv_b, ln_g, ln_b, w_out, g_ffn,
           w_router_group, b_router_group, w_router_expert, b_router_expert, w_gate, w_up, w_down, g_final):
    assert g_mix.shape[0] == 1, "one layer"
    bp, lp, _ = x_prompt.shape
    bs, ls, _ = x_sample.shape
    seq_lens = [lp] * bp + [ls] * bs
    tp, ts = bp * lp, bs * ls
    xp = x_prompt.reshape(tp, D_MODEL)
    xs = x_sample.reshape(ts, D_MODEL)

    q, v, o, u, kt, gr = _inproj(xp, xs, g_mix[0], w_in[0], b_gate[0])
    hf, hb = _mlstm(q, kt, v, gr, seq_lens)

    outs = []
    for x, tile0, lens in ((xp, 0, [lp] * bp), (xs, tp // TM_MIX, [ls] * bs)):
        tg = x.shape[0]
        x1, hn, meta, metat, counts = _mixout(x, tile0, hf, hb, o, u, lens, conv_w[0], conv_b[0], ln_g[0], ln_b[0],
                                              w_out[0], g_ffn[0], w_router_group[0], b_router_group[0],
                                              w_router_expert[0], b_router_expert[0])
        n_tiles = (2 * tg + N_EXPERTS * (TM_EXP - 1)) // TM_EXP + 1
        pos1, pos2, tile_expert, tile_rows = _slot_layout(metat, counts, n_tiles)
        hs = _dispatch(hn, pos1, pos2, n_tiles * TM_EXP)
        ys = _experts(hs, tile_expert, tile_rows, w_gate[0], w_up[0], w_down[0])
        y1, y2 = _collect(ys, pos1, pos2)
        outs.append(_combine(x1, meta, y1, y2, g_final))
    return outs[0].reshape(bp, lp, D_MODEL), outs[1].reshape(bs, ls, D_MODEL)
```

```python
import functools

import jax
import jax.numpy as jnp
from jax import lax
from jax.experimental import pallas as pl
from jax.experimental.pallas import tpu as pltpu
from jax.experimental.pallas import tpu_sc as plsc

F32 = jnp.float32
BF16 = jnp.bfloat16

D_MODEL = 1024
N_HEADS = 4
HEAD_DIM = 128
D_MLSTM = N_HEADS * HEAD_DIM
D_CONV = D_MODEL - D_MLSTM
CONV_WIDTH = 31
CONV_HALO = 16
N_DIR = 2
N_GROUPS = 4
EXPERTS_PER_GROUP = 4
N_EXPERTS = N_GROUPS * EXPERTS_PER_GROUP
D_EXPERT = 512
EPS = 1e-6
K_SCALE = HEAD_DIM ** -0.5

GATE_ROWS = 16
QROWS = 24
ROUTER_ROWS = 32
EXPERT_ROW0 = N_GROUPS
META_LANES = 8

TM_IN = 1024
CHUNK = 512
GP_CHUNKS = 32
TM_MIX = 1024
CONV_ROWS = 256
TM_EXP = 512
TM_OUT = 1024
TM_POS = 8192
VMEM_LIMIT = 48 * 1024 * 1024

SC_CORES = 2
SC_SUBCORES = 16
SC_WORKERS = SC_CORES * SC_SUBCORES
SC_ROWS = 128


def _cparams(n_axes=1):
    return pltpu.CompilerParams(dimension_semantics=("arbitrary",) * n_axes,
                                vmem_limit_bytes=VMEM_LIMIT)


def _nt_dot(a, b):
    return lax.dot_general(a, b, (((1,), (1,)), ((), ())), preferred_element_type=F32)


def _rms(x, g):
    return x * lax.rsqrt(jnp.mean(x * x, axis=-1, keepdims=True) + EPS) * g


def _pack_rows(x):
    n = x.shape[1] // 2
    hi = lax.bitcast_convert_type(x[:, :n].astype(jnp.bfloat16).astype(F32), jnp.int32)
    lo = lax.bitcast_convert_type(x[:, n:].astype(jnp.bfloat16).astype(F32), jnp.int32)
    return hi | lax.shift_right_logical(lo, 16)


def _unpack_rows(p):
    hi = lax.bitcast_convert_type(p & jnp.int32(-65536), F32)
    lo = lax.bitcast_convert_type(lax.shift_left(p, 16), F32)
    return jnp.concatenate([hi, lo], axis=1)


def _two_batch_specs(block, n_first, n_second):
    first = pl.BlockSpec(block, lambda i, *_: (jnp.minimum(i, n_first - 1), 0))
    second = pl.BlockSpec(block, lambda i, *_: (jnp.maximum(i - n_first, 0), 0))
    return first, second


def _inproj_kernel(xp_ref, xs_ref, g_ref, wq_ref, wv_ref, wo_ref, wa_ref, wb_ref, wkg_ref, bg_ref,
                   q_ref, v_ref, o_ref, u_ref, kt_ref, gr_ref, *, n_first):
    x = jnp.where(pl.program_id(0) < n_first, xp_ref[...], xs_ref[...])
    xn = _rms(x, g_ref[...]).astype(BF16)
    q_ref[...] = jnp.dot(xn, wq_ref[...], preferred_element_type=F32).astype(BF16)
    v_ref[...] = jnp.dot(xn, wv_ref[...], preferred_element_type=F32).astype(BF16)
    o_ref[...] = jax.nn.sigmoid(jnp.dot(xn, wo_ref[...], preferred_element_type=F32)).astype(BF16)
    a = jnp.dot(xn, wa_ref[...], preferred_element_type=F32)
    b = jnp.dot(xn, wb_ref[...], preferred_element_type=F32)
    u_ref[...] = (a * jax.nn.sigmoid(b)).astype(BF16)
    kg = _nt_dot(wkg_ref[...], xn)
    kt_ref[...] = (kg[:D_MLSTM] * K_SCALE).astype(BF16)
    gr = kg[D_MLSTM:] + bg_ref[...]
    for c in range(gr_ref.shape[0]):
        gr_ref[c] = gr[:, c * CHUNK:(c + 1) * CHUNK]


def _inproj(xp, xs, g_mix, w_in, b_gate):
    tp, ts = xp.shape[0], xs.shape[0]
    t = tp + ts
    assert tp % TM_IN == 0 and ts % TM_IN == 0
    off_k, off_v, off_o, off_g = D_MLSTM, 2 * D_MLSTM, 3 * D_MLSTM, 4 * D_MLSTM
    off_a = off_g + 2 * N_DIR * N_HEADS
    off_b = off_a + D_CONV
    wq = w_in[:, 0:off_k].astype(BF16)
    wkt = w_in[:, off_k:off_v].T.astype(BF16)
    wv = w_in[:, off_v:off_o].astype(BF16)
    wo = w_in[:, off_o:off_g].astype(BF16)
    wa = w_in[:, off_a:off_b].astype(BF16)
    wb = w_in[:, off_b:off_b + D_CONV].astype(BF16)
    wg = w_in[:, off_g:off_a].T.reshape(N_DIR, 2, N_HEADS, D_MODEL)
    wgt = jnp.zeros((N_DIR, 2, GATE_ROWS // 2, D_MODEL), F32).at[:, :, :N_HEADS].set(wg)
    wkg = jnp.concatenate([wkt, wgt.reshape(N_DIR * GATE_ROWS, D_MODEL).astype(BF16)], axis=0)
    bg = jnp.zeros((N_DIR, 2, GATE_ROWS // 2), F32).at[:, :, :N_HEADS].set(
        b_gate.reshape(N_DIR, 2, N_HEADS)).reshape(N_DIR * GATE_ROWS, 1)

    tok = lambda i: (i, 0)
    fixed = lambda i: (0, 0)
    wspec = pl.BlockSpec((D_MODEL, D_MLSTM), fixed)
    xp_spec, xs_spec = _two_batch_specs((TM_IN, D_MODEL), tp // TM_IN, ts // TM_IN)
    cpt = TM_IN // CHUNK
    return pl.pallas_call(
        functools.partial(_inproj_kernel, n_first=tp // TM_IN),
        grid=(t // TM_IN,),
        in_specs=[xp_spec, xs_spec, pl.BlockSpec((1, D_MODEL), fixed),
                  wspec, wspec, wspec, wspec, wspec,
                  pl.BlockSpec((D_MLSTM + N_DIR * GATE_ROWS, D_MODEL), fixed),
                  pl.BlockSpec((N_DIR * GATE_ROWS, 1), fixed)],
        out_specs=[pl.BlockSpec((TM_IN, D_MLSTM), tok)] * 4 + [
            pl.BlockSpec((D_MLSTM, TM_IN), lambda i: (0, i)),
            pl.BlockSpec((cpt, N_DIR * GATE_ROWS, CHUNK), lambda i: (i, 0, 0))],
        out_shape=[jax.ShapeDtypeStruct((t, D_MLSTM), BF16)] * 4 + [
            jax.ShapeDtypeStruct((D_MLSTM, t), BF16),
            jax.ShapeDtypeStruct((t // CHUNK, N_DIR * GATE_ROWS, CHUNK), F32)],
        compiler_params=_cparams(),
        name="inproj",
    )(xp, xs, g_mix.reshape(1, D_MODEL), wq, wv, wo, wa, wb, wkg, bg)


def _log_sigmoid(x):
    return jnp.minimum(x, 0.0) - jnp.log1p(jnp.exp(-jnp.abs(x)))


def _gateprep_kernel(reset_ref, g_ref, rowq_ref, colq_ref, m_ref, *, rev):
    n, _, c = g_ref.shape
    step = pl.program_id(0)
    blk = pl.num_programs(0) - 1 - step if rev else step

    @pl.when(step == 0)
    def _():
        m_ref[...] = jnp.zeros_like(m_ref)

    ig = g_ref[:, 0:8, :]
    lf = _log_sigmoid(g_ref[:, 8:16, :])
    lane = lax.broadcasted_iota(jnp.int32, (n, 8, c), 2)

    def scan(x, op, ident):
        k = 1
        while k < c:
            if rev:
                shifted, valid = pltpu.roll(x, c - k, axis=2), lane < c - k
            else:
                shifted, valid = pltpu.roll(x, k, axis=2), lane >= k
            x = op(x, jnp.where(valid, shifted, ident))
            k *= 2
        return x

    bc = scan(lf, jnp.add, 0.0)
    a = ig - bc
    cm = scan(a, jnp.maximum, -jnp.inf)
    b_tot = jnp.sum(lf, axis=2, keepdims=True)
    a_max = jnp.max(a, axis=2, keepdims=True)

    m = m_ref[...]
    m_in = [None] * n
    for j in (range(n - 1, -1, -1) if rev else range(n)):
        m = jnp.where(reset_ref[blk * n + j] == 1, 0.0, m)
        m_in[j] = m
        m = b_tot[j] + jnp.maximum(m, a_max[j])
    m_ref[...] = m
    m_old = jnp.stack(m_in)

    mx = jnp.maximum(m_old, cm)
    mx_last = jnp.maximum(m_old, a_max)
    rowq_ref[:, 0:8, :] = a
    rowq_ref[:, 8:16, :] = jnp.exp(a - mx_last)
    rowq_ref[:, 16:24, :] = jnp.exp(m_old - mx_last)
    e1 = jnp.exp(m_old - mx)
    fl = jnp.exp(-(mx + bc))
    for j in range(n):
        colq_ref[j] = jnp.concatenate([mx[j], e1[j], fl[j]], axis=0).T


def _gateprep(gr, reset, rev):
    n_chunks = gr.shape[0]
    gp = min(GP_CHUNKS, n_chunks)
    assert n_chunks % gp == 0
    nb = n_chunks // gp
    d = 1 if rev else 0
    bidx = (lambda s: nb - 1 - s) if rev else (lambda s: s)
    grid_spec = pltpu.PrefetchScalarGridSpec(
        num_scalar_prefetch=1,
        grid=(nb,),
        in_specs=[pl.BlockSpec((gp, GATE_ROWS, CHUNK), lambda s, r: (bidx(s), d, 0))],
        out_specs=[pl.BlockSpec((gp, QROWS, CHUNK), lambda s, r: (bidx(s), 0, 0)),
                   pl.BlockSpec((gp, CHUNK, QROWS), lambda s, r: (bidx(s), 0, 0))],
        scratch_shapes=[pltpu.VMEM((8, CHUNK), F32)],
    )
    return pl.pallas_call(
        functools.partial(_gateprep_kernel, rev=rev),
        grid_spec=grid_spec,
        out_shape=[jax.ShapeDtypeStruct((n_chunks, QROWS, CHUNK), F32),
                   jax.ShapeDtypeStruct((n_chunks, CHUNK, QROWS), F32)],
        compiler_params=_cparams(),
        name="gateprep_bwd" if rev else "gateprep_fwd",
    )(reset, gr)


def _mlstm_kernel(rf_ref, rb_ref,
                  qf_ref, ktf_ref, vf_ref, rowf_ref, colf_ref,
                  qb_ref, ktb_ref, vb_ref, rowb_ref, colb_ref,
                  hf_ref, hb_ref, cst_ref):
    c = qf_ref.shape[0]
    step = pl.program_id(0)
    last = pl.num_programs(0) - 1
    row_i = lax.broadcasted_iota(jnp.int32, (c, c), 0)
    col_i = lax.broadcasted_iota(jnp.int32, (c, c), 1)
    ones = jnp.ones((c, HEAD_DIM), BF16)

    dirs = ((rf_ref[step], col_i <= row_i, qf_ref, ktf_ref, vf_ref, rowf_ref, colf_ref, hf_ref),
            (rb_ref[last - step], col_i >= row_i, qb_ref, ktb_ref, vb_ref, rowb_ref, colb_ref, hb_ref))
    for d, (reset, mask, q_ref, kt_ref, v_ref, row_ref, col_ref, h_ref) in enumerate(dirs):
        @pl.when(reset == 1)
        def _():
            cst_ref[d] = jnp.zeros(cst_ref.shape[1:], F32)

        rowq = row_ref[...]
        colq = col_ref[...]
        for h in range(N_HEADS):
            hs = slice(h * HEAD_DIM, (h + 1) * HEAD_DIM)
            qh = q_ref[:, hs]
            kth = kt_ref[hs, :]
            vext = jnp.concatenate([v_ref[:, hs], ones], axis=1)
            s = jnp.dot(qh, kth, preferred_element_type=F32)
            e = jnp.exp(jnp.where(mask, rowq[h:h + 1, :] - colq[:, h:h + 1], -jnp.inf))
            r1 = jnp.dot((s * e).astype(BF16), vext, preferred_element_type=F32)
            cst = cst_ref[d, h]
            r2 = jnp.dot(qh, cst.astype(BF16), preferred_element_type=F32)
            e1 = colq[:, 8 + h:9 + h]
            num = r1[:, :HEAD_DIM] + e1 * r2[:, :HEAD_DIM]
            den = r1[:, HEAD_DIM:] + e1 * r2[:, HEAD_DIM:]
            h_ref[:, hs] = (num / jnp.maximum(jnp.abs(den), colq[:, 16 + h:17 + h])).astype(h_ref.dtype)
            kw = (kth.astype(F32) * rowq[8 + h:9 + h, :]).astype(BF16)
            cst_ref[d, h] = rowq[16 + h:17 + h, 0:1] * cst + jnp.dot(kw, vext, preferred_element_type=F32)


def _mlstm(q, kt, v, gr, seq_lens):
    t = q.shape[0]
    n = t // CHUNK
    starts, ends, pos = [], [], 0
    for ln in seq_lens:
        assert ln % CHUNK == 0
        starts.append(pos // CHUNK)
        ends.append((pos + ln) // CHUNK - 1)
        pos += ln
    reset_f = jnp.zeros((n,), jnp.int32).at[jnp.array(starts)].set(1)
    reset_b = jnp.zeros((n,), jnp.int32).at[jnp.array(ends)].set(1)
    rowf, colf = _gateprep(gr, reset_f, rev=False)
    rowb, colb = _gateprep(gr, reset_b, rev=True)

    def specs(cidx):
        return [pl.BlockSpec((CHUNK, D_MLSTM), lambda s, rf, rb: (cidx(s), 0)),
                pl.BlockSpec((D_MLSTM, CHUNK), lambda s, rf, rb: (0, cidx(s))),
                pl.BlockSpec((CHUNK, D_MLSTM), lambda s, rf, rb: (cidx(s), 0)),
                pl.BlockSpec((None, QROWS, CHUNK), lambda s, rf, rb: (cidx(s), 0, 0)),
                pl.BlockSpec((None, CHUNK, QROWS), lambda s, rf, rb: (cidx(s), 0, 0))]

    fwd = lambda s: s
    bwd = lambda s: n - 1 - s
    grid_spec = pltpu.PrefetchScalarGridSpec(
        num_scalar_prefetch=2,
        grid=(n,),
        in_specs=specs(fwd) + specs(bwd),
        out_specs=[pl.BlockSpec((CHUNK, D_MLSTM), lambda s, rf, rb: (fwd(s), 0)),
                   pl.BlockSpec((CHUNK, D_MLSTM), lambda s, rf, rb: (bwd(s), 0))],
        scratch_shapes=[pltpu.VMEM((N_DIR, N_HEADS, HEAD_DIM, 2 * HEAD_DIM), F32)],
    )
    return pl.pallas_call(
        _mlstm_kernel,
        grid_spec=grid_spec,
        out_shape=[jax.ShapeDtypeStruct((t, D_MLSTM), BF16)] * 2,
        compiler_params=_cparams(),
        name="mlstm",
    )(reset_f, reset_b, q, kt, v, rowf, colf, q, kt, v, rowb, colb)


def _mixout_kernel(first_ref, last_ref,
                   x_ref, hf_ref, hb_ref, o_ref, u_ref, up_ref, un_ref, cw_ref, cb_ref, lng_ref, lnb_ref,
                   wout_ref, gffn_ref, wr_ref, br_ref, tri_ref,
                   x1_ref, hn_ref, meta_ref, metat_ref, cnt_ref,
                   win_ref, y_ref, run_ref):
    i = pl.program_id(0)
    tm = hf_ref.shape[0]

    @pl.when(i == 0)
    def _():
        run_ref[...] = jnp.zeros_like(run_ref)

    h_sum = hf_ref[...].astype(F32) + hb_ref[...].astype(F32)
    ym = (o_ref[...].astype(F32) * h_sum).astype(BF16)
    x1m = x_ref[...] + jnp.dot(ym, wout_ref[:D_MLSTM, :], preferred_element_type=F32)

    win_ref[0:CONV_HALO, :] = jnp.where(first_ref[i] == 1, 0.0, up_ref[...].astype(F32))
    win_ref[CONV_HALO:CONV_HALO + tm, :] = u_ref[...].astype(F32)
    win_ref[CONV_HALO + tm:, :] = jnp.where(last_ref[i] == 1, 0.0, un_ref[...].astype(F32))

    off0 = CONV_HALO - CONV_WIDTH // 2
    for r0 in range(0, tm, CONV_ROWS):
        tiles = []
        for lt in range(D_CONV // 128):
            ls = slice(lt * 128, (lt + 1) * 128)
            acc = jnp.broadcast_to(cb_ref[:, ls], (CONV_ROWS, 128))
            for s in range(8):
                part = None
                for j in range(CONV_WIDTH):
                    if (off0 + j) % 8 != s:
                        continue
                    base = (off0 + j) // 8 * 8
                    term = win_ref[r0 + base:r0 + base + CONV_ROWS + 8, ls] * cw_ref[j:j + 1, ls]
                    part = term if part is None else part + term
                acc = acc + part[s:s + CONV_ROWS, :]
            tiles.append(acc)
        cv = jnp.concatenate(tiles, axis=1)
        xc = cv - jnp.mean(cv, axis=-1, keepdims=True)
        yc = xc * lax.rsqrt(jnp.mean(xc * xc, axis=-1, keepdims=True) + EPS) * lng_ref[...] + lnb_ref[...]
        y_ref[r0:r0 + CONV_ROWS, :] = (yc * jax.nn.sigmoid(yc)).astype(BF16)

    x1 = x1m + jnp.dot(y_ref[...], wout_ref[D_MLSTM:, :], preferred_element_type=F32)
    x1_ref[...] = x1
    hn = _rms(x1, gffn_ref[...])
    hn_ref[...] = _pack_rows(hn)

    logits = _nt_dot(wr_ref[...], hn.astype(BF16)) + br_ref[...]
    row = lax.broadcasted_iota(jnp.int32, (ROUTER_ROWS, tm), 0).astype(F32)
    neg = -jnp.inf
    no_row = float(ROUTER_ROWS)
    gl = jnp.where(row < N_GROUPS, logits, neg)
    gmax = jnp.max(gl, axis=0, keepdims=True)
    p_top = 1.0 / jnp.sum(jnp.exp(gl - gmax), axis=0, keepdims=True)
    g_idx = jnp.min(jnp.where(gl == gmax, row, no_row), axis=0, keepdims=True)
    lo = EXPERT_ROW0 + EXPERTS_PER_GROUP * g_idx
    in_grp = (row >= lo) & (row < lo + EXPERTS_PER_GROUP)
    el = jnp.where(in_grp, logits, neg)
    ee = jnp.exp(el - jnp.max(el, axis=0, keepdims=True))
    pe = jnp.where(in_grp, ee / jnp.sum(ee, axis=0, keepdims=True), -1.0)
    v1 = jnp.max(pe, axis=0, keepdims=True)
    i1 = jnp.min(jnp.where(pe == v1, row, no_row), axis=0, keepdims=True)
    pe2 = jnp.where(row == i1, -1.0, pe)
    v2 = jnp.max(pe2, axis=0, keepdims=True)
    i2 = jnp.min(jnp.where(pe2 == v2, row, no_row), axis=0, keepdims=True)
    wsum = v1 + v2
    gate1 = p_top * (v1 / wsum)
    gate2 = p_top * (v2 / wsum)

    oh1 = (row == i1).astype(F32)
    oh2 = (row == i2).astype(F32)
    cnt = jnp.dot(jnp.concatenate([oh1, oh2], axis=0).astype(BF16), tri_ref[...], preferred_element_type=F32)
    run = run_ref[...]
    tot1 = jnp.sum(oh1, axis=1, keepdims=True)
    tot2 = jnp.sum(oh2, axis=1, keepdims=True)
    rank1 = jnp.sum(oh1 * (run + cnt[:ROUTER_ROWS]), axis=0, keepdims=True)
    rank2 = jnp.sum(oh2 * (run + tot1 + cnt[ROUTER_ROWS:]), axis=0, keepdims=True)
    run = run + tot1 + tot2
    run_ref[...] = run
    cnt_ref[...] = jnp.broadcast_to(run, cnt_ref.shape)

    mr = lax.broadcasted_iota(jnp.int32, (META_LANES, tm), 0)
    metat = jnp.where(mr == 0, i1 - EXPERT_ROW0, 0.0)
    metat = jnp.where(mr == 1, i2 - EXPERT_ROW0, metat)
    metat = jnp.where(mr == 2, gate1, metat)
    metat = jnp.where(mr == 3, gate2, metat)
    metat = jnp.where(mr == 4, rank1, metat)
    metat = jnp.where(mr == 5, rank2, metat)
    metat_ref[...] = metat
    meta_ref[...] = metat.T


def _mixout(x, tile0, hf, hb, o, u, seq_lens, conv_w, conv_b, ln_g, ln_b, w_out, g_ffn, w_rg, b_rg, w_re, b_re):
    t = x.shape[0]
    n_tiles = t // TM_MIX
    hpt = TM_MIX // CONV_HALO
    n_halo = u.shape[0] // CONV_HALO
    firsts, lasts, pos = [], [], 0
    for ln in seq_lens:
        assert ln % TM_MIX == 0
        firsts.append(pos // TM_MIX)
        lasts.append((pos + ln) // TM_MIX - 1)
        pos += ln
    assert pos == t
    first = jnp.zeros((n_tiles,), jnp.int32).at[jnp.array(firsts)].set(1)
    last = jnp.zeros((n_tiles,), jnp.int32).at[jnp.array(lasts)].set(1)

    cw = jnp.zeros((32, D_CONV), F32).at[:CONV_WIDTH].set(conv_w.reshape(CONV_WIDTH, D_CONV))
    wr = jnp.zeros((ROUTER_ROWS, D_MODEL), F32)
    wr = wr.at[:N_GROUPS].set(w_rg.T).at[EXPERT_ROW0:EXPERT_ROW0 + N_EXPERTS].set(w_re.T).astype(BF16)
    br = jnp.zeros((ROUTER_ROWS, 1), F32)
    br = br.at[:N_GROUPS, 0].set(b_rg).at[EXPERT_ROW0:EXPERT_ROW0 + N_EXPERTS, 0].set(b_re)
    tri = (lax.broadcasted_iota(jnp.int32, (TM_MIX, TM_MIX), 0)
           < lax.broadcasted_iota(jnp.int32, (TM_MIX, TM_MIX), 1)).astype(BF16)

    tok = lambda i, f, l: (i, 0)
    flat = lambda i, f, l: (tile0 + i, 0)
    fixed = lambda i, f, l: (0, 0)
    row = lambda n: pl.BlockSpec((1, n), fixed)
    half = pl.BlockSpec((TM_MIX, D_MLSTM), flat)
    grid_spec = pltpu.PrefetchScalarGridSpec(
        num_scalar_prefetch=2,
        grid=(n_tiles,),
        in_specs=[pl.BlockSpec((TM_MIX, D_MODEL), tok), half, half, half,
                  pl.BlockSpec((TM_MIX, D_CONV), flat),
                  pl.BlockSpec((CONV_HALO, D_CONV), lambda i, f, l: (jnp.maximum((tile0 + i) * hpt - 1, 0), 0)),
                  pl.BlockSpec((CONV_HALO, D_CONV),
                               lambda i, f, l: (jnp.minimum((tile0 + i + 1) * hpt, n_halo - 1), 0)),
                  pl.BlockSpec((32, D_CONV), fixed), row(D_CONV), row(D_CONV), row(D_CONV),
                  pl.BlockSpec((D_MODEL, D_MODEL), fixed), row(D_MODEL),
                  pl.BlockSpec((ROUTER_ROWS, D_MODEL), fixed), pl.BlockSpec((ROUTER_ROWS, 1), fixed),
                  pl.BlockSpec((TM_MIX, TM_MIX), fixed)],
        out_specs=[pl.BlockSpec((TM_MIX, D_MODEL), tok), pl.BlockSpec((TM_MIX, D_MODEL // 2), tok),
                   pl.BlockSpec((TM_MIX, META_LANES), tok),
                   pl.BlockSpec((META_LANES, TM_MIX), lambda i, f, l: (0, i)),
                   pl.BlockSpec((ROUTER_ROWS, 128), fixed)],
        scratch_shapes=[pltpu.VMEM((TM_MIX + 2 * CONV_HALO, D_CONV), F32),
                        pltpu.VMEM((TM_MIX, D_CONV), BF16),
                        pltpu.VMEM((ROUTER_ROWS, 1), F32)],
    )
    return pl.pallas_call(
        _mixout_kernel,
        grid_spec=grid_spec,
        out_shape=[jax.ShapeDtypeStruct((t, D_MODEL), F32), jax.ShapeDtypeStruct((t, D_MODEL // 2), jnp.int32),
                   jax.ShapeDtypeStruct((t, META_LANES), F32), jax.ShapeDtypeStruct((META_LANES, t), F32),
                   jax.ShapeDtypeStruct((ROUTER_ROWS, 128), F32)],
        compiler_params=_cparams(),
        name="mixout",
    )(first, last, x, hf, hb, o, u, u, u, cw, conv_b.reshape(1, D_CONV), ln_g.reshape(1, D_CONV),
      ln_b.reshape(1, D_CONV), w_out.astype(BF16), g_ffn.reshape(1, D_MODEL), wr, br, tri)


def _sc_mesh():
    return plsc.VectorSubcoreMesh(core_axis_name="c", subcore_axis_name="s")


def _sc_worker_base(per_worker):
    wid = lax.axis_index("s") * SC_CORES + lax.axis_index("c")
    return wid * per_worker


def _dispatch(hn, pos1, pos2, n_slots):
    t, d = hn.shape
    per_worker = t // SC_WORKERS
    assert per_worker * SC_WORKERS == t and per_worker % SC_ROWS == 0

    def body(h_hbm, p1_hbm, p2_hbm, out_hbm, i1_v, i2_v, rows_v, sem1, sem2):
        base0 = _sc_worker_base(per_worker)

        @pl.loop(0, per_worker // SC_ROWS)
        def _(c):
            base = pl.multiple_of(base0 + c * SC_ROWS, SC_ROWS)
            pltpu.sync_copy(p1_hbm.at[pl.ds(base, SC_ROWS)], i1_v)
            pltpu.sync_copy(p2_hbm.at[pl.ds(base, SC_ROWS)], i2_v)
            pltpu.sync_copy(h_hbm.at[pl.ds(base, SC_ROWS)], rows_v)
            c1 = pltpu.async_copy(rows_v, out_hbm.at[i1_v], sem1)
            c2 = pltpu.async_copy(rows_v, out_hbm.at[i2_v], sem2)
            c1.wait()
            c2.wait()

    return pl.kernel(
        body,
        out_type=jax.ShapeDtypeStruct((n_slots, d), hn.dtype),
        mesh=_sc_mesh(),
        scratch_types=[pltpu.VMEM((SC_ROWS,), jnp.int32), pltpu.VMEM((SC_ROWS,), jnp.int32),
                       pltpu.VMEM((SC_ROWS, d), hn.dtype), pltpu.SemaphoreType.DMA, pltpu.SemaphoreType.DMA],
        name="dispatch",
    )(hn, pos1, pos2)


def _collect(ys, pos1, pos2):
    t = pos1.shape[0]
    d = ys.shape[1]
    per_worker = t // SC_WORKERS
    assert per_worker * SC_WORKERS == t and per_worker % SC_ROWS == 0

    def body(ys_hbm, p1_hbm, p2_hbm, y1_hbm, y2_hbm, i_v, rows_v, sem):
        base0 = _sc_worker_base(per_worker)

        @pl.loop(0, per_worker // SC_ROWS)
        def _(c):
            base = pl.multiple_of(base0 + c * SC_ROWS, SC_ROWS)
            for p_hbm, y_hbm in ((p1_hbm, y1_hbm), (p2_hbm, y2_hbm)):
                pltpu.sync_copy(p_hbm.at[pl.ds(base, SC_ROWS)], i_v)
                pltpu.async_copy(ys_hbm.at[i_v], rows_v, sem).wait()
                pltpu.sync_copy(rows_v, y_hbm.at[pl.ds(base, SC_ROWS)])

    out = jax.ShapeDtypeStruct((t, d), ys.dtype)
    return pl.kernel(
        body,
        out_type=(out, out),
        mesh=_sc_mesh(),
        scratch_types=[pltpu.VMEM((SC_ROWS,), jnp.int32), pltpu.VMEM((SC_ROWS, d), ys.dtype),
                       pltpu.SemaphoreType.DMA],
        name="collect",
    )(ys, pos1, pos2)


def _experts_kernel(te_ref, rows_ref, hs_ref, wg_ref, wu_ref, wd_ref, ys_ref, wgb_ref, wub_ref, wdb_ref):
    i = pl.program_id(0)

    @pl.when((i == 0) | (te_ref[i] != te_ref[jnp.maximum(i - 1, 0)]))
    def _():
        wgb_ref[...] = wg_ref[...].astype(BF16)
        wub_ref[...] = wu_ref[...].astype(BF16)
        wdb_ref[...] = wd_ref[...].astype(BF16)

    @pl.when(rows_ref[i] > 0)
    def _():
        x = _unpack_rows(hs_ref[...]).astype(BF16)
        hg = jnp.dot(x, wgb_ref[...], preferred_element_type=F32)
        hu = jnp.dot(x, wub_ref[...], preferred_element_type=F32)
        hid = (hg * jax.nn.sigmoid(hg) * hu).astype(BF16)
        ys_ref[...] = _pack_rows(jnp.dot(hid, wdb_ref[...], preferred_element_type=F32))

    @pl.when(rows_ref[i] == 0)
    def _():
        ys_ref[...] = jnp.zeros_like(ys_ref)


def _experts(hs, tile_expert, tile_rows, w_gate, w_up, w_down):
    n_tiles = tile_expert.shape[0]
    wsel = lambda i, te, rows: (te[i], 0, 0)
    slot = lambda i, te, rows: (i, 0)
    grid_spec = pltpu.PrefetchScalarGridSpec(
        num_scalar_prefetch=2,
        grid=(n_tiles,),
        in_specs=[pl.BlockSpec((TM_EXP, D_MODEL // 2), slot),
                  pl.BlockSpec((None, D_MODEL, D_EXPERT), wsel),
                  pl.BlockSpec((None, D_MODEL, D_EXPERT), wsel),
                  pl.BlockSpec((None, D_EXPERT, D_MODEL), wsel)],
        out_specs=pl.BlockSpec((TM_EXP, D_MODEL // 2), slot),
        scratch_shapes=[pltpu.VMEM((D_MODEL, D_EXPERT), BF16), pltpu.VMEM((D_MODEL, D_EXPERT), BF16),
                        pltpu.VMEM((D_EXPERT, D_MODEL), BF16)],
    )
    return pl.pallas_call(
        _experts_kernel,
        grid_spec=grid_spec,
        out_shape=jax.ShapeDtypeStruct((n_tiles * TM_EXP, D_MODEL // 2), jnp.int32),
        compiler_params=_cparams(),
        name="experts",
    )(tile_expert, tile_rows, hs, w_gate, w_up, w_down)


def _combine_kernel(x1_ref, meta_ref, y1_ref, y2_ref, gfin_ref, out_ref):
    meta = meta_ref[...]
    x2 = x1_ref[...] + meta[:, 2:3] * _unpack_rows(y1_ref[...]) + meta[:, 3:4] * _unpack_rows(y2_ref[...])
    out_ref[...] = _rms(x2, gfin_ref[...])


def _combine(x1, meta, y1, y2, g_final):
    t = x1.shape[0]
    assert t % TM_OUT == 0
    tok = pl.BlockSpec((TM_OUT, D_MODEL), lambda i: (i, 0))
    packed = pl.BlockSpec((TM_OUT, D_MODEL // 2), lambda i: (i, 0))
    return pl.pallas_call(
        _combine_kernel,
        grid=(t // TM_OUT,),
        in_specs=[tok, pl.BlockSpec((TM_OUT, META_LANES), lambda i: (i, 0)), packed, packed,
                  pl.BlockSpec((1, D_MODEL), lambda i: (0, 0))],
        out_specs=tok,
        out_shape=jax.ShapeDtypeStruct((t, D_MODEL), F32),
        compiler_params=_cparams(),
        name="combine",
    )(x1, meta, y1, y2, g_final.reshape(1, D_MODEL))


def _layout_kernel(cnt_ref, metat_ref, pos_ref, tiles_ref):
    n_lanes = tiles_ref.shape[1]
    tile_start = lax.broadcasted_iota(jnp.int32, (1, n_lanes), 1).astype(F32) * TM_EXP
    m = metat_ref[...]
    eid = m[0:2]
    p = m[4:6]
    off = jnp.zeros((1, 1), F32)
    t_exp = jnp.full((1, n_lanes), N_EXPERTS - 1.0, F32)
    t_rows = jnp.zeros((1, n_lanes), F32)
    for e in range(N_EXPERTS):
        cnt = cnt_ref[EXPERT_ROW0 + e:EXPERT_ROW0 + e + 1, 0:1]
        padded = jnp.ceil(cnt * (1.0 / TM_EXP)) * TM_EXP
        p = p + jnp.where(eid == e, off, 0.0)
        mine = (tile_start >= off) & (tile_start < off + padded)
        t_exp = jnp.where(mine, float(e), t_exp)
        t_rows = jnp.where(mine, jnp.clip(cnt - (tile_start - off), 0.0, float(TM_EXP)), t_rows)
        off = off + padded
    pos_ref[...] = p.astype(jnp.int32)
    tiles_ref[...] = jnp.concatenate([t_exp, t_rows], axis=0).astype(jnp.int32)


def _slot_layout(metat, counts_rows, n_tiles):
    t = metat.shape[1]
    tp = min(TM_POS, t)
    assert t % tp == 0
    n_lanes = (n_tiles + 127) // 128 * 128
    pos, tiles = pl.pallas_call(
        _layout_kernel,
        grid=(t // tp,),
        in_specs=[pl.BlockSpec(counts_rows.shape, lambda i: (0, 0)),
                  pl.BlockSpec((META_LANES, tp), lambda i: (0, i))],
        out_specs=[pl.BlockSpec((2, tp), lambda i: (0, i)), pl.BlockSpec((2, n_lanes), lambda i: (0, 0))],
        out_shape=[jax.ShapeDtypeStruct((2, t), jnp.int32), jax.ShapeDtypeStruct((2, n_lanes), jnp.int32)],
        compiler_params=_cparams(),
        name="layout",
    )(counts_rows, metat)
    return pos[0], pos[1], tiles[0, :n_tiles], tiles[1, :n_tiles]


def kernel(x_prompt, x_sample, g_mix, w_in, b_gate, conv_w, conv_b, ln_g, ln_b, w_out, g_ffn,
           w_router_group, b_router_group, w_router_expert, b_router_expert, w_gate, w_up, w_down, g_final):
    assert g_mix.shape[0] == 1, "one layer"
    bp, lp, _ = x_prompt.shape
    bs, ls, _ = x_sample.shape
    seq_lens = [lp] * bp + [ls] * bs
    tp, ts = bp * lp, bs * ls
    xp = x_prompt.reshape(tp, D_MODEL)
    xs = x_sample.reshape(ts, D_MODEL)

    q, v, o, u, kt, gr = _inproj(xp, xs, g_mix[0], w_in[0], b_gate[0])
    hf, hb = _mlstm(q, kt, v, gr, seq_lens)

    outs = []
    for x, tile0, lens in ((xp, 0, [lp] * bp), (xs, tp // TM_MIX, [ls] * bs)):
        tg = x.shape[0]
        x1, hn, meta, metat, counts = _mixout(x, tile0, hf, hb, o, u, lens, conv_w[0], conv_b[0], ln_g[0], ln_b[0],
                                              w_out[0], g_ffn[0], w_router_group[0], b_router_group[0],
                                              w_router_expert[0], b_router_expert[0])
        n_tiles = (2 * tg + N_EXPERTS * (TM_EXP - 1)) // TM_EXP + 1
        pos1, pos2, tile_expert, tile_rows = _slot_layout(metat, counts, n_tiles)
        hs = _dispatch(hn, pos1, pos2, n_tiles * TM_EXP)
        ys = _experts(hs, tile_expert, tile_rows, w_gate[0], w_up[0], w_down[0])
        y1, y2 = _collect(ys, pos1, pos2)
        outs.append(_combine(x1, meta, y1, y2, g_final))
    return outs[0].reshape(bp, lp, D_MODEL), outs[1].reshape(bs, ls, D_MODEL)
```

```python
import functools

import jax
import jax.numpy as jnp
from jax import lax
from jax.experimental import pallas as pl
from jax.experimental.pallas import tpu as pltpu
from jax.experimental.pallas import tpu_sc as plsc

F32 = jnp.float32
BF16 = jnp.bfloat16

D_MODEL = 1024
N_HEADS = 4
HEAD_DIM = 128
D_MLSTM = N_HEADS * HEAD_DIM
D_CONV = D_MODEL - D_MLSTM
CONV_WIDTH = 31
CONV_HALO = 16
N_DIR = 2
N_GROUPS = 4
EXPERTS_PER_GROUP = 4
N_EXPERTS = N_GROUPS * EXPERTS_PER_GROUP
D_EXPERT = 512
EPS = 1e-6
K_SCALE = HEAD_DIM ** -0.5

GATE_ROWS = 16
QROWS = 24
ROUTER_ROWS = 32
EXPERT_ROW0 = N_GROUPS
META_LANES = 8

TM_IN = 1024
CHUNK = 512
GP_CHUNKS = 32
TM_MIX = 1024
CONV_ROWS = 256
TM_EXP = 512
TM_OUT = 1024
TM_POS = 8192
VMEM_LIMIT = 48 * 1024 * 1024

SC_CORES = 2
SC_SUBCORES = 16
SC_WORKERS = SC_CORES * SC_SUBCORES
SC_ROWS = 128


def _cparams(n_axes=1):
    return pltpu.CompilerParams(dimension_semantics=("arbitrary",) * n_axes,
                                vmem_limit_bytes=VMEM_LIMIT)


def _nt_dot(a, b):
    return lax.dot_general(a, b, (((1,), (1,)), ((), ())), preferred_element_type=F32)


def _rms(x, g):
    return x * lax.rsqrt(jnp.mean(x * x, axis=-1, keepdims=True) + EPS) * g


def _pack_rows(x):
    n = x.shape[1] // 2
    hi = lax.bitcast_convert_type(x[:, :n].astype(jnp.bfloat16).astype(F32), jnp.int32)
    lo = lax.bitcast_convert_type(x[:, n:].astype(jnp.bfloat16).astype(F32), jnp.int32)
    return hi | lax.shift_right_logical(lo, 16)


def _unpack_rows(p):
    hi = lax.bitcast_convert_type(p & jnp.int32(-65536), F32)
    lo = lax.bitcast_convert_type(lax.shift_left(p, 16), F32)
    return jnp.concatenate([hi, lo], axis=1)


def _two_batch_specs(block, n_first, n_second):
    first = pl.BlockSpec(block, lambda i, *_: (jnp.minimum(i, n_first - 1), 0))
    second = pl.BlockSpec(block, lambda i, *_: (jnp.maximum(i - n_first, 0), 0))
    return first, second


def _inproj_kernel(xp_ref, xs_ref, g_ref, wq_ref, wv_ref, wo_ref, wa_ref, wb_ref, wkg_ref, bg_ref,
                   q_ref, v_ref, o_ref, u_ref, kt_ref, gr_ref, *, n_first):
    x = jnp.where(pl.program_id(0) < n_first, xp_ref[...], xs_ref[...])
    xn = _rms(x, g_ref[...]).astype(BF16)
    q_ref[...] = jnp.dot(xn, wq_ref[...], preferred_element_type=F32).astype(BF16)
    v_ref[...] = jnp.dot(xn, wv_ref[...], preferred_element_type=F32).astype(BF16)
    o_ref[...] = jax.nn.sigmoid(jnp.dot(xn, wo_ref[...], preferred_element_type=F32)).astype(BF16)
    a = jnp.dot(xn, wa_ref[...], preferred_element_type=F32)
    b = jnp.dot(xn, wb_ref[...], preferred_element_type=F32)
    u_ref[...] = (a * jax.nn.sigmoid(b)).astype(BF16)
    kg = _nt_dot(wkg_ref[...], xn)
    kt_ref[...] = (kg[:D_MLSTM] * K_SCALE).astype(BF16)
    gr = kg[D_MLSTM:] + bg_ref[...]
    for c in range(gr_ref.shape[0]):
        gr_ref[c] = gr[:, c * CHUNK:(c + 1) * CHUNK]


def _inproj(xp, xs, g_mix, w_in, b_gate):
    tp, ts = xp.shape[0], xs.shape[0]
    t = tp + ts
    assert tp % TM_IN == 0 and ts % TM_IN == 0
    off_k, off_v, off_o, off_g = D_MLSTM, 2 * D_MLSTM, 3 * D_MLSTM, 4 * D_MLSTM
    off_a = off_g + 2 * N_DIR * N_HEADS
    off_b = off_a + D_CONV
    wq = w_in[:, 0:off_k].astype(BF16)
    wkt = w_in[:, off_k:off_v].T.astype(BF16)
    wv = w_in[:, off_v:off_o].astype(BF16)
    wo = w_in[:, off_o:off_g].astype(BF16)
    wa = w_in[:, off_a:off_b].astype(BF16)
    wb = w_in[:, off_b:off_b + D_CONV].astype(BF16)
    wg = w_in[:, off_g:off_a].T.reshape(N_DIR, 2, N_HEADS, D_MODEL)
    wgt = jnp.zeros((N_DIR, 2, GATE_ROWS // 2, D_MODEL), F32).at[:, :, :N_HEADS].set(wg)
    wkg = jnp.concatenate([wkt, wgt.reshape(N_DIR * GATE_ROWS, D_MODEL).astype(BF16)], axis=0)
    bg = jnp.zeros((N_DIR, 2, GATE_ROWS // 2), F32).at[:, :, :N_HEADS].set(
        b_gate.reshape(N_DIR, 2, N_HEADS)).reshape(N_DIR * GATE_ROWS, 1)

    tok = lambda i: (i, 0)
    fixed = lambda i: (0, 0)
    wspec = pl.BlockSpec((D_MODEL, D_MLSTM), fixed)
    xp_spec, xs_spec = _two_batch_specs((TM_IN, D_MODEL), tp // TM_IN, ts // TM_IN)
    cpt = TM_IN // CHUNK
    return pl.pallas_call(
        functools.partial(_inproj_kernel, n_first=tp // TM_IN),
        grid=(t // TM_IN,),
        in_specs=[xp_spec, xs_spec, pl.BlockSpec((1, D_MODEL), fixed),
                  wspec, wspec, wspec, wspec, wspec,
                  pl.BlockSpec((D_MLSTM + N_DIR * GATE_ROWS, D_MODEL), fixed),
                  pl.BlockSpec((N_DIR * GATE_ROWS, 1), fixed)],
        out_specs=[pl.BlockSpec((TM_IN, D_MLSTM), tok)] * 4 + [
            pl.BlockSpec((D_MLSTM, TM_IN), lambda i: (0, i)),
            pl.BlockSpec((cpt, N_DIR * GATE_ROWS, CHUNK), lambda i: (i, 0, 0))],
        out_shape=[jax.ShapeDtypeStruct((t, D_MLSTM), BF16)] * 4 + [
            jax.ShapeDtypeStruct((D_MLSTM, t), BF16),
            jax.ShapeDtypeStruct((t // CHUNK, N_DIR * GATE_ROWS, CHUNK), F32)],
        compiler_params=_cparams(),
        name="inproj",
    )(xp, xs, g_mix.reshape(1, D_MODEL), wq, wv, wo, wa, wb, wkg, bg)


def _log_sigmoid(x):
    return jnp.minimum(x, 0.0) - jnp.log1p(jnp.exp(-jnp.abs(x)))


def _gateprep_kernel(reset_ref, g_ref, rowq_ref, colq_ref, m_ref, *, rev):
    n, _, c = g_ref.shape
    step = pl.program_id(0)
    blk = pl.num_programs(0) - 1 - step if rev else step

    @pl.when(step == 0)
    def _():
        m_ref[...] = jnp.zeros_like(m_ref)

    ig = g_ref[:, 0:8, :]
    lf = _log_sigmoid(g_ref[:, 8:16, :])
    lane = lax.broadcasted_iota(jnp.int32, (n, 8, c), 2)

    def scan(x, op, ident):
        k = 1
        while k < c:
            if rev:
                shifted, valid = pltpu.roll(x, c - k, axis=2), lane < c - k
            else:
                shifted, valid = pltpu.roll(x, k, axis=2), lane >= k
            x = op(x, jnp.where(valid, shifted, ident))
            k *= 2
        return x

    bc = scan(lf, jnp.add, 0.0)
    a = ig - bc
    cm = scan(a, jnp.maximum, -jnp.inf)
    b_tot = jnp.sum(lf, axis=2, keepdims=True)
    a_max = jnp.max(a, axis=2, keepdims=True)

    m = m_ref[...]
    m_in = [None] * n
    for j in (range(n - 1, -1, -1) if rev else range(n)):
        m = jnp.where(reset_ref[blk * n + j] == 1, 0.0, m)
        m_in[j] = m
        m = b_tot[j] + jnp.maximum(m, a_max[j])
    m_ref[...] = m
    m_old = jnp.stack(m_in)

    mx = jnp.maximum(m_old, cm)
    mx_last = jnp.maximum(m_old, a_max)
    rowq_ref[:, 0:8, :] = a
    rowq_ref[:, 8:16, :] = jnp.exp(a - mx_last)
    rowq_ref[:, 16:24, :] = jnp.exp(m_old - mx_last)
    e1 = jnp.exp(m_old - mx)
    fl = jnp.exp(-(mx + bc))
    for j in range(n):
        colq_ref[j] = jnp.concatenate([mx[j], e1[j], fl[j]], axis=0).T


def _gateprep(gr, reset, rev):
    n_chunks = gr.shape[0]
    gp = min(GP_CHUNKS, n_chunks)
    assert n_chunks % gp == 0
    nb = n_chunks // gp
    d = 1 if rev else 0
    bidx = (lambda s: nb - 1 - s) if rev else (lambda s: s)
    grid_spec = pltpu.PrefetchScalarGridSpec(
        num_scalar_prefetch=1,
        grid=(nb,),
        in_specs=[pl.BlockSpec((gp, GATE_ROWS, CHUNK), lambda s, r: (bidx(s), d, 0))],
        out_specs=[pl.BlockSpec((gp, QROWS, CHUNK), lambda s, r: (bidx(s), 0, 0)),
                   pl.BlockSpec((gp, CHUNK, QROWS), lambda s, r: (bidx(s), 0, 0))],
        scratch_shapes=[pltpu.VMEM((8, CHUNK), F32)],
    )
    return pl.pallas_call(
        functools.partial(_gateprep_kernel, rev=rev),
        grid_spec=grid_spec,
        out_shape=[jax.ShapeDtypeStruct((n_chunks, QROWS, CHUNK), F32),
                   jax.ShapeDtypeStruct((n_chunks, CHUNK, QROWS), F32)],
        compiler_params=_cparams(),
        name="gateprep_bwd" if rev else "gateprep_fwd",
    )(reset, gr)


def _mlstm_kernel(rf_ref, rb_ref,
                  qf_ref, ktf_ref, vf_ref, rowf_ref, colf_ref,
                  qb_ref, ktb_ref, vb_ref, rowb_ref, colb_ref,
                  hf_ref, hb_ref, cst_ref):
    c = qf_ref.shape[0]
    step = pl.program_id(0)
    last = pl.num_programs(0) - 1
    row_i = lax.broadcasted_iota(jnp.int32, (c, c), 0)
    col_i = lax.broadcasted_iota(jnp.int32, (c, c), 1)
    ones = jnp.ones((c, HEAD_DIM), BF16)

    dirs = ((rf_ref[step], col_i <= row_i, qf_ref, ktf_ref, vf_ref, rowf_ref, colf_ref, hf_ref),
            (rb_ref[last - step], col_i >= row_i, qb_ref, ktb_ref, vb_ref, rowb_ref, colb_ref, hb_ref))
    for d, (reset, mask, q_ref, kt_ref, v_ref, row_ref, col_ref, h_ref) in enumerate(dirs):
        @pl.when(reset == 1)
        def _():
            cst_ref[d] = jnp.zeros(cst_ref.shape[1:], F32)

        rowq = row_ref[...]
        colq = col_ref[...]
        for h in range(N_HEADS):
            hs = slice(h * HEAD_DIM, (h + 1) * HEAD_DIM)
            qh = q_ref[:, hs]
            kth = kt_ref[hs, :]
            vext = jnp.concatenate([v_ref[:, hs], ones], axis=1)
            s = jnp.dot(qh, kth, preferred_element_type=F32)
            e = jnp.exp(jnp.where(mask, rowq[h:h + 1, :] - colq[:, h:h + 1], -jnp.inf))
            r1 = jnp.dot((s * e).astype(BF16), vext, preferred_element_type=F32)
            cst = cst_ref[d, h]
            r2 = jnp.dot(qh, cst.astype(BF16), preferred_element_type=F32)
            e1 = colq[:, 8 + h:9 + h]
            num = r1[:, :HEAD_DIM] + e1 * r2[:, :HEAD_DIM]
            den = r1[:, HEAD_DIM:] + e1 * r2[:, HEAD_DIM:]
            h_ref[:, hs] = (num / jnp.maximum(jnp.abs(den), colq[:, 16 + h:17 + h])).astype(h_ref.dtype)
            kw = (kth.astype(F32) * rowq[8 + h:9 + h, :]).astype(BF16)
            cst_ref[d, h] = rowq[16 + h:17 + h, 0:1] * cst + jnp.dot(kw, vext, preferred_element_type=F32)


def _mlstm(q, kt, v, gr, seq_lens):
    t = q.shape[0]
    n = t // CHUNK
    starts, ends, pos = [], [], 0
    for ln in seq_lens:
        assert ln % CHUNK == 0
        starts.append(pos // CHUNK)
        ends.append((pos + ln) // CHUNK - 1)
        pos += ln
    reset_f = jnp.zeros((n,), jnp.int32).at[jnp.array(starts)].set(1)
    reset_b = jnp.zeros((n,), jnp.int32).at[jnp.array(ends)].set(1)
    rowf, colf = _gateprep(gr, reset_f, rev=False)
    rowb, colb = _gateprep(gr, reset_b, rev=True)

    def specs(cidx):
        return [pl.BlockSpec((CHUNK, D_MLSTM), lambda s, rf, rb: (cidx(s), 0)),
                pl.BlockSpec((D_MLSTM, CHUNK), lambda s, rf, rb: (0, cidx(s))),
                pl.BlockSpec((CHUNK, D_MLSTM), lambda s, rf, rb: (cidx(s), 0)),
                pl.BlockSpec((None, QROWS, CHUNK), lambda s, rf, rb: (cidx(s), 0, 0)),
                pl.BlockSpec((None, CHUNK, QROWS), lambda s, rf, rb: (cidx(s), 0, 0))]

    fwd = lambda s: s
    bwd = lambda s: n - 1 - s
    grid_spec = pltpu.PrefetchScalarGridSpec(
        num_scalar_prefetch=2,
        grid=(n,),
        in_specs=specs(fwd) + specs(bwd),
        out_specs=[pl.BlockSpec((CHUNK, D_MLSTM), lambda s, rf, rb: (fwd(s), 0)),
                   pl.BlockSpec((CHUNK, D_MLSTM), lambda s, rf, rb: (bwd(s), 0))],
        scratch_shapes=[pltpu.VMEM((N_DIR, N_HEADS, HEAD_DIM, 2 * HEAD_DIM), F32)],
    )
    return pl.pallas_call(
        _mlstm_kernel,
        grid_spec=grid_spec,
        out_shape=[jax.ShapeDtypeStruct((t, D_MLSTM), BF16)] * 2,
        compiler_params=_cparams(),
        name="mlstm",
    )(reset_f, reset_b, q, kt, v, rowf, colf, q, kt, v, rowb, colb)


def _mixout_kernel(first_ref, last_ref,
                   x_ref, hf_ref, hb_ref, o_ref, u_ref, up_ref, un_ref, cw_ref, cb_ref, lng_ref, lnb_ref,
                   wout_ref, gffn_ref, wr_ref, br_ref, tri_ref,
                   x1_ref, hn_ref, meta_ref, metat_ref, cnt_ref,
                   win_ref, y_ref, run_ref):
    i = pl.program_id(0)
    tm = hf_ref.shape[0]

    @pl.when(i == 0)
    def _():
        run_ref[...] = jnp.zeros_like(run_ref)

    h_sum = hf_ref[...].astype(F32) + hb_ref[...].astype(F32)
    ym = (o_ref[...].astype(F32) * h_sum).astype(BF16)
    x1m = x_ref[...] + jnp.dot(ym, wout_ref[:D_MLSTM, :], preferred_element_type=F32)

    win_ref[0:CONV_HALO, :] = jnp.where(first_ref[i] == 1, 0.0, up_ref[...].astype(F32))
    win_ref[CONV_HALO:CONV_HALO + tm, :] = u_ref[...].astype(F32)
    win_ref[CONV_HALO + tm:, :] = jnp.where(last_ref[i] == 1, 0.0, un_ref[...].astype(F32))

    off0 = CONV_HALO - CONV_WIDTH // 2
    for r0 in range(0, tm, CONV_ROWS):
        tiles = []
        for lt in range(D_CONV // 128):
            ls = slice(lt * 128, (lt + 1) * 128)
            acc = jnp.broadcast_to(cb_ref[:, ls], (CONV_ROWS, 128))
            for s in range(8):
                part = None
                for j in range(CONV_WIDTH):
                    if (off0 + j) % 8 != s:
                        continue
                    base = (off0 + j) // 8 * 8
                    term = win_ref[r0 + base:r0 + base + CONV_ROWS + 8, ls] * cw_ref[j:j + 1, ls]
                    part = term if part is None else part + term
                acc = acc + part[s:s + CONV_ROWS, :]
            tiles.append(acc)
        cv = jnp.concatenate(tiles, axis=1)
        xc = cv - jnp.mean(cv, axis=-1, keepdims=True)
        yc = xc * lax.rsqrt(jnp.mean(xc * xc, axis=-1, keepdims=True) + EPS) * lng_ref[...] + lnb_ref[...]
        y_ref[r0:r0 + CONV_ROWS, :] = (yc * jax.nn.sigmoid(yc)).astype(BF16)

    x1 = x1m + jnp.dot(y_ref[...], wout_ref[D_MLSTM:, :], preferred_element_type=F32)
    x1_ref[...] = x1
    hn = _rms(x1, gffn_ref[...])
    hn_ref[...] = _pack_rows(hn)

    logits = _nt_dot(wr_ref[...], hn.astype(BF16)) + br_ref[...]
    row = lax.broadcasted_iota(jnp.int32, (ROUTER_ROWS, tm), 0).astype(F32)
    neg = -jnp.inf
    no_row = float(ROUTER_ROWS)
    gl = jnp.where(row < N_GROUPS, logits, neg)
    gmax = jnp.max(gl, axis=0, keepdims=True)
    p_top = 1.0 / jnp.sum(jnp.exp(gl - gmax), axis=0, keepdims=True)
    g_idx = jnp.min(jnp.where(gl == gmax, row, no_row), axis=0, keepdims=True)
    lo = EXPERT_ROW0 + EXPERTS_PER_GROUP * g_idx
    in_grp = (row >= lo) & (row < lo + EXPERTS_PER_GROUP)
    el = jnp.where(in_grp, logits, neg)
    ee = jnp.exp(el - jnp.max(el, axis=0, keepdims=True))
    pe = jnp.where(in_grp, ee / jnp.sum(ee, axis=0, keepdims=True), -1.0)
    v1 = jnp.max(pe, axis=0, keepdims=True)
    i1 = jnp.min(jnp.where(pe == v1, row, no_row), axis=0, keepdims=True)
    pe2 = jnp.where(row == i1, -1.0, pe)
    v2 = jnp.max(pe2, axis=0, keepdims=True)
    i2 = jnp.min(jnp.where(pe2 == v2, row, no_row), axis=0, keepdims=True)
    wsum = v1 + v2
    gate1 = p_top * (v1 / wsum)
    gate2 = p_top * (v2 / wsum)

    oh1 = (row == i1).astype(F32)
    oh2 = (row == i2).astype(F32)
    cnt = jnp.dot(jnp.concatenate([oh1, oh2], axis=0).astype(BF16), tri_ref[...], preferred_element_type=F32)
    run = run_ref[...]
    tot1 = jnp.sum(oh1, axis=1, keepdims=True)
    tot2 = jnp.sum(oh2, axis=1, keepdims=True)
    rank1 = jnp.sum(oh1 * (run + cnt[:ROUTER_ROWS]), axis=0, keepdims=True)
    rank2 = jnp.sum(oh2 * (run + tot1 + cnt[ROUTER_ROWS:]), axis=0, keepdims=True)
    run = run + tot1 + tot2
    run_ref[...] = run
    cnt_ref[...] = jnp.broadcast_to(run, cnt_ref.shape)

    mr = lax.broadcasted_iota(jnp.int32, (META_LANES, tm), 0)
    metat = jnp.where(mr == 0, i1 - EXPERT_ROW0, 0.0)
    metat = jnp.where(mr == 1, i2 - EXPERT_ROW0, metat)
    metat = jnp.where(mr == 2, gate1, metat)
    metat = jnp.where(mr == 3, gate2, metat)
    metat = jnp.where(mr == 4, rank1, metat)
    metat = jnp.where(mr == 5, rank2, metat)
    metat_ref[...] = metat
    meta_ref[...] = metat.T


def _mixout(x, tile0, hf, hb, o, u, seq_lens, conv_w, conv_b, ln_g, ln_b, w_out, g_ffn, w_rg, b_rg, w_re, b_re):
    t = x.shape[0]
    n_tiles = t // TM_MIX
    hpt = TM_MIX // CONV_HALO
    n_halo = u.shape[0] // CONV_HALO
    firsts, lasts, pos = [], [], 0
    for ln in seq_lens:
        assert ln % TM_MIX == 0
        firsts.append(pos // TM_MIX)
        lasts.append((pos + ln) // TM_MIX - 1)
        pos += ln
    assert pos == t
    first = jnp.zeros((n_tiles,), jnp.int32).at[jnp.array(firsts)].set(1)
    last = jnp.zeros((n_tiles,), jnp.int32).at[jnp.array(lasts)].set(1)

    cw = jnp.zeros((32, D_CONV), F32).at[:CONV_WIDTH].set(conv_w.reshape(CONV_WIDTH, D_CONV))
    wr = jnp.zeros((ROUTER_ROWS, D_MODEL), F32)
    wr = wr.at[:N_GROUPS].set(w_rg.T).at[EXPERT_ROW0:EXPERT_ROW0 + N_EXPERTS].set(w_re.T).astype(BF16)
    br = jnp.zeros((ROUTER_ROWS, 1), F32)
    br = br.at[:N_GROUPS, 0].set(b_rg).at[EXPERT_ROW0:EXPERT_ROW0 + N_EXPERTS, 0].set(b_re)
    tri = (lax.broadcasted_iota(jnp.int32, (TM_MIX, TM_MIX), 0)
           < lax.broadcasted_iota(jnp.int32, (TM_MIX, TM_MIX), 1)).astype(BF16)

    tok = lambda i, f, l: (i, 0)
    flat = lambda i, f, l: (tile0 + i, 0)
    fixed = lambda i, f, l: (0, 0)
    row = lambda n: pl.BlockSpec((1, n), fixed)
    half = pl.BlockSpec((TM_MIX, D_MLSTM), flat)
    grid_spec = pltpu.PrefetchScalarGridSpec(
        num_scalar_prefetch=2,
        grid=(n_tiles,),
        in_specs=[pl.BlockSpec((TM_MIX, D_MODEL), tok), half, half, half,
                  pl.BlockSpec((TM_MIX, D_CONV), flat),
                  pl.BlockSpec((CONV_HALO, D_CONV), lambda i, f, l: (jnp.maximum((tile0 + i) * hpt - 1, 0), 0)),
                  pl.BlockSpec((CONV_HALO, D_CONV),
                               lambda i, f, l: (jnp.minimum((tile0 + i + 1) * hpt, n_halo - 1), 0)),
                  pl.BlockSpec((32, D_CONV), fixed), row(D_CONV), row(D_CONV), row(D_CONV),
                  pl.BlockSpec((D_MODEL, D_MODEL), fixed), row(D_MODEL),
                  pl.BlockSpec((ROUTER_ROWS, D_MODEL), fixed), pl.BlockSpec((ROUTER_ROWS, 1), fixed),
                  pl.BlockSpec((TM_MIX, TM_MIX), fixed)],
        out_specs=[pl.BlockSpec((TM_MIX, D_MODEL), tok), pl.BlockSpec((TM_MIX, D_MODEL // 2), tok),
                   pl.BlockSpec((TM_MIX, META_LANES), tok),
                   pl.BlockSpec((META_LANES, TM_MIX), lambda i, f, l: (0, i)),
                   pl.BlockSpec((ROUTER_ROWS, 128), fixed)],
        scratch_shapes=[pltpu.VMEM((TM_MIX + 2 * CONV_HALO, D_CONV), F32),
                        pltpu.VMEM((TM_MIX, D_CONV), BF16),
                        pltpu.VMEM((ROUTER_ROWS, 1), F32)],
    )
    return pl.pallas_call(
        _mixout_kernel,
        grid_spec=grid_spec,
        out_shape=[jax.ShapeDtypeStruct((t, D_MODEL), F32), jax.ShapeDtypeStruct((t, D_MODEL // 2), jnp.int32),
                   jax.ShapeDtypeStruct((t, META_LANES), F32), jax.ShapeDtypeStruct((META_LANES, t), F32),
                   jax.ShapeDtypeStruct((ROUTER_ROWS, 128), F32)],
        compiler_params=_cparams(),
        name="mixout",
    )(first, last, x, hf, hb, o, u, u, u, cw, conv_b.reshape(1, D_CONV), ln_g.reshape(1, D_CONV),
      ln_b.reshape(1, D_CONV), w_out.astype(BF16), g_ffn.reshape(1, D_MODEL), wr, br, tri)


def _sc_mesh():
    return plsc.VectorSubcoreMesh(core_axis_name="c", subcore_axis_name="s")


def _sc_worker_base(per_worker):
    wid = lax.axis_index("s") * SC_CORES + lax.axis_index("c")
    return wid * per_worker


def _dispatch(hn, pos1, pos2, n_slots):
    t, d = hn.shape
    per_worker = t // SC_WORKERS
    assert per_worker * SC_WORKERS == t and per_worker % SC_ROWS == 0

    def body(h_hbm, p1_hbm, p2_hbm, out_hbm, i1_v, i2_v, rows_v, sem1, sem2):
        base0 = _sc_worker_base(per_worker)

        @pl.loop(0, per_worker // SC_ROWS)
        def _(c):
            base = pl.multiple_of(base0 + c * SC_ROWS, SC_ROWS)
            pltpu.sync_copy(p1_hbm.at[pl.ds(base, SC_ROWS)], i1_v)
            pltpu.sync_copy(p2_hbm.at[pl.ds(base, SC_ROWS)], i2_v)
            pltpu.sync_copy(h_hbm.at[pl.ds(base, SC_ROWS)], rows_v)
            c1 = pltpu.async_copy(rows_v, out_hbm.at[i1_v], sem1)
            c2 = pltpu.async_copy(rows_v, out_hbm.at[i2_v], sem2)
            c1.wait()
            c2.wait()

    return pl.kernel(
        body,
        out_type=jax.ShapeDtypeStruct((n_slots, d), hn.dtype),
        mesh=_sc_mesh(),
        scratch_types=[pltpu.VMEM((SC_ROWS,), jnp.int32), pltpu.VMEM((SC_ROWS,), jnp.int32),
                       pltpu.VMEM((SC_ROWS, d), hn.dtype), pltpu.SemaphoreType.DMA, pltpu.SemaphoreType.DMA],
        name="dispatch",
    )(hn, pos1, pos2)


def _collect(ys, pos1, pos2):
    t = pos1.shape[0]
    d = ys.shape[1]
    per_worker = t // SC_WORKERS
    assert per_worker * SC_WORKERS == t and per_worker % SC_ROWS == 0

    def body(ys_hbm, p1_hbm, p2_hbm, y1_hbm, y2_hbm, i_v, rows_v, sem):
        base0 = _sc_worker_base(per_worker)

        @pl.loop(0, per_worker // SC_ROWS)
        def _(c):
            base = pl.multiple_of(base0 + c * SC_ROWS, SC_ROWS)
            for p_hbm, y_hbm in ((p1_hbm, y1_hbm), (p2_hbm, y2_hbm)):
                pltpu.sync_copy(p_hbm.at[pl.ds(base, SC_ROWS)], i_v)
                pltpu.async_copy(ys_hbm.at[i_v], rows_v, sem).wait()
                pltpu.sync_copy(rows_v, y_hbm.at[pl.ds(base, SC_ROWS)])

    out = jax.ShapeDtypeStruct((t, d), ys.dtype)
    return pl.kernel(
        body,
        out_type=(out, out),
        mesh=_sc_mesh(),
        scratch_types=[pltpu.VMEM((SC_ROWS,), jnp.int32), pltpu.VMEM((SC_ROWS, d), ys.dtype),
                       pltpu.SemaphoreType.DMA],
        name="collect",
    )(ys, pos1, pos2)


def _experts_kernel(te_ref, rows_ref, hs_ref, wg_ref, wu_ref, wd_ref, ys_ref, wgb_ref, wub_ref, wdb_ref):
    i = pl.program_id(0)

    @pl.when((i == 0) | (te_ref[i] != te_ref[jnp.maximum(i - 1, 0)]))
    def _():
        wgb_ref[...] = wg_ref[...].astype(BF16)
        wub_ref[...] = wu_ref[...].astype(BF16)
        wdb_ref[...] = wd_ref[...].astype(BF16)

    @pl.when(rows_ref[i] > 0)
    def _():
        x = _unpack_rows(hs_ref[...]).astype(BF16)
        hg = jnp.dot(x, wgb_ref[...], preferred_element_type=F32)
        hu = jnp.dot(x, wub_ref[...], preferred_element_type=F32)
        hid = (hg * jax.nn.sigmoid(hg) * hu).astype(BF16)
        ys_ref[...] = _pack_rows(jnp.dot(hid, wdb_ref[...], preferred_element_type=F32))

    @pl.when(rows_ref[i] == 0)
    def _():
        ys_ref[...] = jnp.zeros_like(ys_ref)


def _experts(hs, tile_expert, tile_rows, w_gate, w_up, w_down):
    n_tiles = tile_expert.shape[0]
    wsel = lambda i, te, rows: (te[i], 0, 0)
    slot = lambda i, te, rows: (i, 0)
    grid_spec = pltpu.PrefetchScalarGridSpec(
        num_scalar_prefetch=2,
        grid=(n_tiles,),
        in_specs=[pl.BlockSpec((TM_EXP, D_MODEL // 2), slot),
                  pl.BlockSpec((None, D_MODEL, D_EXPERT), wsel),
                  pl.BlockSpec((None, D_MODEL, D_EXPERT), wsel),
                  pl.BlockSpec((None, D_EXPERT, D_MODEL), wsel)],
        out_specs=pl.BlockSpec((TM_EXP, D_MODEL // 2), slot),
        scratch_shapes=[pltpu.VMEM((D_MODEL, D_EXPERT), BF16), pltpu.VMEM((D_MODEL, D_EXPERT), BF16),
                        pltpu.VMEM((D_EXPERT, D_MODEL), BF16)],
    )
    return pl.pallas_call(
        _experts_kernel,
        grid_spec=grid_spec,
        out_shape=jax.ShapeDtypeStruct((n_tiles * TM_EXP, D_MODEL // 2), jnp.int32),
        compiler_params=_cparams(),
        name="experts",
    )(tile_expert, tile_rows, hs, w_gate, w_up, w_down)


def _combine_kernel(x1_ref, meta_ref, y1_ref, y2_ref, gfin_ref, out_ref):
    meta = meta_ref[...]
    x2 = x1_ref[...] + meta[:, 2:3] * _unpack_rows(y1_ref[...]) + meta[:, 3:4] * _unpack_rows(y2_ref[...])
    out_ref[...] = _rms(x2, gfin_ref[...])


def _combine(x1, meta, y1, y2, g_final):
    t = x1.shape[0]
    assert t % TM_OUT == 0
    tok = pl.BlockSpec((TM_OUT, D_MODEL), lambda i: (i, 0))
    packed = pl.BlockSpec((TM_OUT, D_MODEL // 2), lambda i: (i, 0))
    return pl.pallas_call(
        _combine_kernel,
        grid=(t // TM_OUT,),
        in_specs=[tok, pl.BlockSpec((TM_OUT, META_LANES), lambda i: (i, 0)), packed, packed,
                  pl.BlockSpec((1, D_MODEL), lambda i: (0, 0))],
        out_specs=tok,
        out_shape=jax.ShapeDtypeStruct((t, D_MODEL), F32),
        compiler_params=_cparams(),
        name="combine",
    )(x1, meta, y1, y2, g_final.reshape(1, D_MODEL))


def _layout_kernel(cnt_ref, metat_ref, pos_ref, tiles_ref):
    n_lanes = tiles_ref.shape[1]
    tile_start = lax.broadcasted_iota(jnp.int32, (1, n_lanes), 1).astype(F32) * TM_EXP
    m = metat_ref[...]
    eid = m[0:2]
    p = m[4:6]
    off = jnp.zeros((1, 1), F32)
    t_exp = jnp.full((1, n_lanes), N_EXPERTS - 1.0, F32)
    t_rows = jnp.zeros((1, n_lanes), F32)
    for e in range(N_EXPERTS):
        cnt = cnt_ref[EXPERT_ROW0 + e:EXPERT_ROW0 + e + 1, 0:1]
        padded = jnp.ceil(cnt * (1.0 / TM_EXP)) * TM_EXP
        p = p + jnp.where(eid == e, off, 0.0)
        mine = (tile_start >= off) & (tile_start < off + padded)
        t_exp = jnp.where(mine, float(e), t_exp)
        t_rows = jnp.where(mine, jnp.clip(cnt - (tile_start - off), 0.0, float(TM_EXP)), t_rows)
        off = off + padded
    pos_ref[...] = p.astype(jnp.int32)
    tiles_ref[...] = jnp.concatenate([t_exp, t_rows], axis=0).astype(jnp.int32)


def _slot_layout(metat, counts_rows, n_tiles):
    t = metat.shape[1]
    tp = min(TM_POS, t)
    assert t % tp == 0
    n_lanes = (n_tiles + 127) // 128 * 128
    pos, tiles = pl.pallas_call(
        _layout_kernel,
        grid=(t // tp,),
        in_specs=[pl.BlockSpec(counts_rows.shape, lambda i: (0, 0)),
                  pl.BlockSpec((META_LANES, tp), lambda i: (0, i))],
        out_specs=[pl.BlockSpec((2, tp), lambda i: (0, i)), pl.BlockSpec((2, n_lanes), lambda i: (0, 0))],
        out_shape=[jax.ShapeDtypeStruct((2, t), jnp.int32), jax.ShapeDtypeStruct((2, n_lanes), jnp.int32)],
        compiler_params=_cparams(),
        name="layout",
    )(counts_rows, metat)
    return pos[0], pos[1], tiles[0, :n_tiles], tiles[1, :n_tiles]


def kernel(x_prompt, x_sample, g_mix, w_in, b_gate, conv_w, conv_b, ln_g, ln_b, w_out, g_ffn,
           w_router_group, b_router_group, w_router_expert, b_router_expert, w_gate, w_up, w_down, g_final):
    assert g_mix.shape[0] == 1, "one layer"
    bp, lp, _ = x_prompt.shape
    bs, ls, _ = x_sample.shape
    seq_lens = [lp] * bp + [ls] * bs
    tp, ts = bp * lp, bs * ls
    xp = x_prompt.reshape(tp, D_MODEL)
    xs = x_sample.reshape(ts, D_MODEL)

    q, v, o, u, kt, gr = _inproj(xp, xs, g_mix[0], w_in[0], b_gate[0])
    hf, hb = _mlstm(q, kt, v, gr, seq_lens)

    outs = {}
    groups = (("prompt", xp, 0, [lp] * bp), ("sample", xs, tp // TM_MIX, [ls] * bs))
    for name, x, tile0, lens in sorted(groups, key=lambda g: -g[1].shape[0]):
        tg = x.shape[0]
        x1, hn, meta, metat, counts = _mixout(x, tile0, hf, hb, o, u, lens, conv_w[0], conv_b[0], ln_g[0], ln_b[0],
                                              w_out[0], g_ffn[0], w_router_group[0], b_router_group[0],
                                              w_router_expert[0], b_router_expert[0])
        n_tiles = (2 * tg + N_EXPERTS * (TM_EXP - 1)) // TM_EXP + 1
        pos1, pos2, tile_expert, tile_rows = _slot_layout(metat, counts, n_tiles)
        hs = _dispatch(hn, pos1, pos2, n_tiles * TM_EXP)
        ys = _experts(hs, tile_expert, tile_rows, w_gate[0], w_up[0], w_down[0])
        y1, y2 = _collect(ys, pos1, pos2)
        outs[name] = _combine(x1, meta, y1, y2, g_final)
    return outs["prompt"].reshape(bp, lp, D_MODEL), outs["sample"].reshape(bs, ls, D_MODEL)
```

```python
import functools

import jax
import jax.numpy as jnp
from jax import lax
from jax.experimental import pallas as pl
from jax.experimental.pallas import tpu as pltpu
from jax.experimental.pallas import tpu_sc as plsc

F32 = jnp.float32
BF16 = jnp.bfloat16

D_MODEL = 1024
N_HEADS = 4
HEAD_DIM = 128
D_MLSTM = N_HEADS * HEAD_DIM
D_CONV = D_MODEL - D_MLSTM
CONV_WIDTH = 31
CONV_HALO = 16
N_DIR = 2
N_GROUPS = 4
EXPERTS_PER_GROUP = 4
N_EXPERTS = N_GROUPS * EXPERTS_PER_GROUP
D_EXPERT = 512
EPS = 1e-6
K_SCALE = HEAD_DIM ** -0.5

GATE_ROWS = 16
QROWS = 24
ROUTER_ROWS = 32
EXPERT_ROW0 = N_GROUPS
META_LANES = 8

TM_IN = 1024
CHUNK = 512
GP_CHUNKS = 32
TM_MIX = 1024
CONV_ROWS = 256
TM_EXP = 512
TM_OUT = 1024
TM_POS = 8192
VMEM_LIMIT = 48 * 1024 * 1024

SC_CORES = 2
SC_SUBCORES = 16
SC_WORKERS = SC_CORES * SC_SUBCORES
SC_ROWS = 128


def _cparams(n_axes=1):
    return pltpu.CompilerParams(dimension_semantics=("arbitrary",) * n_axes,
                                vmem_limit_bytes=VMEM_LIMIT)


def _nt_dot(a, b):
    return lax.dot_general(a, b, (((1,), (1,)), ((), ())), preferred_element_type=F32)


def _rms(x, g):
    return x * lax.rsqrt(jnp.mean(x * x, axis=-1, keepdims=True) + EPS) * g


def _pack_rows(x):
    n = x.shape[1] // 2
    hi = lax.bitcast_convert_type(x[:, :n].astype(jnp.bfloat16).astype(F32), jnp.int32)
    lo = lax.bitcast_convert_type(x[:, n:].astype(jnp.bfloat16).astype(F32), jnp.int32)
    return hi | lax.shift_right_logical(lo, 16)


def _unpack_rows(p):
    hi = lax.bitcast_convert_type(p & jnp.int32(-65536), F32)
    lo = lax.bitcast_convert_type(lax.shift_left(p, 16), F32)
    return jnp.concatenate([hi, lo], axis=1)


def _two_batch_specs(block, n_first, n_second):
    first = pl.BlockSpec(block, lambda i, *_: (jnp.minimum(i, n_first - 1), 0))
    second = pl.BlockSpec(block, lambda i, *_: (jnp.maximum(i - n_first, 0), 0))
    return first, second


def _inproj_kernel(xp_ref, xs_ref, g_ref, wq_ref, wv_ref, wo_ref, wa_ref, wb_ref, wkg_ref, bg_ref,
                   q_ref, v_ref, o_ref, u_ref, kt_ref, gr_ref, *, n_first):
    x = jnp.where(pl.program_id(0) < n_first, xp_ref[...], xs_ref[...])
    xn = _rms(x, g_ref[...]).astype(BF16)
    q_ref[...] = jnp.dot(xn, wq_ref[...], preferred_element_type=F32).astype(BF16)
    v_ref[...] = jnp.dot(xn, wv_ref[...], preferred_element_type=F32).astype(BF16)
    o_ref[...] = jax.nn.sigmoid(jnp.dot(xn, wo_ref[...], preferred_element_type=F32)).astype(BF16)
    a = jnp.dot(xn, wa_ref[...], preferred_element_type=F32)
    b = jnp.dot(xn, wb_ref[...], preferred_element_type=F32)
    u_ref[...] = (a * jax.nn.sigmoid(b)).astype(BF16)
    kg = _nt_dot(wkg_ref[...], xn)
    kt_ref[...] = (kg[:D_MLSTM] * K_SCALE).astype(BF16)
    gr = kg[D_MLSTM:] + bg_ref[...]
    for c in range(gr_ref.shape[0]):
        gr_ref[c] = gr[:, c * CHUNK:(c + 1) * CHUNK]


def _inproj(xp, xs, g_mix, w_in, b_gate):
    tp, ts = xp.shape[0], xs.shape[0]
    t = tp + ts
    assert tp % TM_IN == 0 and ts % TM_IN == 0
    off_k, off_v, off_o, off_g = D_MLSTM, 2 * D_MLSTM, 3 * D_MLSTM, 4 * D_MLSTM
    off_a = off_g + 2 * N_DIR * N_HEADS
    off_b = off_a + D_CONV
    wq = w_in[:, 0:off_k].astype(BF16)
    wkt = w_in[:, off_k:off_v].T.astype(BF16)
    wv = w_in[:, off_v:off_o].astype(BF16)
    wo = w_in[:, off_o:off_g].astype(BF16)
    wa = w_in[:, off_a:off_b].astype(BF16)
    wb = w_in[:, off_b:off_b + D_CONV].astype(BF16)
    wg = w_in[:, off_g:off_a].T.reshape(N_DIR, 2, N_HEADS, D_MODEL)
    wgt = jnp.zeros((N_DIR, 2, GATE_ROWS // 2, D_MODEL), F32).at[:, :, :N_HEADS].set(wg)
    wkg = jnp.concatenate([wkt, wgt.reshape(N_DIR * GATE_ROWS, D_MODEL).astype(BF16)], axis=0)
    bg = jnp.zeros((N_DIR, 2, GATE_ROWS // 2), F32).at[:, :, :N_HEADS].set(
        b_gate.reshape(N_DIR, 2, N_HEADS)).reshape(N_DIR * GATE_ROWS, 1)

    tok = lambda i: (i, 0)
    fixed = lambda i: (0, 0)
    wspec = pl.BlockSpec((D_MODEL, D_MLSTM), fixed)
    xp_spec, xs_spec = _two_batch_specs((TM_IN, D_MODEL), tp // TM_IN, ts // TM_IN)
    cpt = TM_IN // CHUNK
    return pl.pallas_call(
        functools.partial(_inproj_kernel, n_first=tp // TM_IN),
        grid=(t // TM_IN,),
        in_specs=[xp_spec, xs_spec, pl.BlockSpec((1, D_MODEL), fixed),
                  wspec, wspec, wspec, wspec, wspec,
                  pl.BlockSpec((D_MLSTM + N_DIR * GATE_ROWS, D_MODEL), fixed),
                  pl.BlockSpec((N_DIR * GATE_ROWS, 1), fixed)],
        out_specs=[pl.BlockSpec((TM_IN, D_MLSTM), tok)] * 4 + [
            pl.BlockSpec((D_MLSTM, TM_IN), lambda i: (0, i)),
            pl.BlockSpec((cpt, N_DIR * GATE_ROWS, CHUNK), lambda i: (i, 0, 0))],
        out_shape=[jax.ShapeDtypeStruct((t, D_MLSTM), BF16)] * 4 + [
            jax.ShapeDtypeStruct((D_MLSTM, t), BF16),
            jax.ShapeDtypeStruct((t // CHUNK, N_DIR * GATE_ROWS, CHUNK), F32)],
        compiler_params=_cparams(),
        name="inproj",
    )(xp, xs, g_mix.reshape(1, D_MODEL), wq, wv, wo, wa, wb, wkg, bg)


def _log_sigmoid(x):
    return jnp.minimum(x, 0.0) - jnp.log1p(jnp.exp(-jnp.abs(x)))


def _gateprep_kernel(reset_ref, g_ref, rowq_ref, colq_ref, m_ref, *, rev):
    n, _, c = g_ref.shape
    step = pl.program_id(0)
    blk = pl.num_programs(0) - 1 - step if rev else step

    @pl.when(step == 0)
    def _():
        m_ref[...] = jnp.zeros_like(m_ref)

    ig = g_ref[:, 0:8, :]
    lf = _log_sigmoid(g_ref[:, 8:16, :])
    lane = lax.broadcasted_iota(jnp.int32, (n, 8, c), 2)

    def scan(x, op, ident):
        k = 1
        while k < c:
            if rev:
                shifted, valid = pltpu.roll(x, c - k, axis=2), lane < c - k
            else:
                shifted, valid = pltpu.roll(x, k, axis=2), lane >= k
            x = op(x, jnp.where(valid, shifted, ident))
            k *= 2
        return x

    bc = scan(lf, jnp.add, 0.0)
    a = ig - bc
    cm = scan(a, jnp.maximum, -jnp.inf)
    b_tot = jnp.sum(lf, axis=2, keepdims=True)
    a_max = jnp.max(a, axis=2, keepdims=True)

    m = m_ref[...]
    m_in = [None] * n
    for j in (range(n - 1, -1, -1) if rev else range(n)):
        m = jnp.where(reset_ref[blk * n + j] == 1, 0.0, m)
        m_in[j] = m
        m = b_tot[j] + jnp.maximum(m, a_max[j])
    m_ref[...] = m
    m_old = jnp.stack(m_in)

    mx = jnp.maximum(m_old, cm)
    mx_last = jnp.maximum(m_old, a_max)
    rowq_ref[:, 0:8, :] = a
    rowq_ref[:, 8:16, :] = jnp.exp(a - mx_last)
    rowq_ref[:, 16:24, :] = jnp.exp(m_old - mx_last)
    e1 = jnp.exp(m_old - mx)
    fl = jnp.exp(-(mx + bc))
    for j in range(n):
        colq_ref[j] = jnp.concatenate([mx[j], e1[j], fl[j]], axis=0).T


def _gateprep(gr, reset, rev):
    n_chunks = gr.shape[0]
    gp = min(GP_CHUNKS, n_chunks)
    assert n_chunks % gp == 0
    nb = n_chunks // gp
    d = 1 if rev else 0
    bidx = (lambda s: nb - 1 - s) if rev else (lambda s: s)
    grid_spec = pltpu.PrefetchScalarGridSpec(
        num_scalar_prefetch=1,
        grid=(nb,),
        in_specs=[pl.BlockSpec((gp, GATE_ROWS, CHUNK), lambda s, r: (bidx(s), d, 0))],
        out_specs=[pl.BlockSpec((gp, QROWS, CHUNK), lambda s, r: (bidx(s), 0, 0)),
                   pl.BlockSpec((gp, CHUNK, QROWS), lambda s, r: (bidx(s), 0, 0))],
        scratch_shapes=[pltpu.VMEM((8, CHUNK), F32)],
    )
    return pl.pallas_call(
        functools.partial(_gateprep_kernel, rev=rev),
        grid_spec=grid_spec,
        out_shape=[jax.ShapeDtypeStruct((n_chunks, QROWS, CHUNK), F32),
                   jax.ShapeDtypeStruct((n_chunks, CHUNK, QROWS), F32)],
        compiler_params=_cparams(),
        name="gateprep_bwd" if rev else "gateprep_fwd",
    )(reset, gr)


def _mlstm_kernel(rf_ref, rb_ref,
                  qf_ref, ktf_ref, vf_ref, rowf_ref, colf_ref,
                  qb_ref, ktb_ref, vb_ref, rowb_ref, colb_ref,
                  hf_ref, hb_ref, cst_ref):
    c = qf_ref.shape[0]
    step = pl.program_id(0)
    last = pl.num_programs(0) - 1
    row_i = lax.broadcasted_iota(jnp.int32, (c, c), 0)
    col_i = lax.broadcasted_iota(jnp.int32, (c, c), 1)
    ones = jnp.ones((c, HEAD_DIM), BF16)

    dirs = ((rf_ref[step], col_i <= row_i, qf_ref, ktf_ref, vf_ref, rowf_ref, colf_ref, hf_ref),
            (rb_ref[last - step], col_i >= row_i, qb_ref, ktb_ref, vb_ref, rowb_ref, colb_ref, hb_ref))
    for d, (reset, mask, q_ref, kt_ref, v_ref, row_ref, col_ref, h_ref) in enumerate(dirs):
        @pl.when(reset == 1)
        def _():
            cst_ref[d] = jnp.zeros(cst_ref.shape[1:], F32)

        rowq = row_ref[...]
        colq = col_ref[...]
        for h in range(N_HEADS):
            hs = slice(h * HEAD_DIM, (h + 1) * HEAD_DIM)
            qh = q_ref[:, hs]
            kth = kt_ref[hs, :]
            vext = jnp.concatenate([v_ref[:, hs], ones], axis=1)
            s = jnp.dot(qh, kth, preferred_element_type=F32)
            e = jnp.exp(jnp.where(mask, rowq[h:h + 1, :] - colq[:, h:h + 1], -jnp.inf))
            r1 = jnp.dot((s * e).astype(BF16), vext, preferred_element_type=F32)
            cst = cst_ref[d, h]
            r2 = jnp.dot(qh, cst.astype(BF16), preferred_element_type=F32)
            e1 = colq[:, 8 + h:9 + h]
            num = r1[:, :HEAD_DIM] + e1 * r2[:, :HEAD_DIM]
            den = r1[:, HEAD_DIM:] + e1 * r2[:, HEAD_DIM:]
            h_ref[:, hs] = (num / jnp.maximum(jnp.abs(den), colq[:, 16 + h:17 + h])).astype(h_ref.dtype)
            kw = (kth.astype(F32) * rowq[8 + h:9 + h, :]).astype(BF16)
            cst_ref[d, h] = rowq[16 + h:17 + h, 0:1] * cst + jnp.dot(kw, vext, preferred_element_type=F32)


def _mlstm(q, kt, v, gr, seq_lens):
    t = q.shape[0]
    n = t // CHUNK
    starts, ends, pos = [], [], 0
    for ln in seq_lens:
        assert ln % CHUNK == 0
        starts.append(pos // CHUNK)
        ends.append((pos + ln) // CHUNK - 1)
        pos += ln
    reset_f = jnp.zeros((n,), jnp.int32).at[jnp.array(starts)].set(1)
    reset_b = jnp.zeros((n,), jnp.int32).at[jnp.array(ends)].set(1)
    rowf, colf = _gateprep(gr, reset_f, rev=False)
    rowb, colb = _gateprep(gr, reset_b, rev=True)

    def specs(cidx):
        return [pl.BlockSpec((CHUNK, D_MLSTM), lambda s, rf, rb: (cidx(s), 0)),
                pl.BlockSpec((D_MLSTM, CHUNK), lambda s, rf, rb: (0, cidx(s))),
                pl.BlockSpec((CHUNK, D_MLSTM), lambda s, rf, rb: (cidx(s), 0)),
                pl.BlockSpec((None, QROWS, CHUNK), lambda s, rf, rb: (cidx(s), 0, 0)),
                pl.BlockSpec((None, CHUNK, QROWS), lambda s, rf, rb: (cidx(s), 0, 0))]

    fwd = lambda s: s
    bwd = lambda s: n - 1 - s
    grid_spec = pltpu.PrefetchScalarGridSpec(
        num_scalar_prefetch=2,
        grid=(n,),
        in_specs=specs(fwd) + specs(bwd),
        out_specs=[pl.BlockSpec((CHUNK, D_MLSTM), lambda s, rf, rb: (fwd(s), 0)),
                   pl.BlockSpec((CHUNK, D_MLSTM), lambda s, rf, rb: (bwd(s), 0))],
        scratch_shapes=[pltpu.VMEM((N_DIR, N_HEADS, HEAD_DIM, 2 * HEAD_DIM), F32)],
    )
    return pl.pallas_call(
        _mlstm_kernel,
        grid_spec=grid_spec,
        out_shape=[jax.ShapeDtypeStruct((t, D_MLSTM), BF16)] * 2,
        compiler_params=_cparams(),
        name="mlstm",
    )(reset_f, reset_b, q, kt, v, rowf, colf, q, kt, v, rowb, colb)


def _mixout_kernel(first_ref, last_ref,
                   x_ref, hf_ref, hb_ref, o_ref, u_ref, up_ref, un_ref, cw_ref, cb_ref, lng_ref, lnb_ref,
                   wout_ref, gffn_ref, wr_ref, br_ref, tri_ref,
                   x1_ref, hn_ref, metat_ref, cnt_ref,
                   win_ref, y_ref, run_ref):
    i = pl.program_id(0)
    tm = hf_ref.shape[0]

    @pl.when(i == 0)
    def _():
        run_ref[...] = jnp.zeros_like(run_ref)

    h_sum = hf_ref[...].astype(F32) + hb_ref[...].astype(F32)
    ym = (o_ref[...].astype(F32) * h_sum).astype(BF16)
    x1m = x_ref[...] + jnp.dot(ym, wout_ref[:D_MLSTM, :], preferred_element_type=F32)

    win_ref[0:CONV_HALO, :] = jnp.where(first_ref[i] == 1, 0.0, up_ref[...].astype(F32))
    win_ref[CONV_HALO:CONV_HALO + tm, :] = u_ref[...].astype(F32)
    win_ref[CONV_HALO + tm:, :] = jnp.where(last_ref[i] == 1, 0.0, un_ref[...].astype(F32))

    off0 = CONV_HALO - CONV_WIDTH // 2
    for r0 in range(0, tm, CONV_ROWS):
        tiles = []
        for lt in range(D_CONV // 128):
            ls = slice(lt * 128, (lt + 1) * 128)
            acc = jnp.broadcast_to(cb_ref[:, ls], (CONV_ROWS, 128))
            for s in range(8):
                part = None
                for j in range(CONV_WIDTH):
                    if (off0 + j) % 8 != s:
                        continue
                    base = (off0 + j) // 8 * 8
                    term = win_ref[r0 + base:r0 + base + CONV_ROWS + 8, ls] * cw_ref[j:j + 1, ls]
                    part = term if part is None else part + term
                acc = acc + part[s:s + CONV_ROWS, :]
            tiles.append(acc)
        cv = jnp.concatenate(tiles, axis=1)
        xc = cv - jnp.mean(cv, axis=-1, keepdims=True)
        yc = xc * lax.rsqrt(jnp.mean(xc * xc, axis=-1, keepdims=True) + EPS) * lng_ref[...] + lnb_ref[...]
        y_ref[r0:r0 + CONV_ROWS, :] = (yc * jax.nn.sigmoid(yc)).astype(BF16)

    x1 = x1m + jnp.dot(y_ref[...], wout_ref[D_MLSTM:, :], preferred_element_type=F32)
    x1_ref[...] = x1
    hn = _rms(x1, gffn_ref[...])
    hn_ref[...] = _pack_rows(hn)

    logits = _nt_dot(wr_ref[...], hn.astype(BF16)) + br_ref[...]
    row = lax.broadcasted_iota(jnp.int32, (ROUTER_ROWS, tm), 0).astype(F32)
    neg = -jnp.inf
    no_row = float(ROUTER_ROWS)
    gl = jnp.where(row < N_GROUPS, logits, neg)
    gmax = jnp.max(gl, axis=0, keepdims=True)
    p_top = 1.0 / jnp.sum(jnp.exp(gl - gmax), axis=0, keepdims=True)
    g_idx = jnp.min(jnp.where(gl == gmax, row, no_row), axis=0, keepdims=True)
    lo = EXPERT_ROW0 + EXPERTS_PER_GROUP * g_idx
    in_grp = (row >= lo) & (row < lo + EXPERTS_PER_GROUP)
    el = jnp.where(in_grp, logits, neg)
    ee = jnp.exp(el - jnp.max(el, axis=0, keepdims=True))
    pe = jnp.where(in_grp, ee / jnp.sum(ee, axis=0, keepdims=True), -1.0)
    v1 = jnp.max(pe, axis=0, keepdims=True)
    i1 = jnp.min(jnp.where(pe == v1, row, no_row), axis=0, keepdims=True)
    pe2 = jnp.where(row == i1, -1.0, pe)
    v2 = jnp.max(pe2, axis=0, keepdims=True)
    i2 = jnp.min(jnp.where(pe2 == v2, row, no_row), axis=0, keepdims=True)
    wsum = v1 + v2
    gate1 = p_top * (v1 / wsum)
    gate2 = p_top * (v2 / wsum)

    oh1 = (row == i1).astype(F32)
    oh2 = (row == i2).astype(F32)
    cnt = jnp.dot(jnp.concatenate([oh1, oh2], axis=0).astype(BF16), tri_ref[...], preferred_element_type=F32)
    run = run_ref[...]
    tot1 = jnp.sum(oh1, axis=1, keepdims=True)
    tot2 = jnp.sum(oh2, axis=1, keepdims=True)
    rank1 = jnp.sum(oh1 * (run + cnt[:ROUTER_ROWS]), axis=0, keepdims=True)
    rank2 = jnp.sum(oh2 * (run + tot1 + cnt[ROUTER_ROWS:]), axis=0, keepdims=True)
    run = run + tot1 + tot2
    run_ref[...] = run
    cnt_ref[...] = jnp.broadcast_to(run, cnt_ref.shape)

    mr = lax.broadcasted_iota(jnp.int32, (META_LANES, tm), 0)
    metat = jnp.where(mr == 0, i1 - EXPERT_ROW0, 0.0)
    metat = jnp.where(mr == 1, i2 - EXPERT_ROW0, metat)
    metat = jnp.where(mr == 2, gate1, metat)
    metat = jnp.where(mr == 3, gate2, metat)
    metat = jnp.where(mr == 4, rank1, metat)
    metat = jnp.where(mr == 5, rank2, metat)
    metat_ref[...] = metat


def _mixout(x, tile0, hf, hb, o, u, seq_lens, conv_w, conv_b, ln_g, ln_b, w_out, g_ffn, w_rg, b_rg, w_re, b_re):
    t = x.shape[0]
    n_tiles = t // TM_MIX
    hpt = TM_MIX // CONV_HALO
    n_halo = u.shape[0] // CONV_HALO
    firsts, lasts, pos = [], [], 0
    for ln in seq_lens:
        assert ln % TM_MIX == 0
        firsts.append(pos // TM_MIX)
        lasts.append((pos + ln) // TM_MIX - 1)
        pos += ln
    assert pos == t
    first = jnp.zeros((n_tiles,), jnp.int32).at[jnp.array(firsts)].set(1)
    last = jnp.zeros((n_tiles,), jnp.int32).at[jnp.array(lasts)].set(1)

    cw = jnp.zeros((32, D_CONV), F32).at[:CONV_WIDTH].set(conv_w.reshape(CONV_WIDTH, D_CONV))
    wr = jnp.zeros((ROUTER_ROWS, D_MODEL), F32)
    wr = wr.at[:N_GROUPS].set(w_rg.T).at[EXPERT_ROW0:EXPERT_ROW0 + N_EXPERTS].set(w_re.T).astype(BF16)
    br = jnp.zeros((ROUTER_ROWS, 1), F32)
    br = br.at[:N_GROUPS, 0].set(b_rg).at[EXPERT_ROW0:EXPERT_ROW0 + N_EXPERTS, 0].set(b_re)
    tri = (lax.broadcasted_iota(jnp.int32, (TM_MIX, TM_MIX), 0)
           < lax.broadcasted_iota(jnp.int32, (TM_MIX, TM_MIX), 1)).astype(BF16)

    tok = lambda i, f, l: (i, 0)
    flat = lambda i, f, l: (tile0 + i, 0)
    fixed = lambda i, f, l: (0, 0)
    row = lambda n: pl.BlockSpec((1, n), fixed)
    half = pl.BlockSpec((TM_MIX, D_MLSTM), flat)
    grid_spec = pltpu.PrefetchScalarGridSpec(
        num_scalar_prefetch=2,
        grid=(n_tiles,),
        in_specs=[pl.BlockSpec((TM_MIX, D_MODEL), tok), half, half, half,
                  pl.BlockSpec((TM_MIX, D_CONV), flat),
                  pl.BlockSpec((CONV_HALO, D_CONV), lambda i, f, l: (jnp.maximum((tile0 + i) * hpt - 1, 0), 0)),
                  pl.BlockSpec((CONV_HALO, D_CONV),
                               lambda i, f, l: (jnp.minimum((tile0 + i + 1) * hpt, n_halo - 1), 0)),
                  pl.BlockSpec((32, D_CONV), fixed), row(D_CONV), row(D_CONV), row(D_CONV),
                  pl.BlockSpec((D_MODEL, D_MODEL), fixed), row(D_MODEL),
                  pl.BlockSpec((ROUTER_ROWS, D_MODEL), fixed), pl.BlockSpec((ROUTER_ROWS, 1), fixed),
                  pl.BlockSpec((TM_MIX, TM_MIX), fixed)],
        out_specs=[pl.BlockSpec((TM_MIX, D_MODEL), tok), pl.BlockSpec((TM_MIX, D_MODEL // 2), tok),
                   pl.BlockSpec((META_LANES, TM_MIX), lambda i, f, l: (0, i)),
                   pl.BlockSpec((ROUTER_ROWS, 128), fixed)],
        scratch_shapes=[pltpu.VMEM((TM_MIX + 2 * CONV_HALO, D_CONV), F32),
                        pltpu.VMEM((TM_MIX, D_CONV), BF16),
                        pltpu.VMEM((ROUTER_ROWS, 1), F32)],
    )
    return pl.pallas_call(
        _mixout_kernel,
        grid_spec=grid_spec,
        out_shape=[jax.ShapeDtypeStruct((t, D_MODEL), F32), jax.ShapeDtypeStruct((t, D_MODEL // 2), jnp.int32),
                   jax.ShapeDtypeStruct((META_LANES, t), F32),
                   jax.ShapeDtypeStruct((ROUTER_ROWS, 128), F32)],
        compiler_params=_cparams(),
        name="mixout",
    )(first, last, x, hf, hb, o, u, u, u, cw, conv_b.reshape(1, D_CONV), ln_g.reshape(1, D_CONV),
      ln_b.reshape(1, D_CONV), w_out.astype(BF16), g_ffn.reshape(1, D_MODEL), wr, br, tri)


def _sc_mesh():
    return plsc.VectorSubcoreMesh(core_axis_name="c", subcore_axis_name="s")


def _sc_worker_base(per_worker):
    wid = lax.axis_index("s") * SC_CORES + lax.axis_index("c")
    return wid * per_worker


def _dispatch(hn, pos1, pos2, n_slots):
    t, d = hn.shape
    per_worker = t // SC_WORKERS
    assert per_worker * SC_WORKERS == t and per_worker % SC_ROWS == 0

    def body(h_hbm, p1_hbm, p2_hbm, out_hbm, i1_v, i2_v, rows_v, sem1, sem2):
        base0 = _sc_worker_base(per_worker)

        @pl.loop(0, per_worker // SC_ROWS)
        def _(c):
            base = pl.multiple_of(base0 + c * SC_ROWS, SC_ROWS)
            pltpu.sync_copy(p1_hbm.at[pl.ds(base, SC_ROWS)], i1_v)
            pltpu.sync_copy(p2_hbm.at[pl.ds(base, SC_ROWS)], i2_v)
            pltpu.sync_copy(h_hbm.at[pl.ds(base, SC_ROWS)], rows_v)
            c1 = pltpu.async_copy(rows_v, out_hbm.at[i1_v], sem1)
            c2 = pltpu.async_copy(rows_v, out_hbm.at[i2_v], sem2)
            c1.wait()
            c2.wait()

    return pl.kernel(
        body,
        out_type=jax.ShapeDtypeStruct((n_slots, d), hn.dtype),
        mesh=_sc_mesh(),
        scratch_types=[pltpu.VMEM((SC_ROWS,), jnp.int32), pltpu.VMEM((SC_ROWS,), jnp.int32),
                       pltpu.VMEM((SC_ROWS, d), hn.dtype), pltpu.SemaphoreType.DMA, pltpu.SemaphoreType.DMA],
        name="dispatch",
    )(hn, pos1, pos2)


def _collect(ys, pos1, pos2):
    t = pos1.shape[0]
    d = ys.shape[1]
    per_worker = t // SC_WORKERS
    assert per_worker * SC_WORKERS == t and per_worker % SC_ROWS == 0

    def body(ys_hbm, p1_hbm, p2_hbm, y1_hbm, y2_hbm, i_v, rows_v, sem):
        base0 = _sc_worker_base(per_worker)

        @pl.loop(0, per_worker // SC_ROWS)
        def _(c):
            base = pl.multiple_of(base0 + c * SC_ROWS, SC_ROWS)
            for p_hbm, y_hbm in ((p1_hbm, y1_hbm), (p2_hbm, y2_hbm)):
                pltpu.sync_copy(p_hbm.at[pl.ds(base, SC_ROWS)], i_v)
                pltpu.async_copy(ys_hbm.at[i_v], rows_v, sem).wait()
                pltpu.sync_copy(rows_v, y_hbm.at[pl.ds(base, SC_ROWS)])

    out = jax.ShapeDtypeStruct((t, d), ys.dtype)
    return pl.kernel(
        body,
        out_type=(out, out),
        mesh=_sc_mesh(),
        scratch_types=[pltpu.VMEM((SC_ROWS,), jnp.int32), pltpu.VMEM((SC_ROWS, d), ys.dtype),
                       pltpu.SemaphoreType.DMA],
        name="collect",
    )(ys, pos1, pos2)


def _experts_kernel(te_ref, rows_ref, hs_ref, wg_ref, wu_ref, wd_ref, ys_ref, wgb_ref, wub_ref, wdb_ref):
    i = pl.program_id(0)

    @pl.when((i == 0) | (te_ref[i] != te_ref[jnp.maximum(i - 1, 0)]))
    def _():
        wgb_ref[...] = wg_ref[...].astype(BF16)
        wub_ref[...] = wu_ref[...].astype(BF16)
        wdb_ref[...] = wd_ref[...].astype(BF16)

    @pl.when(rows_ref[i] > 0)
    def _():
        x = _unpack_rows(hs_ref[...]).astype(BF16)
        hg = jnp.dot(x, wgb_ref[...], preferred_element_type=F32)
        hu = jnp.dot(x, wub_ref[...], preferred_element_type=F32)
        hid = (hg * jax.nn.sigmoid(hg) * hu).astype(BF16)
        ys_ref[...] = _pack_rows(jnp.dot(hid, wdb_ref[...], preferred_element_type=F32))

    @pl.when(rows_ref[i] == 0)
    def _():
        ys_ref[...] = jnp.zeros_like(ys_ref)


def _experts(hs, tile_expert, tile_rows, w_gate, w_up, w_down):
    n_tiles = tile_expert.shape[0]
    wsel = lambda i, te, rows: (te[i], 0, 0)
    slot = lambda i, te, rows: (i, 0)
    grid_spec = pltpu.PrefetchScalarGridSpec(
        num_scalar_prefetch=2,
        grid=(n_tiles,),
        in_specs=[pl.BlockSpec((TM_EXP, D_MODEL // 2), slot),
                  pl.BlockSpec((None, D_MODEL, D_EXPERT), wsel),
                  pl.BlockSpec((None, D_MODEL, D_EXPERT), wsel),
                  pl.BlockSpec((None, D_EXPERT, D_MODEL), wsel)],
        out_specs=pl.BlockSpec((TM_EXP, D_MODEL // 2), slot),
        scratch_shapes=[pltpu.VMEM((D_MODEL, D_EXPERT), BF16), pltpu.VMEM((D_MODEL, D_EXPERT), BF16),
                        pltpu.VMEM((D_EXPERT, D_MODEL), BF16)],
    )
    return pl.pallas_call(
        _experts_kernel,
        grid_spec=grid_spec,
        out_shape=jax.ShapeDtypeStruct((n_tiles * TM_EXP, D_MODEL // 2), jnp.int32),
        compiler_params=_cparams(),
        name="experts",
    )(tile_expert, tile_rows, hs, w_gate, w_up, w_down)


def _combine_kernel(x1_ref, metat_ref, y1_ref, y2_ref, gfin_ref, out_ref):
    meta = metat_ref[...].T
    x2 = x1_ref[...] + meta[:, 2:3] * _unpack_rows(y1_ref[...]) + meta[:, 3:4] * _unpack_rows(y2_ref[...])
    out_ref[...] = _rms(x2, gfin_ref[...])


def _combine(x1, metat, y1, y2, g_final):
    t = x1.shape[0]
    assert t % TM_OUT == 0
    tok = pl.BlockSpec((TM_OUT, D_MODEL), lambda i: (i, 0))
    packed = pl.BlockSpec((TM_OUT, D_MODEL // 2), lambda i: (i, 0))
    return pl.pallas_call(
        _combine_kernel,
        grid=(t // TM_OUT,),
        in_specs=[tok, pl.BlockSpec((META_LANES, TM_OUT), lambda i: (0, i)), packed, packed,
                  pl.BlockSpec((1, D_MODEL), lambda i: (0, 0))],
        out_specs=tok,
        out_shape=jax.ShapeDtypeStruct((t, D_MODEL), F32),
        compiler_params=_cparams(),
        name="combine",
    )(x1, metat, y1, y2, g_final.reshape(1, D_MODEL))


def _layout_kernel(cnt_ref, metat_ref, pos_ref, tiles_ref):
    n_lanes = tiles_ref.shape[1]
    tile_start = lax.broadcasted_iota(jnp.int32, (1, n_lanes), 1).astype(F32) * TM_EXP
    m = metat_ref[...]
    eid = m[0:2]
    p = m[4:6]
    off = jnp.zeros((1, 1), F32)
    t_exp = jnp.full((1, n_lanes), N_EXPERTS - 1.0, F32)
    t_rows = jnp.zeros((1, n_lanes), F32)
    for e in range(N_EXPERTS):
        cnt = cnt_ref[EXPERT_ROW0 + e:EXPERT_ROW0 + e + 1, 0:1]
        padded = jnp.ceil(cnt * (1.0 / TM_EXP)) * TM_EXP
        p = p + jnp.where(eid == e, off, 0.0)
        mine = (tile_start >= off) & (tile_start < off + padded)
        t_exp = jnp.where(mine, float(e), t_exp)
        t_rows = jnp.where(mine, jnp.clip(cnt - (tile_start - off), 0.0, float(TM_EXP)), t_rows)
        off = off + padded
    pos_ref[...] = p.astype(jnp.int32)
    tiles_ref[...] = jnp.concatenate([t_exp, t_rows], axis=0).astype(jnp.int32)


def _slot_layout(metat, counts_rows, n_tiles):
    t = metat.shape[1]
    tp = min(TM_POS, t)
    assert t % tp == 0
    n_lanes = (n_tiles + 127) // 128 * 128
    pos, tiles = pl.pallas_call(
        _layout_kernel,
        grid=(t // tp,),
        in_specs=[pl.BlockSpec(counts_rows.shape, lambda i: (0, 0)),
                  pl.BlockSpec((META_LANES, tp), lambda i: (0, i))],
        out_specs=[pl.BlockSpec((2, tp), lambda i: (0, i)), pl.BlockSpec((2, n_lanes), lambda i: (0, 0))],
        out_shape=[jax.ShapeDtypeStruct((2, t), jnp.int32), jax.ShapeDtypeStruct((2, n_lanes), jnp.int32)],
        compiler_params=_cparams(),
        name="layout",
    )(counts_rows, metat)
    return pos[0], pos[1], tiles[0, :n_tiles], tiles[1, :n_tiles]


def kernel(x_prompt, x_sample, g_mix, w_in, b_gate, conv_w, conv_b, ln_g, ln_b, w_out, g_ffn,
           w_router_group, b_router_group, w_router_expert, b_router_expert, w_gate, w_up, w_down, g_final):
    assert g_mix.shape[0] == 1, "one layer"
    bp, lp, _ = x_prompt.shape
    bs, ls, _ = x_sample.shape
    seq_lens = [lp] * bp + [ls] * bs
    tp, ts = bp * lp, bs * ls
    xp = x_prompt.reshape(tp, D_MODEL)
    xs = x_sample.reshape(ts, D_MODEL)

    q, v, o, u, kt, gr = _inproj(xp, xs, g_mix[0], w_in[0], b_gate[0])
    hf, hb = _mlstm(q, kt, v, gr, seq_lens)

    outs = {}
    groups = (("prompt", xp, 0, [lp] * bp), ("sample", xs, tp // TM_MIX, [ls] * bs))
    for name, x, tile0, lens in sorted(groups, key=lambda g: -g[1].shape[0]):
        tg = x.shape[0]
        x1, hn, metat, counts = _mixout(x, tile0, hf, hb, o, u, lens, conv_w[0], conv_b[0], ln_g[0], ln_b[0],
                                              w_out[0], g_ffn[0], w_router_group[0], b_router_group[0],
                                              w_router_expert[0], b_router_expert[0])
        n_tiles = (2 * tg + N_EXPERTS * (TM_EXP - 1)) // TM_EXP + 1
        pos1, pos2, tile_expert, tile_rows = _slot_layout(metat, counts, n_tiles)
        hs = _dispatch(hn, pos1, pos2, n_tiles * TM_EXP)
        ys = _experts(hs, tile_expert, tile_rows, w_gate[0], w_up[0], w_down[0])
        y1, y2 = _collect(ys, pos1, pos2)
        outs[name] = _combine(x1, metat, y1, y2, g_final)
    return outs["prompt"].reshape(bp, lp, D_MODEL), outs["sample"].reshape(bs, ls, D_MODEL)
```

```python
import functools

import jax
import jax.numpy as jnp
from jax import lax
from jax.experimental import pallas as pl
from jax.experimental.pallas import tpu as pltpu
from jax.experimental.pallas import tpu_sc as plsc

F32 = jnp.float32
BF16 = jnp.bfloat16

D_MODEL = 1024
N_HEADS = 4
HEAD_DIM = 128
D_MLSTM = N_HEADS * HEAD_DIM
D_CONV = D_MODEL - D_MLSTM
CONV_WIDTH = 31
CONV_HALO = 16
N_DIR = 2
N_GROUPS = 4
EXPERTS_PER_GROUP = 4
N_EXPERTS = N_GROUPS * EXPERTS_PER_GROUP
D_EXPERT = 512
EPS = 1e-6
K_SCALE = HEAD_DIM ** -0.5

GATE_ROWS = 16
QROWS = 24
ROUTER_ROWS = 32
EXPERT_ROW0 = N_GROUPS
META_LANES = 8

TM_IN = 1024
CHUNK = 512
GP_CHUNKS = 32
TM_MIX = 1024
CONV_ROWS = 256
TM_EXP = 512
TM_OUT = 1024
TM_POS = 8192
VMEM_LIMIT = 48 * 1024 * 1024

SC_CORES = 2
SC_SUBCORES = 16
SC_WORKERS = SC_CORES * SC_SUBCORES
SC_ROWS = 128


def _cparams(n_axes=1):
    return pltpu.CompilerParams(dimension_semantics=("arbitrary",) * n_axes,
                                vmem_limit_bytes=VMEM_LIMIT)


def _nt_dot(a, b):
    return lax.dot_general(a, b, (((1,), (1,)), ((), ())), preferred_element_type=F32)


def _rms(x, g):
    return x * lax.rsqrt(jnp.mean(x * x, axis=-1, keepdims=True) + EPS) * g


def _pack_rows(x):
    n = x.shape[1] // 2
    hi = lax.bitcast_convert_type(x[:, :n].astype(jnp.bfloat16).astype(F32), jnp.int32)
    lo = lax.bitcast_convert_type(x[:, n:].astype(jnp.bfloat16).astype(F32), jnp.int32)
    return hi | lax.shift_right_logical(lo, 16)


def _unpack_rows(p):
    hi = lax.bitcast_convert_type(p & jnp.int32(-65536), F32)
    lo = lax.bitcast_convert_type(lax.shift_left(p, 16), F32)
    return jnp.concatenate([hi, lo], axis=1)


def _two_batch_specs(block, n_first, n_second):
    first = pl.BlockSpec(block, lambda i, *_: (jnp.minimum(i, n_first - 1), 0))
    second = pl.BlockSpec(block, lambda i, *_: (jnp.maximum(i - n_first, 0), 0))
    return first, second


def _inproj_kernel(xp_ref, xs_ref, g_ref, wq_ref, wv_ref, wo_ref, wa_ref, wb_ref, wkg_ref, bg_ref,
                   q_ref, v_ref, o_ref, u_ref, kt_ref, gr_ref, *, n_first):
    x = jnp.where(pl.program_id(0) < n_first, xp_ref[...], xs_ref[...])
    xn = _rms(x, g_ref[...]).astype(BF16)
    q_ref[...] = jnp.dot(xn, wq_ref[...], preferred_element_type=F32).astype(BF16)
    v_ref[...] = jnp.dot(xn, wv_ref[...], preferred_element_type=F32).astype(BF16)
    o_ref[...] = jax.nn.sigmoid(jnp.dot(xn, wo_ref[...], preferred_element_type=F32)).astype(BF16)
    a = jnp.dot(xn, wa_ref[...], preferred_element_type=F32)
    b = jnp.dot(xn, wb_ref[...], preferred_element_type=F32)
    u_ref[...] = (a * jax.nn.sigmoid(b)).astype(BF16)
    kg = _nt_dot(wkg_ref[...], xn)
    kt_ref[...] = (kg[:D_MLSTM] * K_SCALE).astype(BF16)
    gr = kg[D_MLSTM:] + bg_ref[...]
    for c in range(gr_ref.shape[0]):
        gr_ref[c] = gr[:, c * CHUNK:(c + 1) * CHUNK]


def _inproj(xp, xs, g_mix, w_in, b_gate):
    tp, ts = xp.shape[0], xs.shape[0]
    t = tp + ts
    assert tp % TM_IN == 0 and ts % TM_IN == 0
    off_k, off_v, off_o, off_g = D_MLSTM, 2 * D_MLSTM, 3 * D_MLSTM, 4 * D_MLSTM
    off_a = off_g + 2 * N_DIR * N_HEADS
    off_b = off_a + D_CONV
    wq = w_in[:, 0:off_k].astype(BF16)
    wkt = w_in[:, off_k:off_v].T.astype(BF16)
    wv = w_in[:, off_v:off_o].astype(BF16)
    wo = w_in[:, off_o:off_g].astype(BF16)
    wa = w_in[:, off_a:off_b].astype(BF16)
    wb = w_in[:, off_b:off_b + D_CONV].astype(BF16)
    wg = w_in[:, off_g:off_a].T.reshape(N_DIR, 2, N_HEADS, D_MODEL)
    wgt = jnp.zeros((N_DIR, 2, GATE_ROWS // 2, D_MODEL), F32).at[:, :, :N_HEADS].set(wg)
    wkg = jnp.concatenate([wkt, wgt.reshape(N_DIR * GATE_ROWS, D_MODEL).astype(BF16)], axis=0)
    bg = jnp.zeros((N_DIR, 2, GATE_ROWS // 2), F32).at[:, :, :N_HEADS].set(
        b_gate.reshape(N_DIR, 2, N_HEADS)).reshape(N_DIR * GATE_ROWS, 1)

    tok = lambda i: (i, 0)
    fixed = lambda i: (0, 0)
    wspec = pl.BlockSpec((D_MODEL, D_MLSTM), fixed)
    xp_spec, xs_spec = _two_batch_specs((TM_IN, D_MODEL), tp // TM_IN, ts // TM_IN)
    cpt = TM_IN // CHUNK
    return pl.pallas_call(
        functools.partial(_inproj_kernel, n_first=tp // TM_IN),
        grid=(t // TM_IN,),
        in_specs=[xp_spec, xs_spec, pl.BlockSpec((1, D_MODEL), fixed),
                  wspec, wspec, wspec, wspec, wspec,
                  pl.BlockSpec((D_MLSTM + N_DIR * GATE_ROWS, D_MODEL), fixed),
                  pl.BlockSpec((N_DIR * GATE_ROWS, 1), fixed)],
        out_specs=[pl.BlockSpec((TM_IN, D_MLSTM), tok)] * 4 + [
            pl.BlockSpec((D_MLSTM, TM_IN), lambda i: (0, i)),
            pl.BlockSpec((cpt, N_DIR * GATE_ROWS, CHUNK), lambda i: (i, 0, 0))],
        out_shape=[jax.ShapeDtypeStruct((t, D_MLSTM), BF16)] * 4 + [
            jax.ShapeDtypeStruct((D_MLSTM, t), BF16),
            jax.ShapeDtypeStruct((t // CHUNK, N_DIR * GATE_ROWS, CHUNK), F32)],
        compiler_params=_cparams(),
        name="inproj",
    )(xp, xs, g_mix.reshape(1, D_MODEL), wq, wv, wo, wa, wb, wkg, bg)


def _log_sigmoid(x):
    return jnp.minimum(x, 0.0) - jnp.log1p(jnp.exp(-jnp.abs(x)))


def _gateprep_kernel(reset_ref, g_ref, rowq_ref, colq_ref, m_ref, *, rev):
    n, _, c = g_ref.shape
    step = pl.program_id(0)
    blk = pl.num_programs(0) - 1 - step if rev else step

    @pl.when(step == 0)
    def _():
        m_ref[...] = jnp.zeros_like(m_ref)

    ig = g_ref[:, 0:8, :]
    lf = _log_sigmoid(g_ref[:, 8:16, :])
    lane = lax.broadcasted_iota(jnp.int32, (n, 8, c), 2)

    def scan(x, op, ident):
        k = 1
        while k < c:
            if rev:
                shifted, valid = pltpu.roll(x, c - k, axis=2), lane < c - k
            else:
                shifted, valid = pltpu.roll(x, k, axis=2), lane >= k
            x = op(x, jnp.where(valid, shifted, ident))
            k *= 2
        return x

    bc = scan(lf, jnp.add, 0.0)
    a = ig - bc
    cm = scan(a, jnp.maximum, -jnp.inf)
    b_tot = jnp.sum(lf, axis=2, keepdims=True)
    a_max = jnp.max(a, axis=2, keepdims=True)

    m = m_ref[...]
    m_in = [None] * n
    for j in (range(n - 1, -1, -1) if rev else range(n)):
        m = jnp.where(reset_ref[blk * n + j] == 1, 0.0, m)
        m_in[j] = m
        m = b_tot[j] + jnp.maximum(m, a_max[j])
    m_ref[...] = m
    m_old = jnp.stack(m_in)

    mx = jnp.maximum(m_old, cm)
    mx_last = jnp.maximum(m_old, a_max)
    rowq_ref[:, 0:8, :] = a
    rowq_ref[:, 8:16, :] = jnp.exp(a - mx_last)
    rowq_ref[:, 16:24, :] = jnp.exp(m_old - mx_last)
    e1 = jnp.exp(m_old - mx)
    fl = jnp.exp(-(mx + bc))
    for j in range(n):
        colq_ref[j] = jnp.concatenate([mx[j], e1[j], fl[j]], axis=0).T


def _gateprep(gr, reset, rev):
    n_chunks = gr.shape[0]
    gp = min(GP_CHUNKS, n_chunks)
    assert n_chunks % gp == 0
    nb = n_chunks // gp
    d = 1 if rev else 0
    bidx = (lambda s: nb - 1 - s) if rev else (lambda s: s)
    grid_spec = pltpu.PrefetchScalarGridSpec(
        num_scalar_prefetch=1,
        grid=(nb,),
        in_specs=[pl.BlockSpec((gp, GATE_ROWS, CHUNK), lambda s, r: (bidx(s), d, 0))],
        out_specs=[pl.BlockSpec((gp, QROWS, CHUNK), lambda s, r: (bidx(s), 0, 0)),
                   pl.BlockSpec((gp, CHUNK, QROWS), lambda s, r: (bidx(s), 0, 0))],
        scratch_shapes=[pltpu.VMEM((8, CHUNK), F32)],
    )
    return pl.pallas_call(
        functools.partial(_gateprep_kernel, rev=rev),
        grid_spec=grid_spec,
        out_shape=[jax.ShapeDtypeStruct((n_chunks, QROWS, CHUNK), F32),
                   jax.ShapeDtypeStruct((n_chunks, CHUNK, QROWS), F32)],
        compiler_params=_cparams(),
        name="gateprep_bwd" if rev else "gateprep_fwd",
    )(reset, gr)


def _mlstm_kernel(rf_ref, rb_ref,
                  qf_ref, ktf_ref, vf_ref, rowf_ref, colf_ref,
                  qb_ref, ktb_ref, vb_ref, rowb_ref, colb_ref,
                  hf_ref, hb_ref, cst_ref):
    c = qf_ref.shape[0]
    step = pl.program_id(0)
    last = pl.num_programs(0) - 1
    row_i = lax.broadcasted_iota(jnp.int32, (c, c), 0)
    col_i = lax.broadcasted_iota(jnp.int32, (c, c), 1)
    ones = jnp.ones((c, HEAD_DIM), BF16)

    dirs = ((rf_ref[step], col_i <= row_i, qf_ref, ktf_ref, vf_ref, rowf_ref, colf_ref, hf_ref),
            (rb_ref[last - step], col_i >= row_i, qb_ref, ktb_ref, vb_ref, rowb_ref, colb_ref, hb_ref))
    for d, (reset, mask, q_ref, kt_ref, v_ref, row_ref, col_ref, h_ref) in enumerate(dirs):
        @pl.when(reset == 1)
        def _():
            cst_ref[d] = jnp.zeros(cst_ref.shape[1:], F32)

        rowq = row_ref[...]
        colq = col_ref[...]
        for h in range(N_HEADS):
            hs = slice(h * HEAD_DIM, (h + 1) * HEAD_DIM)
            qh = q_ref[:, hs]
            kth = kt_ref[hs, :]
            vext = jnp.concatenate([v_ref[:, hs], ones], axis=1)
            s = jnp.dot(qh, kth, preferred_element_type=F32)
            e = jnp.exp(jnp.where(mask, rowq[h:h + 1, :] - colq[:, h:h + 1], -jnp.inf))
            r1 = jnp.dot((s * e).astype(BF16), vext, preferred_element_type=F32)
            cst = cst_ref[d, h]
            r2 = jnp.dot(qh, cst.astype(BF16), preferred_element_type=F32)
            e1 = colq[:, 8 + h:9 + h]
            num = r1[:, :HEAD_DIM] + e1 * r2[:, :HEAD_DIM]
            den = r1[:, HEAD_DIM:] + e1 * r2[:, HEAD_DIM:]
            h_ref[:, hs] = (num / jnp.maximum(jnp.abs(den), colq[:, 16 + h:17 + h])).astype(h_ref.dtype)
            kw = (kth.astype(F32) * rowq[8 + h:9 + h, :]).astype(BF16)
            cst_ref[d, h] = rowq[16 + h:17 + h, 0:1] * cst + jnp.dot(kw, vext, preferred_element_type=F32)


def _mlstm(q, kt, v, gr, seq_lens):
    t = q.shape[0]
    n = t // CHUNK
    starts, ends, pos = [], [], 0
    for ln in seq_lens:
        assert ln % CHUNK == 0
        starts.append(pos // CHUNK)
        ends.append((pos + ln) // CHUNK - 1)
        pos += ln
    reset_f = jnp.zeros((n,), jnp.int32).at[jnp.array(starts)].set(1)
    reset_b = jnp.zeros((n,), jnp.int32).at[jnp.array(ends)].set(1)
    rowf, colf = _gateprep(gr, reset_f, rev=False)
    rowb, colb = _gateprep(gr, reset_b, rev=True)

    def specs(cidx):
        return [pl.BlockSpec((CHUNK, D_MLSTM), lambda s, rf, rb: (cidx(s), 0)),
                pl.BlockSpec((D_MLSTM, CHUNK), lambda s, rf, rb: (0, cidx(s))),
                pl.BlockSpec((CHUNK, D_MLSTM), lambda s, rf, rb: (cidx(s), 0)),
                pl.BlockSpec((None, QROWS, CHUNK), lambda s, rf, rb: (cidx(s), 0, 0)),
                pl.BlockSpec((None, CHUNK, QROWS), lambda s, rf, rb: (cidx(s), 0, 0))]

    fwd = lambda s: s
    bwd = lambda s: n - 1 - s
    grid_spec = pltpu.PrefetchScalarGridSpec(
        num_scalar_prefetch=2,
        grid=(n,),
        in_specs=specs(fwd) + specs(bwd),
        out_specs=[pl.BlockSpec((CHUNK, D_MLSTM), lambda s, rf, rb: (fwd(s), 0)),
                   pl.BlockSpec((CHUNK, D_MLSTM), lambda s, rf, rb: (bwd(s), 0))],
        scratch_shapes=[pltpu.VMEM((N_DIR, N_HEADS, HEAD_DIM, 2 * HEAD_DIM), F32)],
    )
    return pl.pallas_call(
        _mlstm_kernel,
        grid_spec=grid_spec,
        out_shape=[jax.ShapeDtypeStruct((t, D_MLSTM), BF16)] * 2,
        compiler_params=_cparams(),
        name="mlstm",
    )(reset_f, reset_b, q, kt, v, rowf, colf, q, kt, v, rowb, colb)


def _mixout_kernel(first_ref, last_ref,
                   x_ref, hf_ref, hb_ref, o_ref, u_ref, up_ref, un_ref, cw_ref, cb_ref, lng_ref, lnb_ref,
                   wout_ref, gffn_ref, wr_ref, br_ref, tri_ref,
                   x1_ref, hn_ref, meta_ref, metat_ref, cnt_ref,
                   win_ref, y_ref, run_ref):
    i = pl.program_id(0)
    tm = hf_ref.shape[0]

    @pl.when(i == 0)
    def _():
        run_ref[...] = jnp.zeros_like(run_ref)

    h_sum = hf_ref[...].astype(F32) + hb_ref[...].astype(F32)
    ym = (o_ref[...].astype(F32) * h_sum).astype(BF16)
    x1m = x_ref[...] + jnp.dot(ym, wout_ref[:D_MLSTM, :], preferred_element_type=F32)

    win_ref[0:CONV_HALO, :] = jnp.where(first_ref[i] == 1, 0.0, up_ref[...].astype(F32))
    win_ref[CONV_HALO:CONV_HALO + tm, :] = u_ref[...].astype(F32)
    win_ref[CONV_HALO + tm:, :] = jnp.where(last_ref[i] == 1, 0.0, un_ref[...].astype(F32))

    off0 = CONV_HALO - CONV_WIDTH // 2
    for r0 in range(0, tm, CONV_ROWS):
        tiles = []
        for lt in range(D_CONV // 128):
            ls = slice(lt * 128, (lt + 1) * 128)
            acc = jnp.broadcast_to(cb_ref[:, ls], (CONV_ROWS, 128))
            for s in range(8):
                part = None
                for j in range(CONV_WIDTH):
                    if (off0 + j) % 8 != s:
                        continue
                    base = (off0 + j) // 8 * 8
                    term = win_ref[r0 + base:r0 + base + CONV_ROWS + 8, ls] * cw_ref[j:j + 1, ls]
                    part = term if part is None else part + term
                acc = acc + part[s:s + CONV_ROWS, :]
            tiles.append(acc)
        cv = jnp.concatenate(tiles, axis=1)
        xc = cv - jnp.mean(cv, axis=-1, keepdims=True)
        yc = xc * lax.rsqrt(jnp.mean(xc * xc, axis=-1, keepdims=True) + EPS) * lng_ref[...] + lnb_ref[...]
        y_ref[r0:r0 + CONV_ROWS, :] = (yc * jax.nn.sigmoid(yc)).astype(BF16)

    x1 = x1m + jnp.dot(y_ref[...], wout_ref[D_MLSTM:, :], preferred_element_type=F32)
    x1_ref[...] = x1
    hn = _rms(x1, gffn_ref[...])
    hn_ref[...] = _pack_rows(hn)

    logits = _nt_dot(wr_ref[...], hn.astype(BF16)) + br_ref[...]
    row = lax.broadcasted_iota(jnp.int32, (ROUTER_ROWS, tm), 0).astype(F32)
    neg = -jnp.inf
    no_row = float(ROUTER_ROWS)
    gl = jnp.where(row < N_GROUPS, logits, neg)
    gmax = jnp.max(gl, axis=0, keepdims=True)
    p_top = 1.0 / jnp.sum(jnp.exp(gl - gmax), axis=0, keepdims=True)
    g_idx = jnp.min(jnp.where(gl == gmax, row, no_row), axis=0, keepdims=True)
    lo = EXPERT_ROW0 + EXPERTS_PER_GROUP * g_idx
    in_grp = (row >= lo) & (row < lo + EXPERTS_PER_GROUP)
    el = jnp.where(in_grp, logits, neg)
    ee = jnp.exp(el - jnp.max(el, axis=0, keepdims=True))
    pe = jnp.where(in_grp, ee / jnp.sum(ee, axis=0, keepdims=True), -1.0)
    v1 = jnp.max(pe, axis=0, keepdims=True)
    i1 = jnp.min(jnp.where(pe == v1, row, no_row), axis=0, keepdims=True)
    pe2 = jnp.where(row == i1, -1.0, pe)
    v2 = jnp.max(pe2, axis=0, keepdims=True)
    i2 = jnp.min(jnp.where(pe2 == v2, row, no_row), axis=0, keepdims=True)
    wsum = v1 + v2
    gate1 = p_top * (v1 / wsum)
    gate2 = p_top * (v2 / wsum)

    oh1 = (row == i1).astype(F32)
    oh2 = (row == i2).astype(F32)
    cnt = jnp.dot(jnp.concatenate([oh1, oh2], axis=0).astype(BF16), tri_ref[...], preferred_element_type=F32)
    run = run_ref[...]
    tot1 = jnp.sum(oh1, axis=1, keepdims=True)
    tot2 = jnp.sum(oh2, axis=1, keepdims=True)
    rank1 = jnp.sum(oh1 * (run + cnt[:ROUTER_ROWS]), axis=0, keepdims=True)
    rank2 = jnp.sum(oh2 * (run + tot1 + cnt[ROUTER_ROWS:]), axis=0, keepdims=True)
    run = run + tot1 + tot2
    run_ref[...] = run
    cnt_ref[...] = jnp.broadcast_to(run, cnt_ref.shape)

    mr = lax.broadcasted_iota(jnp.int32, (META_LANES, tm), 0)
    metat = jnp.where(mr == 0, i1 - EXPERT_ROW0, 0.0)
    metat = jnp.where(mr == 1, i2 - EXPERT_ROW0, metat)
    metat = jnp.where(mr == 2, gate1, metat)
    metat = jnp.where(mr == 3, gate2, metat)
    metat = jnp.where(mr == 4, rank1, metat)
    metat = jnp.where(mr == 5, rank2, metat)
    metat_ref[...] = metat
    meta_ref[...] = metat.T


def _mixout(x, tile0, hf, hb, o, u, seq_lens, conv_w, conv_b, ln_g, ln_b, w_out, g_ffn, w_rg, b_rg, w_re, b_re):
    t = x.shape[0]
    n_tiles = t // TM_MIX
    hpt = TM_MIX // CONV_HALO
    n_halo = u.shape[0] // CONV_HALO
    firsts, lasts, pos = [], [], 0
    for ln in seq_lens:
        assert ln % TM_MIX == 0
        firsts.append(pos // TM_MIX)
        lasts.append((pos + ln) // TM_MIX - 1)
        pos += ln
    assert pos == t
    first = jnp.zeros((n_tiles,), jnp.int32).at[jnp.array(firsts)].set(1)
    last = jnp.zeros((n_tiles,), jnp.int32).at[jnp.array(lasts)].set(1)

    cw = jnp.zeros((32, D_CONV), F32).at[:CONV_WIDTH].set(conv_w.reshape(CONV_WIDTH, D_CONV))
    wr = jnp.zeros((ROUTER_ROWS, D_MODEL), F32)
    wr = wr.at[:N_GROUPS].set(w_rg.T).at[EXPERT_ROW0:EXPERT_ROW0 + N_EXPERTS].set(w_re.T).astype(BF16)
    br = jnp.zeros((ROUTER_ROWS, 1), F32)
    br = br.at[:N_GROUPS, 0].set(b_rg).at[EXPERT_ROW0:EXPERT_ROW0 + N_EXPERTS, 0].set(b_re)
    tri = (lax.broadcasted_iota(jnp.int32, (TM_MIX, TM_MIX), 0)
           < lax.broadcasted_iota(jnp.int32, (TM_MIX, TM_MIX), 1)).astype(BF16)

    tok = lambda i, f, l: (i, 0)
    flat = lambda i, f, l: (tile0 + i, 0)
    fixed = lambda i, f, l: (0, 0)
    row = lambda n: pl.BlockSpec((1, n), fixed)
    half = pl.BlockSpec((TM_MIX, D_MLSTM), flat)
    grid_spec = pltpu.PrefetchScalarGridSpec(
        num_scalar_prefetch=2,
        grid=(n_tiles,),
        in_specs=[pl.BlockSpec((TM_MIX, D_MODEL), tok), half, half, half,
                  pl.BlockSpec((TM_MIX, D_CONV), flat),
                  pl.BlockSpec((CONV_HALO, D_CONV), lambda i, f, l: (jnp.maximum((tile0 + i) * hpt - 1, 0), 0)),
                  pl.BlockSpec((CONV_HALO, D_CONV),
                               lambda i, f, l: (jnp.minimum((tile0 + i + 1) * hpt, n_halo - 1), 0)),
                  pl.BlockSpec((32, D_CONV), fixed), row(D_CONV), row(D_CONV), row(D_CONV),
                  pl.BlockSpec((D_MODEL, D_MODEL), fixed), row(D_MODEL),
                  pl.BlockSpec((ROUTER_ROWS, D_MODEL), fixed), pl.BlockSpec((ROUTER_ROWS, 1), fixed),
                  pl.BlockSpec((TM_MIX, TM_MIX), fixed)],
        out_specs=[pl.BlockSpec((TM_MIX, D_MODEL), tok), pl.BlockSpec((TM_MIX, D_MODEL // 2), tok),
                   pl.BlockSpec((TM_MIX, META_LANES), tok),
                   pl.BlockSpec((META_LANES, TM_MIX), lambda i, f, l: (0, i)),
                   pl.BlockSpec((ROUTER_ROWS, 128), fixed)],
        scratch_shapes=[pltpu.VMEM((TM_MIX + 2 * CONV_HALO, D_CONV), F32),
                        pltpu.VMEM((TM_MIX, D_CONV), BF16),
                        pltpu.VMEM((ROUTER_ROWS, 1), F32)],
    )
    return pl.pallas_call(
        _mixout_kernel,
        grid_spec=grid_spec,
        out_shape=[jax.ShapeDtypeStruct((t, D_MODEL), F32), jax.ShapeDtypeStruct((t, D_MODEL // 2), jnp.int32),
                   jax.ShapeDtypeStruct((t, META_LANES), F32), jax.ShapeDtypeStruct((META_LANES, t), F32),
                   jax.ShapeDtypeStruct((ROUTER_ROWS, 128), F32)],
        compiler_params=_cparams(),
        name="mixout",
    )(first, last, x, hf, hb, o, u, u, u, cw, conv_b.reshape(1, D_CONV), ln_g.reshape(1, D_CONV),
      ln_b.reshape(1, D_CONV), w_out.astype(BF16), g_ffn.reshape(1, D_MODEL), wr, br, tri)


def _sc_mesh():
    return plsc.VectorSubcoreMesh(core_axis_name="c", subcore_axis_name="s")


def _sc_worker_base(per_worker):
    wid = lax.axis_index("s") * SC_CORES + lax.axis_index("c")
    return wid * per_worker


def _dispatch(hn, pos1, pos2, n_slots):
    t, d = hn.shape
    per_worker = t // SC_WORKERS
    assert per_worker * SC_WORKERS == t and per_worker % SC_ROWS == 0

    def body(h_hbm, p1_hbm, p2_hbm, out_hbm, i1_v, i2_v, rows_v, sem1, sem2):
        base0 = _sc_worker_base(per_worker)

        @pl.loop(0, per_worker // SC_ROWS)
        def _(c):
            base = pl.multiple_of(base0 + c * SC_ROWS, SC_ROWS)
            pltpu.sync_copy(p1_hbm.at[pl.ds(base, SC_ROWS)], i1_v)
            pltpu.sync_copy(p2_hbm.at[pl.ds(base, SC_ROWS)], i2_v)
            pltpu.sync_copy(h_hbm.at[pl.ds(base, SC_ROWS)], rows_v)
            c1 = pltpu.async_copy(rows_v, out_hbm.at[i1_v], sem1)
            c2 = pltpu.async_copy(rows_v, out_hbm.at[i2_v], sem2)
            c1.wait()
            c2.wait()

    return pl.kernel(
        body,
        out_type=jax.ShapeDtypeStruct((n_slots, d), hn.dtype),
        mesh=_sc_mesh(),
        scratch_types=[pltpu.VMEM((SC_ROWS,), jnp.int32), pltpu.VMEM((SC_ROWS,), jnp.int32),
                       pltpu.VMEM((SC_ROWS, d), hn.dtype), pltpu.SemaphoreType.DMA, pltpu.SemaphoreType.DMA],
        name="dispatch",
    )(hn, pos1, pos2)


def _collect(ys, pos1, pos2):
    t = pos1.shape[0]
    d = ys.shape[1]
    per_worker = t // SC_WORKERS
    assert per_worker * SC_WORKERS == t and per_worker % SC_ROWS == 0

    def body(ys_hbm, p1_hbm, p2_hbm, y1_hbm, y2_hbm, i_v, rows_v, sem):
        base0 = _sc_worker_base(per_worker)

        @pl.loop(0, per_worker // SC_ROWS)
        def _(c):
            base = pl.multiple_of(base0 + c * SC_ROWS, SC_ROWS)
            for p_hbm, y_hbm in ((p1_hbm, y1_hbm), (p2_hbm, y2_hbm)):
                pltpu.sync_copy(p_hbm.at[pl.ds(base, SC_ROWS)], i_v)
                pltpu.async_copy(ys_hbm.at[i_v], rows_v, sem).wait()
                pltpu.sync_copy(rows_v, y_hbm.at[pl.ds(base, SC_ROWS)])

    out = jax.ShapeDtypeStruct((t, d), ys.dtype)
    return pl.kernel(
        body,
        out_type=(out, out),
        mesh=_sc_mesh(),
        scratch_types=[pltpu.VMEM((SC_ROWS,), jnp.int32), pltpu.VMEM((SC_ROWS, d), ys.dtype),
                       pltpu.SemaphoreType.DMA],
        name="collect",
    )(ys, pos1, pos2)


def _experts_kernel(te_ref, rows_ref, hs_ref, wg_hbm, wu_hbm, wd_hbm, ys_ref,
                    wgf_ref, wuf_ref, wdf_ref, wgb_ref, wub_ref, wdb_ref, sem):
    i = pl.program_id(0)
    n = pl.num_programs(0)
    e = te_ref[i]

    def weight_copies(expert):
        return (pltpu.make_async_copy(wg_hbm.at[expert], wgf_ref, sem.at[0]),
                pltpu.make_async_copy(wu_hbm.at[expert], wuf_ref, sem.at[1]),
                pltpu.make_async_copy(wd_hbm.at[expert], wdf_ref, sem.at[2]))

    @pl.when(i == 0)
    def _():
        for c in weight_copies(e):
            c.start()

    @pl.when((i == 0) | (e != te_ref[jnp.maximum(i - 1, 0)]))
    def _():
        for c in weight_copies(e):
            c.wait()
        wgb_ref[...] = wgf_ref[...].astype(BF16)
        wub_ref[...] = wuf_ref[...].astype(BF16)
        wdb_ref[...] = wdf_ref[...].astype(BF16)
        nxt = lax.while_loop(lambda j: (j < n) & (te_ref[jnp.minimum(j, n - 1)] == e), lambda j: j + 1, i + 1)

        @pl.when(nxt < n)
        def _():
            for c in weight_copies(te_ref[jnp.minimum(nxt, n - 1)]):
                c.start()

    @pl.when(rows_ref[i] > 0)
    def _():
        x = _unpack_rows(hs_ref[...]).astype(BF16)
        hg = jnp.dot(x, wgb_ref[...], preferred_element_type=F32)
        hu = jnp.dot(x, wub_ref[...], preferred_element_type=F32)
        hid = (hg * jax.nn.sigmoid(hg) * hu).astype(BF16)
        ys_ref[...] = _pack_rows(jnp.dot(hid, wdb_ref[...], preferred_element_type=F32))

    @pl.when(rows_ref[i] == 0)
    def _():
        ys_ref[...] = jnp.zeros_like(ys_ref)


def _experts(hs, tile_expert, tile_rows, w_gate, w_up, w_down):
    n_tiles = tile_expert.shape[0]
    slot = lambda i, te, rows: (i, 0)
    hbm = pl.BlockSpec(memory_space=pl.ANY)
    grid_spec = pltpu.PrefetchScalarGridSpec(
        num_scalar_prefetch=2,
        grid=(n_tiles,),
        in_specs=[pl.BlockSpec((TM_EXP, D_MODEL // 2), slot), hbm, hbm, hbm],
        out_specs=pl.BlockSpec((TM_EXP, D_MODEL // 2), slot),
        scratch_shapes=[pltpu.VMEM((D_MODEL, D_EXPERT), F32), pltpu.VMEM((D_MODEL, D_EXPERT), F32),
                        pltpu.VMEM((D_EXPERT, D_MODEL), F32),
                        pltpu.VMEM((D_MODEL, D_EXPERT), BF16), pltpu.VMEM((D_MODEL, D_EXPERT), BF16),
                        pltpu.VMEM((D_EXPERT, D_MODEL), BF16),
                        pltpu.SemaphoreType.DMA((3,))],
    )
    return pl.pallas_call(
        _experts_kernel,
        grid_spec=grid_spec,
        out_shape=jax.ShapeDtypeStruct((n_tiles * TM_EXP, D_MODEL // 2), jnp.int32),
        compiler_params=_cparams(),
        name="experts",
    )(tile_expert, tile_rows, hs, w_gate, w_up, w_down)


def _combine_kernel(x1_ref, meta_ref, y1_ref, y2_ref, gfin_ref, out_ref):
    meta = meta_ref[...]
    x2 = x1_ref[...] + meta[:, 2:3] * _unpack_rows(y1_ref[...]) + meta[:, 3:4] * _unpack_rows(y2_ref[...])
    out_ref[...] = _rms(x2, gfin_ref[...])


def _combine(x1, meta, y1, y2, g_final):
    t = x1.shape[0]
    assert t % TM_OUT == 0
    tok = pl.BlockSpec((TM_OUT, D_MODEL), lambda i: (i, 0))
    packed = pl.BlockSpec((TM_OUT, D_MODEL // 2), lambda i: (i, 0))
    return pl.pallas_call(
        _combine_kernel,
        grid=(t // TM_OUT,),
        in_specs=[tok, pl.BlockSpec((TM_OUT, META_LANES), lambda i: (i, 0)), packed, packed,
                  pl.BlockSpec((1, D_MODEL), lambda i: (0, 0))],
        out_specs=tok,
        out_shape=jax.ShapeDtypeStruct((t, D_MODEL), F32),
        compiler_params=_cparams(),
        name="combine",
    )(x1, meta, y1, y2, g_final.reshape(1, D_MODEL))


def _layout_kernel(cnt_ref, metat_ref, pos_ref, tiles_ref):
    n_lanes = tiles_ref.shape[1]
    tile_start = lax.broadcasted_iota(jnp.int32, (1, n_lanes), 1).astype(F32) * TM_EXP
    m = metat_ref[...]
    eid = m[0:2]
    p = m[4:6]
    off = jnp.zeros((1, 1), F32)
    t_exp = jnp.full((1, n_lanes), N_EXPERTS - 1.0, F32)
    t_rows = jnp.zeros((1, n_lanes), F32)
    for e in range(N_EXPERTS):
        cnt = cnt_ref[EXPERT_ROW0 + e:EXPERT_ROW0 + e + 1, 0:1]
        padded = jnp.ceil(cnt * (1.0 / TM_EXP)) * TM_EXP
        p = p + jnp.where(eid == e, off, 0.0)
        mine = (tile_start >= off) & (tile_start < off + padded)
        t_exp = jnp.where(mine, float(e), t_exp)
        t_rows = jnp.where(mine, jnp.clip(cnt - (tile_start - off), 0.0, float(TM_EXP)), t_rows)
        off = off + padded
    pos_ref[...] = p.astype(jnp.int32)
    tiles_ref[...] = jnp.concatenate([t_exp, t_rows], axis=0).astype(jnp.int32)


def _slot_layout(metat, counts_rows, n_tiles):
    t = metat.shape[1]
    tp = min(TM_POS, t)
    assert t % tp == 0
    n_lanes = (n_tiles + 127) // 128 * 128
    pos, tiles = pl.pallas_call(
        _layout_kernel,
        grid=(t // tp,),
        in_specs=[pl.BlockSpec(counts_rows.shape, lambda i: (0, 0)),
                  pl.BlockSpec((META_LANES, tp), lambda i: (0, i))],
        out_specs=[pl.BlockSpec((2, tp), lambda i: (0, i)), pl.BlockSpec((2, n_lanes), lambda i: (0, 0))],
        out_shape=[jax.ShapeDtypeStruct((2, t), jnp.int32), jax.ShapeDtypeStruct((2, n_lanes), jnp.int32)],
        compiler_params=_cparams(),
        name="layout",
    )(counts_rows, metat)
    return pos[0], pos[1], tiles[0, :n_tiles], tiles[1, :n_tiles]


def kernel(x_prompt, x_sample, g_mix, w_in, b_gate, conv_w, conv_b, ln_g, ln_b, w_out, g_ffn,
           w_router_group, b_router_group, w_router_expert, b_router_expert, w_gate, w_up, w_down, g_final):
    assert g_mix.shape[0] == 1, "one layer"
    bp, lp, _ = x_prompt.shape
    bs, ls, _ = x_sample.shape
    seq_lens = [lp] * bp + [ls] * bs
    tp, ts = bp * lp, bs * ls
    xp = x_prompt.reshape(tp, D_MODEL)
    xs = x_sample.reshape(ts, D_MODEL)

    q, v, o, u, kt, gr = _inproj(xp, xs, g_mix[0], w_in[0], b_gate[0])
    hf, hb = _mlstm(q, kt, v, gr, seq_lens)

    outs = {}
    groups = (("prompt", xp, 0, [lp] * bp), ("sample", xs, tp // TM_MIX, [ls] * bs))
    for name, x, tile0, lens in sorted(groups, key=lambda g: -g[1].shape[0]):
        tg = x.shape[0]
        x1, hn, meta, metat, counts = _mixout(x, tile0, hf, hb, o, u, lens, conv_w[0], conv_b[0], ln_g[0], ln_b[0],
                                              w_out[0], g_ffn[0], w_router_group[0], b_router_group[0],
                                              w_router_expert[0], b_router_expert[0])
        n_tiles = (2 * tg + N_EXPERTS * (TM_EXP - 1)) // TM_EXP + 1
        pos1, pos2, tile_expert, tile_rows = _slot_layout(metat, counts, n_tiles)
        hs = _dispatch(hn, pos1, pos2, n_tiles * TM_EXP)
        ys = _experts(hs, tile_expert, tile_rows, w_gate[0], w_up[0], w_down[0])
        y1, y2 = _collect(ys, pos1, pos2)
        outs[name] = _combine(x1, meta, y1, y2, g_final)
    return outs["prompt"].reshape(bp, lp, D_MODEL), outs["sample"].reshape(bs, ls, D_MODEL)
```

```python
import functools

import jax
import jax.numpy as jnp
from jax import lax
from jax.experimental import pallas as pl
from jax.experimental.pallas import tpu as pltpu
from jax.experimental.pallas import tpu_sc as plsc

F32 = jnp.float32
BF16 = jnp.bfloat16

D_MODEL = 1024
N_HEADS = 4
HEAD_DIM = 128
D_MLSTM = N_HEADS * HEAD_DIM
D_CONV = D_MODEL - D_MLSTM
CONV_WIDTH = 31
CONV_HALO = 16
N_DIR = 2
N_GROUPS = 4
EXPERTS_PER_GROUP = 4
N_EXPERTS = N_GROUPS * EXPERTS_PER_GROUP
D_EXPERT = 512
EPS = 1e-6
K_SCALE = HEAD_DIM ** -0.5

GATE_ROWS = 16
QROWS = 24
ROUTER_ROWS = 32
EXPERT_ROW0 = N_GROUPS
META_LANES = 8

TM_IN = 1024
CHUNK = 512
GP_CHUNKS = 32
TM_MIX = 1024
CONV_ROWS = 256
TM_EXP = 512
TM_OUT = 1024
TM_POS = 8192
VMEM_LIMIT = 48 * 1024 * 1024

SC_CORES = 2
SC_SUBCORES = 16
SC_WORKERS = SC_CORES * SC_SUBCORES
SC_ROWS = 128


def _cparams(n_axes=1):
    return pltpu.CompilerParams(dimension_semantics=("arbitrary",) * n_axes,
                                vmem_limit_bytes=VMEM_LIMIT)


def _nt_dot(a, b):
    return lax.dot_general(a, b, (((1,), (1,)), ((), ())), preferred_element_type=F32)


def _rms(x, g):
    return x * lax.rsqrt(jnp.mean(x * x, axis=-1, keepdims=True) + EPS) * g


def _pack_rows(x):
    n = x.shape[1] // 2
    hi = lax.bitcast_convert_type(x[:, :n].astype(jnp.bfloat16).astype(F32), jnp.int32)
    lo = lax.bitcast_convert_type(x[:, n:].astype(jnp.bfloat16).astype(F32), jnp.int32)
    return hi | lax.shift_right_logical(lo, 16)


def _unpack_rows(p):
    hi = lax.bitcast_convert_type(p & jnp.int32(-65536), F32)
    lo = lax.bitcast_convert_type(lax.shift_left(p, 16), F32)
    return jnp.concatenate([hi, lo], axis=1)


def _two_batch_specs(block, n_first, n_second):
    first = pl.BlockSpec(block, lambda i, *_: (jnp.minimum(i, n_first - 1), 0))
    second = pl.BlockSpec(block, lambda i, *_: (jnp.maximum(i - n_first, 0), 0))
    return first, second


def _inproj_kernel(xp_ref, xs_ref, g_ref, wq_ref, wv_ref, wo_ref, wa_ref, wb_ref, wkg_ref, bg_ref,
                   q_ref, v_ref, o_ref, u_ref, kt_ref, gr_ref, *, n_first):
    x = jnp.where(pl.program_id(0) < n_first, xp_ref[...], xs_ref[...])
    xn = _rms(x, g_ref[...]).astype(BF16)
    q_ref[...] = jnp.dot(xn, wq_ref[...], preferred_element_type=F32).astype(BF16)
    v_ref[...] = jnp.dot(xn, wv_ref[...], preferred_element_type=F32).astype(BF16)
    o_ref[...] = jax.nn.sigmoid(jnp.dot(xn, wo_ref[...], preferred_element_type=F32)).astype(BF16)
    a = jnp.dot(xn, wa_ref[...], preferred_element_type=F32)
    b = jnp.dot(xn, wb_ref[...], preferred_element_type=F32)
    u_ref[...] = (a * jax.nn.sigmoid(b)).astype(BF16)
    kg = _nt_dot(wkg_ref[...], xn)
    kt_ref[...] = (kg[:D_MLSTM] * K_SCALE).astype(BF16)
    gr = kg[D_MLSTM:] + bg_ref[...]
    for c in range(gr_ref.shape[0]):
        gr_ref[c] = gr[:, c * CHUNK:(c + 1) * CHUNK]


def _inproj(xp, xs, g_mix, w_in, b_gate):
    tp, ts = xp.shape[0], xs.shape[0]
    t = tp + ts
    assert tp % TM_IN == 0 and ts % TM_IN == 0
    off_k, off_v, off_o, off_g = D_MLSTM, 2 * D_MLSTM, 3 * D_MLSTM, 4 * D_MLSTM
    off_a = off_g + 2 * N_DIR * N_HEADS
    off_b = off_a + D_CONV
    wq = w_in[:, 0:off_k].astype(BF16)
    wkt = w_in[:, off_k:off_v].T.astype(BF16)
    wv = w_in[:, off_v:off_o].astype(BF16)
    wo = w_in[:, off_o:off_g].astype(BF16)
    wa = w_in[:, off_a:off_b].astype(BF16)
    wb = w_in[:, off_b:off_b + D_CONV].astype(BF16)
    wg = w_in[:, off_g:off_a].T.reshape(N_DIR, 2, N_HEADS, D_MODEL)
    wgt = jnp.zeros((N_DIR, 2, GATE_ROWS // 2, D_MODEL), F32).at[:, :, :N_HEADS].set(wg)
    wkg = jnp.concatenate([wkt, wgt.reshape(N_DIR * GATE_ROWS, D_MODEL).astype(BF16)], axis=0)
    bg = jnp.zeros((N_DIR, 2, GATE_ROWS // 2), F32).at[:, :, :N_HEADS].set(
        b_gate.reshape(N_DIR, 2, N_HEADS)).reshape(N_DIR * GATE_ROWS, 1)

    tok = lambda i: (i, 0)
    fixed = lambda i: (0, 0)
    wspec = pl.BlockSpec((D_MODEL, D_MLSTM), fixed)
    xp_spec, xs_spec = _two_batch_specs((TM_IN, D_MODEL), tp // TM_IN, ts // TM_IN)
    cpt = TM_IN // CHUNK
    return pl.pallas_call(
        functools.partial(_inproj_kernel, n_first=tp // TM_IN),
        grid=(t // TM_IN,),
        in_specs=[xp_spec, xs_spec, pl.BlockSpec((1, D_MODEL), fixed),
                  wspec, wspec, wspec, wspec, wspec,
                  pl.BlockSpec((D_MLSTM + N_DIR * GATE_ROWS, D_MODEL), fixed),
                  pl.BlockSpec((N_DIR * GATE_ROWS, 1), fixed)],
        out_specs=[pl.BlockSpec((TM_IN, D_MLSTM), tok)] * 4 + [
            pl.BlockSpec((D_MLSTM, TM_IN), lambda i: (0, i)),
            pl.BlockSpec((cpt, N_DIR * GATE_ROWS, CHUNK), lambda i: (i, 0, 0))],
        out_shape=[jax.ShapeDtypeStruct((t, D_MLSTM), BF16)] * 4 + [
            jax.ShapeDtypeStruct((D_MLSTM, t), BF16),
            jax.ShapeDtypeStruct((t // CHUNK, N_DIR * GATE_ROWS, CHUNK), F32)],
        compiler_params=_cparams(),
        name="inproj",
    )(xp, xs, g_mix.reshape(1, D_MODEL), wq, wv, wo, wa, wb, wkg, bg)


def _log_sigmoid(x):
    return jnp.minimum(x, 0.0) - jnp.log1p(jnp.exp(-jnp.abs(x)))


def _gateprep_kernel(reset_ref, g_ref, rowq_ref, colq_ref, m_ref, *, rev):
    n, _, c = g_ref.shape
    step = pl.program_id(0)
    blk = pl.num_programs(0) - 1 - step if rev else step

    @pl.when(step == 0)
    def _():
        m_ref[...] = jnp.zeros_like(m_ref)

    ig = g_ref[:, 0:8, :]
    lf = _log_sigmoid(g_ref[:, 8:16, :])
    lane = lax.broadcasted_iota(jnp.int32, (n, 8, c), 2)

    def scan(x, op, ident):
        k = 1
        while k < c:
            if rev:
                shifted, valid = pltpu.roll(x, c - k, axis=2), lane < c - k
            else:
                shifted, valid = pltpu.roll(x, k, axis=2), lane >= k
            x = op(x, jnp.where(valid, shifted, ident))
            k *= 2
        return x

    bc = scan(lf, jnp.add, 0.0)
    a = ig - bc
    cm = scan(a, jnp.maximum, -jnp.inf)
    b_tot = jnp.sum(lf, axis=2, keepdims=True)
    a_max = jnp.max(a, axis=2, keepdims=True)

    m = m_ref[...]
    m_in = [None] * n
    for j in (range(n - 1, -1, -1) if rev else range(n)):
        m = jnp.where(reset_ref[blk * n + j] == 1, 0.0, m)
        m_in[j] = m
        m = b_tot[j] + jnp.maximum(m, a_max[j])
    m_ref[...] = m
    m_old = jnp.stack(m_in)

    mx = jnp.maximum(m_old, cm)
    mx_last = jnp.maximum(m_old, a_max)
    rowq_ref[:, 0:8, :] = a
    rowq_ref[:, 8:16, :] = jnp.exp(a - mx_last)
    rowq_ref[:, 16:24, :] = jnp.exp(m_old - mx_last)
    e1 = jnp.exp(m_old - mx)
    fl = jnp.exp(-(mx + bc))
    for j in range(n):
        colq_ref[j] = jnp.concatenate([mx[j], e1[j], fl[j]], axis=0).T


def _gateprep(gr, reset, rev):
    n_chunks = gr.shape[0]
    gp = min(GP_CHUNKS, n_chunks)
    assert n_chunks % gp == 0
    nb = n_chunks // gp
    d = 1 if rev else 0
    bidx = (lambda s: nb - 1 - s) if rev else (lambda s: s)
    grid_spec = pltpu.PrefetchScalarGridSpec(
        num_scalar_prefetch=1,
        grid=(nb,),
        in_specs=[pl.BlockSpec((gp, GATE_ROWS, CHUNK), lambda s, r: (bidx(s), d, 0))],
        out_specs=[pl.BlockSpec((gp, QROWS, CHUNK), lambda s, r: (bidx(s), 0, 0)),
                   pl.BlockSpec((gp, CHUNK, QROWS), lambda s, r: (bidx(s), 0, 0))],
        scratch_shapes=[pltpu.VMEM((8, CHUNK), F32)],
    )
    return pl.pallas_call(
        functools.partial(_gateprep_kernel, rev=rev),
        grid_spec=grid_spec,
        out_shape=[jax.ShapeDtypeStruct((n_chunks, QROWS, CHUNK), F32),
                   jax.ShapeDtypeStruct((n_chunks, CHUNK, QROWS), F32)],
        compiler_params=_cparams(),
        name="gateprep_bwd" if rev else "gateprep_fwd",
    )(reset, gr)


def _mlstm_kernel(rf_ref, rb_ref,
                  qf_ref, ktf_ref, vf_ref, rowf_ref, colf_ref,
                  qb_ref, ktb_ref, vb_ref, rowb_ref, colb_ref,
                  hf_ref, hb_ref, cst_ref):
    c = qf_ref.shape[0]
    step = pl.program_id(0)
    last = pl.num_programs(0) - 1
    row_i = lax.broadcasted_iota(jnp.int32, (c, c), 0)
    col_i = lax.broadcasted_iota(jnp.int32, (c, c), 1)
    ones = jnp.ones((c, HEAD_DIM), BF16)

    dirs = ((rf_ref[step], col_i <= row_i, qf_ref, ktf_ref, vf_ref, rowf_ref, colf_ref, hf_ref),
            (rb_ref[last - step], col_i >= row_i, qb_ref, ktb_ref, vb_ref, rowb_ref, colb_ref, hb_ref))
    for d, (reset, mask, q_ref, kt_ref, v_ref, row_ref, col_ref, h_ref) in enumerate(dirs):
        @pl.when(reset == 1)
        def _():
            cst_ref[d] = jnp.zeros(cst_ref.shape[1:], F32)

        rowq = row_ref[...]
        colq = col_ref[...]
        for h in range(N_HEADS):
            hs = slice(h * HEAD_DIM, (h + 1) * HEAD_DIM)
            qh = q_ref[:, hs]
            kth = kt_ref[hs, :]
            vext = jnp.concatenate([v_ref[:, hs], ones], axis=1)
            s = jnp.dot(qh, kth, preferred_element_type=F32)
            e = jnp.exp(jnp.where(mask, rowq[h:h + 1, :] - colq[:, h:h + 1], -jnp.inf))
            r1 = jnp.dot((s * e).astype(BF16), vext, preferred_element_type=F32)
            cst = cst_ref[d, h]
            r2 = jnp.dot(qh, cst.astype(BF16), preferred_element_type=F32)
            e1 = colq[:, 8 + h:9 + h]
            num = r1[:, :HEAD_DIM] + e1 * r2[:, :HEAD_DIM]
            den = r1[:, HEAD_DIM:] + e1 * r2[:, HEAD_DIM:]
            h_ref[:, hs] = (num / jnp.maximum(jnp.abs(den), colq[:, 16 + h:17 + h])).astype(h_ref.dtype)
            kw = (kth.astype(F32) * rowq[8 + h:9 + h, :]).astype(BF16)
            cst_ref[d, h] = rowq[16 + h:17 + h, 0:1] * cst + jnp.dot(kw, vext, preferred_element_type=F32)


def _mlstm(q, kt, v, gr, seq_lens):
    t = q.shape[0]
    n = t // CHUNK
    starts, ends, pos = [], [], 0
    for ln in seq_lens:
        assert ln % CHUNK == 0
        starts.append(pos // CHUNK)
        ends.append((pos + ln) // CHUNK - 1)
        pos += ln
    reset_f = jnp.zeros((n,), jnp.int32).at[jnp.array(starts)].set(1)
    reset_b = jnp.zeros((n,), jnp.int32).at[jnp.array(ends)].set(1)
    rowf, colf = _gateprep(gr, reset_f, rev=False)
    rowb, colb = _gateprep(gr, reset_b, rev=True)

    def specs(cidx):
        return [pl.BlockSpec((CHUNK, D_MLSTM), lambda s, rf, rb: (cidx(s), 0)),
                pl.BlockSpec((D_MLSTM, CHUNK), lambda s, rf, rb: (0, cidx(s))),
                pl.BlockSpec((CHUNK, D_MLSTM), lambda s, rf, rb: (cidx(s), 0)),
                pl.BlockSpec((None, QROWS, CHUNK), lambda s, rf, rb: (cidx(s), 0, 0)),
                pl.BlockSpec((None, CHUNK, QROWS), lambda s, rf, rb: (cidx(s), 0, 0))]

    fwd = lambda s: s
    bwd = lambda s: n - 1 - s
    grid_spec = pltpu.PrefetchScalarGridSpec(
        num_scalar_prefetch=2,
        grid=(n,),
        in_specs=specs(fwd) + specs(bwd),
        out_specs=[pl.BlockSpec((CHUNK, D_MLSTM), lambda s, rf, rb: (fwd(s), 0)),
                   pl.BlockSpec((CHUNK, D_MLSTM), lambda s, rf, rb: (bwd(s), 0))],
        scratch_shapes=[pltpu.VMEM((N_DIR, N_HEADS, HEAD_DIM, 2 * HEAD_DIM), F32)],
    )
    return pl.pallas_call(
        _mlstm_kernel,
        grid_spec=grid_spec,
        out_shape=[jax.ShapeDtypeStruct((t, D_MLSTM), BF16)] * 2,
        compiler_params=_cparams(),
        name="mlstm",
    )(reset_f, reset_b, q, kt, v, rowf, colf, q, kt, v, rowb, colb)


def _mixout_kernel(first_ref, last_ref,
                   x_ref, hf_ref, hb_ref, o_ref, u_ref, up_ref, un_ref, cw_ref, cb_ref, lng_ref, lnb_ref,
                   wout_ref, gffn_ref, wr_ref, br_ref, tri_ref,
                   x1_ref, hn_ref, meta_ref, metat_ref, cnt_ref,
                   win_ref, y_ref, run_ref):
    i = pl.program_id(0)
    tm = hf_ref.shape[0]

    @pl.when(i == 0)
    def _():
        run_ref[...] = jnp.zeros_like(run_ref)

    h_sum = hf_ref[...].astype(F32) + hb_ref[...].astype(F32)
    ym = (o_ref[...].astype(F32) * h_sum).astype(BF16)
    x1m = x_ref[...] + jnp.dot(ym, wout_ref[:D_MLSTM, :], preferred_element_type=F32)

    win_ref[0:CONV_HALO, :] = jnp.where(first_ref[i] == 1, 0.0, up_ref[...].astype(F32))
    win_ref[CONV_HALO:CONV_HALO + tm, :] = u_ref[...].astype(F32)
    win_ref[CONV_HALO + tm:, :] = jnp.where(last_ref[i] == 1, 0.0, un_ref[...].astype(F32))

    off0 = CONV_HALO - CONV_WIDTH // 2
    for r0 in range(0, tm, CONV_ROWS):
        tiles = []
        for lt in range(D_CONV // 128):
            ls = slice(lt * 128, (lt + 1) * 128)
            acc = jnp.broadcast_to(cb_ref[:, ls], (CONV_ROWS, 128))
            for s in range(8):
                part = None
                for j in range(CONV_WIDTH):
                    if (off0 + j) % 8 != s:
                        continue
                    base = (off0 + j) // 8 * 8
                    term = win_ref[r0 + base:r0 + base + CONV_ROWS + 8, ls] * cw_ref[j:j + 1, ls]
                    part = term if part is None else part + term
                acc = acc + part[s:s + CONV_ROWS, :]
            tiles.append(acc)
        cv = jnp.concatenate(tiles, axis=1)
        xc = cv - jnp.mean(cv, axis=-1, keepdims=True)
        yc = xc * lax.rsqrt(jnp.mean(xc * xc, axis=-1, keepdims=True) + EPS) * lng_ref[...] + lnb_ref[...]
        y_ref[r0:r0 + CONV_ROWS, :] = (yc * jax.nn.sigmoid(yc)).astype(BF16)

    x1 = x1m + jnp.dot(y_ref[...], wout_ref[D_MLSTM:, :], preferred_element_type=F32)
    x1_ref[...] = x1
    hn = _rms(x1, gffn_ref[...])
    hn_ref[...] = _pack_rows(hn)

    logits = _nt_dot(wr_ref[...], hn.astype(BF16)) + br_ref[...]
    row = lax.broadcasted_iota(jnp.int32, (ROUTER_ROWS, tm), 0).astype(F32)
    neg = -jnp.inf
    no_row = float(ROUTER_ROWS)
    gl = jnp.where(row < N_GROUPS, logits, neg)
    gmax = jnp.max(gl, axis=0, keepdims=True)
    p_top = 1.0 / jnp.sum(jnp.exp(gl - gmax), axis=0, keepdims=True)
    g_idx = jnp.min(jnp.where(gl == gmax, row, no_row), axis=0, keepdims=True)
    lo = EXPERT_ROW0 + EXPERTS_PER_GROUP * g_idx
    in_grp = (row >= lo) & (row < lo + EXPERTS_PER_GROUP)
    el = jnp.where(in_grp, logits, neg)
    ee = jnp.exp(el - jnp.max(el, axis=0, keepdims=True))
    pe = jnp.where(in_grp, ee / jnp.sum(ee, axis=0, keepdims=True), -1.0)
    v1 = jnp.max(pe, axis=0, keepdims=True)
    i1 = jnp.min(jnp.where(pe == v1, row, no_row), axis=0, keepdims=True)
    pe2 = jnp.where(row == i1, -1.0, pe)
    v2 = jnp.max(pe2, axis=0, keepdims=True)
    i2 = jnp.min(jnp.where(pe2 == v2, row, no_row), axis=0, keepdims=True)
    wsum = v1 + v2
    gate1 = p_top * (v1 / wsum)
    gate2 = p_top * (v2 / wsum)

    oh1 = (row == i1).astype(F32)
    oh2 = (row == i2).astype(F32)
    cnt = jnp.dot(jnp.concatenate([oh1, oh2], axis=0).astype(BF16), tri_ref[...], preferred_element_type=F32)
    run = run_ref[...]
    tot1 = jnp.sum(oh1, axis=1, keepdims=True)
    tot2 = jnp.sum(oh2, axis=1, keepdims=True)
    rank1 = jnp.sum(oh1 * (run + cnt[:ROUTER_ROWS]), axis=0, keepdims=True)
    rank2 = jnp.sum(oh2 * (run + tot1 + cnt[ROUTER_ROWS:]), axis=0, keepdims=True)
    run = run + tot1 + tot2
    run_ref[...] = run
    cnt_ref[...] = jnp.broadcast_to(run, cnt_ref.shape)

    mr = lax.broadcasted_iota(jnp.int32, (META_LANES, tm), 0)
    metat = jnp.where(mr == 0, i1 - EXPERT_ROW0, 0.0)
    metat = jnp.where(mr == 1, i2 - EXPERT_ROW0, metat)
    metat = jnp.where(mr == 2, gate1, metat)
    metat = jnp.where(mr == 3, gate2, metat)
    metat = jnp.where(mr == 4, rank1, metat)
    metat = jnp.where(mr == 5, rank2, metat)
    metat_ref[...] = metat
    meta_ref[...] = metat.T


def _mixout(x, tile0, hf, hb, o, u, seq_lens, conv_w, conv_b, ln_g, ln_b, w_out, g_ffn, w_rg, b_rg, w_re, b_re):
    t = x.shape[0]
    n_tiles = t // TM_MIX
    hpt = TM_MIX // CONV_HALO
    n_halo = u.shape[0] // CONV_HALO
    firsts, lasts, pos = [], [], 0
    for ln in seq_lens:
        assert ln % TM_MIX == 0
        firsts.append(pos // TM_MIX)
        lasts.append((pos + ln) // TM_MIX - 1)
        pos += ln
    assert pos == t
    first = jnp.zeros((n_tiles,), jnp.int32).at[jnp.array(firsts)].set(1)
    last = jnp.zeros((n_tiles,), jnp.int32).at[jnp.array(lasts)].set(1)

    cw = jnp.zeros((32, D_CONV), F32).at[:CONV_WIDTH].set(conv_w.reshape(CONV_WIDTH, D_CONV))
    wr = jnp.zeros((ROUTER_ROWS, D_MODEL), F32)
    wr = wr.at[:N_GROUPS].set(w_rg.T).at[EXPERT_ROW0:EXPERT_ROW0 + N_EXPERTS].set(w_re.T).astype(BF16)
    br = jnp.zeros((ROUTER_ROWS, 1), F32)
    br = br.at[:N_GROUPS, 0].set(b_rg).at[EXPERT_ROW0:EXPERT_ROW0 + N_EXPERTS, 0].set(b_re)
    tri = (lax.broadcasted_iota(jnp.int32, (TM_MIX, TM_MIX), 0)
           < lax.broadcasted_iota(jnp.int32, (TM_MIX, TM_MIX), 1)).astype(BF16)

    tok = lambda i, f, l: (i, 0)
    flat = lambda i, f, l: (tile0 + i, 0)
    fixed = lambda i, f, l: (0, 0)
    row = lambda n: pl.BlockSpec((1, n), fixed)
    half = pl.BlockSpec((TM_MIX, D_MLSTM), flat)
    grid_spec = pltpu.PrefetchScalarGridSpec(
        num_scalar_prefetch=2,
        grid=(n_tiles,),
        in_specs=[pl.BlockSpec((TM_MIX, D_MODEL), tok), half, half, half,
                  pl.BlockSpec((TM_MIX, D_CONV), flat),
                  pl.BlockSpec((CONV_HALO, D_CONV), lambda i, f, l: (jnp.maximum((tile0 + i) * hpt - 1, 0), 0)),
                  pl.BlockSpec((CONV_HALO, D_CONV),
                               lambda i, f, l: (jnp.minimum((tile0 + i + 1) * hpt, n_halo - 1), 0)),
                  pl.BlockSpec((32, D_CONV), fixed), row(D_CONV), row(D_CONV), row(D_CONV),
                  pl.BlockSpec((D_MODEL, D_MODEL), fixed), row(D_MODEL),
                  pl.BlockSpec((ROUTER_ROWS, D_MODEL), fixed), pl.BlockSpec((ROUTER_ROWS, 1), fixed),
                  pl.BlockSpec((TM_MIX, TM_MIX), fixed)],
        out_specs=[pl.BlockSpec((TM_MIX, D_MODEL), tok), pl.BlockSpec((TM_MIX, D_MODEL // 2), tok),
                   pl.BlockSpec((TM_MIX, META_LANES), tok),
                   pl.BlockSpec((META_LANES, TM_MIX), lambda i, f, l: (0, i)),
                   pl.BlockSpec((ROUTER_ROWS, 128), fixed)],
        scratch_shapes=[pltpu.VMEM((TM_MIX + 2 * CONV_HALO, D_CONV), F32),
                        pltpu.VMEM((TM_MIX, D_CONV), BF16),
                        pltpu.VMEM((ROUTER_ROWS, 1), F32)],
    )
    return pl.pallas_call(
        _mixout_kernel,
        grid_spec=grid_spec,
        out_shape=[jax.ShapeDtypeStruct((t, D_MODEL), F32), jax.ShapeDtypeStruct((t, D_MODEL // 2), jnp.int32),
                   jax.ShapeDtypeStruct((t, META_LANES), F32), jax.ShapeDtypeStruct((META_LANES, t), F32),
                   jax.ShapeDtypeStruct((ROUTER_ROWS, 128), F32)],
        compiler_params=_cparams(),
        name="mixout",
    )(first, last, x, hf, hb, o, u, u, u, cw, conv_b.reshape(1, D_CONV), ln_g.reshape(1, D_CONV),
      ln_b.reshape(1, D_CONV), w_out.astype(BF16), g_ffn.reshape(1, D_MODEL), wr, br, tri)


def _sc_mesh():
    return plsc.VectorSubcoreMesh(core_axis_name="c", subcore_axis_name="s")


def _sc_worker_base(per_worker):
    wid = lax.axis_index("s") * SC_CORES + lax.axis_index("c")
    return wid * per_worker


def _dispatch(hn, pos1, pos2, n_slots):
    t, d = hn.shape
    per_worker = t // SC_WORKERS
    assert per_worker * SC_WORKERS == t and per_worker % SC_ROWS == 0

    def body(h_hbm, p1_hbm, p2_hbm, out_hbm, i1_v, i2_v, rows_v, sem1, sem2):
        base0 = _sc_worker_base(per_worker)

        @pl.loop(0, per_worker // SC_ROWS)
        def _(c):
            base = pl.multiple_of(base0 + c * SC_ROWS, SC_ROWS)
            pltpu.sync_copy(p1_hbm.at[pl.ds(base, SC_ROWS)], i1_v)
            pltpu.sync_copy(p2_hbm.at[pl.ds(base, SC_ROWS)], i2_v)
            pltpu.sync_copy(h_hbm.at[pl.ds(base, SC_ROWS)], rows_v)
            c1 = pltpu.async_copy(rows_v, out_hbm.at[i1_v], sem1)
            c2 = pltpu.async_copy(rows_v, out_hbm.at[i2_v], sem2)
            c1.wait()
            c2.wait()

    return pl.kernel(
        body,
        out_type=jax.ShapeDtypeStruct((n_slots, d), hn.dtype),
        mesh=_sc_mesh(),
        scratch_types=[pltpu.VMEM((SC_ROWS,), jnp.int32), pltpu.VMEM((SC_ROWS,), jnp.int32),
                       pltpu.VMEM((SC_ROWS, d), hn.dtype), pltpu.SemaphoreType.DMA, pltpu.SemaphoreType.DMA],
        name="dispatch",
    )(hn, pos1, pos2)


def _collect(ys, pos1, pos2):
    t = pos1.shape[0]
    d = ys.shape[1]
    per_worker = t // SC_WORKERS
    assert per_worker * SC_WORKERS == t and per_worker % SC_ROWS == 0

    def body(ys_hbm, p1_hbm, p2_hbm, y1_hbm, y2_hbm, i_v, rows_v, sem):
        base0 = _sc_worker_base(per_worker)

        @pl.loop(0, per_worker // SC_ROWS)
        def _(c):
            base = pl.multiple_of(base0 + c * SC_ROWS, SC_ROWS)
            for p_hbm, y_hbm in ((p1_hbm, y1_hbm), (p2_hbm, y2_hbm)):
                pltpu.sync_copy(p_hbm.at[pl.ds(base, SC_ROWS)], i_v)
                pltpu.async_copy(ys_hbm.at[i_v], rows_v, sem).wait()
                pltpu.sync_copy(rows_v, y_hbm.at[pl.ds(base, SC_ROWS)])

    out = jax.ShapeDtypeStruct((t, d), ys.dtype)
    return pl.kernel(
        body,
        out_type=(out, out),
        mesh=_sc_mesh(),
        scratch_types=[pltpu.VMEM((SC_ROWS,), jnp.int32), pltpu.VMEM((SC_ROWS, d), ys.dtype),
                       pltpu.SemaphoreType.DMA],
        name="collect",
    )(ys, pos1, pos2)


def _experts_kernel(te_ref, rows_ref, hs_ref, wg_hbm, wu_hbm, wd_hbm, ys_ref,
                    wgf_ref, wuf_ref, wdf_ref, wgb_ref, wub_ref, wdb_ref, sem):
    i = pl.program_id(0)
    n = pl.num_programs(0)
    e = te_ref[i]

    def weight_copies(expert):
        return (pltpu.make_async_copy(wg_hbm.at[expert], wgf_ref, sem.at[0]),
                pltpu.make_async_copy(wu_hbm.at[expert], wuf_ref, sem.at[1]),
                pltpu.make_async_copy(wd_hbm.at[expert], wdf_ref, sem.at[2]))

    @pl.when(i == 0)
    def _():
        for c in weight_copies(e):
            c.start()

    @pl.when((i == 0) | (e != te_ref[jnp.maximum(i - 1, 0)]))
    def _():
        for c in weight_copies(e):
            c.wait()
        wgb_ref[...] = wgf_ref[...].astype(BF16)
        wub_ref[...] = wuf_ref[...].astype(BF16)
        wdb_ref[...] = wdf_ref[...].astype(BF16)
        nxt = lax.while_loop(lambda j: (j < n) & (te_ref[jnp.minimum(j, n - 1)] == e), lambda j: j + 1, i + 1)

        @pl.when(nxt < n)
        def _():
            for c in weight_copies(te_ref[jnp.minimum(nxt, n - 1)]):
                c.start(priority=1)

    @pl.when(rows_ref[i] > 0)
    def _():
        x = _unpack_rows(hs_ref[...]).astype(BF16)
        hg = jnp.dot(x, wgb_ref[...], preferred_element_type=F32)
        hu = jnp.dot(x, wub_ref[...], preferred_element_type=F32)
        hid = (hg * jax.nn.sigmoid(hg) * hu).astype(BF16)
        ys_ref[...] = _pack_rows(jnp.dot(hid, wdb_ref[...], preferred_element_type=F32))

    @pl.when(rows_ref[i] == 0)
    def _():
        ys_ref[...] = jnp.zeros_like(ys_ref)


def _experts(hs, tile_expert, tile_rows, w_gate, w_up, w_down):
    n_tiles = tile_expert.shape[0]
    slot = lambda i, te, rows: (i, 0)
    hbm = pl.BlockSpec(memory_space=pl.ANY)
    grid_spec = pltpu.PrefetchScalarGridSpec(
        num_scalar_prefetch=2,
        grid=(n_tiles,),
        in_specs=[pl.BlockSpec((TM_EXP, D_MODEL // 2), slot), hbm, hbm, hbm],
        out_specs=pl.BlockSpec((TM_EXP, D_MODEL // 2), slot),
        scratch_shapes=[pltpu.VMEM((D_MODEL, D_EXPERT), F32), pltpu.VMEM((D_MODEL, D_EXPERT), F32),
                        pltpu.VMEM((D_EXPERT, D_MODEL), F32),
                        pltpu.VMEM((D_MODEL, D_EXPERT), BF16), pltpu.VMEM((D_MODEL, D_EXPERT), BF16),
                        pltpu.VMEM((D_EXPERT, D_MODEL), BF16),
                        pltpu.SemaphoreType.DMA((3,))],
    )
    return pl.pallas_call(
        _experts_kernel,
        grid_spec=grid_spec,
        out_shape=jax.ShapeDtypeStruct((n_tiles * TM_EXP, D_MODEL // 2), jnp.int32),
        compiler_params=_cparams(),
        name="experts",
    )(tile_expert, tile_rows, hs, w_gate, w_up, w_down)


def _combine_kernel(x1_ref, meta_ref, y1_ref, y2_ref, gfin_ref, out_ref):
    meta = meta_ref[...]
    x2 = x1_ref[...] + meta[:, 2:3] * _unpack_rows(y1_ref[...]) + meta[:, 3:4] * _unpack_rows(y2_ref[...])
    out_ref[...] = _rms(x2, gfin_ref[...])


def _combine(x1, meta, y1, y2, g_final):
    t = x1.shape[0]
    assert t % TM_OUT == 0
    tok = pl.BlockSpec((TM_OUT, D_MODEL), lambda i: (i, 0))
    packed = pl.BlockSpec((TM_OUT, D_MODEL // 2), lambda i: (i, 0))
    return pl.pallas_call(
        _combine_kernel,
        grid=(t // TM_OUT,),
        in_specs=[tok, pl.BlockSpec((TM_OUT, META_LANES), lambda i: (i, 0)), packed, packed,
                  pl.BlockSpec((1, D_MODEL), lambda i: (0, 0))],
        out_specs=tok,
        out_shape=jax.ShapeDtypeStruct((t, D_MODEL), F32),
        compiler_params=_cparams(),
        name="combine",
    )(x1, meta, y1, y2, g_final.reshape(1, D_MODEL))


def _layout_kernel(cnt_ref, metat_ref, pos_ref, tiles_ref):
    n_lanes = tiles_ref.shape[1]
    tile_start = lax.broadcasted_iota(jnp.int32, (1, n_lanes), 1).astype(F32) * TM_EXP
    m = metat_ref[...]
    eid = m[0:2]
    p = m[4:6]
    off = jnp.zeros((1, 1), F32)
    t_exp = jnp.full((1, n_lanes), N_EXPERTS - 1.0, F32)
    t_rows = jnp.zeros((1, n_lanes), F32)
    for e in range(N_EXPERTS):
        cnt = cnt_ref[EXPERT_ROW0 + e:EXPERT_ROW0 + e + 1, 0:1]
        padded = jnp.ceil(cnt * (1.0 / TM_EXP)) * TM_EXP
        p = p + jnp.where(eid == e, off, 0.0)
        mine = (tile_start >= off) & (tile_start < off + padded)
        t_exp = jnp.where(mine, float(e), t_exp)
        t_rows = jnp.where(mine, jnp.clip(cnt - (tile_start - off), 0.0, float(TM_EXP)), t_rows)
        off = off + padded
    pos_ref[...] = p.astype(jnp.int32)
    tiles_ref[...] = jnp.concatenate([t_exp, t_rows], axis=0).astype(jnp.int32)


def _slot_layout(metat, counts_rows, n_tiles):
    t = metat.shape[1]
    tp = min(TM_POS, t)
    assert t % tp == 0
    n_lanes = (n_tiles + 127) // 128 * 128
    pos, tiles = pl.pallas_call(
        _layout_kernel,
        grid=(t // tp,),
        in_specs=[pl.BlockSpec(counts_rows.shape, lambda i: (0, 0)),
                  pl.BlockSpec((META_LANES, tp), lambda i: (0, i))],
        out_specs=[pl.BlockSpec((2, tp), lambda i: (0, i)), pl.BlockSpec((2, n_lanes), lambda i: (0, 0))],
        out_shape=[jax.ShapeDtypeStruct((2, t), jnp.int32), jax.ShapeDtypeStruct((2, n_lanes), jnp.int32)],
        compiler_params=_cparams(),
        name="layout",
    )(counts_rows, metat)
    return pos[0], pos[1], tiles[0, :n_tiles], tiles[1, :n_tiles]


def kernel(x_prompt, x_sample, g_mix, w_in, b_gate, conv_w, conv_b, ln_g, ln_b, w_out, g_ffn,
           w_router_group, b_router_group, w_router_expert, b_router_expert, w_gate, w_up, w_down, g_final):
    assert g_mix.shape[0] == 1, "one layer"
    bp, lp, _ = x_prompt.shape
    bs, ls, _ = x_sample.shape
    seq_lens = [lp] * bp + [ls] * bs
    tp, ts = bp * lp, bs * ls
    xp = x_prompt.reshape(tp, D_MODEL)
    xs = x_sample.reshape(ts, D_MODEL)

    q, v, o, u, kt, gr = _inproj(xp, xs, g_mix[0], w_in[0], b_gate[0])
    hf, hb = _mlstm(q, kt, v, gr, seq_lens)

    outs = {}
    groups = (("prompt", xp, 0, [lp] * bp), ("sample", xs, tp // TM_MIX, [ls] * bs))
    for name, x, tile0, lens in sorted(groups, key=lambda g: -g[1].shape[0]):
        tg = x.shape[0]
        x1, hn, meta, metat, counts = _mixout(x, tile0, hf, hb, o, u, lens, conv_w[0], conv_b[0], ln_g[0], ln_b[0],
                                              w_out[0], g_ffn[0], w_router_group[0], b_router_group[0],
                                              w_router_expert[0], b_router_expert[0])
        n_tiles = (2 * tg + N_EXPERTS * (TM_EXP - 1)) // TM_EXP + 1
        pos1, pos2, tile_expert, tile_rows = _slot_layout(metat, counts, n_tiles)
        hs = _dispatch(hn, pos1, pos2, n_tiles * TM_EXP)
        ys = _experts(hs, tile_expert, tile_rows, w_gate[0], w_up[0], w_down[0])
        y1, y2 = _collect(ys, pos1, pos2)
        outs[name] = _combine(x1, meta, y1, y2, g_final)
    return outs["prompt"].reshape(bp, lp, D_MODEL), outs["sample"].reshape(bs, ls, D_MODEL)
```
